```python
import jax, jax.numpy as jnp
from jax import lax
import numpy as np

D_MODEL = 2048
BATCH = 4
SEQ = 4096
DEPTH = 2

POOL_WIDTH = D_MODEL // 2
POOL_WINDOWS = (2, 4, 8, 16)
POOL_GROUP = POOL_WIDTH // len(POOL_WINDOWS)
HEAD_DIM = 128
N_HEADS = (D_MODEL - POOL_WIDTH) // HEAD_DIM
N_KV = 2
GROUP_SIZE = N_HEADS // N_KV
ATT_WIDTH = N_HEADS * HEAD_DIM
KV_WIDTH = N_KV * HEAD_DIM
CMP_LEN = 32
CMP_STRIDE = 16
SLC_LEN = 64
SLC_TOPK = 16
WIN = 512
Q_BLOCK = 128
SLC_Q_CHUNK = 32
D_FF = -(-8 * D_MODEL // (3 * 256)) * 256
IN_COLS = POOL_WIDTH + ATT_WIDTH + 6 * KV_WIDTH + 3 * N_HEADS
NORM_EPS = 1e-6
NEG_INF = -1e30
FORCE_BONUS = 1e3

kernel_name = "hymba_pool_nsa_alibi_adaln_block"


def rmsnorm(x, g):
    x32 = x.astype(jnp.float32)
    y = x32 * lax.rsqrt(jnp.mean(x32 * x32, axis=-1, keepdims=True) + NORM_EPS)
    return (y * g.astype(jnp.float32)).astype(x.dtype)


def masked_softmax(s, mask):
    s = jnp.where(mask, s, NEG_INF)
    p = jax.nn.softmax(s, axis=-1)
    return jnp.where(mask, p, 0.0)


def alibi_slopes():
    sl = 2.0 ** (-8.0 * np.arange(1, N_HEADS + 1) / N_HEADS)
    return jnp.asarray(sl, jnp.float32).reshape(N_KV, GROUP_SIZE)


def split_points():
    sizes = [POOL_WIDTH, ATT_WIDTH] + [KV_WIDTH] * 6 + [3 * N_HEADS]
    return [int(v) for v in np.cumsum(sizes)[:-1]]


def pool_mixer(u, w_pool, pool_scale):
    B, T, _ = u.shape
    u32 = u.astype(jnp.float32)
    cs = jnp.pad(jnp.cumsum(u32, axis=1), ((0, 0), (1, 0), (0, 0)))
    t = np.arange(T)
    outs = []
    for gi, w in enumerate(POOL_WINDOWS):
        sl = slice(gi * POOL_GROUP, (gi + 1) * POOL_GROUP)
        c_g = cs[..., sl]
        lo = np.maximum(t + 1 - w, 0)
        cnt = jnp.asarray((t + 1 - lo).astype(np.float32))
        mean = (c_g[:, 1:] - c_g[:, lo]) / cnt[None, :, None]
        outs.append(mean - u32[..., sl])
    pooled = jnp.stack(outs, axis=2).astype(u.dtype)
    mixed = jnp.einsum('btgc,gcd->btgd', pooled, w_pool)
    return mixed.reshape(B, T, POOL_WIDTH) * pool_scale


def compress(k, pe, w1, w2):
    B, T, G, dh = k.shape
    n_cmp = (T - CMP_LEN) // CMP_STRIDE + 1
    idx = np.arange(n_cmp)[:, None] * CMP_STRIDE + np.arange(CMP_LEN)[None, :]
    blocks = k[:, idx] + pe[None, None, :, None, :]
    flat = blocks.transpose(0, 1, 3, 2, 4).reshape(B, n_cmp, G, CMP_LEN * dh)
    return jax.nn.silu(flat @ w1) @ w2


def nsa_mixer(q, kc, vc, ks, vs, kw, vw, gate_logits, q_gain, k_gain, pe_cmp, w_cmp1, w_cmp2):
    B, T = q.shape[:2]
    G, R, dh = N_KV, GROUP_SIZE, HEAD_DIM
    scale = dh ** -0.5
    slopes = alibi_slopes()
    qg = rmsnorm(q, q_gain).reshape(B, T, G, R, dh)
    t = np.arange(T)

    kcmp = rmsnorm(compress(kc, pe_cmp[0], w_cmp1[0], w_cmp2[0]), k_gain[0])
    vcmp = compress(vc, pe_cmp[1], w_cmp1[1], w_cmp2[1])
    n_cmp = kcmp.shape[1]
    end_pos = np.arange(n_cmp) * CMP_STRIDE + CMP_LEN - 1
    dist_c = (t[:, None] - end_pos[None, :]).astype(np.float32)
    s = jnp.einsum('btgrd,bcgd->bgrtc', qg, kcmp).astype(jnp.float32) * scale
    s = s - slopes[None, :, :, None, None] * jnp.asarray(dist_c)[None, None, None]
    p_cmp = masked_softmax(s, jnp.asarray(dist_c >= 0)[None, None, None])
    o_cmp = jnp.einsum('bgrtc,bcgd->btgrd', p_cmp.astype(vcmp.dtype), vcmp)

    n_slc = T // SLC_LEN
    n_sel = min(SLC_TOPK, n_slc)
    cst = np.arange(n_cmp) * CMP_STRIDE
    sst = np.arange(n_slc) * SLC_LEN
    overlap = ((cst[:, None] < sst[None, :] + SLC_LEN) & (cst[:, None] + CMP_LEN > sst[None, :]))
    imp = jnp.einsum('bgrtc,cs->bgts', p_cmp, jnp.asarray(overlap.astype(np.float32)))
    cur = t // SLC_LEN
    jb = np.arange(n_slc)
    valid = sst[None, :] <= t[:, None]
    forced = (jb[None, :] == 0) | (jb[None, :] == cur[:, None]) | (jb[None, :] == cur[:, None] - 1)
    score = jnp.where(jnp.asarray(valid), imp + FORCE_BONUS * jnp.asarray(forced.astype(np.float32)), NEG_INF)
    idx = lax.top_k(score, n_sel)[1]

    ks_n = rmsnorm(ks, k_gain[1])
    kblk = ks_n.reshape(B, n_slc, SLC_LEN, G, dh).transpose(0, 3, 1, 2, 4)
    vblk = vs.reshape(B, n_slc, SLC_LEN, G, dh).transpose(0, 3, 1, 2, 4)
    nq = T // SLC_Q_CHUNK
    q_ch = qg.reshape(B, nq, SLC_Q_CHUNK, G, R, dh).transpose(1, 0, 3, 2, 4, 5)
    idx_ch = idx.reshape(B, G, nq, SLC_Q_CHUNK, n_sel).transpose(2, 0, 1, 3, 4)
    t_ch = jnp.asarray(t.reshape(nq, SLC_Q_CHUNK).astype(np.int32))
    bi = jnp.arange(B)[:, None, None, None]
    gi = jnp.arange(G)[None, :, None, None]
    offs = jnp.arange(SLC_LEN, dtype=jnp.int32)

    def sel_chunk(args):
        qc, ic, tc = args
        ksel = kblk[bi, gi, ic].reshape(B, G, SLC_Q_CHUNK, n_sel * SLC_LEN, dh)
        vsel = vblk[bi, gi, ic].reshape(B, G, SLC_Q_CHUNK, n_sel * SLC_LEN, dh)
        pos = (ic[..., None] * SLC_LEN + offs).reshape(B, G, SLC_Q_CHUNK, n_sel * SLC_LEN)
        dist = tc[None, None, :, None] - pos
        sc = jnp.einsum('bgqrd,bgqkd->bgrqk', qc, ksel).astype(jnp.float32) * scale
        sc = sc - slopes[None, :, :, None, None] * dist[:, :, None].astype(jnp.float32)
        pr = masked_softmax(sc, (dist >= 0)[:, :, None])
        return jnp.einsum('bgrqk,bgqkd->bqgrd', pr.astype(vsel.dtype), vsel)

    o_slc = lax.map(sel_chunk, (q_ch, idx_ch, t_ch))
    o_slc = o_slc.transpose(1, 0, 2, 3, 4, 5).reshape(B, T, G, R, dh)

    kw_n = rmsnorm(kw, k_gain[2])
    nb = T // Q_BLOCK
    nwb = WIN // Q_BLOCK
    def band(a):
        ab = jnp.pad(a.reshape(B, nb, Q_BLOCK, G, dh), ((0, 0), (nwb, 0), (0, 0), (0, 0), (0, 0)))
        return jnp.concatenate([ab[:, i:i + nb] for i in range(nwb + 1)], axis=2)
    kband, vband = band(kw_n), band(vw)
    jq = np.arange(nb)
    kpos = (jq[:, None] - nwb) * Q_BLOCK + np.arange((nwb + 1) * Q_BLOCK)[None, :]
    tq = jq[:, None] * Q_BLOCK + np.arange(Q_BLOCK)[None, :]
    dist_w = tq[:, :, None] - kpos[:, None, :]
    mask_w = (dist_w >= 0) & (dist_w < WIN) & (kpos[:, None, :] >= 0)
    qw = qg.reshape(B, nb, Q_BLOCK, G, R, dh)
    sw = jnp.einsum('bjqgrd,bjkgd->bjgrqk', qw, kband).astype(jnp.float32) * scale
    sw = sw - slopes[None, None, :, :, None, None] * jnp.asarray(dist_w.astype(np.float32))[None, :, None, None]
    pw = masked_softmax(sw, jnp.asarray(mask_w)[None, :, None, None])
    o_win = jnp.einsum('bjgrqk,bjkgd->bjqgrd', pw.astype(vband.dtype), vband).reshape(B, T, G, R, dh)

    g = jax.nn.sigmoid(gate_logits.reshape(B, T, G, R, 3))
    o = g[..., 0:1] * o_cmp + g[..., 1:2] * o_slc + g[..., 2:3] * o_win
    return o.reshape(B, T, ATT_WIDTH)


def setup_inputs(seed: int = 0) -> dict:
    key = jax.random.key(seed)
    ks = jax.random.split(key, 17)
    f32 = jnp.float32
    nrm = lambda k, shape, s: jax.random.normal(k, shape, f32) * s
    L = DEPTH
    return {
        "x": nrm(ks[0], (BATCH, SEQ, D_MODEL), 1.0),
        "c": nrm(ks[1], (BATCH, D_MODEL), 1.0),
        "w_ada": nrm(ks[2], (L, D_MODEL, 6 * D_MODEL), 0.5 * D_MODEL ** -0.5),
        "b_ada": nrm(ks[3], (L, 6 * D_MODEL), 0.02),
        "norm_g": 1.0 + nrm(ks[4], (L, 2, D_MODEL), 0.05),
        "w_in": nrm(ks[5], (L, D_MODEL, IN_COLS), D_MODEL ** -0.5),
        "q_gain": 1.0 + nrm(ks[6], (L, HEAD_DIM), 0.05),
        "k_gain": 1.0 + nrm(ks[7], (L, 3, HEAD_DIM), 0.05),
        "pe_cmp": nrm(ks[8], (L, 2, CMP_LEN, HEAD_DIM), 0.5),
        "w_cmp1": nrm(ks[9], (L, 2, CMP_LEN * HEAD_DIM, HEAD_DIM), (CMP_LEN * HEAD_DIM) ** -0.5),
        "w_cmp2": nrm(ks[10], (L, 2, HEAD_DIM, HEAD_DIM), HEAD_DIM ** -0.5),
        "w_pool": nrm(ks[11], (L, len(POOL_WINDOWS), POOL_GROUP, POOL_GROUP), POOL_GROUP ** -0.5),
        "pool_scale": 1.0 + nrm(ks[12], (L, POOL_WIDTH), 0.05),
        "w_out": nrm(ks[13], (L, D_MODEL, D_MODEL), D_MODEL ** -0.5),
        "w_gate_up": nrm(ks[14], (L, D_MODEL, 2 * D_FF), D_MODEL ** -0.5),
        "w_down": nrm(ks[15], (L, D_FF, D_MODEL), D_FF ** -0.5),
    }


def reference(x, c, w_ada, b_ada, norm_g, w_in, q_gain, k_gain, pe_cmp, w_cmp1, w_cmp2,
              w_pool, pool_scale, w_out, w_gate_up, w_down):
    B, T, _ = x.shape
    cond = jax.nn.silu(c)
    pts = split_points()
    for l in range(DEPTH):
        mod = (cond @ w_ada[l] + b_ada[l])[:, None, :]
        sh1, sc1, g1, sh2, sc2, g2 = jnp.split(mod, 6, axis=-1)
        h = rmsnorm(x, norm_g[l, 0]) * (1.0 + sc1) + sh1
        proj = h @ w_in[l]
        u, q, kc, vc, ksl, vsl, kwn, vwn, gl = jnp.split(proj, pts, axis=-1)
        kv = lambda a: a.reshape(B, T, N_KV, HEAD_DIM)
        a_out = pool_mixer(u, w_pool[l], pool_scale[l])
        o_out = nsa_mixer(q.reshape(B, T, N_HEADS, HEAD_DIM), kv(kc), kv(vc), kv(ksl), kv(vsl),
                          kv(kwn), kv(vwn), gl, q_gain[l], k_gain[l], pe_cmp[l],
                          w_cmp1[l], w_cmp2[l])
        x = x + g1 * (jnp.concatenate([a_out, o_out], axis=-1) @ w_out[l])
        h2 = rmsnorm(x, norm_g[l, 1]) * (1.0 + sc2) + sh2
        gate, up = jnp.split(h2 @ w_gate_up[l], 2, axis=-1)
        x = x + g2 * ((jax.nn.silu(gate) * up) @ w_down[l])
    return x
```

```python
import functools

import numpy as np
import jax
import jax.numpy as jnp
from jax import lax
from jax.experimental import pallas as pl
from jax.experimental.pallas import tpu as pltpu

F32 = jnp.float32
BF16 = jnp.bfloat16

POOL_WINDOWS = (2, 4, 8, 16)
HEAD_DIM = 128
N_KV = 2
CMP_LEN = 32
CMP_STRIDE = 16
SLC_LEN = 64
SLC_SHIFT = 6
SLC_TOPK = 16
WIN = 512
NORM_EPS = 1e-6
NEG_INF = -1e30
FORCE_BONUS = 1e3

LANES = 128
POOL_HALO = 16
VMEM_LIMIT_BYTES = 56 * 1024 * 1024

ADA_BN = 1024
IN_BM, IN_BN = 512, 1280
KPREP_BT = 1024
POOL_BT = 1024
ATT_TQ = 256
OUT_BM, OUT_BN = 1024, 1024
FFN1_BM, FFN1_BN = 1024, 512
FFN2_BM, FFN2_BN = 1024, 512


def _cparams(*sem):
    return pltpu.CompilerParams(dimension_semantics=sem, vmem_limit_bytes=VMEM_LIMIT_BYTES)


def _dot(a, b):
    return jnp.dot(a, b, preferred_element_type=F32)


def _dot_nt(a, b):
    return lax.dot_general(a, b, (((1,), (1,)), ((), ())), preferred_element_type=F32)


def _rms(x):
    return x * lax.rsqrt(jnp.mean(x * x, axis=-1, keepdims=True) + NORM_EPS)


def _sigmoid(x):
    return 1.0 / (1.0 + jnp.exp(-x))


def _ada_kernel(c_ref, w_ref, b_ref, o_ref):
    c = c_ref[...]
    cs = c * _sigmoid(c)
    o_ref[0] = _dot(cs, w_ref[0]) + b_ref[0]


def _ada_call(c8, w_ada, b_ada):
    L, D, N = w_ada.shape
    rows = c8.shape[0]
    return pl.pallas_call(
        _ada_kernel,
        grid=(L, N // ADA_BN),
        in_specs=[
            pl.BlockSpec((rows, D), lambda l, n: (0, 0)),
            pl.BlockSpec((1, D, ADA_BN), lambda l, n: (l, 0, n)),
            pl.BlockSpec((1, 1, ADA_BN), lambda l, n: (l, 0, n)),
        ],
        out_specs=pl.BlockSpec((1, rows, ADA_BN), lambda l, n: (l, 0, n)),
        out_shape=jax.ShapeDtypeStruct((L, rows, N), F32),
        compiler_params=_cparams("parallel", "arbitrary"),
        name="ada_mod",
    )(c8, w_ada, b_ada.reshape(L, 1, N))


def _norm_mod(x_ref, ng_ref, sc_ref, sh_ref):
    y = _rms(x_ref[...]) * ng_ref[...]
    return (y * (1.0 + sc_ref[0]) + sh_ref[0]).astype(BF16)


def _in_kernel(x_ref, ng_ref, sc_ref, sh_ref, w_ref, o_ref, h_sc):
    @pl.when(pl.program_id(1) == 0)
    def _():
        h_sc[...] = _norm_mod(x_ref, ng_ref, sc_ref, sh_ref)

    o_ref[...] = _dot(h_sc[...], w_ref[...])


def _in_call(xf, ng, sc, sh, w, T):
    M, D = xf.shape
    N = w.shape[1]
    per_b = T // IN_BM
    return pl.pallas_call(
        _in_kernel,
        grid=(M // IN_BM, N // IN_BN),
        in_specs=[
            pl.BlockSpec((IN_BM, D), lambda i, n: (i, 0)),
            pl.BlockSpec((1, D), lambda i, n: (0, 0)),
            pl.BlockSpec((1, 1, D), lambda i, n: (i // per_b, 0, 0)),
            pl.BlockSpec((1, 1, D), lambda i, n: (i // per_b, 0, 0)),
            pl.BlockSpec((D, IN_BN), lambda i, n: (0, n)),
        ],
        out_specs=pl.BlockSpec((IN_BM, IN_BN), lambda i, n: (i, n)),
        out_shape=jax.ShapeDtypeStruct((M, N), F32),
        scratch_shapes=[pltpu.VMEM((IN_BM, D), BF16)],
        compiler_params=_cparams("parallel", "arbitrary"),
        name="in_proj",
    )(xf, ng, sc, sh, w)


def _pool_kernel(u_ref, halo_ref, w_ref, ps_ref, o_ref, ext_sc, *, tiles_per_batch):
    i = pl.program_id(0)
    bt = u_ref.shape[0]
    group = w_ref.shape[1]
    tile_in_batch = i % tiles_per_batch
    ext_sc[0:POOL_HALO, :] = jnp.where(tile_in_batch == 0, 0.0, halo_ref[...])
    ext_sc[POOL_HALO:, :] = u_ref[...]
    t = tile_in_batch * bt + lax.broadcasted_iota(jnp.int32, (bt, 1), 0)
    for gi, w in enumerate(POOL_WINDOWS):
        cols = slice(gi * group, (gi + 1) * group)
        acc = ext_sc[:, cols]
        k = 1
        while k < w:
            acc = acc + pltpu.roll(acc, k, 0)
            k *= 2
        cnt = jnp.minimum(t + 1, w).astype(F32)
        pooled = acc[POOL_HALO:, :] / cnt - u_ref[:, cols]
        mixed = _dot(pooled.astype(BF16), w_ref[gi])
        o_ref[:, cols] = (mixed * ps_ref[:, cols]).astype(o_ref.dtype)


def _pool_call(proj, w_pool, pool_scale, T):
    M = proj.shape[0]
    ng, group, _ = w_pool.shape
    width = ng * group
    tiles_per_batch = T // POOL_BT
    halo_blocks = POOL_BT // POOL_HALO
    return pl.pallas_call(
        functools.partial(_pool_kernel, tiles_per_batch=tiles_per_batch),
        grid=(M // POOL_BT,),
        in_specs=[
            pl.BlockSpec((POOL_BT, width), lambda i: (i, 0)),
            pl.BlockSpec((POOL_HALO, width), lambda i: (jnp.maximum(i * halo_blocks - 1, 0), 0)),
            pl.BlockSpec((ng, group, group), lambda i: (0, 0, 0)),
            pl.BlockSpec((1, width), lambda i: (0, 0)),
        ],
        out_specs=pl.BlockSpec((POOL_BT, width), lambda i: (i, 0)),
        out_shape=jax.ShapeDtypeStruct((M, width), BF16),
        scratch_shapes=[pltpu.VMEM((POOL_BT + POOL_HALO, width), F32)],
        compiler_params=_cparams("parallel"),
        name="pool_mixer",
    )(proj, proj, w_pool, pool_scale)


AUG_BLK_COL = SLC_LEN
AUG_OFF_COL = SLC_LEN + 1


def _key_extra(t):
    lane = lax.broadcasted_iota(jnp.int32, (t.shape[0], LANES), 1)
    blk = lax.shift_right_logical(t, SLC_SHIFT)
    off = t & (SLC_LEN - 1)
    extra = jnp.where(lane == blk, 1.0, 0.0)
    extra = jnp.where(lane == AUG_BLK_COL, blk.astype(F32), extra)
    extra = jnp.where(lane == AUG_OFF_COL, off.astype(F32), extra)
    return extra


def _kprep_kernel(ks_ref, vs_ref, kw_ref, vw_ref, kg_ref, ksa_ref, vso_ref, kwa_ref, vwo_ref,
                  *, tiles_per_batch):
    bt = ks_ref.shape[0]
    t = (pl.program_id(0) % tiles_per_batch) * bt + lax.broadcasted_iota(jnp.int32, (bt, 1), 0)
    extra = _key_extra(t).astype(BF16)
    for g in range(N_KV):
        cols = slice(g * HEAD_DIM, (g + 1) * HEAD_DIM)
        ksn = _rms(ks_ref[:, cols]) * kg_ref[1:2, :]
        kwn = _rms(kw_ref[:, cols]) * kg_ref[2:3, :]
        ksa_ref[0, g, :, 0:HEAD_DIM] = ksn.astype(BF16)
        ksa_ref[0, g, :, HEAD_DIM:] = extra
        kwa_ref[0, g, :, 0:HEAD_DIM] = kwn.astype(BF16)
        kwa_ref[0, g, :, HEAD_DIM:] = extra
        vso_ref[0, g] = vs_ref[:, cols].astype(BF16)
        vwo_ref[0, g] = vw_ref[:, cols].astype(BF16)


def _kprep_call(proj, k_gain, B, T, col0):
    kvw = N_KV * HEAD_DIM
    cb = col0 // kvw
    tiles_per_batch = T // KPREP_BT
    aug = jax.ShapeDtypeStruct((B, N_KV, T, 2 * HEAD_DIM), BF16)
    val = jax.ShapeDtypeStruct((B, N_KV, T, HEAD_DIM), BF16)
    in_spec = lambda j: pl.BlockSpec((KPREP_BT, kvw), lambda i: (i, cb + j))
    out_map = lambda i: (i // tiles_per_batch, 0, i % tiles_per_batch, 0)
    return pl.pallas_call(
        functools.partial(_kprep_kernel, tiles_per_batch=tiles_per_batch),
        grid=(B * tiles_per_batch,),
        in_specs=[in_spec(0), in_spec(1), in_spec(2), in_spec(3),
                  pl.BlockSpec((3, HEAD_DIM), lambda i: (0, 0))],
        out_specs=[pl.BlockSpec((1, N_KV, KPREP_BT, 2 * HEAD_DIM), out_map),
                   pl.BlockSpec((1, N_KV, KPREP_BT, HEAD_DIM), out_map),
                   pl.BlockSpec((1, N_KV, KPREP_BT, 2 * HEAD_DIM), out_map),
                   pl.BlockSpec((1, N_KV, KPREP_BT, HEAD_DIM), out_map)],
        out_shape=[aug, val, aug, val],
        compiler_params=_cparams("parallel"),
        name="kv_prep",
    )(proj, proj, proj, proj, k_gain)


def _cmp_kernel(k_ref, pe_ref, w1_ref, w2_ref, kg_ref, o_ref):
    kv = pl.program_id(1)
    nc = o_ref.shape[3]
    half = CMP_LEN // 2
    assert CMP_STRIDE == half
    xs = [k_ref[pl.ds(j, nc, stride=CMP_STRIDE), :] for j in range(half)]
    x = jnp.concatenate(xs, axis=1)
    kdim = half * HEAD_DIM
    lo = _dot((x + pe_ref[0, 0:1, :]).astype(BF16), w1_ref[0, 0:kdim, :])
    hi = _dot((x + pe_ref[0, 1:2, :]).astype(BF16), w1_ref[0, kdim:, :])
    pre = lo + pltpu.roll(hi, nc - 1, 0)
    hdn = pre * _sigmoid(pre)
    out = _dot(hdn.astype(BF16), w2_ref[0])
    normed = _rms(out) * kg_ref[0:1, :]
    o_ref[0, 0, 0] = jnp.where(kv == 0, normed, out).astype(o_ref.dtype)


def _cmp_call(proj, pe2, w1, w2, k_gain, B, T, col0):
    nc = T // CMP_STRIDE
    cb = col0 // HEAD_DIM
    kdim = CMP_LEN * HEAD_DIM
    return pl.pallas_call(
        _cmp_kernel,
        grid=(B, 2, N_KV),
        in_specs=[
            pl.BlockSpec((T, HEAD_DIM), lambda b, kv, g: (b, cb + kv * N_KV + g)),
            pl.BlockSpec((1, 2, kdim // 2), lambda b, kv, g: (kv, 0, 0)),
            pl.BlockSpec((1, kdim, HEAD_DIM), lambda b, kv, g: (kv, 0, 0)),
            pl.BlockSpec((1, HEAD_DIM, HEAD_DIM), lambda b, kv, g: (kv, 0, 0)),
            pl.BlockSpec((3, HEAD_DIM), lambda b, kv, g: (0, 0)),
        ],
        out_specs=pl.BlockSpec((1, 1, 1, nc, HEAD_DIM), lambda b, kv, g: (b, kv, g, 0, 0)),
        out_shape=jax.ShapeDtypeStruct((B, 2, N_KV, nc, HEAD_DIM), BF16),
        compiler_params=_cparams("parallel", "arbitrary", "arbitrary"),
        name="compress",
    )(proj, pe2, w1, w2, k_gain)


def _softmax_rows(s, mask):
    s = jnp.where(mask, s, NEG_INF)
    m = jnp.max(s, axis=-1, keepdims=True)
    p = jnp.where(mask, jnp.exp(s - m), 0.0)
    l = jnp.sum(p, axis=-1, keepdims=True)
    return p * (1.0 / jnp.where(l > 0.0, l, 1.0))


def _attn_kernel(slope_ref, q_ref, gl_ref, qg_ref, cmp_k_ref, cmp_v_ref, ksa_ref, vs_ref,
                 kwa_ref, vw_ref, ovl_ref, o_ref, score_sc, m_sc, l_sc, acc_sc, *, n_rep):
    g = pl.program_id(1)
    i = pl.program_id(2)
    tq = q_ref.shape[0]
    nc = cmp_k_ref.shape[3]
    n_slc = ovl_ref.shape[0]
    t0 = i * tq
    slopes = [slope_ref[g, r] for r in range(n_rep)]
    scale = HEAD_DIM ** -0.5

    qn = []
    for r in range(n_rep):
        x = q_ref[:, r * HEAD_DIM:(r + 1) * HEAD_DIM]
        qn.append((_rms(x) * qg_ref[...] * scale).astype(BF16))
    qs = jnp.concatenate(qn, axis=0)

    s_all = _dot_nt(qs, cmp_k_ref[0, 0, 0])
    row = lax.broadcasted_iota(jnp.int32, (tq, nc), 0) + t0
    col = lax.broadcasted_iota(jnp.int32, (tq, nc), 1)
    dist_c = (row - (col * CMP_STRIDE + (CMP_LEN - 1))).astype(F32)
    mask_c = dist_c >= 0.0
    vcmp = cmp_v_ref[0, 0, 0]
    o_cmp = []
    p_sum = jnp.zeros((tq, nc), F32)
    for r in range(n_rep):
        p = _softmax_rows(s_all[r * tq:(r + 1) * tq] - slopes[r] * dist_c, mask_c)
        o_cmp.append(_dot(p.astype(BF16), vcmp))
        p_sum = p_sum + p

    p_hi = p_sum.astype(BF16)
    p_lo = (p_sum - p_hi.astype(F32)).astype(BF16)
    ovl = ovl_ref[...]
    imp = _dot_nt(ovl, p_hi) + _dot_nt(ovl, p_lo)
    jb = lax.broadcasted_iota(jnp.int32, (n_slc, tq), 0)
    tt = lax.broadcasted_iota(jnp.int32, (n_slc, tq), 1) + t0
    cur = lax.shift_right_logical(tt, SLC_SHIFT)
    forced = (jb == 0) | (jb == cur) | (jb == cur - 1)
    score = jnp.where(jb * SLC_LEN <= tt, imp + jnp.where(forced, FORCE_BONUS, 0.0), NEG_INF)
    score_sc[...] = score
    rank = jnp.zeros((n_slc, tq), F32)
    for b2 in range(n_slc):
        sb = score_sc[b2:b2 + 1, :]
        beats = (sb > score) | ((sb == score) & (jb > b2))
        rank = rank + jnp.where(beats, 1.0, 0.0)
    n_sel = min(SLC_TOPK, n_slc)
    sel_bias_t = jnp.where(rank < n_sel, 0.0, NEG_INF)
    sel_bias = jnp.concatenate(
        [sel_bias_t, jnp.zeros((LANES - n_slc, tq), F32)], axis=0).T

    lane = lax.broadcasted_iota(jnp.int32, (tq, LANES), 1)
    q_slc, q_win = [], []
    for r in range(n_rep):
        pos_cols = jnp.where(lane == AUG_BLK_COL, slopes[r] * SLC_LEN,
                             jnp.where(lane == AUG_OFF_COL, slopes[r], 0.0))
        q_win.append(jnp.concatenate([qn[r], pos_cols.astype(BF16)], axis=1))
        q_slc.append(jnp.concatenate(
            [qn[r], jnp.where(lane < n_slc, sel_bias, pos_cols).astype(BF16)], axis=1))
    q_slc = jnp.concatenate(q_slc, axis=0)

    m_sc[...] = jnp.full(m_sc.shape, NEG_INF, F32)
    l_sc[...] = jnp.zeros(l_sc.shape, F32)
    acc_sc[...] = jnp.zeros(acc_sc.shape, F32)

    def slc_tile(kt, causal):
        start = pl.multiple_of(kt * tq, tq)
        s = _dot_nt(q_slc, ksa_ref[0, 0, pl.ds(start, tq), :])
        if causal:
            qpos = lax.broadcasted_iota(jnp.int32, s.shape, 0) & (tq - 1)
            kpos = lax.broadcasted_iota(jnp.int32, s.shape, 1)
            s = jnp.where(kpos <= qpos, s, NEG_INF)
        m_old = m_sc[...]
        m_new = jnp.maximum(m_old, jnp.max(s, axis=-1, keepdims=True))
        alpha = jnp.exp(m_old - m_new)
        p = jnp.exp(s - m_new)
        l_sc[...] = alpha * l_sc[...] + jnp.sum(p, axis=-1, keepdims=True)
        acc_sc[...] = alpha * acc_sc[...] + _dot(p.astype(BF16), vs_ref[0, 0, pl.ds(start, tq), :])
        m_sc[...] = m_new

    def slc_body(kt, carry):
        slc_tile(kt, False)
        return carry

    lax.fori_loop(0, i, slc_body, 0)
    slc_tile(i, True)
    o_slc_all = acc_sc[...] * (1.0 / l_sc[...])

    span = WIN + tq
    start_w = pl.multiple_of(jnp.maximum(t0 - WIN, 0), tq)
    kwin = kwa_ref[0, 0, pl.ds(start_w, span), :]
    vwin = vw_ref[0, 0, pl.ds(start_w, span), :]
    d_w = (lax.broadcasted_iota(jnp.int32, (tq, span), 0) + t0) - \
          (lax.broadcasted_iota(jnp.int32, (tq, span), 1) + start_w)
    mask_w = (d_w >= 0) & (d_w < WIN)

    gates = _sigmoid(gl_ref[...])
    for r in range(n_rep):
        p = _softmax_rows(_dot_nt(q_win[r], kwin), mask_w)
        o_win = _dot(p.astype(BF16), vwin)
        o = (gates[:, 3 * r:3 * r + 1] * o_cmp[r]
             + gates[:, 3 * r + 1:3 * r + 2] * o_slc_all[r * tq:(r + 1) * tq]
             + gates[:, 3 * r + 2:3 * r + 3] * o_win)
        o_ref[:, r * HEAD_DIM:(r + 1) * HEAD_DIM] = o.astype(o_ref.dtype)


def _attn_call(proj, q_gain, cmp_kv, ks_aug, vs, kw_aug, vw, slopes, ovl_t, B, T, q_col0, gate_col0):
    n_rep = slopes.shape[1]
    gw = n_rep * HEAD_DIM
    nq = T // ATT_TQ
    nc = cmp_kv.shape[3]
    n_slc = ovl_t.shape[0]
    qcb = q_col0 // gw
    gcb = gate_col0 // LANES
    full_kv = lambda width: pl.BlockSpec((1, 1, T, width), lambda b, g, i: (b, g, 0, 0))
    return pl.pallas_call(
        functools.partial(_attn_kernel, n_rep=n_rep),
        grid=(B, N_KV, nq),
        in_specs=[
            pl.BlockSpec(memory_space=pltpu.SMEM),
            pl.BlockSpec((ATT_TQ, gw), lambda b, g, i: (b * nq + i, qcb + g)),
            pl.BlockSpec((ATT_TQ, LANES), lambda b, g, i: (b * nq + i, gcb + g)),
            pl.BlockSpec((1, HEAD_DIM), lambda b, g, i: (0, 0)),
            pl.BlockSpec((1, 1, 1, nc, HEAD_DIM), lambda b, g, i: (b, 0, g, 0, 0)),
            pl.BlockSpec((1, 1, 1, nc, HEAD_DIM), lambda b, g, i: (b, 1, g, 0, 0)),
            full_kv(2 * HEAD_DIM), full_kv(HEAD_DIM), full_kv(2 * HEAD_DIM), full_kv(HEAD_DIM),
            pl.BlockSpec((n_slc, nc), lambda b, g, i: (0, 0)),
        ],
        out_specs=pl.BlockSpec((ATT_TQ, gw), lambda b, g, i: (b * nq + i, g)),
        out_shape=jax.ShapeDtypeStruct((B * T, N_KV * gw), BF16),
        scratch_shapes=[
            pltpu.VMEM((n_slc, ATT_TQ), F32),
            pltpu.VMEM((n_rep * ATT_TQ, 1), F32),
            pltpu.VMEM((n_rep * ATT_TQ, 1), F32),
            pltpu.VMEM((n_rep * ATT_TQ, HEAD_DIM), F32),
        ],
        compiler_params=_cparams("parallel", "parallel", "arbitrary"),
        name="nsa_attention",
    )(slopes, proj, proj, q_gain, cmp_kv, cmp_kv, ks_aug, vs, kw_aug, vw, ovl_t)


def _out_kernel(a_ref, o_ref, w_ref, x_ref, g_ref, y_ref):
    ka = a_ref.shape[1]
    acc = _dot(a_ref[...], w_ref[0:ka, :]) + _dot(o_ref[...], w_ref[ka:, :])
    y_ref[...] = x_ref[...] + g_ref[0] * acc


def _out_call(a, o, w, xf, gate, T):
    M, D = xf.shape
    ka, ko = a.shape[1], o.shape[1]
    per_b = T // OUT_BM
    return pl.pallas_call(
        _out_kernel,
        grid=(M // OUT_BM, D // OUT_BN),
        in_specs=[
            pl.BlockSpec((OUT_BM, ka), lambda i, n: (i, 0)),
            pl.BlockSpec((OUT_BM, ko), lambda i, n: (i, 0)),
            pl.BlockSpec((ka + ko, OUT_BN), lambda i, n: (0, n)),
            pl.BlockSpec((OUT_BM, OUT_BN), lambda i, n: (i, n)),
            pl.BlockSpec((1, 1, OUT_BN), lambda i, n: (i // per_b, 0, n)),
        ],
        out_specs=pl.BlockSpec((OUT_BM, OUT_BN), lambda i, n: (i, n)),
        out_shape=jax.ShapeDtypeStruct((M, D), F32),
        compiler_params=_cparams("parallel", "arbitrary"),
        name="out_proj",
    )(a, o, w, xf, gate)


def _ffn1_kernel(x_ref, ng_ref, sc_ref, sh_ref, wg_ref, wu_ref, o_ref, h_sc):
    @pl.when(pl.program_id(1) == 0)
    def _():
        h_sc[...] = _norm_mod(x_ref, ng_ref, sc_ref, sh_ref)

    h = h_sc[...]
    gate = _dot(h, wg_ref[...])
    up = _dot(h, wu_ref[...])
    o_ref[...] = (gate * _sigmoid(gate) * up).astype(o_ref.dtype)


def _ffn1_call(xf, ng, sc, sh, w_gu, T):
    M, D = xf.shape
    dff = w_gu.shape[1] // 2
    per_b = T // FFN1_BM
    nt = dff // FFN1_BN
    return pl.pallas_call(
        _ffn1_kernel,
        grid=(M // FFN1_BM, nt),
        in_specs=[
            pl.BlockSpec((FFN1_BM, D), lambda i, n: (i, 0)),
            pl.BlockSpec((1, D), lambda i, n: (0, 0)),
            pl.BlockSpec((1, 1, D), lambda i, n: (i // per_b, 0, 0)),
            pl.BlockSpec((1, 1, D), lambda i, n: (i // per_b, 0, 0)),
            pl.BlockSpec((D, FFN1_BN), lambda i, n: (0, n)),
            pl.BlockSpec((D, FFN1_BN), lambda i, n: (0, n + nt)),
        ],
        out_specs=pl.BlockSpec((FFN1_BM, FFN1_BN), lambda i, n: (i, n)),
        out_shape=jax.ShapeDtypeStruct((M, dff), BF16),
        scratch_shapes=[pltpu.VMEM((FFN1_BM, D), BF16)],
        compiler_params=_cparams("parallel", "arbitrary"),
        name="ffn_up",
    )(xf, ng, sc, sh, w_gu, w_gu)


def _ffn2_kernel(h_ref, w_ref, x_ref, g_ref, y_ref):
    y_ref[...] = x_ref[...] + g_ref[0] * _dot(h_ref[...], w_ref[...])


def _ffn2_call(h, w, xf, gate, T):
    M, D = xf.shape
    dff = h.shape[1]
    per_b = T // FFN2_BM
    return pl.pallas_call(
        _ffn2_kernel,
        grid=(M // FFN2_BM, D // FFN2_BN),
        in_specs=[
            pl.BlockSpec((FFN2_BM, dff), lambda i, n: (i, 0)),
            pl.BlockSpec((dff, FFN2_BN), lambda i, n: (0, n)),
            pl.BlockSpec((FFN2_BM, FFN2_BN), lambda i, n: (i, n)),
            pl.BlockSpec((1, 1, FFN2_BN), lambda i, n: (i // per_b, 0, n)),
        ],
        out_specs=pl.BlockSpec((FFN2_BM, FFN2_BN), lambda i, n: (i, n)),
        out_shape=jax.ShapeDtypeStruct((M, D), F32),
        compiler_params=_cparams("parallel", "arbitrary"),
        name="ffn_down",
    )(h, w, xf, gate)


def _alibi_slopes(n_heads):
    sl = 2.0 ** (-8.0 * np.arange(1, n_heads + 1) / n_heads)
    return jnp.asarray(sl, F32).reshape(N_KV, n_heads // N_KV)


def _overlap_t(T):
    nc = T // CMP_STRIDE
    n_slc = T // SLC_LEN
    cst = np.arange(nc) * CMP_STRIDE
    sst = np.arange(n_slc) * SLC_LEN
    ov = (cst[None, :] < sst[:, None] + SLC_LEN) & (cst[None, :] + CMP_LEN > sst[:, None])
    ov[:, (T - CMP_LEN) // CMP_STRIDE + 1:] = False
    return jnp.asarray(ov.astype(np.float32), BF16)


def kernel(x, c, w_ada, b_ada, norm_g, w_in, q_gain, k_gain, pe_cmp, w_cmp1, w_cmp2,
           w_pool, pool_scale, w_out, w_gate_up, w_down):
    B, T, D = x.shape
    L = w_ada.shape[0]
    pool_w = w_pool.shape[1] * w_pool.shape[2]
    kvw = N_KV * HEAD_DIM
    n_heads = (w_in.shape[2] - pool_w - 6 * kvw) // (HEAD_DIM + 3)
    att_w = n_heads * HEAD_DIM
    n_rep = n_heads // N_KV
    assert w_in.shape[2] == pool_w + att_w + 6 * kvw + 3 * n_heads
    assert T % ATT_TQ == 0 and T >= WIN + ATT_TQ and T % POOL_BT == 0 and T % KPREP_BT == 0
    assert T // SLC_LEN <= AUG_BLK_COL and pool_w % (n_rep * HEAD_DIM) == 0
    assert 1 << SLC_SHIFT == SLC_LEN and ATT_TQ & (ATT_TQ - 1) == 0
    q_col0 = pool_w
    kc_col0 = pool_w + att_w
    ks_col0 = kc_col0 + 2 * kvw
    gate_col0 = kc_col0 + 6 * kvw
    assert ks_col0 % kvw == 0 and gate_col0 % LANES == 0

    xf = x.reshape(B * T, D)
    rows = -(-B // 8) * 8
    c8 = jnp.pad(c, ((0, rows - B), (0, 0)))
    mod = _ada_call(c8, w_ada, b_ada)
    slopes = _alibi_slopes(n_heads)
    ovl_t = _overlap_t(T)

    per_group_gates = 3 * n_rep
    for l in range(L):
        sh1, sc1, g1, sh2, sc2, g2 = [mod[l, :B, k * D:(k + 1) * D].reshape(B, 1, D) for k in range(6)]
        gate_blocks = [jnp.pad(w_in[l][:, gate_col0 + g * per_group_gates:gate_col0 + (g + 1) * per_group_gates],
                               ((0, 0), (0, LANES - per_group_gates))) for g in range(N_KV)]
        w_in_p = jnp.concatenate([w_in[l][:, :gate_col0]] + gate_blocks, axis=1)
        pad_cols = -w_in_p.shape[1] % IN_BN
        w_in_p = jnp.pad(w_in_p, ((0, 0), (0, pad_cols))).astype(BF16)

        proj = _in_call(xf, norm_g[l, 0:1], sc1, sh1, w_in_p, T)
        a_out = _pool_call(proj, w_pool[l].astype(BF16), pool_scale[l].reshape(1, pool_w), T)
        ks_aug, vs, kw_aug, vw = _kprep_call(proj, k_gain[l], B, T, ks_col0)
        pe2 = pe_cmp[l].reshape(2, 2, (CMP_LEN // 2) * HEAD_DIM)
        cmp_kv = _cmp_call(proj, pe2, w_cmp1[l].astype(BF16), w_cmp2[l].astype(BF16), k_gain[l],
                           B, T, kc_col0)
        o_att = _attn_call(proj, q_gain[l].reshape(1, HEAD_DIM), cmp_kv, ks_aug, vs, kw_aug, vw,
                           slopes, ovl_t, B, T, q_col0, gate_col0)
        xf = _out_call(a_out, o_att, w_out[l].astype(BF16), xf, g1, T)
        hidden = _ffn1_call(xf, norm_g[l, 1:2], sc2, sh2, w_gate_up[l].astype(BF16), T)
        xf = _ffn2_call(hidden, w_down[l].astype(BF16), xf, g2, T)
    return xf.reshape(B, T, D)
```

```python
import functools

import numpy as np
import jax
import jax.numpy as jnp
from jax import lax
from jax.experimental import pallas as pl
from jax.experimental.pallas import tpu as pltpu

F32 = jnp.float32
BF16 = jnp.bfloat16

POOL_WINDOWS = (2, 4, 8, 16)
HEAD_DIM = 128
N_KV = 2
CMP_LEN = 32
CMP_STRIDE = 16
SLC_LEN = 64
SLC_SHIFT = 6
SLC_TOPK = 16
WIN = 512
NORM_EPS = 1e-6
NEG_INF = -1e30
FORCE_BONUS = 1e3

LANES = 128
POOL_HALO = 16
VMEM_LIMIT_BYTES = 56 * 1024 * 1024

ADA_BN = 1024
IN_BM, IN_BN = 512, 1280
KPREP_BT = 1024
POOL_BT = 1024
ATT_TQ = 256
OUT_BM, OUT_BN = 1024, 1024
FFN1_BM, FFN1_BN = 1024, 512
FFN2_BM, FFN2_BN = 1024, 512


def _cparams(*sem):
    return pltpu.CompilerParams(dimension_semantics=sem, vmem_limit_bytes=VMEM_LIMIT_BYTES)


def _dot(a, b):
    return jnp.dot(a, b, preferred_element_type=F32)


def _dot_nt(a, b):
    return lax.dot_general(a, b, (((1,), (1,)), ((), ())), preferred_element_type=F32)


def _rms(x):
    return x * lax.rsqrt(jnp.mean(x * x, axis=-1, keepdims=True) + NORM_EPS)


def _sigmoid(x):
    return 1.0 / (1.0 + jnp.exp(-x))


def _ada_kernel(c_ref, w_ref, b_ref, o_ref):
    c = c_ref[...]
    cs = c * _sigmoid(c)
    o_ref[0] = _dot(cs, w_ref[0]) + b_ref[0]


def _ada_call(c8, w_ada, b_ada):
    L, D, N = w_ada.shape
    rows = c8.shape[0]
    return pl.pallas_call(
        _ada_kernel,
        grid=(L, N // ADA_BN),
        in_specs=[
            pl.BlockSpec((rows, D), lambda l, n: (0, 0)),
            pl.BlockSpec((1, D, ADA_BN), lambda l, n: (l, 0, n)),
            pl.BlockSpec((1, 1, ADA_BN), lambda l, n: (l, 0, n)),
        ],
        out_specs=pl.BlockSpec((1, rows, ADA_BN), lambda l, n: (l, 0, n)),
        out_shape=jax.ShapeDtypeStruct((L, rows, N), F32),
        compiler_params=_cparams("parallel", "arbitrary"),
        name="ada_mod",
    )(c8, w_ada, b_ada.reshape(L, 1, N))


def _norm_mod(x_ref, ng_ref, sc_ref, sh_ref):
    y = _rms(x_ref[...]) * ng_ref[...]
    return (y * (1.0 + sc_ref[0]) + sh_ref[0]).astype(BF16)


def _in_kernel(x_ref, ng_ref, sc_ref, sh_ref, w_ref, o_ref, h_sc):
    @pl.when(pl.program_id(1) == 0)
    def _():
        h_sc[...] = _norm_mod(x_ref, ng_ref, sc_ref, sh_ref)

    o_ref[...] = _dot(h_sc[...], w_ref[...])


def _in_call(xf, ng, sc, sh, w, T):
    M, D = xf.shape
    N = w.shape[1]
    per_b = T // IN_BM
    return pl.pallas_call(
        _in_kernel,
        grid=(M // IN_BM, N // IN_BN),
        in_specs=[
            pl.BlockSpec((IN_BM, D), lambda i, n: (i, 0)),
            pl.BlockSpec((1, D), lambda i, n: (0, 0)),
            pl.BlockSpec((1, 1, D), lambda i, n: (i // per_b, 0, 0)),
            pl.BlockSpec((1, 1, D), lambda i, n: (i // per_b, 0, 0)),
            pl.BlockSpec((D, IN_BN), lambda i, n: (0, n)),
        ],
        out_specs=pl.BlockSpec((IN_BM, IN_BN), lambda i, n: (i, n)),
        out_shape=jax.ShapeDtypeStruct((M, N), F32),
        scratch_shapes=[pltpu.VMEM((IN_BM, D), BF16)],
        compiler_params=_cparams("parallel", "arbitrary"),
        name="in_proj",
    )(xf, ng, sc, sh, w)


def _pool_kernel(u_ref, halo_ref, w_ref, ps_ref, o_ref, ext_sc, *, tiles_per_batch):
    i = pl.program_id(0)
    bt = u_ref.shape[0]
    group = w_ref.shape[1]
    tile_in_batch = i % tiles_per_batch
    ext_sc[0:POOL_HALO, :] = jnp.where(tile_in_batch == 0, 0.0, halo_ref[...])
    ext_sc[POOL_HALO:, :] = u_ref[...]
    t = tile_in_batch * bt + lax.broadcasted_iota(jnp.int32, (bt, 1), 0)
    for gi, w in enumerate(POOL_WINDOWS):
        cols = slice(gi * group, (gi + 1) * group)
        acc = ext_sc[:, cols]
        k = 1
        while k < w:
            acc = acc + pltpu.roll(acc, k, 0)
            k *= 2
        cnt = jnp.minimum(t + 1, w).astype(F32)
        pooled = acc[POOL_HALO:, :] / cnt - u_ref[:, cols]
        mixed = _dot(pooled.astype(BF16), w_ref[gi])
        o_ref[:, cols] = (mixed * ps_ref[:, cols]).astype(o_ref.dtype)


def _pool_call(proj, w_pool, pool_scale, T):
    M = proj.shape[0]
    ng, group, _ = w_pool.shape
    width = ng * group
    tiles_per_batch = T // POOL_BT
    halo_blocks = POOL_BT // POOL_HALO
    return pl.pallas_call(
        functools.partial(_pool_kernel, tiles_per_batch=tiles_per_batch),
        grid=(M // POOL_BT,),
        in_specs=[
            pl.BlockSpec((POOL_BT, width), lambda i: (i, 0)),
            pl.BlockSpec((POOL_HALO, width), lambda i: (jnp.maximum(i * halo_blocks - 1, 0), 0)),
            pl.BlockSpec((ng, group, group), lambda i: (0, 0, 0)),
            pl.BlockSpec((1, width), lambda i: (0, 0)),
        ],
        out_specs=pl.BlockSpec((POOL_BT, width), lambda i: (i, 0)),
        out_shape=jax.ShapeDtypeStruct((M, width), BF16),
        scratch_shapes=[pltpu.VMEM((POOL_BT + POOL_HALO, width), F32)],
        compiler_params=_cparams("parallel"),
        name="pool_mixer",
    )(proj, proj, w_pool, pool_scale)


AUG_BLK_COL = SLC_LEN
AUG_OFF_COL = SLC_LEN + 1


def _key_extra(t):
    lane = lax.broadcasted_iota(jnp.int32, (t.shape[0], LANES), 1)
    blk = lax.shift_right_logical(t, SLC_SHIFT)
    off = t & (SLC_LEN - 1)
    extra = jnp.where(lane == blk, 1.0, 0.0)
    extra = jnp.where(lane == AUG_BLK_COL, blk.astype(F32), extra)
    extra = jnp.where(lane == AUG_OFF_COL, off.astype(F32), extra)
    return extra


def _kprep_kernel(ks_ref, vs_ref, kw_ref, vw_ref, kg_ref, ksa_ref, vso_ref, kwa_ref, vwo_ref,
                  *, tiles_per_batch):
    bt = ks_ref.shape[0]
    kt = vso_ref.shape[4]
    t = (pl.program_id(0) % tiles_per_batch) * bt + lax.broadcasted_iota(jnp.int32, (bt, 1), 0)
    extra = _key_extra(t).astype(BF16)
    for g in range(N_KV):
        cols = slice(g * HEAD_DIM, (g + 1) * HEAD_DIM)
        ksn = _rms(ks_ref[:, cols]) * kg_ref[1:2, :]
        kwn = _rms(kw_ref[:, cols]) * kg_ref[2:3, :]
        ksa_ref[0, g, :, 0:HEAD_DIM] = ksn.astype(BF16)
        ksa_ref[0, g, :, HEAD_DIM:] = extra
        kwa_ref[0, g, :, 0:HEAD_DIM] = kwn.astype(BF16)
        kwa_ref[0, g, :, HEAD_DIM:] = extra
        vs_t = vs_ref[:, cols].T.astype(BF16)
        vw_t = vw_ref[:, cols].T.astype(BF16)
        for j in range(bt // kt):
            vso_ref[0, g, j] = vs_t[:, j * kt:(j + 1) * kt]
            vwo_ref[0, g, j] = vw_t[:, j * kt:(j + 1) * kt]


def _kprep_call(proj, k_gain, B, T, col0):
    kvw = N_KV * HEAD_DIM
    cb = col0 // kvw
    tiles_per_batch = T // KPREP_BT
    vt_per_tile = KPREP_BT // ATT_TQ
    aug = jax.ShapeDtypeStruct((B, N_KV, T, 2 * HEAD_DIM), BF16)
    val = jax.ShapeDtypeStruct((B, N_KV, T // ATT_TQ, HEAD_DIM, ATT_TQ), BF16)
    in_spec = lambda j: pl.BlockSpec((KPREP_BT, kvw), lambda i: (i, cb + j))
    out_map = lambda i: (i // tiles_per_batch, 0, i % tiles_per_batch, 0)
    val_map = lambda i: (i // tiles_per_batch, 0, i % tiles_per_batch, 0, 0)
    return pl.pallas_call(
        functools.partial(_kprep_kernel, tiles_per_batch=tiles_per_batch),
        grid=(B * tiles_per_batch,),
        in_specs=[in_spec(0), in_spec(1), in_spec(2), in_spec(3),
                  pl.BlockSpec((3, HEAD_DIM), lambda i: (0, 0))],
        out_specs=[pl.BlockSpec((1, N_KV, KPREP_BT, 2 * HEAD_DIM), out_map),
                   pl.BlockSpec((1, N_KV, vt_per_tile, HEAD_DIM, ATT_TQ), val_map),
                   pl.BlockSpec((1, N_KV, KPREP_BT, 2 * HEAD_DIM), out_map),
                   pl.BlockSpec((1, N_KV, vt_per_tile, HEAD_DIM, ATT_TQ), val_map)],
        out_shape=[aug, val, aug, val],
        compiler_params=_cparams("parallel"),
        name="kv_prep",
    )(proj, proj, proj, proj, k_gain)


def _compress_one(src_ref, pe_ref, w1_ref, kv, nc):
    half = CMP_LEN // 2
    assert CMP_STRIDE == half
    xs = [src_ref[pl.ds(j, nc, stride=CMP_STRIDE), :] for j in range(half)]
    x = jnp.concatenate(xs, axis=1)
    kdim = half * HEAD_DIM
    lo = _dot((x + pe_ref[kv, 0:1, :]).astype(BF16), w1_ref[kv, 0:kdim, :])
    hi = _dot((x + pe_ref[kv, 1:2, :]).astype(BF16), w1_ref[kv, kdim:, :])
    pre = lo + pltpu.roll(hi, nc - 1, 0)
    return (pre * _sigmoid(pre)).astype(BF16)


def _cmp_kernel(k_ref, v_ref, pe_ref, w1_ref, w2k_ref, w2vt_ref, kg_ref, ko_ref, vo_ref):
    nc = ko_ref.shape[2]
    kc = _dot(_compress_one(k_ref, pe_ref, w1_ref, 0, nc), w2k_ref[...])
    ko_ref[0, 0] = (_rms(kc) * kg_ref[0:1, :]).astype(ko_ref.dtype)
    vct = _dot_nt(w2vt_ref[...], _compress_one(v_ref, pe_ref, w1_ref, 1, nc))
    vo_ref[0, 0] = vct.astype(vo_ref.dtype)


def _cmp_call(proj, pe2, w1, w2, k_gain, B, T, col0):
    nc = T // CMP_STRIDE
    cb = col0 // HEAD_DIM
    kdim = CMP_LEN * HEAD_DIM
    return pl.pallas_call(
        _cmp_kernel,
        grid=(B, N_KV),
        in_specs=[
            pl.BlockSpec((T, HEAD_DIM), lambda b, g: (b, cb + g)),
            pl.BlockSpec((T, HEAD_DIM), lambda b, g: (b, cb + N_KV + g)),
            pl.BlockSpec((2, 2, kdim // 2), lambda b, g: (0, 0, 0)),
            pl.BlockSpec((2, kdim, HEAD_DIM), lambda b, g: (0, 0, 0)),
            pl.BlockSpec((HEAD_DIM, HEAD_DIM), lambda b, g: (0, 0)),
            pl.BlockSpec((HEAD_DIM, HEAD_DIM), lambda b, g: (0, 0)),
            pl.BlockSpec((3, HEAD_DIM), lambda b, g: (0, 0)),
        ],
        out_specs=[pl.BlockSpec((1, 1, nc, HEAD_DIM), lambda b, g: (b, g, 0, 0)),
                   pl.BlockSpec((1, 1, HEAD_DIM, nc), lambda b, g: (b, g, 0, 0))],
        out_shape=[jax.ShapeDtypeStruct((B, N_KV, nc, HEAD_DIM), BF16),
                   jax.ShapeDtypeStruct((B, N_KV, HEAD_DIM, nc), BF16)],
        compiler_params=_cparams("parallel", "arbitrary"),
        name="compress",
    )(proj, proj, pe2, w1, w2[0], w2[1].T, k_gain)


def _softmax_cols(s, mask):
    s = jnp.where(mask, s, NEG_INF)
    m = jnp.max(s, axis=0, keepdims=True)
    p = jnp.where(mask, jnp.exp(s - m), 0.0)
    l = jnp.sum(p, axis=0, keepdims=True)
    return p * (1.0 / jnp.where(l > 0.0, l, 1.0))


def _attn_kernel(slope_ref, q_ref, gl_ref, qg_ref, cmp_k_ref, cmp_vt_ref, ksa_ref, vst_ref,
                 kwa_ref, vwt_ref, ovl_ref, o_ref, score_sc, m_sc, l_sc, acc_sc, *, n_rep):
    g = pl.program_id(1)
    i = pl.program_id(2)
    tq = q_ref.shape[0]
    nc = cmp_k_ref.shape[2]
    n_slc = ovl_ref.shape[0]
    t0 = i * tq
    slopes = [slope_ref[g, r] for r in range(n_rep)]
    scale = HEAD_DIM ** -0.5
    head = lambda a, r: a[:, r * tq:(r + 1) * tq]

    qt = []
    for r in range(n_rep):
        x = q_ref[:, r * HEAD_DIM:(r + 1) * HEAD_DIM]
        qt.append((_rms(x) * qg_ref[...] * scale).T.astype(BF16))
    qt_all = jnp.concatenate(qt, axis=1)

    s_all = _dot(cmp_k_ref[0, 0], qt_all)
    c_idx = lax.broadcasted_iota(jnp.int32, (nc, tq), 0)
    t_idx = lax.broadcasted_iota(jnp.int32, (nc, tq), 1) + t0
    dist_c = (t_idx - (c_idx * CMP_STRIDE + (CMP_LEN - 1))).astype(F32)
    mask_c = dist_c >= 0.0
    p_cmp = []
    p_sum = jnp.zeros((nc, tq), F32)
    for r in range(n_rep):
        p = _softmax_cols(head(s_all, r) - slopes[r] * dist_c, mask_c)
        p_cmp.append(p.astype(BF16))
        p_sum = p_sum + p
    o_cmp_t = _dot(cmp_vt_ref[0, 0], jnp.concatenate(p_cmp, axis=1))

    p_hi = p_sum.astype(BF16)
    p_lo = (p_sum - p_hi.astype(F32)).astype(BF16)
    ovl = ovl_ref[...]
    imp = _dot(ovl, p_hi) + _dot(ovl, p_lo)
    jb = lax.broadcasted_iota(jnp.int32, (n_slc, tq), 0)
    tt = lax.broadcasted_iota(jnp.int32, (n_slc, tq), 1) + t0
    cur = lax.shift_right_logical(tt, SLC_SHIFT)
    forced = (jb == 0) | (jb == cur) | (jb == cur - 1)
    score = jnp.where(jb * SLC_LEN <= tt, imp + jnp.where(forced, FORCE_BONUS, 0.0), NEG_INF)
    score_sc[...] = score
    rank = jnp.zeros((n_slc, tq), F32)
    for b2 in range(n_slc):
        sb = score_sc[b2:b2 + 1, :]
        beats = (sb > score) | ((sb == score) & (jb > b2))
        rank = rank + jnp.where(beats, 1.0, 0.0)
    n_sel = min(SLC_TOPK, n_slc)
    sel_bias_t = jnp.where(rank < n_sel, 0.0, NEG_INF)

    frow = lax.broadcasted_iota(jnp.int32, (LANES - AUG_BLK_COL, tq), 0)
    sel_rows = sel_bias_t.astype(BF16)
    if n_slc < AUG_BLK_COL:
        sel_rows = jnp.concatenate([sel_rows, jnp.zeros((AUG_BLK_COL - n_slc, tq), BF16)], axis=0)
    q_slc, q_win = [], []
    for r in range(n_rep):
        pos_rows = jnp.where(frow == 0, slopes[r] * SLC_LEN,
                             jnp.where(frow == AUG_OFF_COL - AUG_BLK_COL, slopes[r], 0.0)).astype(BF16)
        q_slc.append(jnp.concatenate([qt[r], sel_rows, pos_rows], axis=0))
        q_win.append(jnp.concatenate([qt[r], jnp.zeros_like(sel_rows), pos_rows], axis=0))
    q_slc = jnp.concatenate(q_slc, axis=1)
    q_win = jnp.concatenate(q_win, axis=1)

    m_sc[...] = jnp.full(m_sc.shape, NEG_INF, F32)
    l_sc[...] = jnp.zeros(l_sc.shape, F32)
    acc_sc[...] = jnp.zeros(acc_sc.shape, F32)

    def slc_tile(kt, causal):
        start = pl.multiple_of(kt * tq, tq)
        s = _dot(ksa_ref[0, 0, pl.ds(start, tq), :], q_slc)
        if causal:
            kpos = lax.broadcasted_iota(jnp.int32, s.shape, 0)
            qpos = lax.broadcasted_iota(jnp.int32, s.shape, 1) & (tq - 1)
            s = jnp.where(kpos <= qpos, s, NEG_INF)
        m_old = m_sc[...]
        m_new = jnp.maximum(m_old, jnp.max(s, axis=0, keepdims=True))
        alpha = jnp.exp(m_old - m_new)
        p = jnp.exp(s - m_new)
        l_sc[...] = alpha * l_sc[...] + jnp.sum(p, axis=0, keepdims=True)
        acc_sc[...] = alpha * acc_sc[...] + _dot(vst_ref[0, 0, kt], p.astype(BF16))
        m_sc[...] = m_new

    def slc_body(kt, carry):
        slc_tile(kt, False)
        return carry

    lax.fori_loop(0, i, slc_body, 0)
    slc_tile(i, True)
    o_slc_t = acc_sc[...] * (1.0 / l_sc[...])

    n_wt = WIN // tq + 1
    span = n_wt * tq
    j0 = jnp.maximum(i - (n_wt - 1), 0)
    start_w = pl.multiple_of(j0 * tq, tq)
    s_w = _dot(kwa_ref[0, 0, pl.ds(start_w, span), :], q_win)
    d_w = (lax.broadcasted_iota(jnp.int32, (span, tq), 1) + t0) - \
          (lax.broadcasted_iota(jnp.int32, (span, tq), 0) + start_w)
    mask_w = (d_w >= 0) & (d_w < WIN)
    p_w = jnp.concatenate([_softmax_cols(head(s_w, r), mask_w).astype(BF16) for r in range(n_rep)],
                          axis=1)
    o_win_t = _dot(vwt_ref[0, 0, j0], p_w[0:tq])
    for jj in range(1, n_wt):
        o_win_t = o_win_t + _dot(vwt_ref[0, 0, j0 + jj], p_w[jj * tq:(jj + 1) * tq])

    gates = _sigmoid(gl_ref[...].T)
    for r in range(n_rep):
        o_t = (gates[3 * r:3 * r + 1] * head(o_cmp_t, r)
               + gates[3 * r + 1:3 * r + 2] * head(o_slc_t, r)
               + gates[3 * r + 2:3 * r + 3] * head(o_win_t, r))
        o_ref[:, r * HEAD_DIM:(r + 1) * HEAD_DIM] = o_t.T.astype(o_ref.dtype)


def _attn_call(proj, q_gain, cmp_k, cmp_vt, ks_aug, vs_t, kw_aug, vw_t, slopes, ovl_t,
               B, T, q_col0, gate_col0):
    n_rep = slopes.shape[1]
    gw = n_rep * HEAD_DIM
    nq = T // ATT_TQ
    nc = cmp_k.shape[2]
    n_slc = ovl_t.shape[0]
    qcb = q_col0 // gw
    gcb = gate_col0 // LANES
    assert WIN % ATT_TQ == 0
    keys = pl.BlockSpec((1, 1, T, 2 * HEAD_DIM), lambda b, g, i: (b, g, 0, 0))
    vals = pl.BlockSpec((1, 1, nq, HEAD_DIM, ATT_TQ), lambda b, g, i: (b, g, 0, 0, 0))
    return pl.pallas_call(
        functools.partial(_attn_kernel, n_rep=n_rep),
        grid=(B, N_KV, nq),
        in_specs=[
            pl.BlockSpec(memory_space=pltpu.SMEM),
            pl.BlockSpec((ATT_TQ, gw), lambda b, g, i: (b * nq + i, qcb + g)),
            pl.BlockSpec((ATT_TQ, LANES), lambda b, g, i: (b * nq + i, gcb + g)),
            pl.BlockSpec((1, HEAD_DIM), lambda b, g, i: (0, 0)),
            pl.BlockSpec((1, 1, nc, HEAD_DIM), lambda b, g, i: (b, g, 0, 0)),
            pl.BlockSpec((1, 1, HEAD_DIM, nc), lambda b, g, i: (b, g, 0, 0)),
            keys, vals, keys, vals,
            pl.BlockSpec((n_slc, nc), lambda b, g, i: (0, 0)),
        ],
        out_specs=pl.BlockSpec((ATT_TQ, gw), lambda b, g, i: (b * nq + i, g)),
        out_shape=jax.ShapeDtypeStruct((B * T, N_KV * gw), BF16),
        scratch_shapes=[
            pltpu.VMEM((n_slc, ATT_TQ), F32),
            pltpu.VMEM((1, n_rep * ATT_TQ), F32),
            pltpu.VMEM((1, n_rep * ATT_TQ), F32),
            pltpu.VMEM((HEAD_DIM, n_rep * ATT_TQ), F32),
        ],
        compiler_params=_cparams("parallel", "parallel", "arbitrary"),
        name="nsa_attention",
    )(slopes, proj, proj, q_gain, cmp_k, cmp_vt, ks_aug, vs_t, kw_aug, vw_t, ovl_t)


def _out_kernel(a_ref, o_ref, w_ref, x_ref, g_ref, y_ref):
    ka = a_ref.shape[1]
    acc = _dot(a_ref[...], w_ref[0:ka, :]) + _dot(o_ref[...], w_ref[ka:, :])
    y_ref[...] = x_ref[...] + g_ref[0] * acc


def _out_call(a, o, w, xf, gate, T):
    M, D = xf.shape
    ka, ko = a.shape[1], o.shape[1]
    per_b = T // OUT_BM
    return pl.pallas_call(
        _out_kernel,
        grid=(M // OUT_BM, D // OUT_BN),
        in_specs=[
            pl.BlockSpec((OUT_BM, ka), lambda i, n: (i, 0)),
            pl.BlockSpec((OUT_BM, ko), lambda i, n: (i, 0)),
            pl.BlockSpec((ka + ko, OUT_BN), lambda i, n: (0, n)),
            pl.BlockSpec((OUT_BM, OUT_BN), lambda i, n: (i, n)),
            pl.BlockSpec((1, 1, OUT_BN), lambda i, n: (i // per_b, 0, n)),
        ],
        out_specs=pl.BlockSpec((OUT_BM, OUT_BN), lambda i, n: (i, n)),
        out_shape=jax.ShapeDtypeStruct((M, D), F32),
        compiler_params=_cparams("parallel", "arbitrary"),
        name="out_proj",
    )(a, o, w, xf, gate)


def _ffn1_kernel(x_ref, ng_ref, sc_ref, sh_ref, wg_ref, wu_ref, o_ref, h_sc):
    @pl.when(pl.program_id(1) == 0)
    def _():
        h_sc[...] = _norm_mod(x_ref, ng_ref, sc_ref, sh_ref)

    h = h_sc[...]
    gate = _dot(h, wg_ref[...])
    up = _dot(h, wu_ref[...])
    o_ref[...] = (gate * _sigmoid(gate) * up).astype(o_ref.dtype)


def _ffn1_call(xf, ng, sc, sh, w_gu, T):
    M, D = xf.shape
    dff = w_gu.shape[1] // 2
    per_b = T // FFN1_BM
    nt = dff // FFN1_BN
    return pl.pallas_call(
        _ffn1_kernel,
        grid=(M // FFN1_BM, nt),
        in_specs=[
            pl.BlockSpec((FFN1_BM, D), lambda i, n: (i, 0)),
            pl.BlockSpec((1, D), lambda i, n: (0, 0)),
            pl.BlockSpec((1, 1, D), lambda i, n: (i // per_b, 0, 0)),
            pl.BlockSpec((1, 1, D), lambda i, n: (i // per_b, 0, 0)),
            pl.BlockSpec((D, FFN1_BN), lambda i, n: (0, n)),
            pl.BlockSpec((D, FFN1_BN), lambda i, n: (0, n + nt)),
        ],
        out_specs=pl.BlockSpec((FFN1_BM, FFN1_BN), lambda i, n: (i, n)),
        out_shape=jax.ShapeDtypeStruct((M, dff), BF16),
        scratch_shapes=[pltpu.VMEM((FFN1_BM, D), BF16)],
        compiler_params=_cparams("parallel", "arbitrary"),
        name="ffn_up",
    )(xf, ng, sc, sh, w_gu, w_gu)


def _ffn2_kernel(h_ref, w_ref, x_ref, g_ref, y_ref):
    y_ref[...] = x_ref[...] + g_ref[0] * _dot(h_ref[...], w_ref[...])


def _ffn2_call(h, w, xf, gate, T):
    M, D = xf.shape
    dff = h.shape[1]
    per_b = T // FFN2_BM
    return pl.pallas_call(
        _ffn2_kernel,
        grid=(M // FFN2_BM, D // FFN2_BN),
        in_specs=[
            pl.BlockSpec((FFN2_BM, dff), lambda i, n: (i, 0)),
            pl.BlockSpec((dff, FFN2_BN), lambda i, n: (0, n)),
            pl.BlockSpec((FFN2_BM, FFN2_BN), lambda i, n: (i, n)),
            pl.BlockSpec((1, 1, FFN2_BN), lambda i, n: (i // per_b, 0, n)),
        ],
        out_specs=pl.BlockSpec((FFN2_BM, FFN2_BN), lambda i, n: (i, n)),
        out_shape=jax.ShapeDtypeStruct((M, D), F32),
        compiler_params=_cparams("parallel", "arbitrary"),
        name="ffn_down",
    )(h, w, xf, gate)


def _alibi_slopes(n_heads):
    sl = 2.0 ** (-8.0 * np.arange(1, n_heads + 1) / n_heads)
    return jnp.asarray(sl, F32).reshape(N_KV, n_heads // N_KV)


def _overlap_t(T):
    nc = T // CMP_STRIDE
    n_slc = T // SLC_LEN
    cst = np.arange(nc) * CMP_STRIDE
    sst = np.arange(n_slc) * SLC_LEN
    ov = (cst[None, :] < sst[:, None] + SLC_LEN) & (cst[None, :] + CMP_LEN > sst[:, None])
    ov[:, (T - CMP_LEN) // CMP_STRIDE + 1:] = False
    return jnp.asarray(ov.astype(np.float32), BF16)


def kernel(x, c, w_ada, b_ada, norm_g, w_in, q_gain, k_gain, pe_cmp, w_cmp1, w_cmp2,
           w_pool, pool_scale, w_out, w_gate_up, w_down):
    B, T, D = x.shape
    L = w_ada.shape[0]
    pool_w = w_pool.shape[1] * w_pool.shape[2]
    kvw = N_KV * HEAD_DIM
    n_heads = (w_in.shape[2] - pool_w - 6 * kvw) // (HEAD_DIM + 3)
    att_w = n_heads * HEAD_DIM
    n_rep = n_heads // N_KV
    assert w_in.shape[2] == pool_w + att_w + 6 * kvw + 3 * n_heads
    assert T % ATT_TQ == 0 and T >= WIN + ATT_TQ and T % POOL_BT == 0 and T % KPREP_BT == 0
    assert T // SLC_LEN <= AUG_BLK_COL and pool_w % (n_rep * HEAD_DIM) == 0
    assert 1 << SLC_SHIFT == SLC_LEN and ATT_TQ & (ATT_TQ - 1) == 0
    q_col0 = pool_w
    kc_col0 = pool_w + att_w
    ks_col0 = kc_col0 + 2 * kvw
    gate_col0 = kc_col0 + 6 * kvw
    assert ks_col0 % kvw == 0 and gate_col0 % LANES == 0

    xf = x.reshape(B * T, D)
    rows = -(-B // 8) * 8
    c8 = jnp.pad(c, ((0, rows - B), (0, 0)))
    mod = _ada_call(c8, w_ada, b_ada)
    slopes = _alibi_slopes(n_heads)
    ovl_t = _overlap_t(T)

    per_group_gates = 3 * n_rep
    for l in range(L):
        sh1, sc1, g1, sh2, sc2, g2 = [mod[l, :B, k * D:(k + 1) * D].reshape(B, 1, D) for k in range(6)]
        gate_blocks = [jnp.pad(w_in[l][:, gate_col0 + g * per_group_gates:gate_col0 + (g + 1) * per_group_gates],
                               ((0, 0), (0, LANES - per_group_gates))) for g in range(N_KV)]
        w_in_p = jnp.concatenate([w_in[l][:, :gate_col0]] + gate_blocks, axis=1)
        pad_cols = -w_in_p.shape[1] % IN_BN
        w_in_p = jnp.pad(w_in_p, ((0, 0), (0, pad_cols))).astype(BF16)

        proj = _in_call(xf, norm_g[l, 0:1], sc1, sh1, w_in_p, T)
        a_out = _pool_call(proj, w_pool[l].astype(BF16), pool_scale[l].reshape(1, pool_w), T)
        ks_aug, vs_t, kw_aug, vw_t = _kprep_call(proj, k_gain[l], B, T, ks_col0)
        pe2 = pe_cmp[l].reshape(2, 2, (CMP_LEN // 2) * HEAD_DIM)
        cmp_k, cmp_vt = _cmp_call(proj, pe2, w_cmp1[l].astype(BF16), w_cmp2[l].astype(BF16), k_gain[l],
                                  B, T, kc_col0)
        o_att = _attn_call(proj, q_gain[l].reshape(1, HEAD_DIM), cmp_k, cmp_vt, ks_aug, vs_t, kw_aug, vw_t,
                           slopes, ovl_t, B, T, q_col0, gate_col0)
        xf = _out_call(a_out, o_att, w_out[l].astype(BF16), xf, g1, T)
        hidden = _ffn1_call(xf, norm_g[l, 1:2], sc2, sh2, w_gate_up[l].astype(BF16), T)
        xf = _ffn2_call(hidden, w_down[l].astype(BF16), xf, g2, T)
    return xf.reshape(B, T, D)
```

```python
import functools

import numpy as np
import jax
import jax.numpy as jnp
from jax import lax
from jax.experimental import pallas as pl
from jax.experimental.pallas import tpu as pltpu

F32 = jnp.float32
BF16 = jnp.bfloat16

POOL_WINDOWS = (2, 4, 8, 16)
HEAD_DIM = 128
N_KV = 2
CMP_LEN = 32
CMP_STRIDE = 16
SLC_LEN = 64
SLC_SHIFT = 6
SLC_TOPK = 16
WIN = 512
NORM_EPS = 1e-6
NEG_INF = -1e30
FORCE_BONUS = 1e3

LANES = 128
POOL_HALO = 16
VMEM_LIMIT_BYTES = 56 * 1024 * 1024

ADA_BN = 1024
IN_BM, IN_BN = 1024, 1280
KPREP_BT = 1024
POOL_BT = 1024
ATT_TQ = 256
OUT_BM, OUT_BN = 1024, 1024
FFN1_BM, FFN1_BN = 1024, 512
FFN2_BM, FFN2_BN = 1024, 512


def _cparams(*sem):
    return pltpu.CompilerParams(dimension_semantics=sem, vmem_limit_bytes=VMEM_LIMIT_BYTES)


def _dot(a, b):
    return jnp.dot(a, b, preferred_element_type=F32)


def _dot_nt(a, b):
    return lax.dot_general(a, b, (((1,), (1,)), ((), ())), preferred_element_type=F32)


def _rms(x):
    return x * lax.rsqrt(jnp.mean(x * x, axis=-1, keepdims=True) + NORM_EPS)


def _sigmoid(x):
    return 1.0 / (1.0 + jnp.exp(-x))


def _ada_kernel(c_ref, w_ref, b_ref, o_ref):
    c = c_ref[...]
    cs = c * _sigmoid(c)
    o_ref[0] = _dot(cs, w_ref[0]) + b_ref[0]


def _ada_call(c8, w_ada, b_ada):
    L, D, N = w_ada.shape
    rows = c8.shape[0]
    return pl.pallas_call(
        _ada_kernel,
        grid=(L, N // ADA_BN),
        in_specs=[
            pl.BlockSpec((rows, D), lambda l, n: (0, 0)),
            pl.BlockSpec((1, D, ADA_BN), lambda l, n: (l, 0, n)),
            pl.BlockSpec((1, 1, ADA_BN), lambda l, n: (l, 0, n)),
        ],
        out_specs=pl.BlockSpec((1, rows, ADA_BN), lambda l, n: (l, 0, n)),
        out_shape=jax.ShapeDtypeStruct((L, rows, N), F32),
        compiler_params=_cparams("parallel", "arbitrary"),
        name="ada_mod",
    )(c8, w_ada, b_ada.reshape(L, 1, N))


def _norm_mod(x_ref, ng_ref, sc_ref, sh_ref):
    y = _rms(x_ref[...]) * ng_ref[...]
    return (y * (1.0 + sc_ref[0]) + sh_ref[0]).astype(BF16)


def _in_kernel(x_ref, ng_ref, sc_ref, sh_ref, w_ref, o_ref, h_sc):
    @pl.when(pl.program_id(1) == 0)
    def _():
        h_sc[...] = _norm_mod(x_ref, ng_ref, sc_ref, sh_ref)

    o_ref[...] = _dot(h_sc[...], w_ref[...])


def _in_call(xf, ng, sc, sh, w, T):
    M, D = xf.shape
    N = w.shape[1]
    per_b = T // IN_BM
    return pl.pallas_call(
        _in_kernel,
        grid=(M // IN_BM, N // IN_BN),
        in_specs=[
            pl.BlockSpec((IN_BM, D), lambda i, n: (i, 0)),
            pl.BlockSpec((1, D), lambda i, n: (0, 0)),
            pl.BlockSpec((1, 1, D), lambda i, n: (i // per_b, 0, 0)),
            pl.BlockSpec((1, 1, D), lambda i, n: (i // per_b, 0, 0)),
            pl.BlockSpec((D, IN_BN), lambda i, n: (0, n)),
        ],
        out_specs=pl.BlockSpec((IN_BM, IN_BN), lambda i, n: (i, n)),
        out_shape=jax.ShapeDtypeStruct((M, N), F32),
        scratch_shapes=[pltpu.VMEM((IN_BM, D), BF16)],
        compiler_params=_cparams("parallel", "arbitrary"),
        name="in_proj",
    )(xf, ng, sc, sh, w)


def _pool_kernel(u_ref, halo_ref, w_ref, ps_ref, o_ref, ext_sc, *, tiles_per_batch):
    i = pl.program_id(0)
    bt = u_ref.shape[0]
    group = w_ref.shape[1]
    tile_in_batch = i % tiles_per_batch
    ext_sc[0:POOL_HALO, :] = jnp.where(tile_in_batch == 0, 0.0, halo_ref[...])
    ext_sc[POOL_HALO:, :] = u_ref[...]
    t = tile_in_batch * bt + lax.broadcasted_iota(jnp.int32, (bt, 1), 0)
    for gi, w in enumerate(POOL_WINDOWS):
        cols = slice(gi * group, (gi + 1) * group)
        acc = ext_sc[:, cols]
        k = 1
        while k < w:
            acc = acc + pltpu.roll(acc, k, 0)
            k *= 2
        cnt = jnp.minimum(t + 1, w).astype(F32)
        pooled = acc[POOL_HALO:, :] / cnt - u_ref[:, cols]
        mixed = _dot(pooled.astype(BF16), w_ref[gi])
        o_ref[:, cols] = (mixed * ps_ref[:, cols]).astype(o_ref.dtype)


def _pool_call(proj, w_pool, pool_scale, T):
    M = proj.shape[0]
    ng, group, _ = w_pool.shape
    width = ng * group
    tiles_per_batch = T // POOL_BT
    halo_blocks = POOL_BT // POOL_HALO
    return pl.pallas_call(
        functools.partial(_pool_kernel, tiles_per_batch=tiles_per_batch),
        grid=(M // POOL_BT,),
        in_specs=[
            pl.BlockSpec((POOL_BT, width), lambda i: (i, 0)),
            pl.BlockSpec((POOL_HALO, width), lambda i: (jnp.maximum(i * halo_blocks - 1, 0), 0)),
            pl.BlockSpec((ng, group, group), lambda i: (0, 0, 0)),
            pl.BlockSpec((1, width), lambda i: (0, 0)),
        ],
        out_specs=pl.BlockSpec((POOL_BT, width), lambda i: (i, 0)),
        out_shape=jax.ShapeDtypeStruct((M, width), BF16),
        scratch_shapes=[pltpu.VMEM((POOL_BT + POOL_HALO, width), F32)],
        compiler_params=_cparams("parallel"),
        name="pool_mixer",
    )(proj, proj, w_pool, pool_scale)


AUG_SPLIT = 3
AUG_BLK_COL = SLC_LEN
AUG_OFF_COL = SLC_LEN + AUG_SPLIT
V_ROWS = HEAD_DIM + 16
LOG2E = 1.4426950408889634


def _key_extra(t):
    lane = lax.broadcasted_iota(jnp.int32, (t.shape[0], LANES), 1)
    blk = lax.shift_right_logical(t, SLC_SHIFT)
    off = t & (SLC_LEN - 1)
    extra = jnp.where(lane == blk, 1.0, 0.0)
    extra = jnp.where((lane >= AUG_BLK_COL) & (lane < AUG_OFF_COL), blk.astype(F32), extra)
    extra = jnp.where((lane >= AUG_OFF_COL) & (lane < AUG_OFF_COL + AUG_SPLIT), off.astype(F32), extra)
    return extra


def _value_tile_t(v_t):
    pad = lax.broadcasted_iota(jnp.int32, (V_ROWS - HEAD_DIM, v_t.shape[1]), 0)
    return jnp.concatenate([v_t, jnp.where(pad == 0, 1.0, 0.0)], axis=0).astype(BF16)


def _kprep_kernel(ks_ref, vs_ref, kw_ref, vw_ref, kg_ref, ksa_ref, vso_ref, kwa_ref, vwo_ref,
                  *, tiles_per_batch):
    bt = ks_ref.shape[0]
    kt = vso_ref.shape[4]
    t = (pl.program_id(0) % tiles_per_batch) * bt + lax.broadcasted_iota(jnp.int32, (bt, 1), 0)
    extra = _key_extra(t).astype(BF16)
    for g in range(N_KV):
        cols = slice(g * HEAD_DIM, (g + 1) * HEAD_DIM)
        ksn = _rms(ks_ref[:, cols]) * kg_ref[1:2, :]
        kwn = _rms(kw_ref[:, cols]) * kg_ref[2:3, :]
        ksa_ref[0, g, :, 0:HEAD_DIM] = ksn.astype(BF16)
        ksa_ref[0, g, :, HEAD_DIM:] = extra
        kwa_ref[0, g, :, 0:HEAD_DIM] = kwn.astype(BF16)
        kwa_ref[0, g, :, HEAD_DIM:] = extra
        vs_t = _value_tile_t(vs_ref[:, cols].T)
        vw_t = _value_tile_t(vw_ref[:, cols].T)
        for j in range(bt // kt):
            vso_ref[0, g, j] = vs_t[:, j * kt:(j + 1) * kt]
            vwo_ref[0, g, j] = vw_t[:, j * kt:(j + 1) * kt]


def _kprep_call(proj, k_gain, B, T, col0):
    kvw = N_KV * HEAD_DIM
    cb = col0 // kvw
    tiles_per_batch = T // KPREP_BT
    vt_per_tile = KPREP_BT // ATT_TQ
    aug = jax.ShapeDtypeStruct((B, N_KV, T, 2 * HEAD_DIM), BF16)
    val = jax.ShapeDtypeStruct((B, N_KV, T // ATT_TQ, V_ROWS, ATT_TQ), BF16)
    in_spec = lambda j: pl.BlockSpec((KPREP_BT, kvw), lambda i: (i, cb + j))
    out_map = lambda i: (i // tiles_per_batch, 0, i % tiles_per_batch, 0)
    val_map = lambda i: (i // tiles_per_batch, 0, i % tiles_per_batch, 0, 0)
    return pl.pallas_call(
        functools.partial(_kprep_kernel, tiles_per_batch=tiles_per_batch),
        grid=(B * tiles_per_batch,),
        in_specs=[in_spec(0), in_spec(1), in_spec(2), in_spec(3),
                  pl.BlockSpec((3, HEAD_DIM), lambda i: (0, 0))],
        out_specs=[pl.BlockSpec((1, N_KV, KPREP_BT, 2 * HEAD_DIM), out_map),
                   pl.BlockSpec((1, N_KV, vt_per_tile, V_ROWS, ATT_TQ), val_map),
                   pl.BlockSpec((1, N_KV, KPREP_BT, 2 * HEAD_DIM), out_map),
                   pl.BlockSpec((1, N_KV, vt_per_tile, V_ROWS, ATT_TQ), val_map)],
        out_shape=[aug, val, aug, val],
        compiler_params=_cparams("parallel"),
        name="kv_prep",
    )(proj, proj, proj, proj, k_gain)


def _compress_one(src_ref, pe_ref, w1_ref, kv, nc):
    half = CMP_LEN // 2
    assert CMP_STRIDE == half
    xs = [src_ref[pl.ds(j, nc, stride=CMP_STRIDE), :] for j in range(half)]
    x = jnp.concatenate(xs, axis=1)
    kdim = half * HEAD_DIM
    lo = _dot((x + pe_ref[kv, 0:1, :]).astype(BF16), w1_ref[kv, 0:kdim, :])
    hi = _dot((x + pe_ref[kv, 1:2, :]).astype(BF16), w1_ref[kv, kdim:, :])
    pre = lo + pltpu.roll(hi, nc - 1, 0)
    return (pre * _sigmoid(pre)).astype(BF16)


def _cmp_kernel(k_ref, v_ref, pe_ref, w1_ref, w2k_ref, w2vt_ref, kg_ref, ko_ref, vo_ref):
    nc = ko_ref.shape[2]
    kc = _dot(_compress_one(k_ref, pe_ref, w1_ref, 0, nc), w2k_ref[...])
    ko_ref[0, 0] = (_rms(kc) * kg_ref[0:1, :]).astype(ko_ref.dtype)
    vct = _dot_nt(w2vt_ref[...], _compress_one(v_ref, pe_ref, w1_ref, 1, nc))
    vo_ref[0, 0] = vct.astype(vo_ref.dtype)


def _cmp_call(proj, pe2, w1, w2, k_gain, B, T, col0):
    nc = T // CMP_STRIDE
    cb = col0 // HEAD_DIM
    kdim = CMP_LEN * HEAD_DIM
    return pl.pallas_call(
        _cmp_kernel,
        grid=(B, N_KV),
        in_specs=[
            pl.BlockSpec((T, HEAD_DIM), lambda b, g: (b, cb + g)),
            pl.BlockSpec((T, HEAD_DIM), lambda b, g: (b, cb + N_KV + g)),
            pl.BlockSpec((2, 2, kdim // 2), lambda b, g: (0, 0, 0)),
            pl.BlockSpec((2, kdim, HEAD_DIM), lambda b, g: (0, 0, 0)),
            pl.BlockSpec((HEAD_DIM, HEAD_DIM), lambda b, g: (0, 0)),
            pl.BlockSpec((HEAD_DIM, HEAD_DIM), lambda b, g: (0, 0)),
            pl.BlockSpec((3, HEAD_DIM), lambda b, g: (0, 0)),
        ],
        out_specs=[pl.BlockSpec((1, 1, nc, HEAD_DIM), lambda b, g: (b, g, 0, 0)),
                   pl.BlockSpec((1, 1, HEAD_DIM, nc), lambda b, g: (b, g, 0, 0))],
        out_shape=[jax.ShapeDtypeStruct((B, N_KV, nc, HEAD_DIM), BF16),
                   jax.ShapeDtypeStruct((B, N_KV, HEAD_DIM, nc), BF16)],
        compiler_params=_cparams("parallel", "arbitrary"),
        name="compress",
    )(proj, proj, pe2, w1, w2[0], w2[1].T, k_gain)


MAX_FLOOR = 0.1 * NEG_INF


def _exp2_cols(s, mask):
    s = jnp.where(mask, s, NEG_INF)
    m = jnp.maximum(jnp.max(s, axis=0, keepdims=True), MAX_FLOOR)
    return jnp.exp2(s - m)


def _split3(c, shape):
    c = jnp.full(shape, c, F32)
    c1 = c.astype(BF16).astype(F32)
    r1 = c - c1
    c2 = r1.astype(BF16).astype(F32)
    c3 = (r1 - c2).astype(BF16).astype(F32)
    return c1, c2, c3


def _attn_kernel(slope_ref, q_ref, gl_ref, qg_ref, cmp_k_ref, cmp_vt_ref, ksa_ref, vst_ref,
                 kwa_ref, vwt_ref, ovl_ref, o_ref, score_sc, qs_sc, sa_sc, sb_sc, m_sc, acc_sc,
                 *, n_rep):
    g = pl.program_id(1)
    i = pl.program_id(2)
    tq = q_ref.shape[0]
    nc = cmp_k_ref.shape[2]
    n_slc = ovl_ref.shape[0]
    t0 = i * tq
    slopes = [slope_ref[g, r] * LOG2E for r in range(n_rep)]
    scale = HEAD_DIM ** -0.5 * LOG2E
    head = lambda a, r: a[:, r * tq:(r + 1) * tq]

    qt = []
    for r in range(n_rep):
        x = q_ref[:, r * HEAD_DIM:(r + 1) * HEAD_DIM]
        qt.append((_rms(x) * qg_ref[...] * scale).T.astype(BF16))
    qt_all = jnp.concatenate(qt, axis=1)

    s_all = _dot(cmp_k_ref[0, 0], qt_all)
    c_idx = lax.broadcasted_iota(jnp.int32, (nc, tq), 0)
    t_idx = lax.broadcasted_iota(jnp.int32, (nc, tq), 1) + t0
    dist_c = (t_idx - (c_idx * CMP_STRIDE + (CMP_LEN - 1))).astype(F32)
    mask_c = dist_c >= 0.0
    p_cmp = []
    p_sum = jnp.zeros((nc, tq), F32)
    for r in range(n_rep):
        p = _exp2_cols(head(s_all, r) - slopes[r] * dist_c, mask_c)
        l = jnp.sum(p, axis=0, keepdims=True)
        p = p * (1.0 / jnp.where(l > 0.0, l, 1.0))
        p_cmp.append(p.astype(BF16))
        p_sum = p_sum + p
    o_cmp_t = _dot(cmp_vt_ref[0, 0], jnp.concatenate(p_cmp, axis=1))

    p_hi = p_sum.astype(BF16)
    p_lo = (p_sum - p_hi.astype(F32)).astype(BF16)
    ovl = ovl_ref[...]
    imp = _dot(ovl, p_hi) + _dot(ovl, p_lo)
    jb = lax.broadcasted_iota(jnp.int32, (n_slc, tq), 0)
    tt = lax.broadcasted_iota(jnp.int32, (n_slc, tq), 1) + t0
    cur = lax.shift_right_logical(tt, SLC_SHIFT)
    forced = (jb == 0) | (jb == cur) | (jb == cur - 1)
    score = jnp.where(jb * SLC_LEN <= tt, imp + jnp.where(forced, FORCE_BONUS, 0.0), NEG_INF)
    score_sc[...] = score
    sub = 8
    groups = [score[sub * rg:sub * (rg + 1)] for rg in range(n_slc // sub)]
    ranks = [jnp.zeros((sub, tq), F32) for _ in groups]
    jrow = lax.broadcasted_iota(jnp.int32, (sub, tq), 0)
    for b2 in range(n_slc):
        sb = jnp.broadcast_to(score_sc[b2:b2 + 1, :], (sub, tq))
        for rg, sg in enumerate(groups):
            if sub * rg > b2:
                beats = sb >= sg
            elif sub * rg + sub - 1 < b2:
                beats = sb > sg
            else:
                beats = (sb > sg) | ((sb == sg) & (jrow > b2 - sub * rg))
            ranks[rg] = ranks[rg] + jnp.where(beats, 1.0, 0.0)
    n_sel = min(SLC_TOPK, n_slc)
    sel_bias_t = jnp.concatenate([jnp.where(rk < n_sel, 0.0, NEG_INF) for rk in ranks], axis=0)

    pshape = (LANES - AUG_BLK_COL, tq)
    frow = lax.broadcasted_iota(jnp.int32, pshape, 0)
    sel_rows = sel_bias_t.astype(BF16)
    if n_slc < AUG_BLK_COL:
        sel_rows = jnp.concatenate([sel_rows, jnp.zeros((AUG_BLK_COL - n_slc, tq), BF16)], axis=0)
    q_slc, q_win = [], []
    for r in range(n_rep):
        pos_rows = jnp.zeros(pshape, F32)
        for k, ck in enumerate(_split3(slopes[r], pshape)):
            pos_rows = jnp.where(frow == k, ck * SLC_LEN, pos_rows)
            pos_rows = jnp.where(frow == AUG_SPLIT + k, ck, pos_rows)
        pos_rows = pos_rows.astype(BF16)
        q_slc.append(jnp.concatenate([qt[r], sel_rows, pos_rows], axis=0))
        q_win.append(jnp.concatenate([qt[r], jnp.zeros_like(sel_rows), pos_rows], axis=0))
    qs_sc[...] = jnp.concatenate(q_slc, axis=1)
    q_win = jnp.concatenate(q_win, axis=1)

    n_wt = WIN // tq + 1
    span = n_wt * tq
    j0 = jnp.maximum(i - (n_wt - 1), 0)
    start_w = pl.multiple_of(j0 * tq, tq)
    s_w = _dot(kwa_ref[0, 0, pl.ds(start_w, span), :], q_win)
    d_w = (lax.broadcasted_iota(jnp.int32, (span, tq), 1) + t0) - \
          (lax.broadcasted_iota(jnp.int32, (span, tq), 0) + start_w)
    mask_w = (d_w >= 0) & (d_w < WIN)
    p_w = jnp.concatenate([_exp2_cols(head(s_w, r), mask_w).astype(BF16) for r in range(n_rep)], axis=1)
    o_win_t = _dot(vwt_ref[0, 0, j0], p_w[0:tq])
    for jj in range(1, n_wt):
        o_win_t = o_win_t + _dot(vwt_ref[0, 0, j0 + jj], p_w[jj * tq:(jj + 1) * tq])
    o_win_t = o_win_t[0:HEAD_DIM] * (1.0 / o_win_t[HEAD_DIM:HEAD_DIM + 1])

    m_sc[...] = jnp.full(m_sc.shape, NEG_INF, F32)
    acc_sc[...] = jnp.zeros(acc_sc.shape, F32)

    def produce(kt, buf):
        start = pl.multiple_of(kt * tq, tq)
        buf[...] = _dot(ksa_ref[0, 0, pl.ds(start, tq), :], qs_sc[...])

    def consume(kt, buf, causal):
        s = buf[...]
        if causal:
            kpos = lax.broadcasted_iota(jnp.int32, s.shape, 0)
            qpos = lax.broadcasted_iota(jnp.int32, s.shape, 1) & (tq - 1)
            s = jnp.where(kpos <= qpos, s, NEG_INF)
        m_old = m_sc[...]
        m_new = jnp.maximum(m_old, jnp.max(s, axis=0, keepdims=True))
        alpha = jnp.exp2(m_old - m_new)
        p = jnp.exp2(s - m_new)
        acc_sc[...] = alpha * acc_sc[...] + _dot(vst_ref[0, 0, kt], p.astype(BF16))
        m_sc[...] = m_new

    def pair(j, carry):
        produce(2 * j + 1, sb_sc)
        consume(2 * j, sa_sc, False)
        produce(2 * j + 2, sa_sc)
        consume(2 * j + 1, sb_sc, False)
        return carry

    produce(0, sa_sc)
    lax.fori_loop(0, i // 2, pair, 0)

    @pl.when(i % 2 == 1)
    def _():
        produce(i, sb_sc)
        consume(i - 1, sa_sc, False)
        consume(i, sb_sc, True)

    @pl.when(i % 2 == 0)
    def _():
        consume(i, sa_sc, True)

    o_slc_t = acc_sc[0:HEAD_DIM, :] * (1.0 / acc_sc[HEAD_DIM:HEAD_DIM + 1, :])

    gates = _sigmoid(gl_ref[...].T)
    for r in range(n_rep):
        o_t = (gates[3 * r:3 * r + 1] * head(o_cmp_t, r)
               + gates[3 * r + 1:3 * r + 2] * head(o_slc_t, r)
               + gates[3 * r + 2:3 * r + 3] * head(o_win_t, r))
        o_ref[:, r * HEAD_DIM:(r + 1) * HEAD_DIM] = o_t.T.astype(o_ref.dtype)


def _attn_call(proj, q_gain, cmp_k, cmp_vt, ks_aug, vs_t, kw_aug, vw_t, slopes, ovl_t,
               B, T, q_col0, gate_col0):
    n_rep = slopes.shape[1]
    gw = n_rep * HEAD_DIM
    nq = T // ATT_TQ
    nc = cmp_k.shape[2]
    n_slc = ovl_t.shape[0]
    qcb = q_col0 // gw
    gcb = gate_col0 // LANES
    assert WIN % ATT_TQ == 0
    keys = pl.BlockSpec((1, 1, T, 2 * HEAD_DIM), lambda b, g, i: (b, g, 0, 0))
    vals = pl.BlockSpec((1, 1, nq, V_ROWS, ATT_TQ), lambda b, g, i: (b, g, 0, 0, 0))
    return pl.pallas_call(
        functools.partial(_attn_kernel, n_rep=n_rep),
        grid=(B, N_KV, nq),
        in_specs=[
            pl.BlockSpec(memory_space=pltpu.SMEM),
            pl.BlockSpec((ATT_TQ, gw), lambda b, g, i: (b * nq + i, qcb + g)),
            pl.BlockSpec((ATT_TQ, LANES), lambda b, g, i: (b * nq + i, gcb + g)),
            pl.BlockSpec((1, HEAD_DIM), lambda b, g, i: (0, 0)),
            pl.BlockSpec((1, 1, nc, HEAD_DIM), lambda b, g, i: (b, g, 0, 0)),
            pl.BlockSpec((1, 1, HEAD_DIM, nc), lambda b, g, i: (b, g, 0, 0)),
            keys, vals, keys, vals,
            pl.BlockSpec((n_slc, nc), lambda b, g, i: (0, 0)),
        ],
        out_specs=pl.BlockSpec((ATT_TQ, gw), lambda b, g, i: (b * nq + i, g)),
        out_shape=jax.ShapeDtypeStruct((B * T, N_KV * gw), BF16),
        scratch_shapes=[
            pltpu.VMEM((n_slc, ATT_TQ), F32),
            pltpu.VMEM((2 * HEAD_DIM, n_rep * ATT_TQ), BF16),
            pltpu.VMEM((ATT_TQ, n_rep * ATT_TQ), F32),
            pltpu.VMEM((ATT_TQ, n_rep * ATT_TQ), F32),
            pltpu.VMEM((1, n_rep * ATT_TQ), F32),
            pltpu.VMEM((V_ROWS, n_rep * ATT_TQ), F32),
        ],
        compiler_params=_cparams("parallel", "parallel", "arbitrary"),
        name="nsa_attention",
    )(slopes, proj, proj, q_gain, cmp_k, cmp_vt, ks_aug, vs_t, kw_aug, vw_t, ovl_t)


def _out_kernel(a_ref, o_ref, w_ref, x_ref, g_ref, y_ref):
    ka = a_ref.shape[1]
    acc = _dot(a_ref[...], w_ref[0:ka, :]) + _dot(o_ref[...], w_ref[ka:, :])
    y_ref[...] = x_ref[...] + g_ref[0] * acc


def _out_call(a, o, w, xf, gate, T):
    M, D = xf.shape
    ka, ko = a.shape[1], o.shape[1]
    per_b = T // OUT_BM
    return pl.pallas_call(
        _out_kernel,
        grid=(M // OUT_BM, D // OUT_BN),
        in_specs=[
            pl.BlockSpec((OUT_BM, ka), lambda i, n: (i, 0)),
            pl.BlockSpec((OUT_BM, ko), lambda i, n: (i, 0)),
            pl.BlockSpec((ka + ko, OUT_BN), lambda i, n: (0, n)),
            pl.BlockSpec((OUT_BM, OUT_BN), lambda i, n: (i, n)),
            pl.BlockSpec((1, 1, OUT_BN), lambda i, n: (i // per_b, 0, n)),
        ],
        out_specs=pl.BlockSpec((OUT_BM, OUT_BN), lambda i, n: (i, n)),
        out_shape=jax.ShapeDtypeStruct((M, D), F32),
        compiler_params=_cparams("parallel", "arbitrary"),
        name="out_proj",
    )(a, o, w, xf, gate)


def _ffn1_kernel(x_ref, ng_ref, sc_ref, sh_ref, wg_ref, wu_ref, o_ref, h_sc):
    @pl.when(pl.program_id(1) == 0)
    def _():
        h_sc[...] = _norm_mod(x_ref, ng_ref, sc_ref, sh_ref)

    h = h_sc[...]
    gate = _dot(h, wg_ref[...])
    up = _dot(h, wu_ref[...])
    o_ref[...] = (gate * _sigmoid(gate) * up).astype(o_ref.dtype)


def _ffn1_call(xf, ng, sc, sh, w_gu, T):
    M, D = xf.shape
    dff = w_gu.shape[1] // 2
    per_b = T // FFN1_BM
    nt = dff // FFN1_BN
    return pl.pallas_call(
        _ffn1_kernel,
        grid=(M // FFN1_BM, nt),
        in_specs=[
            pl.BlockSpec((FFN1_BM, D), lambda i, n: (i, 0)),
            pl.BlockSpec((1, D), lambda i, n: (0, 0)),
            pl.BlockSpec((1, 1, D), lambda i, n: (i // per_b, 0, 0)),
            pl.BlockSpec((1, 1, D), lambda i, n: (i // per_b, 0, 0)),
            pl.BlockSpec((D, FFN1_BN), lambda i, n: (0, n)),
            pl.BlockSpec((D, FFN1_BN), lambda i, n: (0, n + nt)),
        ],
        out_specs=pl.BlockSpec((FFN1_BM, FFN1_BN), lambda i, n: (i, n)),
        out_shape=jax.ShapeDtypeStruct((M, dff), BF16),
        scratch_shapes=[pltpu.VMEM((FFN1_BM, D), BF16)],
        compiler_params=_cparams("parallel", "arbitrary"),
        name="ffn_up",
    )(xf, ng, sc, sh, w_gu, w_gu)


def _ffn2_kernel(h_ref, w_ref, x_ref, g_ref, y_ref):
    y_ref[...] = x_ref[...] + g_ref[0] * _dot(h_ref[...], w_ref[...])


def _ffn2_call(h, w, xf, gate, T):
    M, D = xf.shape
    dff = h.shape[1]
    per_b = T // FFN2_BM
    return pl.pallas_call(
        _ffn2_kernel,
        grid=(M // FFN2_BM, D // FFN2_BN),
        in_specs=[
            pl.BlockSpec((FFN2_BM, dff), lambda i, n: (i, 0)),
            pl.BlockSpec((dff, FFN2_BN), lambda i, n: (0, n)),
            pl.BlockSpec((FFN2_BM, FFN2_BN), lambda i, n: (i, n)),
            pl.BlockSpec((1, 1, FFN2_BN), lambda i, n: (i // per_b, 0, n)),
        ],
        out_specs=pl.BlockSpec((FFN2_BM, FFN2_BN), lambda i, n: (i, n)),
        out_shape=jax.ShapeDtypeStruct((M, D), F32),
        compiler_params=_cparams("parallel", "arbitrary"),
        name="ffn_down",
    )(h, w, xf, gate)


def _alibi_slopes(n_heads):
    sl = 2.0 ** (-8.0 * np.arange(1, n_heads + 1) / n_heads)
    return jnp.asarray(sl, F32).reshape(N_KV, n_heads // N_KV)


def _overlap_t(T):
    nc = T // CMP_STRIDE
    n_slc = T // SLC_LEN
    cst = np.arange(nc) * CMP_STRIDE
    sst = np.arange(n_slc) * SLC_LEN
    ov = (cst[None, :] < sst[:, None] + SLC_LEN) & (cst[None, :] + CMP_LEN > sst[:, None])
    ov[:, (T - CMP_LEN) // CMP_STRIDE + 1:] = False
    return jnp.asarray(ov.astype(np.float32), BF16)


def kernel(x, c, w_ada, b_ada, norm_g, w_in, q_gain, k_gain, pe_cmp, w_cmp1, w_cmp2,
           w_pool, pool_scale, w_out, w_gate_up, w_down):
    B, T, D = x.shape
    L = w_ada.shape[0]
    pool_w = w_pool.shape[1] * w_pool.shape[2]
    kvw = N_KV * HEAD_DIM
    n_heads = (w_in.shape[2] - pool_w - 6 * kvw) // (HEAD_DIM + 3)
    att_w = n_heads * HEAD_DIM
    n_rep = n_heads // N_KV
    assert w_in.shape[2] == pool_w + att_w + 6 * kvw + 3 * n_heads
    assert T % ATT_TQ == 0 and T >= WIN + ATT_TQ and T % POOL_BT == 0 and T % KPREP_BT == 0
    assert T // SLC_LEN <= AUG_BLK_COL and pool_w % (n_rep * HEAD_DIM) == 0
    assert 1 << SLC_SHIFT == SLC_LEN and ATT_TQ & (ATT_TQ - 1) == 0
    q_col0 = pool_w
    kc_col0 = pool_w + att_w
    ks_col0 = kc_col0 + 2 * kvw
    gate_col0 = kc_col0 + 6 * kvw
    assert ks_col0 % kvw == 0 and gate_col0 % LANES == 0

    xf = x.reshape(B * T, D)
    rows = -(-B // 8) * 8
    c8 = jnp.pad(c, ((0, rows - B), (0, 0)))
    mod = _ada_call(c8, w_ada, b_ada)
    slopes = _alibi_slopes(n_heads)
    ovl_t = _overlap_t(T)

    per_group_gates = 3 * n_rep
    for l in range(L):
        sh1, sc1, g1, sh2, sc2, g2 = [mod[l, :B, k * D:(k + 1) * D].reshape(B, 1, D) for k in range(6)]
        gate_blocks = [jnp.pad(w_in[l][:, gate_col0 + g * per_group_gates:gate_col0 + (g + 1) * per_group_gates],
                               ((0, 0), (0, LANES - per_group_gates))) for g in range(N_KV)]
        w_in_p = jnp.concatenate([w_in[l][:, :gate_col0]] + gate_blocks, axis=1)
        pad_cols = -w_in_p.shape[1] % IN_BN
        w_in_p = jnp.pad(w_in_p, ((0, 0), (0, pad_cols))).astype(BF16)

        proj = _in_call(xf, norm_g[l, 0:1], sc1, sh1, w_in_p, T)
        a_out = _pool_call(proj, w_pool[l].astype(BF16), pool_scale[l].reshape(1, pool_w), T)
        ks_aug, vs_t, kw_aug, vw_t = _kprep_call(proj, k_gain[l], B, T, ks_col0)
        pe2 = pe_cmp[l].reshape(2, 2, (CMP_LEN // 2) * HEAD_DIM)
        cmp_k, cmp_vt = _cmp_call(proj, pe2, w_cmp1[l].astype(BF16), w_cmp2[l].astype(BF16), k_gain[l],
                                  B, T, kc_col0)
        o_att = _attn_call(proj, q_gain[l].reshape(1, HEAD_DIM), cmp_k, cmp_vt, ks_aug, vs_t, kw_aug, vw_t,
                           slopes, ovl_t, B, T, q_col0, gate_col0)
        xf = _out_call(a_out, o_att, w_out[l].astype(BF16), xf, g1, T)
        hidden = _ffn1_call(xf, norm_g[l, 1:2], sc2, sh2, w_gate_up[l].astype(BF16), T)
        xf = _ffn2_call(hidden, w_down[l].astype(BF16), xf, g2, T)
    return xf.reshape(B, T, D)
```

```python
import functools

import numpy as np
import jax
import jax.numpy as jnp
from jax import lax
from jax.experimental import pallas as pl
from jax.experimental.pallas import tpu as pltpu

F32 = jnp.float32
BF16 = jnp.bfloat16

POOL_WINDOWS = (2, 4, 8, 16)
HEAD_DIM = 128
N_KV = 2
CMP_LEN = 32
CMP_STRIDE = 16
SLC_LEN = 64
SLC_SHIFT = 6
SLC_TOPK = 16
WIN = 512
NORM_EPS = 1e-6
NEG_INF = -1e30
FORCE_BONUS = 1e3

LANES = 128
POOL_HALO = 16
VMEM_LIMIT_BYTES = 56 * 1024 * 1024

ADA_BN = 1024
IN_BM, IN_BN = 1024, 1280
KPREP_BT = 1024
POOL_BT = 1024
ATT_TQ = 256
OUT_BM, OUT_BN = 2048, 512
FFN1_BM, FFN1_BN = 1024, 512
FFN2_BM, FFN2_BN = 1024, 512


def _cparams(*sem):
    return pltpu.CompilerParams(dimension_semantics=sem, vmem_limit_bytes=VMEM_LIMIT_BYTES)


def _dot(a, b):
    return jnp.dot(a, b, preferred_element_type=F32)


def _dot_nt(a, b):
    return lax.dot_general(a, b, (((1,), (1,)), ((), ())), preferred_element_type=F32)


def _rms(x):
    return x * lax.rsqrt(jnp.mean(x * x, axis=-1, keepdims=True) + NORM_EPS)


def _sigmoid(x):
    return 1.0 / (1.0 + jnp.exp(-x))


def _ada_kernel(c_ref, w_ref, b_ref, o_ref):
    c = c_ref[...]
    cs = c * _sigmoid(c)
    o_ref[0] = _dot(cs, w_ref[0]) + b_ref[0]


def _ada_call(c8, w_ada, b_ada):
    L, D, N = w_ada.shape
    rows = c8.shape[0]
    return pl.pallas_call(
        _ada_kernel,
        grid=(L, N // ADA_BN),
        in_specs=[
            pl.BlockSpec((rows, D), lambda l, n: (0, 0)),
            pl.BlockSpec((1, D, ADA_BN), lambda l, n: (l, 0, n)),
            pl.BlockSpec((1, 1, ADA_BN), lambda l, n: (l, 0, n)),
        ],
        out_specs=pl.BlockSpec((1, rows, ADA_BN), lambda l, n: (l, 0, n)),
        out_shape=jax.ShapeDtypeStruct((L, rows, N), F32),
        compiler_params=_cparams("parallel", "arbitrary"),
        name="ada_mod",
    )(c8, w_ada, b_ada.reshape(L, 1, N))


def _norm_mod(x, ng_ref, sc_ref, sh_ref):
    y = _rms(x) * ng_ref[...]
    return (y * (1.0 + sc_ref[0]) + sh_ref[0]).astype(BF16)


def _lookahead_row_tile(n_row_tiles):
    def idx(i, n):
        return jnp.where((i == 0) & (n == 0), 0, jnp.minimum(i + 1, n_row_tiles - 1))
    return idx


def _norm_chunks(n_col_steps):
    assert n_col_steps >= 2
    return 1 << ((n_col_steps - 1).bit_length() - 1)


def _norm_matmul_steps(x_ref, ng_ref, sc_ref, sh_ref, h_sc, emit, n_chunks):
    i, n = pl.program_id(0), pl.program_id(1)
    rows = x_ref.shape[0] // n_chunks
    slab_step = (n >= 1) & (n <= n_chunks)

    @pl.when((i == 0) & (n == 0))
    def _():
        h_sc[0] = _norm_mod(x_ref[...], ng_ref, sc_ref, sh_ref)

    @pl.when(jnp.logical_not(slab_step))
    def _():
        emit(h_sc[i % 2])

    for slot in (0, 1):
        @pl.when(slab_step & (i % 2 == slot))
        def _(slot=slot):
            emit(h_sc[slot])
            slab = pl.ds(pl.multiple_of((n - 1) * rows, rows), rows)
            h_sc[1 - slot, slab, :] = _norm_mod(x_ref[slab, :], ng_ref, sc_ref, sh_ref)


def _in_kernel(x_ref, ng_ref, sc_ref, sh_ref, w_ref, o_ref, h_sc, *, n_chunks):
    def emit(h):
        o_ref[...] = _dot(h, w_ref[...])

    _norm_matmul_steps(x_ref, ng_ref, sc_ref, sh_ref, h_sc, emit, n_chunks)


def _in_call(xf, ng, sc, sh, w, T):
    M, D = xf.shape
    N = w.shape[1]
    per_b = T // IN_BM
    nm = M // IN_BM
    n_chunks = _norm_chunks(N // IN_BN)
    row = _lookahead_row_tile(nm)
    return pl.pallas_call(
        functools.partial(_in_kernel, n_chunks=n_chunks),
        grid=(nm, N // IN_BN),
        in_specs=[
            pl.BlockSpec((IN_BM, D), lambda i, n: (row(i, n), 0)),
            pl.BlockSpec((1, D), lambda i, n: (0, 0)),
            pl.BlockSpec((1, 1, D), lambda i, n: (row(i, n) // per_b, 0, 0)),
            pl.BlockSpec((1, 1, D), lambda i, n: (row(i, n) // per_b, 0, 0)),
            pl.BlockSpec((D, IN_BN), lambda i, n: (0, n)),
        ],
        out_specs=pl.BlockSpec((IN_BM, IN_BN), lambda i, n: (i, n)),
        out_shape=jax.ShapeDtypeStruct((M, N), F32),
        scratch_shapes=[pltpu.VMEM((2, IN_BM, D), BF16)],
        compiler_params=_cparams("arbitrary", "arbitrary"),
        name="in_proj",
    )(xf, ng, sc, sh, w)


def _pool_kernel(u_ref, halo_ref, w_ref, ps_ref, o_ref, ext_sc, *, tiles_per_batch):
    i = pl.program_id(0)
    bt = u_ref.shape[0]
    group = w_ref.shape[1]
    tile_in_batch = i % tiles_per_batch
    ext_sc[0:POOL_HALO, :] = jnp.where(tile_in_batch == 0, 0.0, halo_ref[...])
    ext_sc[POOL_HALO:, :] = u_ref[...]
    t = tile_in_batch * bt + lax.broadcasted_iota(jnp.int32, (bt, 1), 0)
    for gi, w in enumerate(POOL_WINDOWS):
        cols = slice(gi * group, (gi + 1) * group)
        acc = ext_sc[:, cols]
        k = 1
        while k < w:
            acc = acc + pltpu.roll(acc, k, 0)
            k *= 2
        cnt = jnp.minimum(t + 1, w).astype(F32)
        pooled = acc[POOL_HALO:, :] / cnt - u_ref[:, cols]
        mixed = _dot(pooled.astype(BF16), w_ref[gi])
        o_ref[:, cols] = (mixed * ps_ref[:, cols]).astype(o_ref.dtype)


def _pool_call(proj, w_pool, pool_scale, T):
    M = proj.shape[0]
    ng, group, _ = w_pool.shape
    width = ng * group
    tiles_per_batch = T // POOL_BT
    halo_blocks = POOL_BT // POOL_HALO
    return pl.pallas_call(
        functools.partial(_pool_kernel, tiles_per_batch=tiles_per_batch),
        grid=(M // POOL_BT,),
        in_specs=[
            pl.BlockSpec((POOL_BT, width), lambda i: (i, 0)),
            pl.BlockSpec((POOL_HALO, width), lambda i: (jnp.maximum(i * halo_blocks - 1, 0), 0)),
            pl.BlockSpec((ng, group, group), lambda i: (0, 0, 0)),
            pl.BlockSpec((1, width), lambda i: (0, 0)),
        ],
        out_specs=pl.BlockSpec((POOL_BT, width), lambda i: (i, 0)),
        out_shape=jax.ShapeDtypeStruct((M, width), BF16),
        scratch_shapes=[pltpu.VMEM((POOL_BT + POOL_HALO, width), F32)],
        compiler_params=_cparams("parallel"),
        name="pool_mixer",
    )(proj, proj, w_pool, pool_scale)


AUG_SPLIT = 3
AUG_BLK_COL = SLC_LEN
AUG_OFF_COL = SLC_LEN + AUG_SPLIT
V_ROWS = HEAD_DIM + 16
LOG2E = 1.4426950408889634


def _key_extra(t):
    lane = lax.broadcasted_iota(jnp.int32, (t.shape[0], LANES), 1)
    blk = lax.shift_right_logical(t, SLC_SHIFT)
    off = t & (SLC_LEN - 1)
    extra = jnp.where(lane == blk, 1.0, 0.0)
    extra = jnp.where((lane >= AUG_BLK_COL) & (lane < AUG_OFF_COL), blk.astype(F32), extra)
    extra = jnp.where((lane >= AUG_OFF_COL) & (lane < AUG_OFF_COL + AUG_SPLIT), off.astype(F32), extra)
    return extra


def _value_tile_t(v_t):
    pad = lax.broadcasted_iota(jnp.int32, (V_ROWS - HEAD_DIM, v_t.shape[1]), 0)
    return jnp.concatenate([v_t, jnp.where(pad == 0, 1.0, 0.0)], axis=0).astype(BF16)


def _kprep_kernel(ks_ref, vs_ref, kw_ref, vw_ref, kg_ref, ksa_ref, vso_ref, kwa_ref, vwo_ref,
                  *, tiles_per_batch):
    bt = ks_ref.shape[0]
    kt = vso_ref.shape[4]
    t = (pl.program_id(0) % tiles_per_batch) * bt + lax.broadcasted_iota(jnp.int32, (bt, 1), 0)
    extra = _key_extra(t).astype(BF16)
    for g in range(N_KV):
        cols = slice(g * HEAD_DIM, (g + 1) * HEAD_DIM)
        ksn = _rms(ks_ref[:, cols]) * kg_ref[1:2, :]
        kwn = _rms(kw_ref[:, cols]) * kg_ref[2:3, :]
        ksa_ref[0, g, :, 0:HEAD_DIM] = ksn.astype(BF16)
        ksa_ref[0, g, :, HEAD_DIM:] = extra
        kwa_ref[0, g, :, 0:HEAD_DIM] = kwn.astype(BF16)
        kwa_ref[0, g, :, HEAD_DIM:] = extra
        vs_t = _value_tile_t(vs_ref[:, cols].T)
        vw_t = _value_tile_t(vw_ref[:, cols].T)
        for j in range(bt // kt):
            vso_ref[0, g, j] = vs_t[:, j * kt:(j + 1) * kt]
            vwo_ref[0, g, j] = vw_t[:, j * kt:(j + 1) * kt]


def _kprep_call(proj, k_gain, B, T, col0):
    kvw = N_KV * HEAD_DIM
    cb = col0 // kvw
    tiles_per_batch = T // KPREP_BT
    vt_per_tile = KPREP_BT // ATT_TQ
    aug = jax.ShapeDtypeStruct((B, N_KV, T, 2 * HEAD_DIM), BF16)
    val = jax.ShapeDtypeStruct((B, N_KV, T // ATT_TQ, V_ROWS, ATT_TQ), BF16)
    in_spec = lambda j: pl.BlockSpec((KPREP_BT, kvw), lambda i: (i, cb + j))
    out_map = lambda i: (i // tiles_per_batch, 0, i % tiles_per_batch, 0)
    val_map = lambda i: (i // tiles_per_batch, 0, i % tiles_per_batch, 0, 0)
    return pl.pallas_call(
        functools.partial(_kprep_kernel, tiles_per_batch=tiles_per_batch),
        grid=(B * tiles_per_batch,),
        in_specs=[in_spec(0), in_spec(1), in_spec(2), in_spec(3),
                  pl.BlockSpec((3, HEAD_DIM), lambda i: (0, 0))],
        out_specs=[pl.BlockSpec((1, N_KV, KPREP_BT, 2 * HEAD_DIM), out_map),
                   pl.BlockSpec((1, N_KV, vt_per_tile, V_ROWS, ATT_TQ), val_map),
                   pl.BlockSpec((1, N_KV, KPREP_BT, 2 * HEAD_DIM), out_map),
                   pl.BlockSpec((1, N_KV, vt_per_tile, V_ROWS, ATT_TQ), val_map)],
        out_shape=[aug, val, aug, val],
        compiler_params=_cparams("parallel"),
        name="kv_prep",
    )(proj, proj, proj, proj, k_gain)


def _compress_one(src_ref, pe_ref, w1_ref, kv, nc):
    half = CMP_LEN // 2
    assert CMP_STRIDE == half
    xs = [src_ref[pl.ds(j, nc, stride=CMP_STRIDE), :] for j in range(half)]
    x = jnp.concatenate(xs, axis=1)
    kdim = half * HEAD_DIM
    lo = _dot((x + pe_ref[kv, 0:1, :]).astype(BF16), w1_ref[kv, 0:kdim, :])
    hi = _dot((x + pe_ref[kv, 1:2, :]).astype(BF16), w1_ref[kv, kdim:, :])
    pre = lo + pltpu.roll(hi, nc - 1, 0)
    return (pre * _sigmoid(pre)).astype(BF16)


def _cmp_kernel(k_ref, v_ref, pe_ref, w1_ref, w2k_ref, w2vt_ref, kg_ref, ko_ref, vo_ref):
    nc = ko_ref.shape[2]
    kc = _dot(_compress_one(k_ref, pe_ref, w1_ref, 0, nc), w2k_ref[...])
    ko_ref[0, 0] = (_rms(kc) * kg_ref[0:1, :]).astype(ko_ref.dtype)
    vct = _dot_nt(w2vt_ref[...], _compress_one(v_ref, pe_ref, w1_ref, 1, nc))
    vo_ref[0, 0] = vct.astype(vo_ref.dtype)


def _cmp_call(proj, pe2, w1, w2, k_gain, B, T, col0):
    nc = T // CMP_STRIDE
    cb = col0 // HEAD_DIM
    kdim = CMP_LEN * HEAD_DIM
    return pl.pallas_call(
        _cmp_kernel,
        grid=(B, N_KV),
        in_specs=[
            pl.BlockSpec((T, HEAD_DIM), lambda b, g: (b, cb + g)),
            pl.BlockSpec((T, HEAD_DIM), lambda b, g: (b, cb + N_KV + g)),
            pl.BlockSpec((2, 2, kdim // 2), lambda b, g: (0, 0, 0)),
            pl.BlockSpec((2, kdim, HEAD_DIM), lambda b, g: (0, 0, 0)),
            pl.BlockSpec((HEAD_DIM, HEAD_DIM), lambda b, g: (0, 0)),
            pl.BlockSpec((HEAD_DIM, HEAD_DIM), lambda b, g: (0, 0)),
            pl.BlockSpec((3, HEAD_DIM), lambda b, g: (0, 0)),
        ],
        out_specs=[pl.BlockSpec((1, 1, nc, HEAD_DIM), lambda b, g: (b, g, 0, 0)),
                   pl.BlockSpec((1, 1, HEAD_DIM, nc), lambda b, g: (b, g, 0, 0))],
        out_shape=[jax.ShapeDtypeStruct((B, N_KV, nc, HEAD_DIM), BF16),
                   jax.ShapeDtypeStruct((B, N_KV, HEAD_DIM, nc), BF16)],
        compiler_params=_cparams("parallel", "arbitrary"),
        name="compress",
    )(proj, proj, pe2, w1, w2[0], w2[1].T, k_gain)


MAX_FLOOR = 0.1 * NEG_INF


def _exp2_cols(s, mask):
    s = jnp.where(mask, s, NEG_INF)
    m = jnp.maximum(jnp.max(s, axis=0, keepdims=True), MAX_FLOOR)
    return jnp.exp2(s - m)


def _split3(c, shape):
    c = jnp.full(shape, c, F32)
    c1 = c.astype(BF16).astype(F32)
    r1 = c - c1
    c2 = r1.astype(BF16).astype(F32)
    c3 = (r1 - c2).astype(BF16).astype(F32)
    return c1, c2, c3


def _attn_kernel(slope_ref, q_ref, gl_ref, qg_ref, cmp_k_ref, cmp_vt_ref, ksa_ref, vst_ref,
                 kwa_ref, vwt_ref, ovl_ref, o_ref, score_sc, qs_sc, sa_sc, sb_sc, m_sc, acc_sc,
                 *, n_rep):
    g = pl.program_id(1)
    i = pl.program_id(2)
    tq = q_ref.shape[0]
    nc = cmp_k_ref.shape[2]
    n_slc = ovl_ref.shape[0]
    t0 = i * tq
    slopes = [slope_ref[g, r] * LOG2E for r in range(n_rep)]
    scale = HEAD_DIM ** -0.5 * LOG2E
    head = lambda a, r: a[:, r * tq:(r + 1) * tq]

    qt = []
    for r in range(n_rep):
        x = q_ref[:, r * HEAD_DIM:(r + 1) * HEAD_DIM]
        qt.append((_rms(x) * qg_ref[...] * scale).T.astype(BF16))
    qt_all = jnp.concatenate(qt, axis=1)

    s_all = _dot(cmp_k_ref[0, 0], qt_all)
    c_idx = lax.broadcasted_iota(jnp.int32, (nc, tq), 0)
    t_idx = lax.broadcasted_iota(jnp.int32, (nc, tq), 1) + t0
    dist_c = (t_idx - (c_idx * CMP_STRIDE + (CMP_LEN - 1))).astype(F32)
    mask_c = dist_c >= 0.0
    p_cmp = []
    p_sum = jnp.zeros((nc, tq), F32)
    for r in range(n_rep):
        p = _exp2_cols(head(s_all, r) - slopes[r] * dist_c, mask_c)
        l = jnp.sum(p, axis=0, keepdims=True)
        p = p * (1.0 / jnp.where(l > 0.0, l, 1.0))
        p_cmp.append(p.astype(BF16))
        p_sum = p_sum + p
    o_cmp_t = _dot(cmp_vt_ref[0, 0], jnp.concatenate(p_cmp, axis=1))

    p_hi = p_sum.astype(BF16)
    p_lo = (p_sum - p_hi.astype(F32)).astype(BF16)
    ovl = ovl_ref[...]
    imp = _dot(ovl, p_hi) + _dot(ovl, p_lo)
    jb = lax.broadcasted_iota(jnp.int32, (n_slc, tq), 0)
    tt = lax.broadcasted_iota(jnp.int32, (n_slc, tq), 1) + t0
    cur = lax.shift_right_logical(tt, SLC_SHIFT)
    forced = (jb == 0) | (jb == cur) | (jb == cur - 1)
    score = jnp.where(jb * SLC_LEN <= tt, imp + jnp.where(forced, FORCE_BONUS, 0.0), NEG_INF)
    score_sc[...] = score
    sub = 8
    groups = [score[sub * rg:sub * (rg + 1)] for rg in range(n_slc // sub)]
    ranks = [jnp.zeros((sub, tq), F32) for _ in groups]
    jrow = lax.broadcasted_iota(jnp.int32, (sub, tq), 0)
    for b2 in range(n_slc):
        sb = jnp.broadcast_to(score_sc[b2:b2 + 1, :], (sub, tq))
        for rg, sg in enumerate(groups):
            if sub * rg > b2:
                beats = sb >= sg
            elif sub * rg + sub - 1 < b2:
                beats = sb > sg
            else:
                beats = (sb > sg) | ((sb == sg) & (jrow > b2 - sub * rg))
            ranks[rg] = ranks[rg] + jnp.where(beats, 1.0, 0.0)
    n_sel = min(SLC_TOPK, n_slc)
    sel_bias_t = jnp.concatenate([jnp.where(rk < n_sel, 0.0, NEG_INF) for rk in ranks], axis=0)

    pshape = (LANES - AUG_BLK_COL, tq)
    frow = lax.broadcasted_iota(jnp.int32, pshape, 0)
    sel_rows = sel_bias_t.astype(BF16)
    if n_slc < AUG_BLK_COL:
        sel_rows = jnp.concatenate([sel_rows, jnp.zeros((AUG_BLK_COL - n_slc, tq), BF16)], axis=0)
    q_slc, q_win = [], []
    for r in range(n_rep):
        pos_rows = jnp.zeros(pshape, F32)
        for k, ck in enumerate(_split3(slopes[r], pshape)):
            pos_rows = jnp.where(frow == k, ck * SLC_LEN, pos_rows)
            pos_rows = jnp.where(frow == AUG_SPLIT + k, ck, pos_rows)
        pos_rows = pos_rows.astype(BF16)
        q_slc.append(jnp.concatenate([qt[r], sel_rows, pos_rows], axis=0))
        q_win.append(jnp.concatenate([qt[r], jnp.zeros_like(sel_rows), pos_rows], axis=0))
    qs_sc[...] = jnp.concatenate(q_slc, axis=1)
    q_win = jnp.concatenate(q_win, axis=1)

    n_wt = WIN // tq + 1
    span = n_wt * tq
    j0 = jnp.maximum(i - (n_wt - 1), 0)
    start_w = pl.multiple_of(j0 * tq, tq)
    s_w = _dot(kwa_ref[0, 0, pl.ds(start_w, span), :], q_win)
    d_w = (lax.broadcasted_iota(jnp.int32, (span, tq), 1) + t0) - \
          (lax.broadcasted_iota(jnp.int32, (span, tq), 0) + start_w)
    mask_w = (d_w >= 0) & (d_w < WIN)
    p_w = jnp.concatenate([_exp2_cols(head(s_w, r), mask_w).astype(BF16) for r in range(n_rep)], axis=1)
    o_win_t = _dot(vwt_ref[0, 0, j0], p_w[0:tq])
    for jj in range(1, n_wt):
        o_win_t = o_win_t + _dot(vwt_ref[0, 0, j0 + jj], p_w[jj * tq:(jj + 1) * tq])
    o_win_t = o_win_t[0:HEAD_DIM] * (1.0 / o_win_t[HEAD_DIM:HEAD_DIM + 1])

    m_sc[...] = jnp.full(m_sc.shape, NEG_INF, F32)
    acc_sc[...] = jnp.zeros(acc_sc.shape, F32)

    def produce(kt, buf):
        start = pl.multiple_of(kt * tq, tq)
        buf[...] = _dot(ksa_ref[0, 0, pl.ds(start, tq), :], qs_sc[...])

    def consume(kt, buf, causal):
        s = buf[...]
        if causal:
            kpos = lax.broadcasted_iota(jnp.int32, s.shape, 0)
            qpos = lax.broadcasted_iota(jnp.int32, s.shape, 1) & (tq - 1)
            s = jnp.where(kpos <= qpos, s, NEG_INF)
        m_old = m_sc[...]
        m_new = jnp.maximum(m_old, jnp.max(s, axis=0, keepdims=True))
        alpha = jnp.exp2(m_old - m_new)
        p = jnp.exp2(s - m_new)
        acc_sc[...] = alpha * acc_sc[...] + _dot(vst_ref[0, 0, kt], p.astype(BF16))
        m_sc[...] = m_new

    def pair(j, carry):
        produce(2 * j + 1, sb_sc)
        consume(2 * j, sa_sc, False)
        produce(2 * j + 2, sa_sc)
        consume(2 * j + 1, sb_sc, False)
        return carry

    produce(0, sa_sc)
    lax.fori_loop(0, i // 2, pair, 0)

    @pl.when(i % 2 == 1)
    def _():
        produce(i, sb_sc)
        consume(i - 1, sa_sc, False)
        consume(i, sb_sc, True)

    @pl.when(i % 2 == 0)
    def _():
        consume(i, sa_sc, True)

    o_slc_t = acc_sc[0:HEAD_DIM, :] * (1.0 / acc_sc[HEAD_DIM:HEAD_DIM + 1, :])

    gates = _sigmoid(gl_ref[...].T)
    for r in range(n_rep):
        o_t = (gates[3 * r:3 * r + 1] * head(o_cmp_t, r)
               + gates[3 * r + 1:3 * r + 2] * head(o_slc_t, r)
               + gates[3 * r + 2:3 * r + 3] * head(o_win_t, r))
        o_ref[:, r * HEAD_DIM:(r + 1) * HEAD_DIM] = o_t.T.astype(o_ref.dtype)


def _attn_call(proj, q_gain, cmp_k, cmp_vt, ks_aug, vs_t, kw_aug, vw_t, slopes, ovl_t,
               B, T, q_col0, gate_col0):
    n_rep = slopes.shape[1]
    gw = n_rep * HEAD_DIM
    nq = T // ATT_TQ
    nc = cmp_k.shape[2]
    n_slc = ovl_t.shape[0]
    qcb = q_col0 // gw
    gcb = gate_col0 // LANES
    assert WIN % ATT_TQ == 0
    keys = pl.BlockSpec((1, 1, T, 2 * HEAD_DIM), lambda b, g, i: (b, g, 0, 0))
    vals = pl.BlockSpec((1, 1, nq, V_ROWS, ATT_TQ), lambda b, g, i: (b, g, 0, 0, 0))
    return pl.pallas_call(
        functools.partial(_attn_kernel, n_rep=n_rep),
        grid=(B, N_KV, nq),
        in_specs=[
            pl.BlockSpec(memory_space=pltpu.SMEM),
            pl.BlockSpec((ATT_TQ, gw), lambda b, g, i: (b * nq + i, qcb + g)),
            pl.BlockSpec((ATT_TQ, LANES), lambda b, g, i: (b * nq + i, gcb + g)),
            pl.BlockSpec((1, HEAD_DIM), lambda b, g, i: (0, 0)),
            pl.BlockSpec((1, 1, nc, HEAD_DIM), lambda b, g, i: (b, g, 0, 0)),
            pl.BlockSpec((1, 1, HEAD_DIM, nc), lambda b, g, i: (b, g, 0, 0)),
            keys, vals, keys, vals,
            pl.BlockSpec((n_slc, nc), lambda b, g, i: (0, 0)),
        ],
        out_specs=pl.BlockSpec((ATT_TQ, gw), lambda b, g, i: (b * nq + i, g)),
        out_shape=jax.ShapeDtypeStruct((B * T, N_KV * gw), BF16),
        scratch_shapes=[
            pltpu.VMEM((n_slc, ATT_TQ), F32),
            pltpu.VMEM((2 * HEAD_DIM, n_rep * ATT_TQ), BF16),
            pltpu.VMEM((ATT_TQ, n_rep * ATT_TQ), F32),
            pltpu.VMEM((ATT_TQ, n_rep * ATT_TQ), F32),
            pltpu.VMEM((1, n_rep * ATT_TQ), F32),
            pltpu.VMEM((V_ROWS, n_rep * ATT_TQ), F32),
        ],
        compiler_params=_cparams("parallel", "parallel", "arbitrary"),
        name="nsa_attention",
    )(slopes, proj, proj, q_gain, cmp_k, cmp_vt, ks_aug, vs_t, kw_aug, vw_t, ovl_t)


def _out_kernel(a_ref, o_ref, w_ref, x_ref, g_ref, y_ref):
    ka = a_ref.shape[1]
    acc = _dot(a_ref[...], w_ref[0:ka, :]) + _dot(o_ref[...], w_ref[ka:, :])
    y_ref[...] = x_ref[...] + g_ref[0] * acc


def _out_call(a, o, w, xf, gate, T):
    M, D = xf.shape
    ka, ko = a.shape[1], o.shape[1]
    per_b = T // OUT_BM
    return pl.pallas_call(
        _out_kernel,
        grid=(M // OUT_BM, D // OUT_BN),
        in_specs=[
            pl.BlockSpec((OUT_BM, ka), lambda i, n: (i, 0)),
            pl.BlockSpec((OUT_BM, ko), lambda i, n: (i, 0)),
            pl.BlockSpec((ka + ko, OUT_BN), lambda i, n: (0, n)),
            pl.BlockSpec((OUT_BM, OUT_BN), lambda i, n: (i, n)),
            pl.BlockSpec((1, 1, OUT_BN), lambda i, n: (i // per_b, 0, n)),
        ],
        out_specs=pl.BlockSpec((OUT_BM, OUT_BN), lambda i, n: (i, n)),
        out_shape=jax.ShapeDtypeStruct((M, D), F32),
        compiler_params=_cparams("parallel", "arbitrary"),
        name="out_proj",
    )(a, o, w, xf, gate)


def _ffn1_kernel(x_ref, ng_ref, sc_ref, sh_ref, wg_ref, wu_ref, o_ref, h_sc, *, n_chunks):
    def emit(h):
        gate = _dot(h, wg_ref[...])
        up = _dot(h, wu_ref[...])
        o_ref[...] = (gate * _sigmoid(gate) * up).astype(o_ref.dtype)

    _norm_matmul_steps(x_ref, ng_ref, sc_ref, sh_ref, h_sc, emit, n_chunks)


def _ffn1_call(xf, ng, sc, sh, w_gu, T):
    M, D = xf.shape
    dff = w_gu.shape[1] // 2
    per_b = T // FFN1_BM
    nt = dff // FFN1_BN
    nm = M // FFN1_BM
    n_chunks = _norm_chunks(nt)
    row = _lookahead_row_tile(nm)
    return pl.pallas_call(
        functools.partial(_ffn1_kernel, n_chunks=n_chunks),
        grid=(nm, nt),
        in_specs=[
            pl.BlockSpec((FFN1_BM, D), lambda i, n: (row(i, n), 0)),
            pl.BlockSpec((1, D), lambda i, n: (0, 0)),
            pl.BlockSpec((1, 1, D), lambda i, n: (row(i, n) // per_b, 0, 0)),
            pl.BlockSpec((1, 1, D), lambda i, n: (row(i, n) // per_b, 0, 0)),
            pl.BlockSpec((D, FFN1_BN), lambda i, n: (0, n)),
            pl.BlockSpec((D, FFN1_BN), lambda i, n: (0, n + nt)),
        ],
        out_specs=pl.BlockSpec((FFN1_BM, FFN1_BN), lambda i, n: (i, n)),
        out_shape=jax.ShapeDtypeStruct((M, dff), BF16),
        scratch_shapes=[pltpu.VMEM((2, FFN1_BM, D), BF16)],
        compiler_params=_cparams("arbitrary", "arbitrary"),
        name="ffn_up",
    )(xf, ng, sc, sh, w_gu, w_gu)


def _ffn2_kernel(h_ref, w_ref, x_ref, g_ref, y_ref):
    y_ref[...] = x_ref[...] + g_ref[0] * _dot(h_ref[...], w_ref[...])


def _ffn2_call(h, w, xf, gate, T):
    M, D = xf.shape
    dff = h.shape[1]
    per_b = T // FFN2_BM
    return pl.pallas_call(
        _ffn2_kernel,
        grid=(M // FFN2_BM, D // FFN2_BN),
        in_specs=[
            pl.BlockSpec((FFN2_BM, dff), lambda i, n: (i, 0)),
            pl.BlockSpec((dff, FFN2_BN), lambda i, n: (0, n)),
            pl.BlockSpec((FFN2_BM, FFN2_BN), lambda i, n: (i, n)),
            pl.BlockSpec((1, 1, FFN2_BN), lambda i, n: (i // per_b, 0, n)),
        ],
        out_specs=pl.BlockSpec((FFN2_BM, FFN2_BN), lambda i, n: (i, n)),
        out_shape=jax.ShapeDtypeStruct((M, D), F32),
        compiler_params=_cparams("parallel", "arbitrary"),
        name="ffn_down",
    )(h, w, xf, gate)


def _alibi_slopes(n_heads):
    sl = 2.0 ** (-8.0 * np.arange(1, n_heads + 1) / n_heads)
    return jnp.asarray(sl, F32).reshape(N_KV, n_heads // N_KV)


def _overlap_t(T):
    nc = T // CMP_STRIDE
    n_slc = T // SLC_LEN
    cst = np.arange(nc) * CMP_STRIDE
    sst = np.arange(n_slc) * SLC_LEN
    ov = (cst[None, :] < sst[:, None] + SLC_LEN) & (cst[None, :] + CMP_LEN > sst[:, None])
    ov[:, (T - CMP_LEN) // CMP_STRIDE + 1:] = False
    return jnp.asarray(ov.astype(np.float32), BF16)


def kernel(x, c, w_ada, b_ada, norm_g, w_in, q_gain, k_gain, pe_cmp, w_cmp1, w_cmp2,
           w_pool, pool_scale, w_out, w_gate_up, w_down):
    B, T, D = x.shape
    L = w_ada.shape[0]
    pool_w = w_pool.shape[1] * w_pool.shape[2]
    kvw = N_KV * HEAD_DIM
    n_heads = (w_in.shape[2] - pool_w - 6 * kvw) // (HEAD_DIM + 3)
    att_w = n_heads * HEAD_DIM
    n_rep = n_heads // N_KV
    assert w_in.shape[2] == pool_w + att_w + 6 * kvw + 3 * n_heads
    assert T % ATT_TQ == 0 and T >= WIN + ATT_TQ and T % POOL_BT == 0 and T % KPREP_BT == 0
    assert T // SLC_LEN <= AUG_BLK_COL and pool_w % (n_rep * HEAD_DIM) == 0
    assert 1 << SLC_SHIFT == SLC_LEN and ATT_TQ & (ATT_TQ - 1) == 0
    q_col0 = pool_w
    kc_col0 = pool_w + att_w
    ks_col0 = kc_col0 + 2 * kvw
    gate_col0 = kc_col0 + 6 * kvw
    assert ks_col0 % kvw == 0 and gate_col0 % LANES == 0

    xf = x.reshape(B * T, D)
    rows = -(-B // 8) * 8
    c8 = jnp.pad(c, ((0, rows - B), (0, 0)))
    mod = _ada_call(c8, w_ada, b_ada)
    slopes = _alibi_slopes(n_heads)
    ovl_t = _overlap_t(T)

    per_group_gates = 3 * n_rep
    for l in range(L):
        sh1, sc1, g1, sh2, sc2, g2 = [mod[l, :B, k * D:(k + 1) * D].reshape(B, 1, D) for k in range(6)]
        gate_blocks = [jnp.pad(w_in[l][:, gate_col0 + g * per_group_gates:gate_col0 + (g + 1) * per_group_gates],
                               ((0, 0), (0, LANES - per_group_gates))) for g in range(N_KV)]
        w_in_p = jnp.concatenate([w_in[l][:, :gate_col0]] + gate_blocks, axis=1)
        pad_cols = -w_in_p.shape[1] % IN_BN
        w_in_p = jnp.pad(w_in_p, ((0, 0), (0, pad_cols))).astype(BF16)

        proj = _in_call(xf, norm_g[l, 0:1], sc1, sh1, w_in_p, T)
        a_out = _pool_call(proj, w_pool[l].astype(BF16), pool_scale[l].reshape(1, pool_w), T)
        ks_aug, vs_t, kw_aug, vw_t = _kprep_call(proj, k_gain[l], B, T, ks_col0)
        pe2 = pe_cmp[l].reshape(2, 2, (CMP_LEN // 2) * HEAD_DIM)
        cmp_k, cmp_vt = _cmp_call(proj, pe2, w_cmp1[l].astype(BF16), w_cmp2[l].astype(BF16), k_gain[l],
                                  B, T, kc_col0)
        o_att = _attn_call(proj, q_gain[l].reshape(1, HEAD_DIM), cmp_k, cmp_vt, ks_aug, vs_t, kw_aug, vw_t,
                           slopes, ovl_t, B, T, q_col0, gate_col0)
        xf = _out_call(a_out, o_att, w_out[l].astype(BF16), xf, g1, T)
        hidden = _ffn1_call(xf, norm_g[l, 1:2], sc2, sh2, w_gate_up[l].astype(BF16), T)
        xf = _ffn2_call(hidden, w_down[l].astype(BF16), xf, g2, T)
    return xf.reshape(B, T, D)
```

```python
import functools

import numpy as np
import jax
import jax.numpy as jnp
from jax import lax
from jax.experimental import pallas as pl
from jax.experimental.pallas import tpu as pltpu

F32 = jnp.float32
BF16 = jnp.bfloat16

POOL_WINDOWS = (2, 4, 8, 16)
HEAD_DIM = 128
N_KV = 2
CMP_LEN = 32
CMP_STRIDE = 16
SLC_LEN = 64
SLC_SHIFT = 6
SLC_TOPK = 16
WIN = 512
NORM_EPS = 1e-6
NEG_INF = -1e30
FORCE_BONUS = 1e3

LANES = 128
POOL_HALO = 16
VMEM_LIMIT_BYTES = 56 * 1024 * 1024

ADA_BN = 1024
IN_BM, IN_BN = 1024, 1280
KPREP_BT = 1024
POOL_BT = 1024
ATT_TQ = 256
OUT_BM, OUT_BN = 2048, 512
FFN1_BM, FFN1_BN = 1024, 512
FFN2_BM, FFN2_BN = 1024, 512


def _cparams(*sem):
    return pltpu.CompilerParams(dimension_semantics=sem, vmem_limit_bytes=VMEM_LIMIT_BYTES)


def _dot(a, b):
    return jnp.dot(a, b, preferred_element_type=F32)


def _dot_nt(a, b):
    return lax.dot_general(a, b, (((1,), (1,)), ((), ())), preferred_element_type=F32)


def _rms(x):
    return x * lax.rsqrt(jnp.mean(x * x, axis=-1, keepdims=True) + NORM_EPS)


def _sigmoid(x):
    return 1.0 / (1.0 + jnp.exp(-x))


def _ada_kernel(c_ref, w_ref, b_ref, o_ref):
    c = c_ref[...]
    cs = c * _sigmoid(c)
    o_ref[0] = _dot(cs, w_ref[0]) + b_ref[0]


def _ada_call(c8, w_ada, b_ada):
    L, D, N = w_ada.shape
    rows = c8.shape[0]
    return pl.pallas_call(
        _ada_kernel,
        grid=(L, N // ADA_BN),
        in_specs=[
            pl.BlockSpec((rows, D), lambda l, n: (0, 0)),
            pl.BlockSpec((1, D, ADA_BN), lambda l, n: (l, 0, n)),
            pl.BlockSpec((1, 1, ADA_BN), lambda l, n: (l, 0, n)),
        ],
        out_specs=pl.BlockSpec((1, rows, ADA_BN), lambda l, n: (l, 0, n)),
        out_shape=jax.ShapeDtypeStruct((L, rows, N), F32),
        compiler_params=_cparams("parallel", "arbitrary"),
        name="ada_mod",
    )(c8, w_ada, b_ada.reshape(L, 1, N))


def _norm_mod(x, ng_ref, sc_ref, sh_ref):
    y = _rms(x) * ng_ref[...]
    return (y * (1.0 + sc_ref[0]) + sh_ref[0]).astype(BF16)


def _lookahead_row_tile(n_row_tiles):
    def idx(i, n):
        return jnp.where((i == 0) & (n == 0), 0, jnp.minimum(i + 1, n_row_tiles - 1))
    return idx


def _norm_chunks(n_col_steps):
    assert n_col_steps >= 2
    return 1 << ((n_col_steps - 1).bit_length() - 1)


def _norm_matmul_steps(x_ref, ng_ref, sc_ref, sh_ref, h_sc, emit, n_chunks):
    i, n = pl.program_id(0), pl.program_id(1)
    rows = x_ref.shape[0] // n_chunks
    slab_step = (n >= 1) & (n <= n_chunks)

    @pl.when((i == 0) & (n == 0))
    def _():
        h_sc[0] = _norm_mod(x_ref[...], ng_ref, sc_ref, sh_ref)

    @pl.when(jnp.logical_not(slab_step))
    def _():
        emit(h_sc[i % 2])

    for slot in (0, 1):
        @pl.when(slab_step & (i % 2 == slot))
        def _(slot=slot):
            emit(h_sc[slot])
            slab = pl.ds(pl.multiple_of((n - 1) * rows, rows), rows)
            h_sc[1 - slot, slab, :] = _norm_mod(x_ref[slab, :], ng_ref, sc_ref, sh_ref)


def _in_kernel(x_ref, ng_ref, sc_ref, sh_ref, w_ref, o_ref, h_sc, *, n_chunks):
    def emit(h):
        o_ref[...] = _dot(h, w_ref[...]).astype(o_ref.dtype)

    _norm_matmul_steps(x_ref, ng_ref, sc_ref, sh_ref, h_sc, emit, n_chunks)


def _in_call(xf, ng, sc, sh, w_all, layer, T):
    M, D = xf.shape
    N = w_all.shape[2]
    per_b = T // IN_BM
    nm = M // IN_BM
    n_chunks = _norm_chunks(N // IN_BN)
    row = _lookahead_row_tile(nm)
    return pl.pallas_call(
        functools.partial(_in_kernel, n_chunks=n_chunks),
        grid=(nm, N // IN_BN),
        in_specs=[
            pl.BlockSpec((IN_BM, D), lambda i, n: (row(i, n), 0)),
            pl.BlockSpec((1, D), lambda i, n: (0, 0)),
            pl.BlockSpec((1, 1, D), lambda i, n: (row(i, n) // per_b, 0, 0)),
            pl.BlockSpec((1, 1, D), lambda i, n: (row(i, n) // per_b, 0, 0)),
            pl.BlockSpec((None, D, IN_BN), lambda i, n: (layer, 0, n)),
        ],
        out_specs=pl.BlockSpec((IN_BM, IN_BN), lambda i, n: (i, n)),
        out_shape=jax.ShapeDtypeStruct((M, N), BF16),
        scratch_shapes=[pltpu.VMEM((2, IN_BM, D), BF16)],
        compiler_params=_cparams("arbitrary", "arbitrary"),
        name="in_proj",
    )(xf, ng, sc, sh, w_all)


def _pool_kernel(u_ref, halo_ref, w_ref, ps_ref, o_ref, ext_sc, *, tiles_per_batch):
    i = pl.program_id(0)
    bt = u_ref.shape[0]
    group = w_ref.shape[1]
    tile_in_batch = i % tiles_per_batch
    ext_sc[0:POOL_HALO, :] = jnp.where(tile_in_batch == 0, 0.0, halo_ref[...].astype(F32))
    ext_sc[POOL_HALO:, :] = u_ref[...].astype(F32)
    t = tile_in_batch * bt + lax.broadcasted_iota(jnp.int32, (bt, 1), 0)
    for gi, w in enumerate(POOL_WINDOWS):
        cols = slice(gi * group, (gi + 1) * group)
        acc = ext_sc[:, cols]
        k = 1
        while k < w:
            acc = acc + pltpu.roll(acc, k, 0)
            k *= 2
        cnt = jnp.minimum(t + 1, w).astype(F32)
        pooled = acc[POOL_HALO:, :] / cnt - ext_sc[POOL_HALO:, cols]
        mixed = _dot(pooled.astype(BF16), w_ref[gi])
        o_ref[:, cols] = (mixed * ps_ref[:, cols]).astype(o_ref.dtype)


def _pool_call(proj, w_pool_all, pool_scale_all, layer, T):
    M = proj.shape[0]
    _, ng, group, _ = w_pool_all.shape
    width = ng * group
    tiles_per_batch = T // POOL_BT
    halo_blocks = POOL_BT // POOL_HALO
    return pl.pallas_call(
        functools.partial(_pool_kernel, tiles_per_batch=tiles_per_batch),
        grid=(M // POOL_BT,),
        in_specs=[
            pl.BlockSpec((POOL_BT, width), lambda i: (i, 0)),
            pl.BlockSpec((POOL_HALO, width), lambda i: (jnp.maximum(i * halo_blocks - 1, 0), 0)),
            pl.BlockSpec((None, ng, group, group), lambda i: (layer, 0, 0, 0)),
            pl.BlockSpec((None, 1, width), lambda i: (layer, 0, 0)),
        ],
        out_specs=pl.BlockSpec((POOL_BT, width), lambda i: (i, 0)),
        out_shape=jax.ShapeDtypeStruct((M, width), BF16),
        scratch_shapes=[pltpu.VMEM((POOL_BT + POOL_HALO, width), F32)],
        compiler_params=_cparams("parallel"),
        name="pool_mixer",
    )(proj, proj, w_pool_all, pool_scale_all)


AUG_SPLIT = 3
AUG_BLK_COL = SLC_LEN
AUG_OFF_COL = SLC_LEN + AUG_SPLIT
V_ROWS = HEAD_DIM + 16
LOG2E = 1.4426950408889634


def _key_extra(t):
    lane = lax.broadcasted_iota(jnp.int32, (t.shape[0], LANES), 1)
    blk = lax.shift_right_logical(t, SLC_SHIFT)
    off = t & (SLC_LEN - 1)
    extra = jnp.where(lane == blk, 1.0, 0.0)
    extra = jnp.where((lane >= AUG_BLK_COL) & (lane < AUG_OFF_COL), blk.astype(F32), extra)
    extra = jnp.where((lane >= AUG_OFF_COL) & (lane < AUG_OFF_COL + AUG_SPLIT), off.astype(F32), extra)
    return extra


def _value_tile_t(v_t):
    pad = lax.broadcasted_iota(jnp.int32, (V_ROWS - HEAD_DIM, v_t.shape[1]), 0)
    return jnp.concatenate([v_t, jnp.where(pad == 0, 1.0, 0.0)], axis=0).astype(BF16)


def _kprep_kernel(ks_ref, vs_ref, kw_ref, vw_ref, kg_ref, ksa_ref, vso_ref, kwa_ref, vwo_ref,
                  *, tiles_per_batch):
    bt = ks_ref.shape[0]
    kt = vso_ref.shape[4]
    t = (pl.program_id(0) % tiles_per_batch) * bt + lax.broadcasted_iota(jnp.int32, (bt, 1), 0)
    extra = _key_extra(t).astype(BF16)
    for g in range(N_KV):
        cols = slice(g * HEAD_DIM, (g + 1) * HEAD_DIM)
        ksn = _rms(ks_ref[:, cols].astype(F32)) * kg_ref[1:2, :]
        kwn = _rms(kw_ref[:, cols].astype(F32)) * kg_ref[2:3, :]
        ksa_ref[0, g, :, 0:HEAD_DIM] = ksn.astype(BF16)
        ksa_ref[0, g, :, HEAD_DIM:] = extra
        kwa_ref[0, g, :, 0:HEAD_DIM] = kwn.astype(BF16)
        kwa_ref[0, g, :, HEAD_DIM:] = extra
        vs_t = _value_tile_t(vs_ref[:, cols].astype(F32).T)
        vw_t = _value_tile_t(vw_ref[:, cols].astype(F32).T)
        for j in range(bt // kt):
            vso_ref[0, g, j] = vs_t[:, j * kt:(j + 1) * kt]
            vwo_ref[0, g, j] = vw_t[:, j * kt:(j + 1) * kt]


def _kprep_call(proj, k_gain, B, T, col0):
    kvw = N_KV * HEAD_DIM
    cb = col0 // kvw
    tiles_per_batch = T // KPREP_BT
    vt_per_tile = KPREP_BT // ATT_TQ
    aug = jax.ShapeDtypeStruct((B, N_KV, T, 2 * HEAD_DIM), BF16)
    val = jax.ShapeDtypeStruct((B, N_KV, T // ATT_TQ, V_ROWS, ATT_TQ), BF16)
    in_spec = lambda j: pl.BlockSpec((KPREP_BT, kvw), lambda i: (i, cb + j))
    out_map = lambda i: (i // tiles_per_batch, 0, i % tiles_per_batch, 0)
    val_map = lambda i: (i // tiles_per_batch, 0, i % tiles_per_batch, 0, 0)
    return pl.pallas_call(
        functools.partial(_kprep_kernel, tiles_per_batch=tiles_per_batch),
        grid=(B * tiles_per_batch,),
        in_specs=[in_spec(0), in_spec(1), in_spec(2), in_spec(3),
                  pl.BlockSpec((3, HEAD_DIM), lambda i: (0, 0))],
        out_specs=[pl.BlockSpec((1, N_KV, KPREP_BT, 2 * HEAD_DIM), out_map),
                   pl.BlockSpec((1, N_KV, vt_per_tile, V_ROWS, ATT_TQ), val_map),
                   pl.BlockSpec((1, N_KV, KPREP_BT, 2 * HEAD_DIM), out_map),
                   pl.BlockSpec((1, N_KV, vt_per_tile, V_ROWS, ATT_TQ), val_map)],
        out_shape=[aug, val, aug, val],
        compiler_params=_cparams("parallel"),
        name="kv_prep",
    )(proj, proj, proj, proj, k_gain)


def _compress_one(src_ref, f32_sc, pe_ref, w1_ref, kv, nc):
    half = CMP_LEN // 2
    assert CMP_STRIDE == half
    f32_sc[...] = src_ref[...].astype(F32)
    xs = [f32_sc[pl.ds(j, nc, stride=CMP_STRIDE), :] for j in range(half)]
    x = jnp.concatenate(xs, axis=1)
    kdim = half * HEAD_DIM
    lo = _dot((x + pe_ref[kv, 0:1, :]).astype(BF16), w1_ref[kv, 0:kdim, :])
    hi = _dot((x + pe_ref[kv, 1:2, :]).astype(BF16), w1_ref[kv, kdim:, :])
    pre = lo + pltpu.roll(hi, nc - 1, 0)
    return (pre * _sigmoid(pre)).astype(BF16)


def _cmp_kernel(k_ref, v_ref, pe_ref, w1_ref, w2k_ref, w2vt_ref, kg_ref, ko_ref, vo_ref, f32_sc):
    nc = ko_ref.shape[2]
    kc = _dot(_compress_one(k_ref, f32_sc, pe_ref, w1_ref, 0, nc), w2k_ref[...])
    ko_ref[0, 0] = (_rms(kc) * kg_ref[0:1, :]).astype(ko_ref.dtype)
    vct = _dot_nt(w2vt_ref[...], _compress_one(v_ref, f32_sc, pe_ref, w1_ref, 1, nc))
    vo_ref[0, 0] = vct.astype(vo_ref.dtype)


def _cmp_call(proj, pe2, w1, w2, k_gain, B, T, col0):
    nc = T // CMP_STRIDE
    cb = col0 // HEAD_DIM
    kdim = CMP_LEN * HEAD_DIM
    return pl.pallas_call(
        _cmp_kernel,
        grid=(B, N_KV),
        in_specs=[
            pl.BlockSpec((T, HEAD_DIM), lambda b, g: (b, cb + g)),
            pl.BlockSpec((T, HEAD_DIM), lambda b, g: (b, cb + N_KV + g)),
            pl.BlockSpec((2, 2, kdim // 2), lambda b, g: (0, 0, 0)),
            pl.BlockSpec((2, kdim, HEAD_DIM), lambda b, g: (0, 0, 0)),
            pl.BlockSpec((HEAD_DIM, HEAD_DIM), lambda b, g: (0, 0)),
            pl.BlockSpec((HEAD_DIM, HEAD_DIM), lambda b, g: (0, 0)),
            pl.BlockSpec((3, HEAD_DIM), lambda b, g: (0, 0)),
        ],
        out_specs=[pl.BlockSpec((1, 1, nc, HEAD_DIM), lambda b, g: (b, g, 0, 0)),
                   pl.BlockSpec((1, 1, HEAD_DIM, nc), lambda b, g: (b, g, 0, 0))],
        out_shape=[jax.ShapeDtypeStruct((B, N_KV, nc, HEAD_DIM), BF16),
                   jax.ShapeDtypeStruct((B, N_KV, HEAD_DIM, nc), BF16)],
        scratch_shapes=[pltpu.VMEM((T, HEAD_DIM), F32)],
        compiler_params=_cparams("parallel", "arbitrary"),
        name="compress",
    )(proj, proj, pe2, w1, w2[0], w2[1].T, k_gain)


MAX_FLOOR = 0.1 * NEG_INF


def _exp2_cols(s, mask):
    s = jnp.where(mask, s, NEG_INF)
    m = jnp.maximum(jnp.max(s, axis=0, keepdims=True), MAX_FLOOR)
    return jnp.exp2(s - m)


def _split3(c, shape):
    c = jnp.full(shape, c, F32)
    c1 = c.astype(BF16).astype(F32)
    r1 = c - c1
    c2 = r1.astype(BF16).astype(F32)
    c3 = (r1 - c2).astype(BF16).astype(F32)
    return c1, c2, c3


def _attn_kernel(slope_ref, q_ref, gl_ref, qg_ref, cmp_k_ref, cmp_vt_ref, ksa_ref, vst_ref,
                 kwa_ref, vwt_ref, ovl_ref, o_ref, score_sc, qs_sc, sa_sc, sb_sc, m_sc, acc_sc,
                 *, n_rep):
    g = pl.program_id(1)
    i = pl.program_id(2)
    tq = q_ref.shape[0]
    nc = cmp_k_ref.shape[2]
    n_slc = ovl_ref.shape[0]
    t0 = i * tq
    slopes = [slope_ref[g, r] * LOG2E for r in range(n_rep)]
    scale = HEAD_DIM ** -0.5 * LOG2E
    head = lambda a, r: a[:, r * tq:(r + 1) * tq]

    qt = []
    for r in range(n_rep):
        x = q_ref[:, r * HEAD_DIM:(r + 1) * HEAD_DIM].astype(F32)
        qt.append((_rms(x) * qg_ref[...] * scale).T.astype(BF16))
    qt_all = jnp.concatenate(qt, axis=1)

    s_all = _dot(cmp_k_ref[0, 0], qt_all)
    c_idx = lax.broadcasted_iota(jnp.int32, (nc, tq), 0)
    t_idx = lax.broadcasted_iota(jnp.int32, (nc, tq), 1) + t0
    dist_c = (t_idx - (c_idx * CMP_STRIDE + (CMP_LEN - 1))).astype(F32)
    mask_c = dist_c >= 0.0
    p_cmp = []
    p_sum = jnp.zeros((nc, tq), F32)
    for r in range(n_rep):
        p = _exp2_cols(head(s_all, r) - slopes[r] * dist_c, mask_c)
        l = jnp.sum(p, axis=0, keepdims=True)
        p = p * (1.0 / jnp.where(l > 0.0, l, 1.0))
        p_cmp.append(p.astype(BF16))
        p_sum = p_sum + p
    o_cmp_t = _dot(cmp_vt_ref[0, 0], jnp.concatenate(p_cmp, axis=1))

    p_hi = p_sum.astype(BF16)
    p_lo = (p_sum - p_hi.astype(F32)).astype(BF16)
    ovl = ovl_ref[...]
    imp = _dot(ovl, p_hi) + _dot(ovl, p_lo)
    jb = lax.broadcasted_iota(jnp.int32, (n_slc, tq), 0)
    tt = lax.broadcasted_iota(jnp.int32, (n_slc, tq), 1) + t0
    cur = lax.shift_right_logical(tt, SLC_SHIFT)
    forced = (jb == 0) | (jb == cur) | (jb == cur - 1)
    score = jnp.where(jb * SLC_LEN <= tt, imp + jnp.where(forced, FORCE_BONUS, 0.0), NEG_INF)
    score_sc[...] = score
    sub = 8
    groups = [score[sub * rg:sub * (rg + 1)] for rg in range(n_slc // sub)]
    ranks = [jnp.zeros((sub, tq), F32) for _ in groups]
    jrow = lax.broadcasted_iota(jnp.int32, (sub, tq), 0)
    for b2 in range(n_slc):
        sb = jnp.broadcast_to(score_sc[b2:b2 + 1, :], (sub, tq))
        for rg, sg in enumerate(groups):
            if sub * rg > b2:
                beats = sb >= sg
            elif sub * rg + sub - 1 < b2:
                beats = sb > sg
            else:
                beats = (sb > sg) | ((sb == sg) & (jrow > b2 - sub * rg))
            ranks[rg] = ranks[rg] + jnp.where(beats, 1.0, 0.0)
    n_sel = min(SLC_TOPK, n_slc)
    sel_bias_t = jnp.concatenate([jnp.where(rk < n_sel, 0.0, NEG_INF) for rk in ranks], axis=0)

    pshape = (LANES - AUG_BLK_COL, tq)
    frow = lax.broadcasted_iota(jnp.int32, pshape, 0)
    sel_rows = sel_bias_t.astype(BF16)
    if n_slc < AUG_BLK_COL:
        sel_rows = jnp.concatenate([sel_rows, jnp.zeros((AUG_BLK_COL - n_slc, tq), BF16)], axis=0)
    q_slc, q_win = [], []
    for r in range(n_rep):
        pos_rows = jnp.zeros(pshape, F32)
        for k, ck in enumerate(_split3(slopes[r], pshape)):
            pos_rows = jnp.where(frow == k, ck * SLC_LEN, pos_rows)
            pos_rows = jnp.where(frow == AUG_SPLIT + k, ck, pos_rows)
        pos_rows = pos_rows.astype(BF16)
        q_slc.append(jnp.concatenate([qt[r], sel_rows, pos_rows], axis=0))
        q_win.append(jnp.concatenate([qt[r], jnp.zeros_like(sel_rows), pos_rows], axis=0))
    qs_sc[...] = jnp.concatenate(q_slc, axis=1)
    q_win = jnp.concatenate(q_win, axis=1)

    n_wt = WIN // tq + 1
    span = n_wt * tq
    j0 = jnp.maximum(i - (n_wt - 1), 0)
    start_w = pl.multiple_of(j0 * tq, tq)
    s_w = _dot(kwa_ref[0, 0, pl.ds(start_w, span), :], q_win)
    d_w = (lax.broadcasted_iota(jnp.int32, (span, tq), 1) + t0) - \
          (lax.broadcasted_iota(jnp.int32, (span, tq), 0) + start_w)
    mask_w = (d_w >= 0) & (d_w < WIN)
    p_w = jnp.concatenate([_exp2_cols(head(s_w, r), mask_w).astype(BF16) for r in range(n_rep)], axis=1)
    o_win_t = _dot(vwt_ref[0, 0, j0], p_w[0:tq])
    for jj in range(1, n_wt):
        o_win_t = o_win_t + _dot(vwt_ref[0, 0, j0 + jj], p_w[jj * tq:(jj + 1) * tq])
    o_win_t = o_win_t[0:HEAD_DIM] * (1.0 / o_win_t[HEAD_DIM:HEAD_DIM + 1])

    m_sc[...] = jnp.full(m_sc.shape, NEG_INF, F32)
    acc_sc[...] = jnp.zeros(acc_sc.shape, F32)

    def produce(kt, buf):
        start = pl.multiple_of(kt * tq, tq)
        buf[...] = _dot(ksa_ref[0, 0, pl.ds(start, tq), :], qs_sc[...])

    def consume(kt, buf, causal):
        s = buf[...]
        if causal:
            kpos = lax.broadcasted_iota(jnp.int32, s.shape, 0)
            qpos = lax.broadcasted_iota(jnp.int32, s.shape, 1) & (tq - 1)
            s = jnp.where(kpos <= qpos, s, NEG_INF)
        m_old = m_sc[...]
        m_new = jnp.maximum(m_old, jnp.max(s, axis=0, keepdims=True))
        alpha = jnp.exp2(m_old - m_new)
        p = jnp.exp2(s - m_new)
        acc_sc[...] = alpha * acc_sc[...] + _dot(vst_ref[0, 0, kt], p.astype(BF16))
        m_sc[...] = m_new

    def pair(j, carry):
        produce(2 * j + 1, sb_sc)
        consume(2 * j, sa_sc, False)
        produce(2 * j + 2, sa_sc)
        consume(2 * j + 1, sb_sc, False)
        return carry

    produce(0, sa_sc)
    lax.fori_loop(0, i // 2, pair, 0)

    @pl.when(i % 2 == 1)
    def _():
        produce(i, sb_sc)
        consume(i - 1, sa_sc, False)
        consume(i, sb_sc, True)

    @pl.when(i % 2 == 0)
    def _():
        consume(i, sa_sc, True)

    o_slc_t = acc_sc[0:HEAD_DIM, :] * (1.0 / acc_sc[HEAD_DIM:HEAD_DIM + 1, :])

    gates = _sigmoid(gl_ref[...].astype(F32).T)
    for r in range(n_rep):
        o_t = (gates[3 * r:3 * r + 1] * head(o_cmp_t, r)
               + gates[3 * r + 1:3 * r + 2] * head(o_slc_t, r)
               + gates[3 * r + 2:3 * r + 3] * head(o_win_t, r))
        o_ref[:, r * HEAD_DIM:(r + 1) * HEAD_DIM] = o_t.T.astype(o_ref.dtype)


def _attn_call(proj, q_gain, cmp_k, cmp_vt, ks_aug, vs_t, kw_aug, vw_t, slopes, ovl_t,
               B, T, q_col0, gate_col0):
    n_rep = slopes.shape[1]
    gw = n_rep * HEAD_DIM
    nq = T // ATT_TQ
    nc = cmp_k.shape[2]
    n_slc = ovl_t.shape[0]
    qcb = q_col0 // gw
    gcb = gate_col0 // LANES
    assert WIN % ATT_TQ == 0
    keys = pl.BlockSpec((1, 1, T, 2 * HEAD_DIM), lambda b, g, i: (b, g, 0, 0))
    vals = pl.BlockSpec((1, 1, nq, V_ROWS, ATT_TQ), lambda b, g, i: (b, g, 0, 0, 0))
    return pl.pallas_call(
        functools.partial(_attn_kernel, n_rep=n_rep),
        grid=(B, N_KV, nq),
        in_specs=[
            pl.BlockSpec(memory_space=pltpu.SMEM),
            pl.BlockSpec((ATT_TQ, gw), lambda b, g, i: (b * nq + i, qcb + g)),
            pl.BlockSpec((ATT_TQ, LANES), lambda b, g, i: (b * nq + i, gcb + g)),
            pl.BlockSpec((1, HEAD_DIM), lambda b, g, i: (0, 0)),
            pl.BlockSpec((1, 1, nc, HEAD_DIM), lambda b, g, i: (b, g, 0, 0)),
            pl.BlockSpec((1, 1, HEAD_DIM, nc), lambda b, g, i: (b, g, 0, 0)),
            keys, vals, keys, vals,
            pl.BlockSpec((n_slc, nc), lambda b, g, i: (0, 0)),
        ],
        out_specs=pl.BlockSpec((ATT_TQ, gw), lambda b, g, i: (b * nq + i, g)),
        out_shape=jax.ShapeDtypeStruct((B * T, N_KV * gw), BF16),
        scratch_shapes=[
            pltpu.VMEM((n_slc, ATT_TQ), F32),
            pltpu.VMEM((2 * HEAD_DIM, n_rep * ATT_TQ), BF16),
            pltpu.VMEM((ATT_TQ, n_rep * ATT_TQ), F32),
            pltpu.VMEM((ATT_TQ, n_rep * ATT_TQ), F32),
            pltpu.VMEM((1, n_rep * ATT_TQ), F32),
            pltpu.VMEM((V_ROWS, n_rep * ATT_TQ), F32),
        ],
        compiler_params=_cparams("parallel", "parallel", "arbitrary"),
        name="nsa_attention",
    )(slopes, proj, proj, q_gain, cmp_k, cmp_vt, ks_aug, vs_t, kw_aug, vw_t, ovl_t)


def _out_kernel(a_ref, o_ref, w_ref, x_ref, g_ref, y_ref):
    ka = a_ref.shape[1]
    acc = _dot(a_ref[...], w_ref[0:ka, :]) + _dot(o_ref[...], w_ref[ka:, :])
    y_ref[...] = x_ref[...] + g_ref[0] * acc


def _out_call(a, o, w_all, layer, xf, gate, T):
    M, D = xf.shape
    ka, ko = a.shape[1], o.shape[1]
    per_b = T // OUT_BM
    return pl.pallas_call(
        _out_kernel,
        grid=(M // OUT_BM, D // OUT_BN),
        in_specs=[
            pl.BlockSpec((OUT_BM, ka), lambda i, n: (i, 0)),
            pl.BlockSpec((OUT_BM, ko), lambda i, n: (i, 0)),
            pl.BlockSpec((None, ka + ko, OUT_BN), lambda i, n: (layer, 0, n)),
            pl.BlockSpec((OUT_BM, OUT_BN), lambda i, n: (i, n)),
            pl.BlockSpec((1, 1, OUT_BN), lambda i, n: (i // per_b, 0, n)),
        ],
        out_specs=pl.BlockSpec((OUT_BM, OUT_BN), lambda i, n: (i, n)),
        out_shape=jax.ShapeDtypeStruct((M, D), F32),
        compiler_params=_cparams("parallel", "arbitrary"),
        name="out_proj",
    )(a, o, w_all, xf, gate)


def _ffn1_kernel(x_ref, ng_ref, sc_ref, sh_ref, wg_ref, wu_ref, o_ref, h_sc, *, n_chunks):
    def emit(h):
        gate = _dot(h, wg_ref[...])
        up = _dot(h, wu_ref[...])
        o_ref[...] = (gate * _sigmoid(gate) * up).astype(o_ref.dtype)

    _norm_matmul_steps(x_ref, ng_ref, sc_ref, sh_ref, h_sc, emit, n_chunks)


def _ffn1_call(xf, ng, sc, sh, w_gu, layer, T):
    M, D = xf.shape
    dff = w_gu.shape[2] // 2
    per_b = T // FFN1_BM
    nt = dff // FFN1_BN
    nm = M // FFN1_BM
    n_chunks = _norm_chunks(nt)
    row = _lookahead_row_tile(nm)
    return pl.pallas_call(
        functools.partial(_ffn1_kernel, n_chunks=n_chunks),
        grid=(nm, nt),
        in_specs=[
            pl.BlockSpec((FFN1_BM, D), lambda i, n: (row(i, n), 0)),
            pl.BlockSpec((1, D), lambda i, n: (0, 0)),
            pl.BlockSpec((1, 1, D), lambda i, n: (row(i, n) // per_b, 0, 0)),
            pl.BlockSpec((1, 1, D), lambda i, n: (row(i, n) // per_b, 0, 0)),
            pl.BlockSpec((None, D, FFN1_BN), lambda i, n: (layer, 0, n)),
            pl.BlockSpec((None, D, FFN1_BN), lambda i, n: (layer, 0, n + nt)),
        ],
        out_specs=pl.BlockSpec((FFN1_BM, FFN1_BN), lambda i, n: (i, n)),
        out_shape=jax.ShapeDtypeStruct((M, dff), BF16),
        scratch_shapes=[pltpu.VMEM((2, FFN1_BM, D), BF16)],
        compiler_params=_cparams("arbitrary", "arbitrary"),
        name="ffn_up",
    )(xf, ng, sc, sh, w_gu, w_gu)


def _ffn2_kernel(h_ref, w_ref, x_ref, g_ref, y_ref):
    y_ref[...] = x_ref[...] + g_ref[0] * _dot(h_ref[...], w_ref[...])


def _ffn2_call(h, w_all, layer, xf, gate, T):
    M, D = xf.shape
    dff = h.shape[1]
    per_b = T // FFN2_BM
    return pl.pallas_call(
        _ffn2_kernel,
        grid=(M // FFN2_BM, D // FFN2_BN),
        in_specs=[
            pl.BlockSpec((FFN2_BM, dff), lambda i, n: (i, 0)),
            pl.BlockSpec((None, dff, FFN2_BN), lambda i, n: (layer, 0, n)),
            pl.BlockSpec((FFN2_BM, FFN2_BN), lambda i, n: (i, n)),
            pl.BlockSpec((1, 1, FFN2_BN), lambda i, n: (i // per_b, 0, n)),
        ],
        out_specs=pl.BlockSpec((FFN2_BM, FFN2_BN), lambda i, n: (i, n)),
        out_shape=jax.ShapeDtypeStruct((M, D), F32),
        compiler_params=_cparams("parallel", "arbitrary"),
        name="ffn_down",
    )(h, w_all, xf, gate)


def _alibi_slopes(n_heads):
    sl = 2.0 ** (-8.0 * np.arange(1, n_heads + 1) / n_heads)
    return jnp.asarray(sl, F32).reshape(N_KV, n_heads // N_KV)


def _overlap_t(T):
    nc = T // CMP_STRIDE
    n_slc = T // SLC_LEN
    cst = np.arange(nc) * CMP_STRIDE
    sst = np.arange(n_slc) * SLC_LEN
    ov = (cst[None, :] < sst[:, None] + SLC_LEN) & (cst[None, :] + CMP_LEN > sst[:, None])
    ov[:, (T - CMP_LEN) // CMP_STRIDE + 1:] = False
    return jnp.asarray(ov.astype(np.float32), BF16)


def kernel(x, c, w_ada, b_ada, norm_g, w_in, q_gain, k_gain, pe_cmp, w_cmp1, w_cmp2,
           w_pool, pool_scale, w_out, w_gate_up, w_down):
    B, T, D = x.shape
    L = w_ada.shape[0]
    pool_w = w_pool.shape[1] * w_pool.shape[2]
    kvw = N_KV * HEAD_DIM
    n_heads = (w_in.shape[2] - pool_w - 6 * kvw) // (HEAD_DIM + 3)
    att_w = n_heads * HEAD_DIM
    n_rep = n_heads // N_KV
    assert w_in.shape[2] == pool_w + att_w + 6 * kvw + 3 * n_heads
    assert T % ATT_TQ == 0 and T >= WIN + ATT_TQ and T % POOL_BT == 0 and T % KPREP_BT == 0
    assert T // SLC_LEN <= AUG_BLK_COL and pool_w % (n_rep * HEAD_DIM) == 0
    assert 1 << SLC_SHIFT == SLC_LEN and ATT_TQ & (ATT_TQ - 1) == 0
    q_col0 = pool_w
    kc_col0 = pool_w + att_w
    ks_col0 = kc_col0 + 2 * kvw
    gate_col0 = kc_col0 + 6 * kvw
    assert ks_col0 % kvw == 0 and gate_col0 % LANES == 0

    xf = x.reshape(B * T, D)
    rows = -(-B // 8) * 8
    c8 = jnp.pad(c, ((0, rows - B), (0, 0)))
    mod = _ada_call(c8, w_ada, b_ada)
    slopes = _alibi_slopes(n_heads)
    ovl_t = _overlap_t(T)

    per_group_gates = 3 * n_rep
    gate_blocks = [jnp.pad(w_in[:, :, gate_col0 + g * per_group_gates:gate_col0 + (g + 1) * per_group_gates],
                           ((0, 0), (0, 0), (0, LANES - per_group_gates))) for g in range(N_KV)]
    w_in_p = jnp.concatenate([w_in[:, :, :gate_col0]] + gate_blocks, axis=2)
    w_in_p = jnp.pad(w_in_p, ((0, 0), (0, 0), (0, -w_in_p.shape[2] % IN_BN))).astype(BF16)
    w_pool_b = w_pool.astype(BF16)
    pool_scale_r = pool_scale.reshape(L, 1, pool_w)
    w_out_b = w_out.astype(BF16)
    w_gu_b = w_gate_up.astype(BF16)
    w_down_b = w_down.astype(BF16)

    for l in range(L):
        sh1, sc1, g1, sh2, sc2, g2 = [mod[l, :B, k * D:(k + 1) * D].reshape(B, 1, D) for k in range(6)]
        proj = _in_call(xf, norm_g[l, 0:1], sc1, sh1, w_in_p, l, T)
        a_out = _pool_call(proj, w_pool_b, pool_scale_r, l, T)
        ks_aug, vs_t, kw_aug, vw_t = _kprep_call(proj, k_gain[l], B, T, ks_col0)
        pe2 = pe_cmp[l].reshape(2, 2, (CMP_LEN // 2) * HEAD_DIM)
        cmp_k, cmp_vt = _cmp_call(proj, pe2, w_cmp1[l].astype(BF16), w_cmp2[l].astype(BF16), k_gain[l],
                                  B, T, kc_col0)
        o_att = _attn_call(proj, q_gain[l].reshape(1, HEAD_DIM), cmp_k, cmp_vt, ks_aug, vs_t, kw_aug, vw_t,
                           slopes, ovl_t, B, T, q_col0, gate_col0)
        xf = _out_call(a_out, o_att, w_out_b, l, xf, g1, T)
        hidden = _ffn1_call(xf, norm_g[l, 1:2], sc2, sh2, w_gu_b, l, T)
        xf = _ffn2_call(hidden, w_down_b, l, xf, g2, T)
    return xf.reshape(B, T, D)
```

```python
import functools

import numpy as np
import jax
import jax.numpy as jnp
from jax import lax
from jax.experimental import pallas as pl
from jax.experimental.pallas import tpu as pltpu

F32 = jnp.float32
BF16 = jnp.bfloat16

POOL_WINDOWS = (2, 4, 8, 16)
HEAD_DIM = 128
N_KV = 2
CMP_LEN = 32
CMP_STRIDE = 16
SLC_LEN = 64
SLC_SHIFT = 6
SLC_TOPK = 16
WIN = 512
NORM_EPS = 1e-6
NEG_INF = -1e30
FORCE_BONUS = 1e3

LANES = 128
POOL_HALO = 16
VMEM_LIMIT_BYTES = 56 * 1024 * 1024

ADA_BN = 1024
IN_BM, IN_BN = 1024, 1280
KPREP_BT = 1024
POOL_BT = 1024
ATT_TQ = 256
OUT_BM, OUT_BN = 2048, 512
FFN1_BM, FFN1_BN = 1024, 512
FFN2_BM, FFN2_BN = 1024, 512


def _cparams(*sem):
    return pltpu.CompilerParams(dimension_semantics=sem, vmem_limit_bytes=VMEM_LIMIT_BYTES)


def _dot(a, b):
    return jnp.dot(a, b, preferred_element_type=F32)


def _dot_nt(a, b):
    return lax.dot_general(a, b, (((1,), (1,)), ((), ())), preferred_element_type=F32)


def _rms(x):
    return x * lax.rsqrt(jnp.mean(x * x, axis=-1, keepdims=True) + NORM_EPS)


def _sigmoid(x):
    return 1.0 / (1.0 + jnp.exp(-x))


def _ada_kernel(c_ref, w_ref, b_ref, o_ref):
    c = c_ref[...]
    cs = c * _sigmoid(c)
    o_ref[0] = _dot(cs, w_ref[0]) + b_ref[0]


def _ada_call(c8, w_ada, b_ada):
    L, D, N = w_ada.shape
    rows = c8.shape[0]
    return pl.pallas_call(
        _ada_kernel,
        grid=(L, N // ADA_BN),
        in_specs=[
            pl.BlockSpec((rows, D), lambda l, n: (0, 0)),
            pl.BlockSpec((1, D, ADA_BN), lambda l, n: (l, 0, n)),
            pl.BlockSpec((1, 1, ADA_BN), lambda l, n: (l, 0, n)),
        ],
        out_specs=pl.BlockSpec((1, rows, ADA_BN), lambda l, n: (l, 0, n)),
        out_shape=jax.ShapeDtypeStruct((L, rows, N), F32),
        compiler_params=_cparams("parallel", "arbitrary"),
        name="ada_mod",
    )(c8, w_ada, b_ada.reshape(L, 1, N))


def _norm_mod(x, ng_ref, sc_ref, sh_ref):
    y = _rms(x) * ng_ref[...]
    return (y * (1.0 + sc_ref[0]) + sh_ref[0]).astype(BF16)


def _lookahead_row_tile(n_row_tiles):
    def idx(i, n):
        return jnp.where((i == 0) & (n == 0), 0, jnp.minimum(i + 1, n_row_tiles - 1))
    return idx


def _norm_chunks(n_col_steps):
    assert n_col_steps >= 2
    return 1 << ((n_col_steps - 1).bit_length() - 1)


def _norm_matmul_steps(x_ref, ng_ref, sc_ref, sh_ref, h_sc, emit, n_chunks):
    i, n = pl.program_id(0), pl.program_id(1)
    rows = x_ref.shape[0] // n_chunks
    slab_step = (n >= 1) & (n <= n_chunks)

    @pl.when((i == 0) & (n == 0))
    def _():
        h_sc[0] = _norm_mod(x_ref[...], ng_ref, sc_ref, sh_ref)

    @pl.when(jnp.logical_not(slab_step))
    def _():
        emit(h_sc[i % 2])

    for slot in (0, 1):
        @pl.when(slab_step & (i % 2 == slot))
        def _(slot=slot):
            emit(h_sc[slot])
            slab = pl.ds(pl.multiple_of((n - 1) * rows, rows), rows)
            h_sc[1 - slot, slab, :] = _norm_mod(x_ref[slab, :], ng_ref, sc_ref, sh_ref)


def _in_kernel(x_ref, ng_ref, sc_ref, sh_ref, w_ref, o_ref, h_sc, *, n_chunks):
    def emit(h):
        o_ref[...] = _dot(h, w_ref[...]).astype(o_ref.dtype)

    _norm_matmul_steps(x_ref, ng_ref, sc_ref, sh_ref, h_sc, emit, n_chunks)


def _in_call(xf, ng, sc, sh, w_all, layer, T):
    M, D = xf.shape
    N = w_all.shape[2]
    per_b = T // IN_BM
    nm = M // IN_BM
    n_chunks = _norm_chunks(N // IN_BN)
    row = _lookahead_row_tile(nm)
    return pl.pallas_call(
        functools.partial(_in_kernel, n_chunks=n_chunks),
        grid=(nm, N // IN_BN),
        in_specs=[
            pl.BlockSpec((IN_BM, D), lambda i, n: (row(i, n), 0)),
            pl.BlockSpec((1, D), lambda i, n: (0, 0)),
            pl.BlockSpec((1, 1, D), lambda i, n: (row(i, n) // per_b, 0, 0)),
            pl.BlockSpec((1, 1, D), lambda i, n: (row(i, n) // per_b, 0, 0)),
            pl.BlockSpec((None, D, IN_BN), lambda i, n: (layer, 0, n)),
        ],
        out_specs=pl.BlockSpec((IN_BM, IN_BN), lambda i, n: (i, n)),
        out_shape=jax.ShapeDtypeStruct((M, N), BF16),
        scratch_shapes=[pltpu.VMEM((2, IN_BM, D), BF16)],
        compiler_params=_cparams("arbitrary", "arbitrary"),
        name="in_proj",
    )(xf, ng, sc, sh, w_all)


def _pool_kernel(u_ref, halo_ref, w_ref, ps_ref, o_ref, ext_sc, *, tiles_per_batch):
    i = pl.program_id(0)
    bt = u_ref.shape[0]
    group = w_ref.shape[1]
    tile_in_batch = i % tiles_per_batch
    ext_sc[0:POOL_HALO, :] = jnp.where(tile_in_batch == 0, 0.0, halo_ref[...].astype(F32))
    ext_sc[POOL_HALO:, :] = u_ref[...].astype(F32)
    t = tile_in_batch * bt + lax.broadcasted_iota(jnp.int32, (bt, 1), 0)
    for gi, w in enumerate(POOL_WINDOWS):
        cols = slice(gi * group, (gi + 1) * group)
        acc = ext_sc[:, cols]
        k = 1
        while k < w:
            acc = acc + pltpu.roll(acc, k, 0)
            k *= 2
        cnt = jnp.minimum(t + 1, w).astype(F32)
        pooled = acc[POOL_HALO:, :] / cnt - ext_sc[POOL_HALO:, cols]
        mixed = _dot(pooled.astype(BF16), w_ref[gi])
        o_ref[:, cols] = (mixed * ps_ref[:, cols]).astype(o_ref.dtype)


def _pool_call(proj, w_pool_all, pool_scale_all, layer, T):
    M = proj.shape[0]
    _, ng, group, _ = w_pool_all.shape
    width = ng * group
    tiles_per_batch = T // POOL_BT
    halo_blocks = POOL_BT // POOL_HALO
    return pl.pallas_call(
        functools.partial(_pool_kernel, tiles_per_batch=tiles_per_batch),
        grid=(M // POOL_BT,),
        in_specs=[
            pl.BlockSpec((POOL_BT, width), lambda i: (i, 0)),
            pl.BlockSpec((POOL_HALO, width), lambda i: (jnp.maximum(i * halo_blocks - 1, 0), 0)),
            pl.BlockSpec((None, ng, group, group), lambda i: (layer, 0, 0, 0)),
            pl.BlockSpec((None, 1, width), lambda i: (layer, 0, 0)),
        ],
        out_specs=pl.BlockSpec((POOL_BT, width), lambda i: (i, 0)),
        out_shape=jax.ShapeDtypeStruct((M, width), BF16),
        scratch_shapes=[pltpu.VMEM((POOL_BT + POOL_HALO, width), F32)],
        compiler_params=_cparams("parallel"),
        name="pool_mixer",
    )(proj, proj, w_pool_all, pool_scale_all)


AUG_SPLIT = 3
AUG_BLK_COL = SLC_LEN
AUG_OFF_COL = SLC_LEN + AUG_SPLIT
V_ROWS = HEAD_DIM + 16
LOG2E = 1.4426950408889634


def _key_extra(t):
    lane = lax.broadcasted_iota(jnp.int32, (t.shape[0], LANES), 1)
    blk = lax.shift_right_logical(t, SLC_SHIFT)
    off = t & (SLC_LEN - 1)
    extra = jnp.where(lane == blk, 1.0, 0.0)
    extra = jnp.where((lane >= AUG_BLK_COL) & (lane < AUG_OFF_COL), blk.astype(F32), extra)
    extra = jnp.where((lane >= AUG_OFF_COL) & (lane < AUG_OFF_COL + AUG_SPLIT), off.astype(F32), extra)
    return extra


def _value_tile_t(v_t):
    pad = lax.broadcasted_iota(jnp.int32, (V_ROWS - HEAD_DIM, v_t.shape[1]), 0)
    return jnp.concatenate([v_t, jnp.where(pad == 0, 1.0, 0.0)], axis=0).astype(BF16)


def _kprep_kernel(ks_ref, vs_ref, kw_ref, vw_ref, kg_ref, ksa_ref, vso_ref, kwa_ref, vwo_ref,
                  *, tiles_per_batch):
    bt = ks_ref.shape[0]
    kt = vso_ref.shape[4]
    t = (pl.program_id(0) % tiles_per_batch) * bt + lax.broadcasted_iota(jnp.int32, (bt, 1), 0)
    extra = _key_extra(t).astype(BF16)
    for g in range(N_KV):
        cols = slice(g * HEAD_DIM, (g + 1) * HEAD_DIM)
        ksn = _rms(ks_ref[:, cols].astype(F32)) * kg_ref[1:2, :]
        kwn = _rms(kw_ref[:, cols].astype(F32)) * kg_ref[2:3, :]
        ksa_ref[0, g, :, 0:HEAD_DIM] = ksn.astype(BF16)
        ksa_ref[0, g, :, HEAD_DIM:] = extra
        kwa_ref[0, g, :, 0:HEAD_DIM] = kwn.astype(BF16)
        kwa_ref[0, g, :, HEAD_DIM:] = extra
        vs_t = _value_tile_t(vs_ref[:, cols].astype(F32).T)
        vw_t = _value_tile_t(vw_ref[:, cols].astype(F32).T)
        for j in range(bt // kt):
            vso_ref[0, g, j] = vs_t[:, j * kt:(j + 1) * kt]
            vwo_ref[0, g, j] = vw_t[:, j * kt:(j + 1) * kt]


def _kprep_call(proj, k_gain, B, T, col0):
    kvw = N_KV * HEAD_DIM
    cb = col0 // kvw
    tiles_per_batch = T // KPREP_BT
    vt_per_tile = KPREP_BT // ATT_TQ
    aug = jax.ShapeDtypeStruct((B, N_KV, T, 2 * HEAD_DIM), BF16)
    val = jax.ShapeDtypeStruct((B, N_KV, T // ATT_TQ, V_ROWS, ATT_TQ), BF16)
    in_spec = lambda j: pl.BlockSpec((KPREP_BT, kvw), lambda i: (i, cb + j))
    out_map = lambda i: (i // tiles_per_batch, 0, i % tiles_per_batch, 0)
    val_map = lambda i: (i // tiles_per_batch, 0, i % tiles_per_batch, 0, 0)
    return pl.pallas_call(
        functools.partial(_kprep_kernel, tiles_per_batch=tiles_per_batch),
        grid=(B * tiles_per_batch,),
        in_specs=[in_spec(0), in_spec(1), in_spec(2), in_spec(3),
                  pl.BlockSpec((3, HEAD_DIM), lambda i: (0, 0))],
        out_specs=[pl.BlockSpec((1, N_KV, KPREP_BT, 2 * HEAD_DIM), out_map),
                   pl.BlockSpec((1, N_KV, vt_per_tile, V_ROWS, ATT_TQ), val_map),
                   pl.BlockSpec((1, N_KV, KPREP_BT, 2 * HEAD_DIM), out_map),
                   pl.BlockSpec((1, N_KV, vt_per_tile, V_ROWS, ATT_TQ), val_map)],
        out_shape=[aug, val, aug, val],
        compiler_params=_cparams("parallel"),
        name="kv_prep",
    )(proj, proj, proj, proj, k_gain)


def _compress_one(src_ref, f32_sc, pe_ref, w1_ref, kv, nc):
    half = CMP_LEN // 2
    assert CMP_STRIDE == half
    f32_sc[...] = src_ref[...].astype(F32)
    xs = [f32_sc[pl.ds(j, nc, stride=CMP_STRIDE), :] for j in range(half)]
    x = jnp.concatenate(xs, axis=1)
    kdim = half * HEAD_DIM
    lo = _dot((x + pe_ref[kv, 0:1, :]).astype(BF16), w1_ref[kv, 0:kdim, :])
    hi = _dot((x + pe_ref[kv, 1:2, :]).astype(BF16), w1_ref[kv, kdim:, :])
    pre = lo + pltpu.roll(hi, nc - 1, 0)
    return (pre * _sigmoid(pre)).astype(BF16)


def _cmp_kernel(k_ref, v_ref, pe_ref, w1_ref, w2k_ref, w2vt_ref, kg_ref, ko_ref, vo_ref, f32_sc):
    nc = ko_ref.shape[2]
    kc = _dot(_compress_one(k_ref, f32_sc, pe_ref, w1_ref, 0, nc), w2k_ref[...])
    ko_ref[0, 0] = (_rms(kc) * kg_ref[0:1, :]).astype(ko_ref.dtype)
    vct = _dot_nt(w2vt_ref[...], _compress_one(v_ref, f32_sc, pe_ref, w1_ref, 1, nc))
    vo_ref[0, 0] = vct.astype(vo_ref.dtype)


def _cmp_call(proj, pe2, w1, w2, k_gain, B, T, col0):
    nc = T // CMP_STRIDE
    cb = col0 // HEAD_DIM
    kdim = CMP_LEN * HEAD_DIM
    return pl.pallas_call(
        _cmp_kernel,
        grid=(B, N_KV),
        in_specs=[
            pl.BlockSpec((T, HEAD_DIM), lambda b, g: (b, cb + g)),
            pl.BlockSpec((T, HEAD_DIM), lambda b, g: (b, cb + N_KV + g)),
            pl.BlockSpec((2, 2, kdim // 2), lambda b, g: (0, 0, 0)),
            pl.BlockSpec((2, kdim, HEAD_DIM), lambda b, g: (0, 0, 0)),
            pl.BlockSpec((HEAD_DIM, HEAD_DIM), lambda b, g: (0, 0)),
            pl.BlockSpec((HEAD_DIM, HEAD_DIM), lambda b, g: (0, 0)),
            pl.BlockSpec((3, HEAD_DIM), lambda b, g: (0, 0)),
        ],
        out_specs=[pl.BlockSpec((1, 1, nc, HEAD_DIM), lambda b, g: (b, g, 0, 0)),
                   pl.BlockSpec((1, 1, HEAD_DIM, nc), lambda b, g: (b, g, 0, 0))],
        out_shape=[jax.ShapeDtypeStruct((B, N_KV, nc, HEAD_DIM), BF16),
                   jax.ShapeDtypeStruct((B, N_KV, HEAD_DIM, nc), BF16)],
        scratch_shapes=[pltpu.VMEM((T, HEAD_DIM), F32)],
        compiler_params=_cparams("parallel", "arbitrary"),
        name="compress",
    )(proj, proj, pe2, w1, w2[0], w2[1].T, k_gain)


MAX_FLOOR = 0.1 * NEG_INF


def _exp2_cols(s, mask):
    s = jnp.where(mask, s, NEG_INF)
    m = jnp.maximum(jnp.max(s, axis=0, keepdims=True), MAX_FLOOR)
    return jnp.exp2(s - m)


def _split3(c, shape):
    c = jnp.full(shape, c, F32)
    c1 = c.astype(BF16).astype(F32)
    r1 = c - c1
    c2 = r1.astype(BF16).astype(F32)
    c3 = (r1 - c2).astype(BF16).astype(F32)
    return c1, c2, c3


def _attn_kernel(slope_ref, q_ref, gl_ref, qg_ref, cmp_k_ref, cmp_vt_ref, ksa_ref, vst_ref,
                 kwa_ref, vwt_ref, ovl_ref, o_ref, score_sc, qs_sc, sa_sc, sb_sc, m_sc, acc_sc, seq_ref,
                 *, n_rep):
    g = pl.program_id(1)
    i = pl.program_id(2)
    tq = q_ref.shape[0]
    nc = cmp_k_ref.shape[2]
    n_slc = ovl_ref.shape[0]
    t0 = i * tq
    slopes = [slope_ref[g, r] * LOG2E for r in range(n_rep)]
    scale = HEAD_DIM ** -0.5 * LOG2E
    head = lambda a, r: a[:, r * tq:(r + 1) * tq]

    qt = []
    for r in range(n_rep):
        x = q_ref[:, r * HEAD_DIM:(r + 1) * HEAD_DIM].astype(F32)
        qt.append((_rms(x) * qg_ref[...] * scale).T.astype(BF16))
    qt_all = jnp.concatenate(qt, axis=1)

    s_all = _dot(cmp_k_ref[0, 0], qt_all)
    c_idx = lax.broadcasted_iota(jnp.int32, (nc, tq), 0)
    t_idx = lax.broadcasted_iota(jnp.int32, (nc, tq), 1) + t0
    dist_c = (t_idx - (c_idx * CMP_STRIDE + (CMP_LEN - 1))).astype(F32)
    mask_c = dist_c >= 0.0
    p_cmp = []
    p_sum = jnp.zeros((nc, tq), F32)
    for r in range(n_rep):
        p = _exp2_cols(head(s_all, r) - slopes[r] * dist_c, mask_c)
        l = jnp.sum(p, axis=0, keepdims=True)
        p = p * (1.0 / jnp.where(l > 0.0, l, 1.0))
        p_cmp.append(p.astype(BF16))
        p_sum = p_sum + p
    o_cmp_t = _dot(cmp_vt_ref[0, 0], jnp.concatenate(p_cmp, axis=1))

    p_hi = p_sum.astype(BF16)
    p_lo = (p_sum - p_hi.astype(F32)).astype(BF16)
    ovl = ovl_ref[...]
    imp = _dot(ovl, p_hi) + _dot(ovl, p_lo)
    jb = lax.broadcasted_iota(jnp.int32, (n_slc, tq), 0)
    tt = lax.broadcasted_iota(jnp.int32, (n_slc, tq), 1) + t0
    cur = lax.shift_right_logical(tt, SLC_SHIFT)
    forced = (jb == 0) | (jb == cur) | (jb == cur - 1)
    score = jnp.where(jb * SLC_LEN <= tt, imp + jnp.where(forced, FORCE_BONUS, 0.0), NEG_INF)
    score_sc[...] = score
    sub = 8
    groups = [score[sub * rg:sub * (rg + 1)] for rg in range(n_slc // sub)]
    ranks = [jnp.zeros((sub, tq), F32) for _ in groups]
    jrow = lax.broadcasted_iota(jnp.int32, (sub, tq), 0)
    for b2 in range(n_slc):
        sb = jnp.broadcast_to(score_sc[b2:b2 + 1, :], (sub, tq))
        for rg, sg in enumerate(groups):
            if sub * rg > b2:
                beats = sb >= sg
            elif sub * rg + sub - 1 < b2:
                beats = sb > sg
            else:
                beats = (sb > sg) | ((sb == sg) & (jrow > b2 - sub * rg))
            ranks[rg] = ranks[rg] + jnp.where(beats, 1.0, 0.0)
    n_sel = min(SLC_TOPK, n_slc)
    sel_bias_t = jnp.concatenate([jnp.where(rk < n_sel, 0.0, NEG_INF) for rk in ranks], axis=0)

    bpt = tq // SLC_LEN
    n_list = jnp.int32(0)
    for kt in range(n_slc // bpt - 1):
        rg, off = divmod(kt * bpt, sub)
        hit = jnp.max(jnp.where(ranks[rg][off:off + bpt] < n_sel, 1.0, 0.0), axis=1, keepdims=True)
        hit = jnp.max(hit, axis=0, keepdims=True)[0, 0]
        seq_ref[n_list] = jnp.int32(kt)
        n_list = n_list + ((hit > 0.0) & (kt < i)).astype(jnp.int32)
    seq_ref[n_list] = i

    pshape = (LANES - AUG_BLK_COL, tq)
    frow = lax.broadcasted_iota(jnp.int32, pshape, 0)
    sel_rows = sel_bias_t.astype(BF16)
    if n_slc < AUG_BLK_COL:
        sel_rows = jnp.concatenate([sel_rows, jnp.zeros((AUG_BLK_COL - n_slc, tq), BF16)], axis=0)
    q_slc, q_win = [], []
    for r in range(n_rep):
        pos_rows = jnp.zeros(pshape, F32)
        for k, ck in enumerate(_split3(slopes[r], pshape)):
            pos_rows = jnp.where(frow == k, ck * SLC_LEN, pos_rows)
            pos_rows = jnp.where(frow == AUG_SPLIT + k, ck, pos_rows)
        pos_rows = pos_rows.astype(BF16)
        q_slc.append(jnp.concatenate([qt[r], sel_rows, pos_rows], axis=0))
        q_win.append(jnp.concatenate([qt[r], jnp.zeros_like(sel_rows), pos_rows], axis=0))
    qs_sc[...] = jnp.concatenate(q_slc, axis=1)
    q_win = jnp.concatenate(q_win, axis=1)

    n_wt = WIN // tq + 1
    span = n_wt * tq
    j0 = jnp.maximum(i - (n_wt - 1), 0)
    start_w = pl.multiple_of(j0 * tq, tq)
    s_w = _dot(kwa_ref[0, 0, pl.ds(start_w, span), :], q_win)
    d_w = (lax.broadcasted_iota(jnp.int32, (span, tq), 1) + t0) - \
          (lax.broadcasted_iota(jnp.int32, (span, tq), 0) + start_w)
    mask_w = (d_w >= 0) & (d_w < WIN)
    p_w = jnp.concatenate([_exp2_cols(head(s_w, r), mask_w).astype(BF16) for r in range(n_rep)], axis=1)
    o_win_t = _dot(vwt_ref[0, 0, j0], p_w[0:tq])
    for jj in range(1, n_wt):
        o_win_t = o_win_t + _dot(vwt_ref[0, 0, j0 + jj], p_w[jj * tq:(jj + 1) * tq])
    o_win_t = o_win_t[0:HEAD_DIM] * (1.0 / o_win_t[HEAD_DIM:HEAD_DIM + 1])

    m_sc[...] = jnp.full(m_sc.shape, NEG_INF, F32)
    acc_sc[...] = jnp.zeros(acc_sc.shape, F32)

    def produce(kt, buf):
        start = pl.multiple_of(kt * tq, tq)
        buf[...] = _dot(ksa_ref[0, 0, pl.ds(start, tq), :], qs_sc[...])

    def consume(kt, buf, causal):
        s = buf[...]
        if causal:
            kpos = lax.broadcasted_iota(jnp.int32, s.shape, 0)
            qpos = lax.broadcasted_iota(jnp.int32, s.shape, 1) & (tq - 1)
            s = jnp.where(kpos <= qpos, s, NEG_INF)
        m_old = m_sc[...]
        m_new = jnp.maximum(m_old, jnp.max(s, axis=0, keepdims=True))
        alpha = jnp.exp2(m_old - m_new)
        p = jnp.exp2(s - m_new)
        acc_sc[...] = alpha * acc_sc[...] + _dot(vst_ref[0, 0, kt], p.astype(BF16))
        m_sc[...] = m_new

    def pair(j, carry):
        produce(seq_ref[2 * j + 1], sb_sc)
        consume(seq_ref[2 * j], sa_sc, False)
        produce(seq_ref[2 * j + 2], sa_sc)
        consume(seq_ref[2 * j + 1], sb_sc, False)
        return carry

    produce(seq_ref[0], sa_sc)
    lax.fori_loop(0, n_list // 2, pair, 0)

    @pl.when(n_list % 2 == 1)
    def _():
        produce(i, sb_sc)
        consume(seq_ref[n_list - 1], sa_sc, False)
        consume(i, sb_sc, True)

    @pl.when(n_list % 2 == 0)
    def _():
        consume(i, sa_sc, True)

    o_slc_t = acc_sc[0:HEAD_DIM, :] * (1.0 / acc_sc[HEAD_DIM:HEAD_DIM + 1, :])

    gates = _sigmoid(gl_ref[...].astype(F32).T)
    for r in range(n_rep):
        o_t = (gates[3 * r:3 * r + 1] * head(o_cmp_t, r)
               + gates[3 * r + 1:3 * r + 2] * head(o_slc_t, r)
               + gates[3 * r + 2:3 * r + 3] * head(o_win_t, r))
        o_ref[:, r * HEAD_DIM:(r + 1) * HEAD_DIM] = o_t.T.astype(o_ref.dtype)


def _attn_call(proj, q_gain, cmp_k, cmp_vt, ks_aug, vs_t, kw_aug, vw_t, slopes, ovl_t,
               B, T, q_col0, gate_col0):
    n_rep = slopes.shape[1]
    gw = n_rep * HEAD_DIM
    nq = T // ATT_TQ
    nc = cmp_k.shape[2]
    n_slc = ovl_t.shape[0]
    qcb = q_col0 // gw
    gcb = gate_col0 // LANES
    assert WIN % ATT_TQ == 0
    keys = pl.BlockSpec((1, 1, T, 2 * HEAD_DIM), lambda b, g, i: (b, g, 0, 0))
    vals = pl.BlockSpec((1, 1, nq, V_ROWS, ATT_TQ), lambda b, g, i: (b, g, 0, 0, 0))
    return pl.pallas_call(
        functools.partial(_attn_kernel, n_rep=n_rep),
        grid=(B, N_KV, nq),
        in_specs=[
            pl.BlockSpec(memory_space=pltpu.SMEM),
            pl.BlockSpec((ATT_TQ, gw), lambda b, g, i: (b * nq + i, qcb + g)),
            pl.BlockSpec((ATT_TQ, LANES), lambda b, g, i: (b * nq + i, gcb + g)),
            pl.BlockSpec((1, HEAD_DIM), lambda b, g, i: (0, 0)),
            pl.BlockSpec((1, 1, nc, HEAD_DIM), lambda b, g, i: (b, g, 0, 0)),
            pl.BlockSpec((1, 1, HEAD_DIM, nc), lambda b, g, i: (b, g, 0, 0)),
            keys, vals, keys, vals,
            pl.BlockSpec((n_slc, nc), lambda b, g, i: (0, 0)),
        ],
        out_specs=pl.BlockSpec((ATT_TQ, gw), lambda b, g, i: (b * nq + i, g)),
        out_shape=jax.ShapeDtypeStruct((B * T, N_KV * gw), BF16),
        scratch_shapes=[
            pltpu.VMEM((n_slc, ATT_TQ), F32),
            pltpu.VMEM((2 * HEAD_DIM, n_rep * ATT_TQ), BF16),
            pltpu.VMEM((ATT_TQ, n_rep * ATT_TQ), F32),
            pltpu.VMEM((ATT_TQ, n_rep * ATT_TQ), F32),
            pltpu.VMEM((1, n_rep * ATT_TQ), F32),
            pltpu.VMEM((V_ROWS, n_rep * ATT_TQ), F32),
            pltpu.SMEM((nq + 1,), jnp.int32),
        ],
        compiler_params=_cparams("parallel", "parallel", "arbitrary"),
        name="nsa_attention",
    )(slopes, proj, proj, q_gain, cmp_k, cmp_vt, ks_aug, vs_t, kw_aug, vw_t, ovl_t)


def _out_kernel(a_ref, o_ref, w_ref, x_ref, g_ref, y_ref):
    ka = a_ref.shape[1]
    acc = _dot(a_ref[...], w_ref[0:ka, :]) + _dot(o_ref[...], w_ref[ka:, :])
    y_ref[...] = x_ref[...] + g_ref[0] * acc


def _out_call(a, o, w_all, layer, xf, gate, T):
    M, D = xf.shape
    ka, ko = a.shape[1], o.shape[1]
    per_b = T // OUT_BM
    return pl.pallas_call(
        _out_kernel,
        grid=(M // OUT_BM, D // OUT_BN),
        in_specs=[
            pl.BlockSpec((OUT_BM, ka), lambda i, n: (i, 0)),
            pl.BlockSpec((OUT_BM, ko), lambda i, n: (i, 0)),
            pl.BlockSpec((None, ka + ko, OUT_BN), lambda i, n: (layer, 0, n)),
            pl.BlockSpec((OUT_BM, OUT_BN), lambda i, n: (i, n)),
            pl.BlockSpec((1, 1, OUT_BN), lambda i, n: (i // per_b, 0, n)),
        ],
        out_specs=pl.BlockSpec((OUT_BM, OUT_BN), lambda i, n: (i, n)),
        out_shape=jax.ShapeDtypeStruct((M, D), F32),
        compiler_params=_cparams("parallel", "arbitrary"),
        name="out_proj",
    )(a, o, w_all, xf, gate)


def _ffn1_kernel(x_ref, ng_ref, sc_ref, sh_ref, wg_ref, wu_ref, o_ref, h_sc, *, n_chunks):
    def emit(h):
        gate = _dot(h, wg_ref[...])
        up = _dot(h, wu_ref[...])
        o_ref[...] = (gate * _sigmoid(gate) * up).astype(o_ref.dtype)

    _norm_matmul_steps(x_ref, ng_ref, sc_ref, sh_ref, h_sc, emit, n_chunks)


def _ffn1_call(xf, ng, sc, sh, w_gu, layer, T):
    M, D = xf.shape
    dff = w_gu.shape[2] // 2
    per_b = T // FFN1_BM
    nt = dff // FFN1_BN
    nm = M // FFN1_BM
    n_chunks = _norm_chunks(nt)
    row = _lookahead_row_tile(nm)
    return pl.pallas_call(
        functools.partial(_ffn1_kernel, n_chunks=n_chunks),
        grid=(nm, nt),
        in_specs=[
            pl.BlockSpec((FFN1_BM, D), lambda i, n: (row(i, n), 0)),
            pl.BlockSpec((1, D), lambda i, n: (0, 0)),
            pl.BlockSpec((1, 1, D), lambda i, n: (row(i, n) // per_b, 0, 0)),
            pl.BlockSpec((1, 1, D), lambda i, n: (row(i, n) // per_b, 0, 0)),
            pl.BlockSpec((None, D, FFN1_BN), lambda i, n: (layer, 0, n)),
            pl.BlockSpec((None, D, FFN1_BN), lambda i, n: (layer, 0, n + nt)),
        ],
        out_specs=pl.BlockSpec((FFN1_BM, FFN1_BN), lambda i, n: (i, n)),
        out_shape=jax.ShapeDtypeStruct((M, dff), BF16),
        scratch_shapes=[pltpu.VMEM((2, FFN1_BM, D), BF16)],
        compiler_params=_cparams("arbitrary", "arbitrary"),
        name="ffn_up",
    )(xf, ng, sc, sh, w_gu, w_gu)


def _ffn2_kernel(h_ref, w_ref, x_ref, g_ref, y_ref):
    y_ref[...] = x_ref[...] + g_ref[0] * _dot(h_ref[...], w_ref[...])


def _ffn2_call(h, w_all, layer, xf, gate, T):
    M, D = xf.shape
    dff = h.shape[1]
    per_b = T // FFN2_BM
    return pl.pallas_call(
        _ffn2_kernel,
        grid=(M // FFN2_BM, D // FFN2_BN),
        in_specs=[
            pl.BlockSpec((FFN2_BM, dff), lambda i, n: (i, 0)),
            pl.BlockSpec((None, dff, FFN2_BN), lambda i, n: (layer, 0, n)),
            pl.BlockSpec((FFN2_BM, FFN2_BN), lambda i, n: (i, n)),
            pl.BlockSpec((1, 1, FFN2_BN), lambda i, n: (i // per_b, 0, n)),
        ],
        out_specs=pl.BlockSpec((FFN2_BM, FFN2_BN), lambda i, n: (i, n)),
        out_shape=jax.ShapeDtypeStruct((M, D), F32),
        compiler_params=_cparams("parallel", "arbitrary"),
        name="ffn_down",
    )(h, w_all, xf, gate)


def _alibi_slopes(n_heads):
    sl = 2.0 ** (-8.0 * np.arange(1, n_heads + 1) / n_heads)
    return jnp.asarray(sl, F32).reshape(N_KV, n_heads // N_KV)


def _overlap_t(T):
    nc = T // CMP_STRIDE
    n_slc = T // SLC_LEN
    cst = np.arange(nc) * CMP_STRIDE
    sst = np.arange(n_slc) * SLC_LEN
    ov = (cst[None, :] < sst[:, None] + SLC_LEN) & (cst[None, :] + CMP_LEN > sst[:, None])
    ov[:, (T - CMP_LEN) // CMP_STRIDE + 1:] = False
    return jnp.asarray(ov.astype(np.float32), BF16)


def kernel(x, c, w_ada, b_ada, norm_g, w_in, q_gain, k_gain, pe_cmp, w_cmp1, w_cmp2,
           w_pool, pool_scale, w_out, w_gate_up, w_down):
    B, T, D = x.shape
    L = w_ada.shape[0]
    pool_w = w_pool.shape[1] * w_pool.shape[2]
    kvw = N_KV * HEAD_DIM
    n_heads = (w_in.shape[2] - pool_w - 6 * kvw) // (HEAD_DIM + 3)
    att_w = n_heads * HEAD_DIM
    n_rep = n_heads // N_KV
    assert w_in.shape[2] == pool_w + att_w + 6 * kvw + 3 * n_heads
    assert T % ATT_TQ == 0 and T >= WIN + ATT_TQ and T % POOL_BT == 0 and T % KPREP_BT == 0
    assert T // SLC_LEN <= AUG_BLK_COL and pool_w % (n_rep * HEAD_DIM) == 0
    assert 1 << SLC_SHIFT == SLC_LEN and ATT_TQ & (ATT_TQ - 1) == 0
    q_col0 = pool_w
    kc_col0 = pool_w + att_w
    ks_col0 = kc_col0 + 2 * kvw
    gate_col0 = kc_col0 + 6 * kvw
    assert ks_col0 % kvw == 0 and gate_col0 % LANES == 0

    xf = x.reshape(B * T, D)
    rows = -(-B // 8) * 8
    c8 = jnp.pad(c, ((0, rows - B), (0, 0)))
    mod = _ada_call(c8, w_ada, b_ada)
    slopes = _alibi_slopes(n_heads)
    ovl_t = _overlap_t(T)

    per_group_gates = 3 * n_rep
    gate_blocks = [jnp.pad(w_in[:, :, gate_col0 + g * per_group_gates:gate_col0 + (g + 1) * per_group_gates],
                           ((0, 0), (0, 0), (0, LANES - per_group_gates))) for g in range(N_KV)]
    w_in_p = jnp.concatenate([w_in[:, :, :gate_col0]] + gate_blocks, axis=2)
    w_in_p = jnp.pad(w_in_p, ((0, 0), (0, 0), (0, -w_in_p.shape[2] % IN_BN))).astype(BF16)
    w_pool_b = w_pool.astype(BF16)
    pool_scale_r = pool_scale.reshape(L, 1, pool_w)
    w_out_b = w_out.astype(BF16)
    w_gu_b = w_gate_up.astype(BF16)
    w_down_b = w_down.astype(BF16)

    for l in range(L):
        sh1, sc1, g1, sh2, sc2, g2 = [mod[l, :B, k * D:(k + 1) * D].reshape(B, 1, D) for k in range(6)]
        proj = _in_call(xf, norm_g[l, 0:1], sc1, sh1, w_in_p, l, T)
        a_out = _pool_call(proj, w_pool_b, pool_scale_r, l, T)
        ks_aug, vs_t, kw_aug, vw_t = _kprep_call(proj, k_gain[l], B, T, ks_col0)
        pe2 = pe_cmp[l].reshape(2, 2, (CMP_LEN // 2) * HEAD_DIM)
        cmp_k, cmp_vt = _cmp_call(proj, pe2, w_cmp1[l].astype(BF16), w_cmp2[l].astype(BF16), k_gain[l],
                                  B, T, kc_col0)
        o_att = _attn_call(proj, q_gain[l].reshape(1, HEAD_DIM), cmp_k, cmp_vt, ks_aug, vs_t, kw_aug, vw_t,
                           slopes, ovl_t, B, T, q_col0, gate_col0)
        xf = _out_call(a_out, o_att, w_out_b, l, xf, g1, T)
        hidden = _ffn1_call(xf, norm_g[l, 1:2], sc2, sh2, w_gu_b, l, T)
        xf = _ffn2_call(hidden, w_down_b, l, xf, g2, T)
    return xf.reshape(B, T, D)
```

```python
import functools

import numpy as np
import jax
import jax.numpy as jnp
from jax import lax
from jax.experimental import pallas as pl
from jax.experimental.pallas import tpu as pltpu

F32 = jnp.float32
BF16 = jnp.bfloat16

POOL_WINDOWS = (2, 4, 8, 16)
HEAD_DIM = 128
N_KV = 2
CMP_LEN = 32
CMP_STRIDE = 16
SLC_LEN = 64
SLC_SHIFT = 6
SLC_TOPK = 16
WIN = 512
NORM_EPS = 1e-6
NEG_INF = -1e30
FORCE_BONUS = 1e3

LANES = 128
POOL_HALO = 16
VMEM_LIMIT_BYTES = 56 * 1024 * 1024

ADA_BN = 1024
IN_BM, IN_BN = 1024, 1280
KPREP_BT = 1024
POOL_BT = 1024
ATT_TQ = 256
OUT_BM = 512
OUT_ROW_CHUNKS = 2
FFN1_BM, FFN1_BN = 1024, 512
FFN2_BM, FFN2_BN = 1024, 512


def _cparams(*sem):
    return pltpu.CompilerParams(dimension_semantics=sem, vmem_limit_bytes=VMEM_LIMIT_BYTES)


def _dot(a, b):
    return jnp.dot(a, b, preferred_element_type=F32)


def _dot_nt(a, b):
    return lax.dot_general(a, b, (((1,), (1,)), ((), ())), preferred_element_type=F32)


def _rms(x):
    return x * lax.rsqrt(jnp.mean(x * x, axis=-1, keepdims=True) + NORM_EPS)


def _sigmoid(x):
    return 1.0 / (1.0 + jnp.exp(-x))


def _ada_kernel(c_ref, w_ref, b_ref, o_ref):
    c = c_ref[...]
    cs = c * _sigmoid(c)
    o_ref[0] = _dot(cs, w_ref[0]) + b_ref[0]


def _ada_call(c8, w_ada, b_ada):
    L, D, N = w_ada.shape
    rows = c8.shape[0]
    return pl.pallas_call(
        _ada_kernel,
        grid=(L, N // ADA_BN),
        in_specs=[
            pl.BlockSpec((rows, D), lambda l, n: (0, 0)),
            pl.BlockSpec((1, D, ADA_BN), lambda l, n: (l, 0, n)),
            pl.BlockSpec((1, 1, ADA_BN), lambda l, n: (l, 0, n)),
        ],
        out_specs=pl.BlockSpec((1, rows, ADA_BN), lambda l, n: (l, 0, n)),
        out_shape=jax.ShapeDtypeStruct((L, rows, N), F32),
        compiler_params=_cparams("parallel", "arbitrary"),
        name="ada_mod",
    )(c8, w_ada, b_ada.reshape(L, 1, N))


def _norm_mod(x, ng_ref, sc_ref, sh_ref):
    y = _rms(x) * ng_ref[...]
    return (y * (1.0 + sc_ref[0]) + sh_ref[0]).astype(BF16)


def _lookahead_row_tile(n_row_tiles):
    def idx(i, n):
        return jnp.where((i == 0) & (n == 0), 0, jnp.minimum(i + 1, n_row_tiles - 1))
    return idx


def _norm_chunks(n_col_steps):
    assert n_col_steps >= 2
    return 1 << ((n_col_steps - 1).bit_length() - 1)


def _norm_matmul_steps(x_ref, ng_ref, sc_ref, sh_ref, h_sc, emit, n_chunks):
    i, n = pl.program_id(0), pl.program_id(1)
    rows = x_ref.shape[0] // n_chunks
    slab_step = (n >= 1) & (n <= n_chunks)

    @pl.when((i == 0) & (n == 0))
    def _():
        h_sc[0] = _norm_mod(x_ref[...], ng_ref, sc_ref, sh_ref)

    @pl.when(jnp.logical_not(slab_step))
    def _():
        emit(h_sc[i % 2])

    for slot in (0, 1):
        @pl.when(slab_step & (i % 2 == slot))
        def _(slot=slot):
            emit(h_sc[slot])
            slab = pl.ds(pl.multiple_of((n - 1) * rows, rows), rows)
            h_sc[1 - slot, slab, :] = _norm_mod(x_ref[slab, :], ng_ref, sc_ref, sh_ref)


def _in_kernel(x_ref, ng_ref, sc_ref, sh_ref, w_ref, o_ref, h_sc, *, n_chunks):
    def emit(h):
        o_ref[...] = _dot(h, w_ref[...]).astype(o_ref.dtype)

    _norm_matmul_steps(x_ref, ng_ref, sc_ref, sh_ref, h_sc, emit, n_chunks)


def _in_call(xf, ng, sc, sh, w_all, layer, T):
    M, D = xf.shape
    N = w_all.shape[2]
    per_b = T // IN_BM
    nm = M // IN_BM
    n_chunks = _norm_chunks(N // IN_BN)
    row = _lookahead_row_tile(nm)
    return pl.pallas_call(
        functools.partial(_in_kernel, n_chunks=n_chunks),
        grid=(nm, N // IN_BN),
        in_specs=[
            pl.BlockSpec((IN_BM, D), lambda i, n: (row(i, n), 0)),
            pl.BlockSpec((1, D), lambda i, n: (0, 0)),
            pl.BlockSpec((1, 1, D), lambda i, n: (row(i, n) // per_b, 0, 0)),
            pl.BlockSpec((1, 1, D), lambda i, n: (row(i, n) // per_b, 0, 0)),
            pl.BlockSpec((None, D, IN_BN), lambda i, n: (layer, 0, n)),
        ],
        out_specs=pl.BlockSpec((IN_BM, IN_BN), lambda i, n: (i, n)),
        out_shape=jax.ShapeDtypeStruct((M, N), BF16),
        scratch_shapes=[pltpu.VMEM((2, IN_BM, D), BF16)],
        compiler_params=_cparams("arbitrary", "arbitrary"),
        name="in_proj",
    )(xf, ng, sc, sh, w_all)


def _pool_kernel(u_ref, halo_ref, w_ref, ps_ref, o_ref, ext_sc, *, tiles_per_batch):
    i = pl.program_id(0)
    bt = u_ref.shape[0]
    group = w_ref.shape[1]
    tile_in_batch = i % tiles_per_batch
    ext_sc[0:POOL_HALO, :] = jnp.where(tile_in_batch == 0, 0.0, halo_ref[...].astype(F32))
    ext_sc[POOL_HALO:, :] = u_ref[...].astype(F32)
    t = tile_in_batch * bt + lax.broadcasted_iota(jnp.int32, (bt, 1), 0)
    for gi, w in enumerate(POOL_WINDOWS):
        cols = slice(gi * group, (gi + 1) * group)
        acc = ext_sc[:, cols]
        k = 1
        while k < w:
            acc = acc + pltpu.roll(acc, k, 0)
            k *= 2
        cnt = jnp.minimum(t + 1, w).astype(F32)
        pooled = acc[POOL_HALO:, :] / cnt - ext_sc[POOL_HALO:, cols]
        mixed = _dot(pooled.astype(BF16), w_ref[gi])
        o_ref[:, cols] = (mixed * ps_ref[:, cols]).astype(o_ref.dtype)


def _pool_call(proj, w_pool_all, pool_scale_all, layer, T):
    M = proj.shape[0]
    _, ng, group, _ = w_pool_all.shape
    width = ng * group
    tiles_per_batch = T // POOL_BT
    halo_blocks = POOL_BT // POOL_HALO
    return pl.pallas_call(
        functools.partial(_pool_kernel, tiles_per_batch=tiles_per_batch),
        grid=(M // POOL_BT,),
        in_specs=[
            pl.BlockSpec((POOL_BT, width), lambda i: (i, 0)),
            pl.BlockSpec((POOL_HALO, width), lambda i: (jnp.maximum(i * halo_blocks - 1, 0), 0)),
            pl.BlockSpec((None, ng, group, group), lambda i: (layer, 0, 0, 0)),
            pl.BlockSpec((None, 1, width), lambda i: (layer, 0, 0)),
        ],
        out_specs=pl.BlockSpec((POOL_BT, width), lambda i: (i, 0)),
        out_shape=jax.ShapeDtypeStruct((M, width), BF16),
        scratch_shapes=[pltpu.VMEM((POOL_BT + POOL_HALO, width), F32)],
        compiler_params=_cparams("parallel"),
        name="pool_mixer",
    )(proj, proj, w_pool_all, pool_scale_all)


AUG_SPLIT = 3
AUG_BLK_COL = SLC_LEN
AUG_OFF_COL = SLC_LEN + AUG_SPLIT
V_ROWS = HEAD_DIM + 16
LOG2E = 1.4426950408889634


def _key_extra(t):
    lane = lax.broadcasted_iota(jnp.int32, (t.shape[0], LANES), 1)
    blk = lax.shift_right_logical(t, SLC_SHIFT)
    off = t & (SLC_LEN - 1)
    extra = jnp.where(lane == blk, 1.0, 0.0)
    extra = jnp.where((lane >= AUG_BLK_COL) & (lane < AUG_OFF_COL), blk.astype(F32), extra)
    extra = jnp.where((lane >= AUG_OFF_COL) & (lane < AUG_OFF_COL + AUG_SPLIT), off.astype(F32), extra)
    return extra


def _value_tile_t(v_t):
    pad = lax.broadcasted_iota(jnp.int32, (V_ROWS - HEAD_DIM, v_t.shape[1]), 0)
    return jnp.concatenate([v_t, jnp.where(pad == 0, 1.0, 0.0)], axis=0).astype(BF16)


def _kprep_kernel(ks_ref, vs_ref, kw_ref, vw_ref, kg_ref, ksa_ref, vso_ref, kwa_ref, vwo_ref,
                  *, tiles_per_batch):
    bt = ks_ref.shape[0]
    kt = vso_ref.shape[4]
    t = (pl.program_id(0) % tiles_per_batch) * bt + lax.broadcasted_iota(jnp.int32, (bt, 1), 0)
    extra = _key_extra(t).astype(BF16)
    for g in range(N_KV):
        cols = slice(g * HEAD_DIM, (g + 1) * HEAD_DIM)
        ksn = _rms(ks_ref[:, cols].astype(F32)) * kg_ref[1:2, :]
        kwn = _rms(kw_ref[:, cols].astype(F32)) * kg_ref[2:3, :]
        ksa_ref[0, g, :, 0:HEAD_DIM] = ksn.astype(BF16)
        ksa_ref[0, g, :, HEAD_DIM:] = extra
        kwa_ref[0, g, :, 0:HEAD_DIM] = kwn.astype(BF16)
        kwa_ref[0, g, :, HEAD_DIM:] = extra
        vs_t = _value_tile_t(vs_ref[:, cols].astype(F32).T)
        vw_t = _value_tile_t(vw_ref[:, cols].astype(F32).T)
        for j in range(bt // kt):
            vso_ref[0, g, j] = vs_t[:, j * kt:(j + 1) * kt]
            vwo_ref[0, g, j] = vw_t[:, j * kt:(j + 1) * kt]


def _kprep_call(proj, k_gain, B, T, col0):
    kvw = N_KV * HEAD_DIM
    cb = col0 // kvw
    tiles_per_batch = T // KPREP_BT
    vt_per_tile = KPREP_BT // ATT_TQ
    aug = jax.ShapeDtypeStruct((B, N_KV, T, 2 * HEAD_DIM), BF16)
    val = jax.ShapeDtypeStruct((B, N_KV, T // ATT_TQ, V_ROWS, ATT_TQ), BF16)
    in_spec = lambda j: pl.BlockSpec((KPREP_BT, kvw), lambda i: (i, cb + j))
    out_map = lambda i: (i // tiles_per_batch, 0, i % tiles_per_batch, 0)
    val_map = lambda i: (i // tiles_per_batch, 0, i % tiles_per_batch, 0, 0)
    return pl.pallas_call(
        functools.partial(_kprep_kernel, tiles_per_batch=tiles_per_batch),
        grid=(B * tiles_per_batch,),
        in_specs=[in_spec(0), in_spec(1), in_spec(2), in_spec(3),
                  pl.BlockSpec((3, HEAD_DIM), lambda i: (0, 0))],
        out_specs=[pl.BlockSpec((1, N_KV, KPREP_BT, 2 * HEAD_DIM), out_map),
                   pl.BlockSpec((1, N_KV, vt_per_tile, V_ROWS, ATT_TQ), val_map),
                   pl.BlockSpec((1, N_KV, KPREP_BT, 2 * HEAD_DIM), out_map),
                   pl.BlockSpec((1, N_KV, vt_per_tile, V_ROWS, ATT_TQ), val_map)],
        out_shape=[aug, val, aug, val],
        compiler_params=_cparams("parallel"),
        name="kv_prep",
    )(proj, proj, proj, proj, k_gain)


def _compress_one(src_ref, f32_sc, pe_ref, w1_ref, kv, nc):
    half = CMP_LEN // 2
    assert CMP_STRIDE == half
    f32_sc[...] = src_ref[...].astype(F32)
    xs = [f32_sc[pl.ds(j, nc, stride=CMP_STRIDE), :] for j in range(half)]
    x = jnp.concatenate(xs, axis=1)
    kdim = half * HEAD_DIM
    lo = _dot((x + pe_ref[kv, 0:1, :]).astype(BF16), w1_ref[kv, 0:kdim, :])
    hi = _dot((x + pe_ref[kv, 1:2, :]).astype(BF16), w1_ref[kv, kdim:, :])
    pre = lo + pltpu.roll(hi, nc - 1, 0)
    return (pre * _sigmoid(pre)).astype(BF16)


def _cmp_kernel(k_ref, v_ref, pe_ref, w1_ref, w2k_ref, w2vt_ref, kg_ref, ko_ref, vo_ref, f32_sc):
    nc = ko_ref.shape[2]
    kc = _dot(_compress_one(k_ref, f32_sc, pe_ref, w1_ref, 0, nc), w2k_ref[...])
    ko_ref[0, 0] = (_rms(kc) * kg_ref[0:1, :]).astype(ko_ref.dtype)
    vct = _dot_nt(w2vt_ref[...], _compress_one(v_ref, f32_sc, pe_ref, w1_ref, 1, nc))
    vo_ref[0, 0] = vct.astype(vo_ref.dtype)


def _cmp_call(proj, pe2, w1, w2, k_gain, B, T, col0):
    nc = T // CMP_STRIDE
    cb = col0 // HEAD_DIM
    kdim = CMP_LEN * HEAD_DIM
    return pl.pallas_call(
        _cmp_kernel,
        grid=(B, N_KV),
        in_specs=[
            pl.BlockSpec((T, HEAD_DIM), lambda b, g: (b, cb + g)),
            pl.BlockSpec((T, HEAD_DIM), lambda b, g: (b, cb + N_KV + g)),
            pl.BlockSpec((2, 2, kdim // 2), lambda b, g: (0, 0, 0)),
            pl.BlockSpec((2, kdim, HEAD_DIM), lambda b, g: (0, 0, 0)),
            pl.BlockSpec((HEAD_DIM, HEAD_DIM), lambda b, g: (0, 0)),
            pl.BlockSpec((HEAD_DIM, HEAD_DIM), lambda b, g: (0, 0)),
            pl.BlockSpec((3, HEAD_DIM), lambda b, g: (0, 0)),
        ],
        out_specs=[pl.BlockSpec((1, 1, nc, HEAD_DIM), lambda b, g: (b, g, 0, 0)),
                   pl.BlockSpec((1, 1, HEAD_DIM, nc), lambda b, g: (b, g, 0, 0))],
        out_shape=[jax.ShapeDtypeStruct((B, N_KV, nc, HEAD_DIM), BF16),
                   jax.ShapeDtypeStruct((B, N_KV, HEAD_DIM, nc), BF16)],
        scratch_shapes=[pltpu.VMEM((T, HEAD_DIM), F32)],
        compiler_params=_cparams("parallel", "arbitrary"),
        name="compress",
    )(proj, proj, pe2, w1, w2[0], w2[1].T, k_gain)


MAX_FLOOR = 0.1 * NEG_INF


def _exp2_cols(s, mask):
    s = jnp.where(mask, s, NEG_INF)
    m = jnp.maximum(jnp.max(s, axis=0, keepdims=True), MAX_FLOOR)
    return jnp.exp2(s - m)


def _split3(c, shape):
    c = jnp.full(shape, c, F32)
    c1 = c.astype(BF16).astype(F32)
    r1 = c - c1
    c2 = r1.astype(BF16).astype(F32)
    c3 = (r1 - c2).astype(BF16).astype(F32)
    return c1, c2, c3


def _attn_kernel(slope_ref, q_ref, gl_ref, qg_ref, cmp_k_ref, cmp_vt_ref, ksa_ref, vst_ref,
                 kwa_ref, vwt_ref, ovl_ref, o_ref, score_sc, qs_sc, sa_sc, sb_sc, m_sc, acc_sc, seq_ref,
                 *, n_rep):
    g = pl.program_id(1)
    i = pl.program_id(2)
    tq = q_ref.shape[0]
    nc = cmp_k_ref.shape[2]
    n_slc = ovl_ref.shape[0]
    t0 = i * tq
    slopes = [slope_ref[g, r] * LOG2E for r in range(n_rep)]
    scale = HEAD_DIM ** -0.5 * LOG2E
    head = lambda a, r: a[:, r * tq:(r + 1) * tq]

    qt = []
    for r in range(n_rep):
        x = q_ref[:, r * HEAD_DIM:(r + 1) * HEAD_DIM].astype(F32)
        qt.append((_rms(x) * qg_ref[...] * scale).T.astype(BF16))
    qt_all = jnp.concatenate(qt, axis=1)

    s_all = _dot(cmp_k_ref[0, 0], qt_all)
    c_idx = lax.broadcasted_iota(jnp.int32, (nc, tq), 0)
    t_idx = lax.broadcasted_iota(jnp.int32, (nc, tq), 1) + t0
    dist_c = (t_idx - (c_idx * CMP_STRIDE + (CMP_LEN - 1))).astype(F32)
    mask_c = dist_c >= 0.0
    p_cmp = []
    p_sum = jnp.zeros((nc, tq), F32)
    for r in range(n_rep):
        p = _exp2_cols(head(s_all, r) - slopes[r] * dist_c, mask_c)
        l = jnp.sum(p, axis=0, keepdims=True)
        p = p * (1.0 / jnp.where(l > 0.0, l, 1.0))
        p_cmp.append(p.astype(BF16))
        p_sum = p_sum + p
    o_cmp_t = _dot(cmp_vt_ref[0, 0], jnp.concatenate(p_cmp, axis=1))

    p_hi = p_sum.astype(BF16)
    p_lo = (p_sum - p_hi.astype(F32)).astype(BF16)
    ovl = ovl_ref[...]
    imp = _dot(ovl, p_hi) + _dot(ovl, p_lo)
    jb = lax.broadcasted_iota(jnp.int32, (n_slc, tq), 0)
    tt = lax.broadcasted_iota(jnp.int32, (n_slc, tq), 1) + t0
    cur = lax.shift_right_logical(tt, SLC_SHIFT)
    forced = (jb == 0) | (jb == cur) | (jb == cur - 1)
    score = jnp.where(jb * SLC_LEN <= tt, imp + jnp.where(forced, FORCE_BONUS, 0.0), NEG_INF)
    score_sc[...] = score
    sub = 8
    groups = [score[sub * rg:sub * (rg + 1)] for rg in range(n_slc // sub)]
    ranks = [jnp.zeros((sub, tq), F32) for _ in groups]
    jrow = lax.broadcasted_iota(jnp.int32, (sub, tq), 0)
    for b2 in range(n_slc):
        sb = jnp.broadcast_to(score_sc[b2:b2 + 1, :], (sub, tq))
        for rg, sg in enumerate(groups):
            if sub * rg > b2:
                beats = sb >= sg
            elif sub * rg + sub - 1 < b2:
                beats = sb > sg
            else:
                beats = (sb > sg) | ((sb == sg) & (jrow > b2 - sub * rg))
            ranks[rg] = ranks[rg] + jnp.where(beats, 1.0, 0.0)
    n_sel = min(SLC_TOPK, n_slc)
    sel_bias_t = jnp.concatenate([jnp.where(rk < n_sel, 0.0, NEG_INF) for rk in ranks], axis=0)

    bpt = tq // SLC_LEN
    n_list = jnp.int32(0)
    for kt in range(n_slc // bpt - 1):
        rg, off = divmod(kt * bpt, sub)
        hit = jnp.max(jnp.where(ranks[rg][off:off + bpt] < n_sel, 1.0, 0.0), axis=1, keepdims=True)
        hit = jnp.max(hit, axis=0, keepdims=True)[0, 0]
        seq_ref[n_list] = jnp.int32(kt)
        n_list = n_list + ((hit > 0.0) & (kt < i)).astype(jnp.int32)
    seq_ref[n_list] = i

    pshape = (LANES - AUG_BLK_COL, tq)
    frow = lax.broadcasted_iota(jnp.int32, pshape, 0)
    sel_rows = sel_bias_t.astype(BF16)
    if n_slc < AUG_BLK_COL:
        sel_rows = jnp.concatenate([sel_rows, jnp.zeros((AUG_BLK_COL - n_slc, tq), BF16)], axis=0)
    q_slc, q_win = [], []
    for r in range(n_rep):
        pos_rows = jnp.zeros(pshape, F32)
        for k, ck in enumerate(_split3(slopes[r], pshape)):
            pos_rows = jnp.where(frow == k, ck * SLC_LEN, pos_rows)
            pos_rows = jnp.where(frow == AUG_SPLIT + k, ck, pos_rows)
        pos_rows = pos_rows.astype(BF16)
        q_slc.append(jnp.concatenate([qt[r], sel_rows, pos_rows], axis=0))
        q_win.append(jnp.concatenate([qt[r], jnp.zeros_like(sel_rows), pos_rows], axis=0))
    qs_sc[...] = jnp.concatenate(q_slc, axis=1)
    q_win = jnp.concatenate(q_win, axis=1)

    n_wt = WIN // tq + 1
    span = n_wt * tq
    j0 = jnp.maximum(i - (n_wt - 1), 0)
    start_w = pl.multiple_of(j0 * tq, tq)
    s_w = _dot(kwa_ref[0, 0, pl.ds(start_w, span), :], q_win)
    d_w = (lax.broadcasted_iota(jnp.int32, (span, tq), 1) + t0) - \
          (lax.broadcasted_iota(jnp.int32, (span, tq), 0) + start_w)
    mask_w = (d_w >= 0) & (d_w < WIN)
    p_w = jnp.concatenate([_exp2_cols(head(s_w, r), mask_w).astype(BF16) for r in range(n_rep)], axis=1)
    o_win_t = _dot(vwt_ref[0, 0, j0], p_w[0:tq])
    for jj in range(1, n_wt):
        o_win_t = o_win_t + _dot(vwt_ref[0, 0, j0 + jj], p_w[jj * tq:(jj + 1) * tq])
    o_win_t = o_win_t[0:HEAD_DIM] * (1.0 / o_win_t[HEAD_DIM:HEAD_DIM + 1])

    m_sc[...] = jnp.full(m_sc.shape, NEG_INF, F32)
    acc_sc[...] = jnp.zeros(acc_sc.shape, F32)

    def produce(kt, buf):
        start = pl.multiple_of(kt * tq, tq)
        buf[...] = _dot(ksa_ref[0, 0, pl.ds(start, tq), :], qs_sc[...])

    def consume(kt, buf, causal):
        s = buf[...]
        if causal:
            kpos = lax.broadcasted_iota(jnp.int32, s.shape, 0)
            qpos = lax.broadcasted_iota(jnp.int32, s.shape, 1) & (tq - 1)
            s = jnp.where(kpos <= qpos, s, NEG_INF)
        m_old = m_sc[...]
        m_new = jnp.maximum(m_old, jnp.max(s, axis=0, keepdims=True))
        alpha = jnp.exp2(m_old - m_new)
        p = jnp.exp2(s - m_new)
        acc_sc[...] = alpha * acc_sc[...] + _dot(vst_ref[0, 0, kt], p.astype(BF16))
        m_sc[...] = m_new

    def pair(j, carry):
        produce(seq_ref[2 * j + 1], sb_sc)
        consume(seq_ref[2 * j], sa_sc, False)
        produce(seq_ref[2 * j + 2], sa_sc)
        consume(seq_ref[2 * j + 1], sb_sc, False)
        return carry

    produce(seq_ref[0], sa_sc)
    lax.fori_loop(0, n_list // 2, pair, 0)

    @pl.when(n_list % 2 == 1)
    def _():
        produce(i, sb_sc)
        consume(seq_ref[n_list - 1], sa_sc, False)
        consume(i, sb_sc, True)

    @pl.when(n_list % 2 == 0)
    def _():
        consume(i, sa_sc, True)

    o_slc_t = acc_sc[0:HEAD_DIM, :] * (1.0 / acc_sc[HEAD_DIM:HEAD_DIM + 1, :])

    gates = _sigmoid(gl_ref[...].astype(F32).T)
    for r in range(n_rep):
        o_t = (gates[3 * r:3 * r + 1] * head(o_cmp_t, r)
               + gates[3 * r + 1:3 * r + 2] * head(o_slc_t, r)
               + gates[3 * r + 2:3 * r + 3] * head(o_win_t, r))
        o_ref[:, r * HEAD_DIM:(r + 1) * HEAD_DIM] = o_t.T.astype(o_ref.dtype)


def _attn_call(proj, q_gain, cmp_k, cmp_vt, ks_aug, vs_t, kw_aug, vw_t, slopes, ovl_t,
               B, T, q_col0, gate_col0):
    n_rep = slopes.shape[1]
    gw = n_rep * HEAD_DIM
    nq = T // ATT_TQ
    nc = cmp_k.shape[2]
    n_slc = ovl_t.shape[0]
    qcb = q_col0 // gw
    gcb = gate_col0 // LANES
    assert WIN % ATT_TQ == 0
    keys = pl.BlockSpec((1, 1, T, 2 * HEAD_DIM), lambda b, g, i: (b, g, 0, 0))
    vals = pl.BlockSpec((1, 1, nq, V_ROWS, ATT_TQ), lambda b, g, i: (b, g, 0, 0, 0))
    return pl.pallas_call(
        functools.partial(_attn_kernel, n_rep=n_rep),
        grid=(B, N_KV, nq),
        in_specs=[
            pl.BlockSpec(memory_space=pltpu.SMEM),
            pl.BlockSpec((ATT_TQ, gw), lambda b, g, i: (b * nq + i, qcb + g)),
            pl.BlockSpec((ATT_TQ, LANES), lambda b, g, i: (b * nq + i, gcb + g)),
            pl.BlockSpec((1, HEAD_DIM), lambda b, g, i: (0, 0)),
            pl.BlockSpec((1, 1, nc, HEAD_DIM), lambda b, g, i: (b, g, 0, 0)),
            pl.BlockSpec((1, 1, HEAD_DIM, nc), lambda b, g, i: (b, g, 0, 0)),
            keys, vals, keys, vals,
            pl.BlockSpec((n_slc, nc), lambda b, g, i: (0, 0)),
        ],
        out_specs=pl.BlockSpec((ATT_TQ, gw), lambda b, g, i: (b * nq + i, g)),
        out_shape=jax.ShapeDtypeStruct((B * T, N_KV * gw), BF16),
        scratch_shapes=[
            pltpu.VMEM((n_slc, ATT_TQ), F32),
            pltpu.VMEM((2 * HEAD_DIM, n_rep * ATT_TQ), BF16),
            pltpu.VMEM((ATT_TQ, n_rep * ATT_TQ), F32),
            pltpu.VMEM((ATT_TQ, n_rep * ATT_TQ), F32),
            pltpu.VMEM((1, n_rep * ATT_TQ), F32),
            pltpu.VMEM((V_ROWS, n_rep * ATT_TQ), F32),
            pltpu.SMEM((nq + 1,), jnp.int32),
        ],
        compiler_params=_cparams("parallel", "parallel", "arbitrary"),
        name="nsa_attention",
    )(slopes, proj, proj, q_gain, cmp_k, cmp_vt, ks_aug, vs_t, kw_aug, vw_t, ovl_t)


def _out_kernel(a_ref, o_ref, w_ref, x_ref, g_ref, ng_ref, sc_ref, sh_ref, y_ref, h_ref):
    ka = a_ref.shape[1]
    rc = a_ref.shape[0] // OUT_ROW_CHUNKS
    for c in range(OUT_ROW_CHUNKS):
        rows = slice(c * rc, (c + 1) * rc)
        acc = _dot(a_ref[rows, :], w_ref[0:ka, :]) + _dot(o_ref[rows, :], w_ref[ka:, :])
        y = x_ref[rows, :] + g_ref[0] * acc
        y_ref[rows, :] = y
        h_ref[rows, :] = _norm_mod(y, ng_ref, sc_ref, sh_ref)


def _out_call(a, o, w_all, layer, xf, gate, ng, sc, sh, T):
    M, D = xf.shape
    ka, ko = a.shape[1], o.shape[1]
    per_b = T // OUT_BM
    row_spec = lambda width: pl.BlockSpec((OUT_BM, width), lambda i: (i, 0))
    mod_spec = pl.BlockSpec((1, 1, D), lambda i: (i // per_b, 0, 0))
    return pl.pallas_call(
        _out_kernel,
        grid=(M // OUT_BM,),
        in_specs=[
            row_spec(ka), row_spec(ko),
            pl.BlockSpec((None, ka + ko, D), lambda i: (layer, 0, 0)),
            row_spec(D), mod_spec,
            pl.BlockSpec((1, D), lambda i: (0, 0)),
            mod_spec, mod_spec,
        ],
        out_specs=[row_spec(D), row_spec(D)],
        out_shape=[jax.ShapeDtypeStruct((M, D), F32), jax.ShapeDtypeStruct((M, D), BF16)],
        compiler_params=_cparams("parallel"),
        name="out_proj",
    )(a, o, w_all, xf, gate, ng, sc, sh)


def _ffn1_kernel(h_ref, wg_ref, wu_ref, o_ref):
    h = h_ref[...]
    gate = _dot(h, wg_ref[...])
    up = _dot(h, wu_ref[...])
    o_ref[...] = (gate * _sigmoid(gate) * up).astype(o_ref.dtype)


def _ffn1_call(h, w_gu, layer):
    M, D = h.shape
    dff = w_gu.shape[2] // 2
    nt = dff // FFN1_BN
    return pl.pallas_call(
        _ffn1_kernel,
        grid=(M // FFN1_BM, nt),
        in_specs=[
            pl.BlockSpec((FFN1_BM, D), lambda i, n: (i, 0)),
            pl.BlockSpec((None, D, FFN1_BN), lambda i, n: (layer, 0, n)),
            pl.BlockSpec((None, D, FFN1_BN), lambda i, n: (layer, 0, n + nt)),
        ],
        out_specs=pl.BlockSpec((FFN1_BM, FFN1_BN), lambda i, n: (i, n)),
        out_shape=jax.ShapeDtypeStruct((M, dff), BF16),
        compiler_params=_cparams("parallel", "arbitrary"),
        name="ffn_up",
    )(h, w_gu, w_gu)


def _ffn2_kernel(h_ref, w_ref, x_ref, g_ref, y_ref):
    y_ref[...] = x_ref[...] + g_ref[0] * _dot(h_ref[...], w_ref[...])


def _ffn2_call(h, w_all, layer, xf, gate, T):
    M, D = xf.shape
    dff = h.shape[1]
    per_b = T // FFN2_BM
    return pl.pallas_call(
        _ffn2_kernel,
        grid=(M // FFN2_BM, D // FFN2_BN),
        in_specs=[
            pl.BlockSpec((FFN2_BM, dff), lambda i, n: (i, 0)),
            pl.BlockSpec((None, dff, FFN2_BN), lambda i, n: (layer, 0, n)),
            pl.BlockSpec((FFN2_BM, FFN2_BN), lambda i, n: (i, n)),
            pl.BlockSpec((1, 1, FFN2_BN), lambda i, n: (i // per_b, 0, n)),
        ],
        out_specs=pl.BlockSpec((FFN2_BM, FFN2_BN), lambda i, n: (i, n)),
        out_shape=jax.ShapeDtypeStruct((M, D), F32),
        compiler_params=_cparams("parallel", "arbitrary"),
        name="ffn_down",
    )(h, w_all, xf, gate)


def _alibi_slopes(n_heads):
    sl = 2.0 ** (-8.0 * np.arange(1, n_heads + 1) / n_heads)
    return jnp.asarray(sl, F32).reshape(N_KV, n_heads // N_KV)


def _overlap_t(T):
    nc = T // CMP_STRIDE
    n_slc = T // SLC_LEN
    cst = np.arange(nc) * CMP_STRIDE
    sst = np.arange(n_slc) * SLC_LEN
    ov = (cst[None, :] < sst[:, None] + SLC_LEN) & (cst[None, :] + CMP_LEN > sst[:, None])
    ov[:, (T - CMP_LEN) // CMP_STRIDE + 1:] = False
    return jnp.asarray(ov.astype(np.float32), BF16)


def kernel(x, c, w_ada, b_ada, norm_g, w_in, q_gain, k_gain, pe_cmp, w_cmp1, w_cmp2,
           w_pool, pool_scale, w_out, w_gate_up, w_down):
    B, T, D = x.shape
    L = w_ada.shape[0]
    pool_w = w_pool.shape[1] * w_pool.shape[2]
    kvw = N_KV * HEAD_DIM
    n_heads = (w_in.shape[2] - pool_w - 6 * kvw) // (HEAD_DIM + 3)
    att_w = n_heads * HEAD_DIM
    n_rep = n_heads // N_KV
    assert w_in.shape[2] == pool_w + att_w + 6 * kvw + 3 * n_heads
    assert T % ATT_TQ == 0 and T >= WIN + ATT_TQ and T % POOL_BT == 0 and T % KPREP_BT == 0
    assert T // SLC_LEN <= AUG_BLK_COL and pool_w % (n_rep * HEAD_DIM) == 0
    assert 1 << SLC_SHIFT == SLC_LEN and ATT_TQ & (ATT_TQ - 1) == 0
    q_col0 = pool_w
    kc_col0 = pool_w + att_w
    ks_col0 = kc_col0 + 2 * kvw
    gate_col0 = kc_col0 + 6 * kvw
    assert ks_col0 % kvw == 0 and gate_col0 % LANES == 0

    xf = x.reshape(B * T, D)
    rows = -(-B // 8) * 8
    c8 = jnp.pad(c, ((0, rows - B), (0, 0)))
    mod = _ada_call(c8, w_ada, b_ada)
    slopes = _alibi_slopes(n_heads)
    ovl_t = _overlap_t(T)

    per_group_gates = 3 * n_rep
    gate_blocks = [jnp.pad(w_in[:, :, gate_col0 + g * per_group_gates:gate_col0 + (g + 1) * per_group_gates],
                           ((0, 0), (0, 0), (0, LANES - per_group_gates))) for g in range(N_KV)]
    w_in_p = jnp.concatenate([w_in[:, :, :gate_col0]] + gate_blocks, axis=2)
    w_in_p = jnp.pad(w_in_p, ((0, 0), (0, 0), (0, -w_in_p.shape[2] % IN_BN))).astype(BF16)
    w_pool_b = w_pool.astype(BF16)
    pool_scale_r = pool_scale.reshape(L, 1, pool_w)
    w_out_b = w_out.astype(BF16)
    w_gu_b = w_gate_up.astype(BF16)
    w_down_b = w_down.astype(BF16)

    for l in range(L):
        sh1, sc1, g1, sh2, sc2, g2 = [mod[l, :B, k * D:(k + 1) * D].reshape(B, 1, D) for k in range(6)]
        proj = _in_call(xf, norm_g[l, 0:1], sc1, sh1, w_in_p, l, T)
        a_out = _pool_call(proj, w_pool_b, pool_scale_r, l, T)
        ks_aug, vs_t, kw_aug, vw_t = _kprep_call(proj, k_gain[l], B, T, ks_col0)
        pe2 = pe_cmp[l].reshape(2, 2, (CMP_LEN // 2) * HEAD_DIM)
        cmp_k, cmp_vt = _cmp_call(proj, pe2, w_cmp1[l].astype(BF16), w_cmp2[l].astype(BF16), k_gain[l],
                                  B, T, kc_col0)
        o_att = _attn_call(proj, q_gain[l].reshape(1, HEAD_DIM), cmp_k, cmp_vt, ks_aug, vs_t, kw_aug, vw_t,
                           slopes, ovl_t, B, T, q_col0, gate_col0)
        xf, h2 = _out_call(a_out, o_att, w_out_b, l, xf, g1, norm_g[l, 1:2], sc2, sh2, T)
        hidden = _ffn1_call(h2, w_gu_b, l)
        xf = _ffn2_call(hidden, w_down_b, l, xf, g2, T)
    return xf.reshape(B, T, D)
```

```python
import functools

import numpy as np
import jax
import jax.numpy as jnp
from jax import lax
from jax.experimental import pallas as pl
from jax.experimental.pallas import tpu as pltpu

F32 = jnp.float32
BF16 = jnp.bfloat16

POOL_WINDOWS = (2, 4, 8, 16)
HEAD_DIM = 128
N_KV = 2
CMP_LEN = 32
CMP_STRIDE = 16
SLC_LEN = 64
SLC_SHIFT = 6
SLC_TOPK = 16
WIN = 512
NORM_EPS = 1e-6
NEG_INF = -1e30
FORCE_BONUS = 1e3

LANES = 128
POOL_HALO = 16
VMEM_LIMIT_BYTES = 56 * 1024 * 1024

ADA_BN = 1024
IN_BM, IN_BN = 1024, 1280
KPREP_BT = 1024
POOL_BT = 1024
ATT_TQ = 256
OUT_BM = 512
OUT_ROW_CHUNKS = 2
FFN1_BM, FFN1_BN = 1024, 512
FFN2_BM, FFN2_BN = 1024, 512


def _cparams(*sem):
    return pltpu.CompilerParams(dimension_semantics=sem, vmem_limit_bytes=VMEM_LIMIT_BYTES)


def _dot(a, b):
    return jnp.dot(a, b, preferred_element_type=F32)


def _dot_nt(a, b):
    return lax.dot_general(a, b, (((1,), (1,)), ((), ())), preferred_element_type=F32)


def _rms(x):
    return x * lax.rsqrt(jnp.mean(x * x, axis=-1, keepdims=True) + NORM_EPS)


def _sigmoid(x):
    return 1.0 / (1.0 + jnp.exp(-x))


def _ada_kernel(c_ref, w_ref, b_ref, o_ref):
    c = c_ref[...]
    cs = c * _sigmoid(c)
    o_ref[0] = _dot(cs, w_ref[0]) + b_ref[0]


def _ada_call(c8, w_ada, b_ada):
    L, D, N = w_ada.shape
    rows = c8.shape[0]
    return pl.pallas_call(
        _ada_kernel,
        grid=(L, N // ADA_BN),
        in_specs=[
            pl.BlockSpec((rows, D), lambda l, n: (0, 0)),
            pl.BlockSpec((1, D, ADA_BN), lambda l, n: (l, 0, n)),
            pl.BlockSpec((1, 1, ADA_BN), lambda l, n: (l, 0, n)),
        ],
        out_specs=pl.BlockSpec((1, rows, ADA_BN), lambda l, n: (l, 0, n)),
        out_shape=jax.ShapeDtypeStruct((L, rows, N), F32),
        compiler_params=_cparams("parallel", "arbitrary"),
        name="ada_mod",
    )(c8, w_ada, b_ada.reshape(L, 1, N))


def _norm_mod(x, ng_ref, sc_ref, sh_ref):
    y = _rms(x) * ng_ref[...]
    return (y * (1.0 + sc_ref[0]) + sh_ref[0]).astype(BF16)


def _lookahead_row_tile(n_row_tiles):
    def idx(i, n):
        return jnp.where((i == 0) & (n == 0), 0, jnp.minimum(i + 1, n_row_tiles - 1))
    return idx


def _norm_chunks(n_col_steps):
    assert n_col_steps >= 2
    return 1 << ((n_col_steps - 1).bit_length() - 1)


def _norm_matmul_steps(x_ref, ng_ref, sc_ref, sh_ref, h_sc, emit, n_chunks):
    i, n = pl.program_id(0), pl.program_id(1)
    rows = x_ref.shape[0] // n_chunks
    slab_step = (n >= 1) & (n <= n_chunks)

    @pl.when((i == 0) & (n == 0))
    def _():
        h_sc[0] = _norm_mod(x_ref[...], ng_ref, sc_ref, sh_ref)

    @pl.when(jnp.logical_not(slab_step))
    def _():
        emit(h_sc[i % 2])

    for slot in (0, 1):
        @pl.when(slab_step & (i % 2 == slot))
        def _(slot=slot):
            emit(h_sc[slot])
            slab = pl.ds(pl.multiple_of((n - 1) * rows, rows), rows)
            h_sc[1 - slot, slab, :] = _norm_mod(x_ref[slab, :], ng_ref, sc_ref, sh_ref)


def _in_kernel(x_ref, ng_ref, sc_ref, sh_ref, w_ref, o_ref, h_sc, *, n_chunks):
    def emit(h):
        o_ref[...] = _dot(h, w_ref[...]).astype(o_ref.dtype)

    _norm_matmul_steps(x_ref, ng_ref, sc_ref, sh_ref, h_sc, emit, n_chunks)


def _in_call(xf, ng, sc, sh, w_all, layer, T):
    M, D = xf.shape
    N = w_all.shape[2]
    per_b = T // IN_BM
    nm = M // IN_BM
    n_chunks = _norm_chunks(N // IN_BN)
    row = _lookahead_row_tile(nm)
    return pl.pallas_call(
        functools.partial(_in_kernel, n_chunks=n_chunks),
        grid=(nm, N // IN_BN),
        in_specs=[
            pl.BlockSpec((IN_BM, D), lambda i, n: (row(i, n), 0)),
            pl.BlockSpec((1, D), lambda i, n: (0, 0)),
            pl.BlockSpec((1, 1, D), lambda i, n: (row(i, n) // per_b, 0, 0)),
            pl.BlockSpec((1, 1, D), lambda i, n: (row(i, n) // per_b, 0, 0)),
            pl.BlockSpec((None, D, IN_BN), lambda i, n: (layer, 0, n)),
        ],
        out_specs=pl.BlockSpec((IN_BM, IN_BN), lambda i, n: (i, n)),
        out_shape=jax.ShapeDtypeStruct((M, N), BF16),
        scratch_shapes=[pltpu.VMEM((2, IN_BM, D), BF16)],
        compiler_params=_cparams("arbitrary", "arbitrary"),
        name="in_proj",
    )(xf, ng, sc, sh, w_all)


def _pool_kernel(u_ref, halo_ref, w_ref, ps_ref, o_ref, ext_sc, *, tiles_per_batch):
    i = pl.program_id(0)
    bt = u_ref.shape[0]
    group = w_ref.shape[1]
    tile_in_batch = i % tiles_per_batch
    ext_sc[0:POOL_HALO, :] = jnp.where(tile_in_batch == 0, 0.0, halo_ref[...].astype(F32))
    ext_sc[POOL_HALO:, :] = u_ref[...].astype(F32)
    t = tile_in_batch * bt + lax.broadcasted_iota(jnp.int32, (bt, 1), 0)
    for gi, w in enumerate(POOL_WINDOWS):
        cols = slice(gi * group, (gi + 1) * group)
        acc = ext_sc[:, cols]
        k = 1
        while k < w:
            acc = acc + pltpu.roll(acc, k, 0)
            k *= 2
        cnt = jnp.minimum(t + 1, w).astype(F32)
        pooled = acc[POOL_HALO:, :] / cnt - ext_sc[POOL_HALO:, cols]
        mixed = _dot(pooled.astype(BF16), w_ref[gi])
        o_ref[:, cols] = (mixed * ps_ref[:, cols]).astype(o_ref.dtype)


def _pool_call(proj, w_pool_all, pool_scale_all, layer, T):
    M = proj.shape[0]
    _, ng, group, _ = w_pool_all.shape
    width = ng * group
    tiles_per_batch = T // POOL_BT
    halo_blocks = POOL_BT // POOL_HALO
    return pl.pallas_call(
        functools.partial(_pool_kernel, tiles_per_batch=tiles_per_batch),
        grid=(M // POOL_BT,),
        in_specs=[
            pl.BlockSpec((POOL_BT, width), lambda i: (i, 0)),
            pl.BlockSpec((POOL_HALO, width), lambda i: (jnp.maximum(i * halo_blocks - 1, 0), 0)),
            pl.BlockSpec((None, ng, group, group), lambda i: (layer, 0, 0, 0)),
            pl.BlockSpec((None, 1, width), lambda i: (layer, 0, 0)),
        ],
        out_specs=pl.BlockSpec((POOL_BT, width), lambda i: (i, 0)),
        out_shape=jax.ShapeDtypeStruct((M, width), BF16),
        scratch_shapes=[pltpu.VMEM((POOL_BT + POOL_HALO, width), F32)],
        compiler_params=_cparams("parallel"),
        name="pool_mixer",
    )(proj, proj, w_pool_all, pool_scale_all)


AUG_SPLIT = 3
AUG_BLK_COL = SLC_LEN
AUG_OFF_COL = SLC_LEN + AUG_SPLIT
V_ROWS = HEAD_DIM + 16
LOG2E = 1.4426950408889634


def _key_extra(t):
    lane = lax.broadcasted_iota(jnp.int32, (t.shape[0], LANES), 1)
    blk = lax.shift_right_logical(t, SLC_SHIFT)
    off = t & (SLC_LEN - 1)
    extra = jnp.where(lane == blk, 1.0, 0.0)
    extra = jnp.where((lane >= AUG_BLK_COL) & (lane < AUG_OFF_COL), blk.astype(F32), extra)
    extra = jnp.where((lane >= AUG_OFF_COL) & (lane < AUG_OFF_COL + AUG_SPLIT), off.astype(F32), extra)
    return extra


def _value_tile_t(v_t):
    pad = lax.broadcasted_iota(jnp.int32, (V_ROWS - HEAD_DIM, v_t.shape[1]), 0)
    return jnp.concatenate([v_t, jnp.where(pad == 0, 1.0, 0.0)], axis=0).astype(BF16)


def _kprep_kernel(ks_ref, vs_ref, kw_ref, vw_ref, kg_ref, ksa_ref, vso_ref, kwa_ref, vwo_ref,
                  *, tiles_per_batch):
    bt = ks_ref.shape[0]
    kt = vso_ref.shape[4]
    t = (pl.program_id(0) % tiles_per_batch) * bt + lax.broadcasted_iota(jnp.int32, (bt, 1), 0)
    extra = _key_extra(t).astype(BF16)
    for g in range(N_KV):
        cols = slice(g * HEAD_DIM, (g + 1) * HEAD_DIM)
        ksn = _rms(ks_ref[:, cols].astype(F32)) * kg_ref[1:2, :]
        kwn = _rms(kw_ref[:, cols].astype(F32)) * kg_ref[2:3, :]
        ksa_ref[0, g, :, 0:HEAD_DIM] = ksn.astype(BF16)
        ksa_ref[0, g, :, HEAD_DIM:] = extra
        kwa_ref[0, g, :, 0:HEAD_DIM] = kwn.astype(BF16)
        kwa_ref[0, g, :, HEAD_DIM:] = extra
        vs_t = _value_tile_t(vs_ref[:, cols].astype(F32).T)
        vw_t = _value_tile_t(vw_ref[:, cols].astype(F32).T)
        for j in range(bt // kt):
            vso_ref[0, g, j] = vs_t[:, j * kt:(j + 1) * kt]
            vwo_ref[0, g, j] = vw_t[:, j * kt:(j + 1) * kt]


def _kprep_call(proj, k_gain, B, T, col0):
    kvw = N_KV * HEAD_DIM
    cb = col0 // kvw
    tiles_per_batch = T // KPREP_BT
    vt_per_tile = KPREP_BT // ATT_TQ
    aug = jax.ShapeDtypeStruct((B, N_KV, T, 2 * HEAD_DIM), BF16)
    val = jax.ShapeDtypeStruct((B, N_KV, T // ATT_TQ, V_ROWS, ATT_TQ), BF16)
    in_spec = lambda j: pl.BlockSpec((KPREP_BT, kvw), lambda i: (i, cb + j))
    out_map = lambda i: (i // tiles_per_batch, 0, i % tiles_per_batch, 0)
    val_map = lambda i: (i // tiles_per_batch, 0, i % tiles_per_batch, 0, 0)
    return pl.pallas_call(
        functools.partial(_kprep_kernel, tiles_per_batch=tiles_per_batch),
        grid=(B * tiles_per_batch,),
        in_specs=[in_spec(0), in_spec(1), in_spec(2), in_spec(3),
                  pl.BlockSpec((3, HEAD_DIM), lambda i: (0, 0))],
        out_specs=[pl.BlockSpec((1, N_KV, KPREP_BT, 2 * HEAD_DIM), out_map),
                   pl.BlockSpec((1, N_KV, vt_per_tile, V_ROWS, ATT_TQ), val_map),
                   pl.BlockSpec((1, N_KV, KPREP_BT, 2 * HEAD_DIM), out_map),
                   pl.BlockSpec((1, N_KV, vt_per_tile, V_ROWS, ATT_TQ), val_map)],
        out_shape=[aug, val, aug, val],
        compiler_params=_cparams("parallel"),
        name="kv_prep",
    )(proj, proj, proj, proj, k_gain)


def _compress_one(src_ref, f32_sc, pe_ref, w1_ref, kv, nc):
    half = CMP_LEN // 2
    assert CMP_STRIDE == half
    f32_sc[...] = src_ref[...].astype(F32)
    xs = [f32_sc[pl.ds(j, nc, stride=CMP_STRIDE), :] for j in range(half)]
    x = jnp.concatenate(xs, axis=1)
    kdim = half * HEAD_DIM
    lo = _dot((x + pe_ref[kv, 0:1, :]).astype(BF16), w1_ref[kv, 0:kdim, :])
    hi = _dot((x + pe_ref[kv, 1:2, :]).astype(BF16), w1_ref[kv, kdim:, :])
    pre = lo + pltpu.roll(hi, nc - 1, 0)
    return (pre * _sigmoid(pre)).astype(BF16)


def _cmp_kernel(k_ref, v_ref, pe_ref, w1_ref, w2k_ref, w2vt_ref, kg_ref, ko_ref, vo_ref, f32_sc):
    nc = ko_ref.shape[2]
    kc = _dot(_compress_one(k_ref, f32_sc, pe_ref, w1_ref, 0, nc), w2k_ref[...])
    ko_ref[0, 0] = (_rms(kc) * kg_ref[0:1, :]).astype(ko_ref.dtype)
    vct = _dot_nt(w2vt_ref[...], _compress_one(v_ref, f32_sc, pe_ref, w1_ref, 1, nc))
    vo_ref[0, 0] = vct.astype(vo_ref.dtype)


def _cmp_call(proj, pe2, w1, w2, k_gain, B, T, col0):
    nc = T // CMP_STRIDE
    cb = col0 // HEAD_DIM
    kdim = CMP_LEN * HEAD_DIM
    return pl.pallas_call(
        _cmp_kernel,
        grid=(B, N_KV),
        in_specs=[
            pl.BlockSpec((T, HEAD_DIM), lambda b, g: (b, cb + g)),
            pl.BlockSpec((T, HEAD_DIM), lambda b, g: (b, cb + N_KV + g)),
            pl.BlockSpec((2, 2, kdim // 2), lambda b, g: (0, 0, 0)),
            pl.BlockSpec((2, kdim, HEAD_DIM), lambda b, g: (0, 0, 0)),
            pl.BlockSpec((HEAD_DIM, HEAD_DIM), lambda b, g: (0, 0)),
            pl.BlockSpec((HEAD_DIM, HEAD_DIM), lambda b, g: (0, 0)),
            pl.BlockSpec((3, HEAD_DIM), lambda b, g: (0, 0)),
        ],
        out_specs=[pl.BlockSpec((1, 1, nc, HEAD_DIM), lambda b, g: (b, g, 0, 0)),
                   pl.BlockSpec((1, 1, HEAD_DIM, nc), lambda b, g: (b, g, 0, 0))],
        out_shape=[jax.ShapeDtypeStruct((B, N_KV, nc, HEAD_DIM), BF16),
                   jax.ShapeDtypeStruct((B, N_KV, HEAD_DIM, nc), BF16)],
        scratch_shapes=[pltpu.VMEM((T, HEAD_DIM), F32)],
        compiler_params=_cparams("parallel", "arbitrary"),
        name="compress",
    )(proj, proj, pe2, w1, w2[0], w2[1].T, k_gain)


MAX_FLOOR = 0.1 * NEG_INF


def _exp2_cols(s, mask):
    s = jnp.where(mask, s, NEG_INF)
    m = jnp.maximum(jnp.max(s, axis=0, keepdims=True), MAX_FLOOR)
    return jnp.exp2(s - m)


def _split3(c, shape):
    c = jnp.full(shape, c, F32)
    c1 = c.astype(BF16).astype(F32)
    r1 = c - c1
    c2 = r1.astype(BF16).astype(F32)
    c3 = (r1 - c2).astype(BF16).astype(F32)
    return c1, c2, c3


def _attn_kernel(slope_ref, q_ref, gl_ref, qg_ref, cmp_k_ref, cmp_vt_ref, ksa_ref, vst_ref,
                 kwa_ref, vwt_ref, ovl_ref, o_ref, score_sc, qs_sc, sa_sc, sb_sc, m_sc, acc_sc, seq_ref,
                 *, n_rep):
    g = pl.program_id(1)
    i = pl.program_id(2)
    tq = q_ref.shape[0]
    nc = cmp_k_ref.shape[2]
    n_slc = ovl_ref.shape[0]
    t0 = i * tq
    slopes = [slope_ref[g, r] * LOG2E for r in range(n_rep)]
    scale = HEAD_DIM ** -0.5 * LOG2E
    head = lambda a, r: a[:, r * tq:(r + 1) * tq]

    qt = []
    for r in range(n_rep):
        x = q_ref[:, r * HEAD_DIM:(r + 1) * HEAD_DIM].astype(F32)
        qt.append((_rms(x) * qg_ref[...] * scale).T.astype(BF16))
    qt_all = jnp.concatenate(qt, axis=1)

    s_all = _dot(cmp_k_ref[0, 0], qt_all)
    c_idx = lax.broadcasted_iota(jnp.int32, (nc, tq), 0)
    t_idx = lax.broadcasted_iota(jnp.int32, (nc, tq), 1) + t0
    dist_c = (t_idx - (c_idx * CMP_STRIDE + (CMP_LEN - 1))).astype(F32)
    mask_c = dist_c >= 0.0
    p_cmp = []
    p_sum = jnp.zeros((nc, tq), F32)
    for r in range(n_rep):
        p = _exp2_cols(head(s_all, r) - slopes[r] * dist_c, mask_c)
        l = jnp.sum(p, axis=0, keepdims=True)
        p = p * (1.0 / jnp.where(l > 0.0, l, 1.0))
        p_cmp.append(p.astype(BF16))
        p_sum = p_sum + p
    o_cmp_t = _dot(cmp_vt_ref[0, 0], jnp.concatenate(p_cmp, axis=1))

    p_hi = p_sum.astype(BF16)
    p_lo = (p_sum - p_hi.astype(F32)).astype(BF16)
    ovl = ovl_ref[...]
    imp = _dot(ovl, p_hi) + _dot(ovl, p_lo)
    jb = lax.broadcasted_iota(jnp.int32, (n_slc, tq), 0)
    tt = lax.broadcasted_iota(jnp.int32, (n_slc, tq), 1) + t0
    cur = lax.shift_right_logical(tt, SLC_SHIFT)
    forced = (jb == 0) | (jb == cur) | (jb == cur - 1)
    score = jnp.where(jb * SLC_LEN <= tt, imp + jnp.where(forced, FORCE_BONUS, 0.0), NEG_INF)
    score_sc[...] = score
    sub = 8
    groups = [score[sub * rg:sub * (rg + 1)] for rg in range(n_slc // sub)]
    ranks = [jnp.zeros((sub, tq), F32) for _ in groups]
    jrow = lax.broadcasted_iota(jnp.int32, (sub, tq), 0)
    for b2 in range(n_slc):
        sb = jnp.broadcast_to(score_sc[b2:b2 + 1, :], (sub, tq))
        for rg, sg in enumerate(groups):
            if sub * rg > b2:
                beats = sb >= sg
            elif sub * rg + sub - 1 < b2:
                beats = sb > sg
            else:
                beats = (sb > sg) | ((sb == sg) & (jrow > b2 - sub * rg))
            ranks[rg] = ranks[rg] + jnp.where(beats, 1.0, 0.0)
    n_sel = min(SLC_TOPK, n_slc)
    sel_bias_t = jnp.concatenate([jnp.where(rk < n_sel, 0.0, NEG_INF) for rk in ranks], axis=0)

    bpt = tq // SLC_LEN
    n_list = jnp.int32(0)
    for kt in range(n_slc // bpt - 1):
        rg, off = divmod(kt * bpt, sub)
        hit = jnp.max(jnp.where(ranks[rg][off:off + bpt] < n_sel, 1.0, 0.0), axis=1, keepdims=True)
        hit = jnp.max(hit, axis=0, keepdims=True)[0, 0]
        seq_ref[n_list] = jnp.int32(kt)
        n_list = n_list + ((hit > 0.0) & (kt < i)).astype(jnp.int32)
    seq_ref[n_list] = i

    pshape = (LANES - AUG_BLK_COL, tq)
    frow = lax.broadcasted_iota(jnp.int32, pshape, 0)
    sel_rows = sel_bias_t.astype(BF16)
    if n_slc < AUG_BLK_COL:
        sel_rows = jnp.concatenate([sel_rows, jnp.zeros((AUG_BLK_COL - n_slc, tq), BF16)], axis=0)
    q_slc, q_win = [], []
    for r in range(n_rep):
        pos_rows = jnp.zeros(pshape, F32)
        for k, ck in enumerate(_split3(slopes[r], pshape)):
            pos_rows = jnp.where(frow == k, ck * SLC_LEN, pos_rows)
            pos_rows = jnp.where(frow == AUG_SPLIT + k, ck, pos_rows)
        pos_rows = pos_rows.astype(BF16)
        q_slc.append(jnp.concatenate([qt[r], sel_rows, pos_rows], axis=0))
        q_win.append(jnp.concatenate([qt[r], jnp.zeros_like(sel_rows), pos_rows], axis=0))
    qs_sc[...] = jnp.concatenate(q_slc, axis=1)
    q_win = jnp.concatenate(q_win, axis=1)

    n_wt = WIN // tq + 1
    span = n_wt * tq
    j0 = jnp.maximum(i - (n_wt - 1), 0)
    start_w = pl.multiple_of(j0 * tq, tq)
    s_w = _dot(kwa_ref[0, 0, pl.ds(start_w, span), :], q_win)
    d_w = (lax.broadcasted_iota(jnp.int32, (span, tq), 1) + t0) - \
          (lax.broadcasted_iota(jnp.int32, (span, tq), 0) + start_w)
    mask_w = (d_w >= 0) & (d_w < WIN)
    p_w = jnp.concatenate([_exp2_cols(head(s_w, r), mask_w).astype(BF16) for r in range(n_rep)], axis=1)
    o_win_t = _dot(vwt_ref[0, 0, j0], p_w[0:tq])
    for jj in range(1, n_wt):
        o_win_t = o_win_t + _dot(vwt_ref[0, 0, j0 + jj], p_w[jj * tq:(jj + 1) * tq])
    o_win_t = o_win_t[0:HEAD_DIM] * (1.0 / o_win_t[HEAD_DIM:HEAD_DIM + 1])

    m_sc[...] = jnp.full(m_sc.shape, NEG_INF, F32)
    acc_sc[...] = jnp.zeros(acc_sc.shape, F32)

    def produce(kt, buf):
        start = pl.multiple_of(kt * tq, tq)
        buf[...] = _dot(ksa_ref[0, 0, pl.ds(start, tq), :], qs_sc[...])

    def consume(kt, buf, causal):
        s = buf[...]
        if causal:
            kpos = lax.broadcasted_iota(jnp.int32, s.shape, 0)
            qpos = lax.broadcasted_iota(jnp.int32, s.shape, 1) & (tq - 1)
            s = jnp.where(kpos <= qpos, s, NEG_INF)
        m_old = m_sc[...]
        m_new = jnp.maximum(m_old, jnp.max(s, axis=0, keepdims=True))
        alpha = jnp.exp2(m_old - m_new)
        p = jnp.exp2(s - m_new)
        acc_sc[...] = alpha * acc_sc[...] + _dot(vst_ref[0, 0, kt], p.astype(BF16))
        m_sc[...] = m_new

    def pair(j, carry):
        produce(seq_ref[2 * j + 1], sb_sc)
        consume(seq_ref[2 * j], sa_sc, False)
        produce(seq_ref[2 * j + 2], sa_sc)
        consume(seq_ref[2 * j + 1], sb_sc, False)
        return carry

    produce(seq_ref[0], sa_sc)
    lax.fori_loop(0, n_list // 2, pair, 0)

    @pl.when(n_list % 2 == 1)
    def _():
        produce(i, sb_sc)
        consume(seq_ref[n_list - 1], sa_sc, False)
        consume(i, sb_sc, True)

    @pl.when(n_list % 2 == 0)
    def _():
        consume(i, sa_sc, True)

    o_slc_t = acc_sc[0:HEAD_DIM, :] * (1.0 / acc_sc[HEAD_DIM:HEAD_DIM + 1, :])

    gates = _sigmoid(gl_ref[...].astype(F32).T)
    for r in range(n_rep):
        o_t = (gates[3 * r:3 * r + 1] * head(o_cmp_t, r)
               + gates[3 * r + 1:3 * r + 2] * head(o_slc_t, r)
               + gates[3 * r + 2:3 * r + 3] * head(o_win_t, r))
        o_ref[:, r * HEAD_DIM:(r + 1) * HEAD_DIM] = o_t.T.astype(o_ref.dtype)


def _attn_call(proj, q_gain, cmp_k, cmp_vt, ks_aug, vs_t, kw_aug, vw_t, slopes, ovl_t,
               B, T, q_col0, gate_col0):
    n_rep = slopes.shape[1]
    gw = n_rep * HEAD_DIM
    nq = T // ATT_TQ
    nc = cmp_k.shape[2]
    n_slc = ovl_t.shape[0]
    qcb = q_col0 // gw
    gcb = gate_col0 // LANES
    assert WIN % ATT_TQ == 0
    keys = pl.BlockSpec((1, 1, T, 2 * HEAD_DIM), lambda b, g, i: (b, g, 0, 0))
    vals = pl.BlockSpec((1, 1, nq, V_ROWS, ATT_TQ), lambda b, g, i: (b, g, 0, 0, 0))
    return pl.pallas_call(
        functools.partial(_attn_kernel, n_rep=n_rep),
        grid=(B, N_KV, nq),
        in_specs=[
            pl.BlockSpec(memory_space=pltpu.SMEM),
            pl.BlockSpec((ATT_TQ, gw), lambda b, g, i: (b * nq + i, qcb + g)),
            pl.BlockSpec((ATT_TQ, LANES), lambda b, g, i: (b * nq + i, gcb + g)),
            pl.BlockSpec((1, HEAD_DIM), lambda b, g, i: (0, 0)),
            pl.BlockSpec((1, 1, nc, HEAD_DIM), lambda b, g, i: (b, g, 0, 0)),
            pl.BlockSpec((1, 1, HEAD_DIM, nc), lambda b, g, i: (b, g, 0, 0)),
            keys, vals, keys, vals,
            pl.BlockSpec((n_slc, nc), lambda b, g, i: (0, 0)),
        ],
        out_specs=pl.BlockSpec((ATT_TQ, gw), lambda b, g, i: (b * nq + i, g)),
        out_shape=jax.ShapeDtypeStruct((B * T, N_KV * gw), BF16),
        scratch_shapes=[
            pltpu.VMEM((n_slc, ATT_TQ), F32),
            pltpu.VMEM((2 * HEAD_DIM, n_rep * ATT_TQ), BF16),
            pltpu.VMEM((ATT_TQ, n_rep * ATT_TQ), F32),
            pltpu.VMEM((ATT_TQ, n_rep * ATT_TQ), F32),
            pltpu.VMEM((1, n_rep * ATT_TQ), F32),
            pltpu.VMEM((V_ROWS, n_rep * ATT_TQ), F32),
            pltpu.SMEM((nq + 1,), jnp.int32),
        ],
        compiler_params=_cparams("parallel", "parallel", "arbitrary"),
        name="nsa_attention",
    )(slopes, proj, proj, q_gain, cmp_k, cmp_vt, ks_aug, vs_t, kw_aug, vw_t, ovl_t)


def _out_kernel(a_ref, o_ref, w_ref, x_ref, g_ref, ng_ref, sc_ref, sh_ref, y_ref, h_ref):
    ka = a_ref.shape[1]
    rc = a_ref.shape[0] // OUT_ROW_CHUNKS
    for c in range(OUT_ROW_CHUNKS):
        rows = slice(c * rc, (c + 1) * rc)
        acc = _dot(a_ref[rows, :], w_ref[0:ka, :]) + _dot(o_ref[rows, :], w_ref[ka:, :])
        y = x_ref[rows, :] + g_ref[0] * acc
        y_ref[rows, :] = y
        h_ref[rows, :] = _norm_mod(y, ng_ref, sc_ref, sh_ref)


def _out_call(a, o, w_all, layer, xf, gate, ng, sc, sh, T):
    M, D = xf.shape
    ka, ko = a.shape[1], o.shape[1]
    per_b = T // OUT_BM
    row_spec = lambda width: pl.BlockSpec((OUT_BM, width), lambda i: (i, 0))
    mod_spec = pl.BlockSpec((1, 1, D), lambda i: (i // per_b, 0, 0))
    return pl.pallas_call(
        _out_kernel,
        grid=(M // OUT_BM,),
        in_specs=[
            row_spec(ka), row_spec(ko),
            pl.BlockSpec((None, ka + ko, D), lambda i: (layer, 0, 0)),
            row_spec(D), mod_spec,
            pl.BlockSpec((1, D), lambda i: (0, 0)),
            mod_spec, mod_spec,
        ],
        out_specs=[row_spec(D), row_spec(D)],
        out_shape=[jax.ShapeDtypeStruct((M, D), F32), jax.ShapeDtypeStruct((M, D), BF16)],
        compiler_params=_cparams("parallel"),
        name="out_proj",
    )(a, o, w_all, xf, gate, ng, sc, sh)


def _ffn1_kernel(h_ref, wg_ref, wu_ref, o_ref, wg_sc, wu_sc):
    @pl.when(pl.program_id(1) == 0)
    def _():
        wg_sc[...] = wg_ref[...].astype(BF16)
        wu_sc[...] = wu_ref[...].astype(BF16)

    h = h_ref[...]
    gate = _dot(h, wg_sc[...])
    up = _dot(h, wu_sc[...])
    o_ref[...] = (gate * _sigmoid(gate) * up).astype(o_ref.dtype)


def _ffn1_call(h, w_gu, layer):
    M, D = h.shape
    dff = w_gu.shape[2] // 2
    nt = dff // FFN1_BN
    return pl.pallas_call(
        _ffn1_kernel,
        grid=(nt, M // FFN1_BM),
        in_specs=[
            pl.BlockSpec((FFN1_BM, D), lambda n, i: (i, 0)),
            pl.BlockSpec((None, D, FFN1_BN), lambda n, i: (layer, 0, n)),
            pl.BlockSpec((None, D, FFN1_BN), lambda n, i: (layer, 0, n + nt)),
        ],
        out_specs=pl.BlockSpec((FFN1_BM, FFN1_BN), lambda n, i: (i, n)),
        out_shape=jax.ShapeDtypeStruct((M, dff), BF16),
        scratch_shapes=[pltpu.VMEM((D, FFN1_BN), BF16), pltpu.VMEM((D, FFN1_BN), BF16)],
        compiler_params=_cparams("arbitrary", "arbitrary"),
        name="ffn_up",
    )(h, w_gu, w_gu)


def _ffn2_kernel(h_ref, w_ref, x_ref, g_ref, y_ref):
    y_ref[...] = x_ref[...] + g_ref[0] * _dot(h_ref[...], w_ref[...])


def _ffn2_call(h, w_all, layer, xf, gate, T):
    M, D = xf.shape
    dff = h.shape[1]
    per_b = T // FFN2_BM
    return pl.pallas_call(
        _ffn2_kernel,
        grid=(M // FFN2_BM, D // FFN2_BN),
        in_specs=[
            pl.BlockSpec((FFN2_BM, dff), lambda i, n: (i, 0)),
            pl.BlockSpec((None, dff, FFN2_BN), lambda i, n: (layer, 0, n)),
            pl.BlockSpec((FFN2_BM, FFN2_BN), lambda i, n: (i, n)),
            pl.BlockSpec((1, 1, FFN2_BN), lambda i, n: (i // per_b, 0, n)),
        ],
        out_specs=pl.BlockSpec((FFN2_BM, FFN2_BN), lambda i, n: (i, n)),
        out_shape=jax.ShapeDtypeStruct((M, D), F32),
        compiler_params=_cparams("parallel", "arbitrary"),
        name="ffn_down",
    )(h, w_all, xf, gate)


def _alibi_slopes(n_heads):
    sl = 2.0 ** (-8.0 * np.arange(1, n_heads + 1) / n_heads)
    return jnp.asarray(sl, F32).reshape(N_KV, n_heads // N_KV)


def _overlap_t(T):
    nc = T // CMP_STRIDE
    n_slc = T // SLC_LEN
    cst = np.arange(nc) * CMP_STRIDE
    sst = np.arange(n_slc) * SLC_LEN
    ov = (cst[None, :] < sst[:, None] + SLC_LEN) & (cst[None, :] + CMP_LEN > sst[:, None])
    ov[:, (T - CMP_LEN) // CMP_STRIDE + 1:] = False
    return jnp.asarray(ov.astype(np.float32), BF16)


def kernel(x, c, w_ada, b_ada, norm_g, w_in, q_gain, k_gain, pe_cmp, w_cmp1, w_cmp2,
           w_pool, pool_scale, w_out, w_gate_up, w_down):
    B, T, D = x.shape
    L = w_ada.shape[0]
    pool_w = w_pool.shape[1] * w_pool.shape[2]
    kvw = N_KV * HEAD_DIM
    n_heads = (w_in.shape[2] - pool_w - 6 * kvw) // (HEAD_DIM + 3)
    att_w = n_heads * HEAD_DIM
    n_rep = n_heads // N_KV
    assert w_in.shape[2] == pool_w + att_w + 6 * kvw + 3 * n_heads
    assert T % ATT_TQ == 0 and T >= WIN + ATT_TQ and T % POOL_BT == 0 and T % KPREP_BT == 0
    assert T // SLC_LEN <= AUG_BLK_COL and pool_w % (n_rep * HEAD_DIM) == 0
    assert 1 << SLC_SHIFT == SLC_LEN and ATT_TQ & (ATT_TQ - 1) == 0
    q_col0 = pool_w
    kc_col0 = pool_w + att_w
    ks_col0 = kc_col0 + 2 * kvw
    gate_col0 = kc_col0 + 6 * kvw
    assert ks_col0 % kvw == 0 and gate_col0 % LANES == 0

    xf = x.reshape(B * T, D)
    rows = -(-B // 8) * 8
    c8 = jnp.pad(c, ((0, rows - B), (0, 0)))
    mod = _ada_call(c8, w_ada, b_ada)
    slopes = _alibi_slopes(n_heads)
    ovl_t = _overlap_t(T)

    per_group_gates = 3 * n_rep
    n_cols = -(-(gate_col0 + N_KV * LANES) // IN_BN) * IN_BN
    w_in_p = jnp.zeros((L, D, n_cols), BF16)
    w_in_p = lax.dynamic_update_slice(w_in_p, w_in[:, :, :gate_col0].astype(BF16), (0, 0, 0))
    for g in range(N_KV):
        cols = slice(gate_col0 + g * per_group_gates, gate_col0 + (g + 1) * per_group_gates)
        w_in_p = lax.dynamic_update_slice(w_in_p, w_in[:, :, cols].astype(BF16), (0, 0, gate_col0 + g * LANES))
    w_pool_b = w_pool.astype(BF16)
    pool_scale_r = pool_scale.reshape(L, 1, pool_w)
    w_out_b = w_out.astype(BF16)
    w_down_b = w_down.astype(BF16)

    for l in range(L):
        sh1, sc1, g1, sh2, sc2, g2 = [mod[l, :B, k * D:(k + 1) * D].reshape(B, 1, D) for k in range(6)]
        proj = _in_call(xf, norm_g[l, 0:1], sc1, sh1, w_in_p, l, T)
        a_out = _pool_call(proj, w_pool_b, pool_scale_r, l, T)
        ks_aug, vs_t, kw_aug, vw_t = _kprep_call(proj, k_gain[l], B, T, ks_col0)
        pe2 = pe_cmp[l].reshape(2, 2, (CMP_LEN // 2) * HEAD_DIM)
        cmp_k, cmp_vt = _cmp_call(proj, pe2, w_cmp1[l].astype(BF16), w_cmp2[l].astype(BF16), k_gain[l],
                                  B, T, kc_col0)
        o_att = _attn_call(proj, q_gain[l].reshape(1, HEAD_DIM), cmp_k, cmp_vt, ks_aug, vs_t, kw_aug, vw_t,
                           slopes, ovl_t, B, T, q_col0, gate_col0)
        xf, h2 = _out_call(a_out, o_att, w_out_b, l, xf, g1, norm_g[l, 1:2], sc2, sh2, T)
        hidden = _ffn1_call(h2, w_gate_up, l)
        xf = _ffn2_call(hidden, w_down_b, l, xf, g2, T)
    return xf.reshape(B, T, D)
```

```python
import functools

import numpy as np
import jax
import jax.numpy as jnp
from jax import lax
from jax.experimental import pallas as pl
from jax.experimental.pallas import tpu as pltpu

F32 = jnp.float32
BF16 = jnp.bfloat16

POOL_WINDOWS = (2, 4, 8, 16)
HEAD_DIM = 128
N_KV = 2
CMP_LEN = 32
CMP_STRIDE = 16
SLC_LEN = 64
SLC_SHIFT = 6
SLC_TOPK = 16
WIN = 512
NORM_EPS = 1e-6
NEG_INF = -1e30
FORCE_BONUS = 1e3

LANES = 128
POOL_HALO = 16
VMEM_LIMIT_BYTES = 56 * 1024 * 1024

ADA_BN = 1024
IN_BM, IN_BN = 1024, 1280
KPREP_BT = 1024
POOL_BT = 1024
ATT_TQ = 256
OUT_BM = 512
OUT_ROW_CHUNKS = 2
FFN1_BM, FFN1_BN = 1024, 512
FFN2_BM, FFN2_BN = 1024, 512


def _cparams(*sem):
    return pltpu.CompilerParams(dimension_semantics=sem, vmem_limit_bytes=VMEM_LIMIT_BYTES)


def _dot(a, b):
    return jnp.dot(a, b, preferred_element_type=F32)


def _dot_nt(a, b):
    return lax.dot_general(a, b, (((1,), (1,)), ((), ())), preferred_element_type=F32)


def _rms(x):
    return x * lax.rsqrt(jnp.mean(x * x, axis=-1, keepdims=True) + NORM_EPS)


def _sigmoid(x):
    return 1.0 / (1.0 + jnp.exp(-x))


def _ada_kernel(c_ref, w_ref, b_ref, o_ref):
    c = c_ref[...]
    cs = c * _sigmoid(c)
    o_ref[0] = _dot(cs, w_ref[0]) + b_ref[0]


def _ada_call(c8, w_ada, b_ada):
    L, D, N = w_ada.shape
    rows = c8.shape[0]
    return pl.pallas_call(
        _ada_kernel,
        grid=(L, N // ADA_BN),
        in_specs=[
            pl.BlockSpec((rows, D), lambda l, n: (0, 0)),
            pl.BlockSpec((1, D, ADA_BN), lambda l, n: (l, 0, n)),
            pl.BlockSpec((1, 1, ADA_BN), lambda l, n: (l, 0, n)),
        ],
        out_specs=pl.BlockSpec((1, rows, ADA_BN), lambda l, n: (l, 0, n)),
        out_shape=jax.ShapeDtypeStruct((L, rows, N), F32),
        compiler_params=_cparams("parallel", "arbitrary"),
        name="ada_mod",
    )(c8, w_ada, b_ada.reshape(L, 1, N))


def _norm_mod(x, ng_ref, sc_ref, sh_ref):
    y = _rms(x) * ng_ref[...]
    return (y * (1.0 + sc_ref[0]) + sh_ref[0]).astype(BF16)


def _lookahead_row_tile(n_row_tiles):
    def idx(i, n):
        return jnp.where((i == 0) & (n == 0), 0, jnp.minimum(i + 1, n_row_tiles - 1))
    return idx


def _norm_chunks(n_col_steps):
    assert n_col_steps >= 2
    return 1 << ((n_col_steps - 1).bit_length() - 1)


def _norm_matmul_steps(x_ref, ng_ref, sc_ref, sh_ref, h_sc, emit, n_chunks):
    i, n = pl.program_id(0), pl.program_id(1)
    rows = x_ref.shape[0] // n_chunks
    slab_step = (n >= 1) & (n <= n_chunks)

    @pl.when((i == 0) & (n == 0))
    def _():
        h_sc[0] = _norm_mod(x_ref[...], ng_ref, sc_ref, sh_ref)

    @pl.when(jnp.logical_not(slab_step))
    def _():
        emit(h_sc[i % 2])

    for slot in (0, 1):
        @pl.when(slab_step & (i % 2 == slot))
        def _(slot=slot):
            emit(h_sc[slot])
            slab = pl.ds(pl.multiple_of((n - 1) * rows, rows), rows)
            h_sc[1 - slot, slab, :] = _norm_mod(x_ref[slab, :], ng_ref, sc_ref, sh_ref)


def _in_kernel(x_ref, ng_ref, sc_ref, sh_ref, w_ref, o_ref, h_sc, *, n_chunks):
    def emit(h):
        o_ref[...] = _dot(h, w_ref[...]).astype(o_ref.dtype)

    _norm_matmul_steps(x_ref, ng_ref, sc_ref, sh_ref, h_sc, emit, n_chunks)


def _in_call(xf, ng, sc, sh, w_all, layer, T):
    M, D = xf.shape
    N = w_all.shape[2]
    per_b = T // IN_BM
    nm = M // IN_BM
    n_chunks = _norm_chunks(N // IN_BN)
    row = _lookahead_row_tile(nm)
    return pl.pallas_call(
        functools.partial(_in_kernel, n_chunks=n_chunks),
        grid=(nm, N // IN_BN),
        in_specs=[
            pl.BlockSpec((IN_BM, D), lambda i, n: (row(i, n), 0)),
            pl.BlockSpec((1, D), lambda i, n: (0, 0)),
            pl.BlockSpec((1, 1, D), lambda i, n: (row(i, n) // per_b, 0, 0)),
            pl.BlockSpec((1, 1, D), lambda i, n: (row(i, n) // per_b, 0, 0)),
            pl.BlockSpec((None, D, IN_BN), lambda i, n: (layer, 0, n)),
        ],
        out_specs=pl.BlockSpec((IN_BM, IN_BN), lambda i, n: (i, n)),
        out_shape=jax.ShapeDtypeStruct((M, N), BF16),
        scratch_shapes=[pltpu.VMEM((2, IN_BM, D), BF16)],
        compiler_params=_cparams("arbitrary", "arbitrary"),
        name="in_proj",
    )(xf, ng, sc, sh, w_all)


def _pool_kernel(u_ref, halo_ref, w_ref, ps_ref, o_ref, ext_sc, *, tiles_per_batch):
    i = pl.program_id(0)
    bt = u_ref.shape[0]
    group = w_ref.shape[1]
    tile_in_batch = i % tiles_per_batch
    ext_sc[0:POOL_HALO, :] = jnp.where(tile_in_batch == 0, 0.0, halo_ref[...].astype(F32))
    ext_sc[POOL_HALO:, :] = u_ref[...].astype(F32)
    t = tile_in_batch * bt + lax.broadcasted_iota(jnp.int32, (bt, 1), 0)
    for gi, w in enumerate(POOL_WINDOWS):
        cols = slice(gi * group, (gi + 1) * group)
        acc = ext_sc[:, cols]
        k = 1
        while k < w:
            acc = acc + pltpu.roll(acc, k, 0)
            k *= 2
        cnt = jnp.minimum(t + 1, w).astype(F32)
        pooled = acc[POOL_HALO:, :] / cnt - ext_sc[POOL_HALO:, cols]
        mixed = _dot(pooled.astype(BF16), w_ref[gi])
        o_ref[:, cols] = (mixed * ps_ref[:, cols]).astype(o_ref.dtype)


def _pool_call(proj, w_pool_all, pool_scale_all, layer, T):
    M = proj.shape[0]
    _, ng, group, _ = w_pool_all.shape
    width = ng * group
    tiles_per_batch = T // POOL_BT
    halo_blocks = POOL_BT // POOL_HALO
    return pl.pallas_call(
        functools.partial(_pool_kernel, tiles_per_batch=tiles_per_batch),
        grid=(M // POOL_BT,),
        in_specs=[
            pl.BlockSpec((POOL_BT, width), lambda i: (i, 0)),
            pl.BlockSpec((POOL_HALO, width), lambda i: (jnp.maximum(i * halo_blocks - 1, 0), 0)),
            pl.BlockSpec((None, ng, group, group), lambda i: (layer, 0, 0, 0)),
            pl.BlockSpec((None, 1, width), lambda i: (layer, 0, 0)),
        ],
        out_specs=pl.BlockSpec((POOL_BT, width), lambda i: (i, 0)),
        out_shape=jax.ShapeDtypeStruct((M, width), BF16),
        scratch_shapes=[pltpu.VMEM((POOL_BT + POOL_HALO, width), F32)],
        compiler_params=_cparams("parallel"),
        name="pool_mixer",
    )(proj, proj, w_pool_all, pool_scale_all)


AUG_SPLIT = 3
AUG_BLK_COL = SLC_LEN
AUG_OFF_COL = SLC_LEN + AUG_SPLIT
V_ROWS = HEAD_DIM + 16
LOG2E = 1.4426950408889634


def _key_extra(t):
    lane = lax.broadcasted_iota(jnp.int32, (t.shape[0], LANES), 1)
    blk = lax.shift_right_logical(t, SLC_SHIFT)
    off = t & (SLC_LEN - 1)
    extra = jnp.where(lane == blk, 1.0, 0.0)
    extra = jnp.where((lane >= AUG_BLK_COL) & (lane < AUG_OFF_COL), blk.astype(F32), extra)
    extra = jnp.where((lane >= AUG_OFF_COL) & (lane < AUG_OFF_COL + AUG_SPLIT), off.astype(F32), extra)
    return extra


def _value_tile_t(v_t):
    pad = lax.broadcasted_iota(jnp.int32, (V_ROWS - HEAD_DIM, v_t.shape[1]), 0)
    return jnp.concatenate([v_t, jnp.where(pad == 0, 1.0, 0.0)], axis=0).astype(BF16)


def _kprep_kernel(ks_ref, vs_ref, kw_ref, vw_ref, kg_ref, ksa_ref, vso_ref, kwa_ref, vwo_ref,
                  *, tiles_per_batch):
    bt = ks_ref.shape[0]
    kt = vso_ref.shape[4]
    t = (pl.program_id(0) % tiles_per_batch) * bt + lax.broadcasted_iota(jnp.int32, (bt, 1), 0)
    extra = _key_extra(t).astype(BF16)
    for g in range(N_KV):
        cols = slice(g * HEAD_DIM, (g + 1) * HEAD_DIM)
        ksn = _rms(ks_ref[:, cols].astype(F32)) * kg_ref[1:2, :]
        kwn = _rms(kw_ref[:, cols].astype(F32)) * kg_ref[2:3, :]
        ksa_ref[0, g, :, 0:HEAD_DIM] = ksn.astype(BF16)
        ksa_ref[0, g, :, HEAD_DIM:] = extra
        kwa_ref[0, g, :, 0:HEAD_DIM] = kwn.astype(BF16)
        kwa_ref[0, g, :, HEAD_DIM:] = extra
        vs_t = _value_tile_t(vs_ref[:, cols].astype(F32).T)
        vw_t = _value_tile_t(vw_ref[:, cols].astype(F32).T)
        for j in range(bt // kt):
            vso_ref[0, g, j] = vs_t[:, j * kt:(j + 1) * kt]
            vwo_ref[0, g, j] = vw_t[:, j * kt:(j + 1) * kt]


def _kprep_call(proj, k_gain, B, T, col0):
    kvw = N_KV * HEAD_DIM
    cb = col0 // kvw
    tiles_per_batch = T // KPREP_BT
    vt_per_tile = KPREP_BT // ATT_TQ
    aug = jax.ShapeDtypeStruct((B, N_KV, T, 2 * HEAD_DIM), BF16)
    val = jax.ShapeDtypeStruct((B, N_KV, T // ATT_TQ, V_ROWS, ATT_TQ), BF16)
    in_spec = lambda j: pl.BlockSpec((KPREP_BT, kvw), lambda i: (i, cb + j))
    out_map = lambda i: (i // tiles_per_batch, 0, i % tiles_per_batch, 0)
    val_map = lambda i: (i // tiles_per_batch, 0, i % tiles_per_batch, 0, 0)
    return pl.pallas_call(
        functools.partial(_kprep_kernel, tiles_per_batch=tiles_per_batch),
        grid=(B * tiles_per_batch,),
        in_specs=[in_spec(0), in_spec(1), in_spec(2), in_spec(3),
                  pl.BlockSpec((3, HEAD_DIM), lambda i: (0, 0))],
        out_specs=[pl.BlockSpec((1, N_KV, KPREP_BT, 2 * HEAD_DIM), out_map),
                   pl.BlockSpec((1, N_KV, vt_per_tile, V_ROWS, ATT_TQ), val_map),
                   pl.BlockSpec((1, N_KV, KPREP_BT, 2 * HEAD_DIM), out_map),
                   pl.BlockSpec((1, N_KV, vt_per_tile, V_ROWS, ATT_TQ), val_map)],
        out_shape=[aug, val, aug, val],
        compiler_params=_cparams("parallel"),
        name="kv_prep",
    )(proj, proj, proj, proj, k_gain)


def _compress_one(src_ref, f32_sc, pe_ref, w1_ref, kv, nc):
    half = CMP_LEN // 2
    assert CMP_STRIDE == half
    f32_sc[...] = src_ref[...].astype(F32)
    xs = [f32_sc[pl.ds(j, nc, stride=CMP_STRIDE), :] for j in range(half)]
    x = jnp.concatenate(xs, axis=1)
    kdim = half * HEAD_DIM
    lo = _dot((x + pe_ref[kv, 0:1, :]).astype(BF16), w1_ref[kv, 0:kdim, :])
    hi = _dot((x + pe_ref[kv, 1:2, :]).astype(BF16), w1_ref[kv, kdim:, :])
    pre = lo + pltpu.roll(hi, nc - 1, 0)
    return (pre * _sigmoid(pre)).astype(BF16)


def _cmp_kernel(k_ref, v_ref, pe_ref, w1_ref, w2k_ref, w2vt_ref, kg_ref, ko_ref, vo_ref, f32_sc):
    nc = ko_ref.shape[2]
    kc = _dot(_compress_one(k_ref, f32_sc, pe_ref, w1_ref, 0, nc), w2k_ref[...])
    ko_ref[0, 0] = (_rms(kc) * kg_ref[0:1, :]).astype(ko_ref.dtype)
    vct = _dot_nt(w2vt_ref[...], _compress_one(v_ref, f32_sc, pe_ref, w1_ref, 1, nc))
    vo_ref[0, 0] = vct.astype(vo_ref.dtype)


def _cmp_call(proj, pe2, w1, w2, k_gain, B, T, col0):
    nc = T // CMP_STRIDE
    cb = col0 // HEAD_DIM
    kdim = CMP_LEN * HEAD_DIM
    return pl.pallas_call(
        _cmp_kernel,
        grid=(B, N_KV),
        in_specs=[
            pl.BlockSpec((T, HEAD_DIM), lambda b, g: (b, cb + g)),
            pl.BlockSpec((T, HEAD_DIM), lambda b, g: (b, cb + N_KV + g)),
            pl.BlockSpec((2, 2, kdim // 2), lambda b, g: (0, 0, 0)),
            pl.BlockSpec((2, kdim, HEAD_DIM), lambda b, g: (0, 0, 0)),
            pl.BlockSpec((HEAD_DIM, HEAD_DIM), lambda b, g: (0, 0)),
            pl.BlockSpec((HEAD_DIM, HEAD_DIM), lambda b, g: (0, 0)),
            pl.BlockSpec((3, HEAD_DIM), lambda b, g: (0, 0)),
        ],
        out_specs=[pl.BlockSpec((1, 1, nc, HEAD_DIM), lambda b, g: (b, g, 0, 0)),
                   pl.BlockSpec((1, 1, HEAD_DIM, nc), lambda b, g: (b, g, 0, 0))],
        out_shape=[jax.ShapeDtypeStruct((B, N_KV, nc, HEAD_DIM), BF16),
                   jax.ShapeDtypeStruct((B, N_KV, HEAD_DIM, nc), BF16)],
        scratch_shapes=[pltpu.VMEM((T, HEAD_DIM), F32)],
        compiler_params=_cparams("parallel", "arbitrary"),
        name="compress",
    )(proj, proj, pe2, w1, w2[0], w2[1].T, k_gain)


MAX_FLOOR = 0.1 * NEG_INF


def _exp2_cols(s, mask):
    s = jnp.where(mask, s, NEG_INF)
    m = jnp.maximum(jnp.max(s, axis=0, keepdims=True), MAX_FLOOR)
    return jnp.exp2(s - m)


def _split3(c, shape):
    c = jnp.full(shape, c, F32)
    c1 = c.astype(BF16).astype(F32)
    r1 = c - c1
    c2 = r1.astype(BF16).astype(F32)
    c3 = (r1 - c2).astype(BF16).astype(F32)
    return c1, c2, c3


def _attn_kernel(slope_ref, q_ref, gl_ref, qg_ref, cmp_k_ref, cmp_vt_ref, ksa_ref, vst_ref,
                 kwa_ref, vwt_ref, ovl_ref, o_ref, score_sc, qs_sc, sa_sc, sb_sc, m_sc, acc_sc, gate_sc,
                 seq_ref, *, n_rep):
    g = pl.program_id(1)
    i = pl.program_id(2)
    tq = q_ref.shape[0]
    nc = cmp_k_ref.shape[2]
    n_slc = ovl_ref.shape[0]
    t0 = i * tq
    slopes = [slope_ref[g, r] * LOG2E for r in range(n_rep)]
    scale = HEAD_DIM ** -0.5 * LOG2E
    head = lambda a, r: a[:, r * tq:(r + 1) * tq]

    qt = []
    for r in range(n_rep):
        x = q_ref[:, r * HEAD_DIM:(r + 1) * HEAD_DIM].astype(F32)
        qt.append((_rms(x) * qg_ref[...] * scale).T.astype(BF16))
    qt_all = jnp.concatenate(qt, axis=1)

    s_all = _dot(cmp_k_ref[0, 0], qt_all)
    c_idx = lax.broadcasted_iota(jnp.int32, (nc, tq), 0)
    t_idx = lax.broadcasted_iota(jnp.int32, (nc, tq), 1) + t0
    dist_c = (t_idx - (c_idx * CMP_STRIDE + (CMP_LEN - 1))).astype(F32)
    mask_c = dist_c >= 0.0
    p_cmp = []
    p_sum = jnp.zeros((nc, tq), F32)
    for r in range(n_rep):
        p = _exp2_cols(head(s_all, r) - slopes[r] * dist_c, mask_c)
        l = jnp.sum(p, axis=0, keepdims=True)
        p = p * (1.0 / jnp.where(l > 0.0, l, 1.0))
        p_cmp.append(p.astype(BF16))
        p_sum = p_sum + p
    o_cmp_t = _dot(cmp_vt_ref[0, 0], jnp.concatenate(p_cmp, axis=1))

    p_hi = p_sum.astype(BF16)
    p_lo = (p_sum - p_hi.astype(F32)).astype(BF16)
    ovl = ovl_ref[...]
    imp = _dot(ovl, p_hi) + _dot(ovl, p_lo)
    jb = lax.broadcasted_iota(jnp.int32, (n_slc, tq), 0)
    tt = lax.broadcasted_iota(jnp.int32, (n_slc, tq), 1) + t0
    cur = lax.shift_right_logical(tt, SLC_SHIFT)
    forced = (jb == 0) | (jb == cur) | (jb == cur - 1)
    score = jnp.where(jb * SLC_LEN <= tt, imp + jnp.where(forced, FORCE_BONUS, 0.0), NEG_INF)
    score_sc[...] = score
    sub = 8
    groups = [score[sub * rg:sub * (rg + 1)] for rg in range(n_slc // sub)]
    ranks = [jnp.zeros((sub, tq), F32) for _ in groups]
    jrow = lax.broadcasted_iota(jnp.int32, (sub, tq), 0)
    for b2 in range(n_slc):
        sb = jnp.broadcast_to(score_sc[b2:b2 + 1, :], (sub, tq))
        for rg, sg in enumerate(groups):
            if sub * rg > b2:
                beats = sb >= sg
            elif sub * rg + sub - 1 < b2:
                beats = sb > sg
            else:
                beats = (sb > sg) | ((sb == sg) & (jrow > b2 - sub * rg))
            ranks[rg] = ranks[rg] + jnp.where(beats, 1.0, 0.0)
    n_sel = min(SLC_TOPK, n_slc)
    sel_bias_t = jnp.concatenate([jnp.where(rk < n_sel, 0.0, NEG_INF) for rk in ranks], axis=0)

    bpt = tq // SLC_LEN
    n_list = jnp.int32(0)
    for kt in range(n_slc // bpt - 1):
        rg, off = divmod(kt * bpt, sub)
        hit = jnp.max(jnp.where(ranks[rg][off:off + bpt] < n_sel, 1.0, 0.0), axis=1, keepdims=True)
        hit = jnp.max(hit, axis=0, keepdims=True)[0, 0]
        seq_ref[n_list] = jnp.int32(kt)
        n_list = n_list + ((hit > 0.0) & (kt < i)).astype(jnp.int32)
    seq_ref[n_list] = i

    pshape = (LANES - AUG_BLK_COL, tq)
    frow = lax.broadcasted_iota(jnp.int32, pshape, 0)
    sel_rows = sel_bias_t.astype(BF16)
    if n_slc < AUG_BLK_COL:
        sel_rows = jnp.concatenate([sel_rows, jnp.zeros((AUG_BLK_COL - n_slc, tq), BF16)], axis=0)
    q_slc, q_win = [], []
    for r in range(n_rep):
        pos_rows = jnp.zeros(pshape, F32)
        for k, ck in enumerate(_split3(slopes[r], pshape)):
            pos_rows = jnp.where(frow == k, ck * SLC_LEN, pos_rows)
            pos_rows = jnp.where(frow == AUG_SPLIT + k, ck, pos_rows)
        pos_rows = pos_rows.astype(BF16)
        q_slc.append(jnp.concatenate([qt[r], sel_rows, pos_rows], axis=0))
        q_win.append(jnp.concatenate([qt[r], jnp.zeros_like(sel_rows), pos_rows], axis=0))
    qs_sc[...] = jnp.concatenate(q_slc, axis=1)
    q_win = jnp.concatenate(q_win, axis=1)

    n_wt = WIN // tq + 1
    span = n_wt * tq
    j0 = jnp.maximum(i - (n_wt - 1), 0)
    start_w = pl.multiple_of(j0 * tq, tq)
    s_w = _dot(kwa_ref[0, 0, pl.ds(start_w, span), :], q_win)
    d_w = (lax.broadcasted_iota(jnp.int32, (span, tq), 1) + t0) - \
          (lax.broadcasted_iota(jnp.int32, (span, tq), 0) + start_w)
    mask_w = (d_w >= 0) & (d_w < WIN)
    p_w = jnp.concatenate([_exp2_cols(head(s_w, r), mask_w).astype(BF16) for r in range(n_rep)], axis=1)
    o_win_t = _dot(vwt_ref[0, 0, j0], p_w[0:tq])
    for jj in range(1, n_wt):
        o_win_t = o_win_t + _dot(vwt_ref[0, 0, j0 + jj], p_w[jj * tq:(jj + 1) * tq])
    o_win_t = o_win_t[0:HEAD_DIM] * (1.0 / o_win_t[HEAD_DIM:HEAD_DIM + 1])

    m_sc[...] = jnp.full(m_sc.shape, NEG_INF, F32)
    acc_sc[...] = jnp.zeros(acc_sc.shape, F32)

    def produce(kt, buf):
        start = pl.multiple_of(kt * tq, tq)
        buf[...] = _dot(ksa_ref[0, 0, pl.ds(start, tq), :], qs_sc[...])

    def consume(kt, buf, causal):
        s = buf[...]
        if causal:
            kpos = lax.broadcasted_iota(jnp.int32, s.shape, 0)
            qpos = lax.broadcasted_iota(jnp.int32, s.shape, 1) & (tq - 1)
            s = jnp.where(kpos <= qpos, s, NEG_INF)
        m_old = m_sc[...]
        m_new = jnp.maximum(m_old, jnp.max(s, axis=0, keepdims=True))
        alpha = jnp.exp2(m_old - m_new)
        p = jnp.exp2(s - m_new)
        acc_sc[...] = alpha * acc_sc[...] + _dot(vst_ref[0, 0, kt], p.astype(BF16))
        m_sc[...] = m_new

    def pair(j, carry):
        produce(seq_ref[2 * j + 1], sb_sc)
        consume(seq_ref[2 * j], sa_sc, False)
        produce(seq_ref[2 * j + 2], sa_sc)
        consume(seq_ref[2 * j + 1], sb_sc, False)
        return carry

    produce(seq_ref[0], sa_sc)
    lax.fori_loop(0, n_list // 2, pair, 0)

    @pl.when(n_list % 2 == 1)
    def _():
        produce(i, sb_sc)
        consume(seq_ref[n_list - 1], sa_sc, False)
        consume(i, sb_sc, True)

    @pl.when(n_list % 2 == 0)
    def _():
        consume(i, sa_sc, True)

    o_slc_t = acc_sc[0:HEAD_DIM, :] * (1.0 / acc_sc[HEAD_DIM:HEAD_DIM + 1, :])

    gate_sc[...] = _sigmoid(gl_ref[...].astype(F32).T)
    row0 = 3 * n_rep * g
    gate = lambda k: gate_sc[pl.ds(row0 + k, 1), :]
    for r in range(n_rep):
        o_t = (gate(3 * r) * head(o_cmp_t, r)
               + gate(3 * r + 1) * head(o_slc_t, r)
               + gate(3 * r + 2) * head(o_win_t, r))
        o_ref[:, r * HEAD_DIM:(r + 1) * HEAD_DIM] = o_t.T.astype(o_ref.dtype)


def _attn_call(proj, q_gain, cmp_k, cmp_vt, ks_aug, vs_t, kw_aug, vw_t, slopes, ovl_t,
               B, T, q_col0, gate_col0):
    n_rep = slopes.shape[1]
    gw = n_rep * HEAD_DIM
    nq = T // ATT_TQ
    nc = cmp_k.shape[2]
    n_slc = ovl_t.shape[0]
    qcb = q_col0 // gw
    gcb = gate_col0 // LANES
    assert WIN % ATT_TQ == 0
    keys = pl.BlockSpec((1, 1, T, 2 * HEAD_DIM), lambda b, g, i: (b, g, 0, 0))
    vals = pl.BlockSpec((1, 1, nq, V_ROWS, ATT_TQ), lambda b, g, i: (b, g, 0, 0, 0))
    return pl.pallas_call(
        functools.partial(_attn_kernel, n_rep=n_rep),
        grid=(B, N_KV, nq),
        in_specs=[
            pl.BlockSpec(memory_space=pltpu.SMEM),
            pl.BlockSpec((ATT_TQ, gw), lambda b, g, i: (b * nq + i, qcb + g)),
            pl.BlockSpec((ATT_TQ, LANES), lambda b, g, i: (b * nq + i, gcb)),
            pl.BlockSpec((1, HEAD_DIM), lambda b, g, i: (0, 0)),
            pl.BlockSpec((1, 1, nc, HEAD_DIM), lambda b, g, i: (b, g, 0, 0)),
            pl.BlockSpec((1, 1, HEAD_DIM, nc), lambda b, g, i: (b, g, 0, 0)),
            keys, vals, keys, vals,
            pl.BlockSpec((n_slc, nc), lambda b, g, i: (0, 0)),
        ],
        out_specs=pl.BlockSpec((ATT_TQ, gw), lambda b, g, i: (b * nq + i, g)),
        out_shape=jax.ShapeDtypeStruct((B * T, N_KV * gw), BF16),
        scratch_shapes=[
            pltpu.VMEM((n_slc, ATT_TQ), F32),
            pltpu.VMEM((2 * HEAD_DIM, n_rep * ATT_TQ), BF16),
            pltpu.VMEM((ATT_TQ, n_rep * ATT_TQ), F32),
            pltpu.VMEM((ATT_TQ, n_rep * ATT_TQ), F32),
            pltpu.VMEM((1, n_rep * ATT_TQ), F32),
            pltpu.VMEM((V_ROWS, n_rep * ATT_TQ), F32),
            pltpu.VMEM((LANES, ATT_TQ), F32),
            pltpu.SMEM((nq + 1,), jnp.int32),
        ],
        compiler_params=_cparams("parallel", "parallel", "arbitrary"),
        name="nsa_attention",
    )(slopes, proj, proj, q_gain, cmp_k, cmp_vt, ks_aug, vs_t, kw_aug, vw_t, ovl_t)


def _out_kernel(a_ref, o_ref, w_ref, x_ref, g_ref, ng_ref, sc_ref, sh_ref, y_ref, h_ref):
    ka = a_ref.shape[1]
    rc = a_ref.shape[0] // OUT_ROW_CHUNKS
    for c in range(OUT_ROW_CHUNKS):
        rows = slice(c * rc, (c + 1) * rc)
        acc = _dot(a_ref[rows, :], w_ref[0:ka, :]) + _dot(o_ref[rows, :], w_ref[ka:, :])
        y = x_ref[rows, :] + g_ref[0] * acc
        y_ref[rows, :] = y
        h_ref[rows, :] = _norm_mod(y, ng_ref, sc_ref, sh_ref)


def _out_call(a, o, w_all, layer, xf, gate, ng, sc, sh, T):
    M, D = xf.shape
    ka, ko = a.shape[1], o.shape[1]
    per_b = T // OUT_BM
    row_spec = lambda width: pl.BlockSpec((OUT_BM, width), lambda i: (i, 0))
    mod_spec = pl.BlockSpec((1, 1, D), lambda i: (i // per_b, 0, 0))
    return pl.pallas_call(
        _out_kernel,
        grid=(M // OUT_BM,),
        in_specs=[
            row_spec(ka), row_spec(ko),
            pl.BlockSpec((None, ka + ko, D), lambda i: (layer, 0, 0)),
            row_spec(D), mod_spec,
            pl.BlockSpec((1, D), lambda i: (0, 0)),
            mod_spec, mod_spec,
        ],
        out_specs=[row_spec(D), row_spec(D)],
        out_shape=[jax.ShapeDtypeStruct((M, D), F32), jax.ShapeDtypeStruct((M, D), BF16)],
        compiler_params=_cparams("parallel"),
        name="out_proj",
    )(a, o, w_all, xf, gate, ng, sc, sh)


def _ffn1_kernel(h_ref, wg_ref, wu_ref, o_ref, wg_sc, wu_sc):
    @pl.when(pl.program_id(1) == 0)
    def _():
        wg_sc[...] = wg_ref[...].astype(BF16)
        wu_sc[...] = wu_ref[...].astype(BF16)

    h = h_ref[...]
    gate = _dot(h, wg_sc[...])
    up = _dot(h, wu_sc[...])
    o_ref[...] = (gate * _sigmoid(gate) * up).astype(o_ref.dtype)


def _ffn1_call(h, w_gu, layer):
    M, D = h.shape
    dff = w_gu.shape[2] // 2
    nt = dff // FFN1_BN
    return pl.pallas_call(
        _ffn1_kernel,
        grid=(nt, M // FFN1_BM),
        in_specs=[
            pl.BlockSpec((FFN1_BM, D), lambda n, i: (i, 0)),
            pl.BlockSpec((None, D, FFN1_BN), lambda n, i: (layer, 0, n)),
            pl.BlockSpec((None, D, FFN1_BN), lambda n, i: (layer, 0, n + nt)),
        ],
        out_specs=pl.BlockSpec((FFN1_BM, FFN1_BN), lambda n, i: (i, n)),
        out_shape=jax.ShapeDtypeStruct((M, dff), BF16),
        scratch_shapes=[pltpu.VMEM((D, FFN1_BN), BF16), pltpu.VMEM((D, FFN1_BN), BF16)],
        compiler_params=_cparams("arbitrary", "arbitrary"),
        name="ffn_up",
    )(h, w_gu, w_gu)


def _ffn2_kernel(h_ref, w_ref, x_ref, g_ref, y_ref):
    y_ref[...] = x_ref[...] + g_ref[0] * _dot(h_ref[...], w_ref[...])


def _ffn2_call(h, w_all, layer, xf, gate, T):
    M, D = xf.shape
    dff = h.shape[1]
    per_b = T // FFN2_BM
    return pl.pallas_call(
        _ffn2_kernel,
        grid=(M // FFN2_BM, D // FFN2_BN),
        in_specs=[
            pl.BlockSpec((FFN2_BM, dff), lambda i, n: (i, 0)),
            pl.BlockSpec((None, dff, FFN2_BN), lambda i, n: (layer, 0, n)),
            pl.BlockSpec((FFN2_BM, FFN2_BN), lambda i, n: (i, n)),
            pl.BlockSpec((1, 1, FFN2_BN), lambda i, n: (i // per_b, 0, n)),
        ],
        out_specs=pl.BlockSpec((FFN2_BM, FFN2_BN), lambda i, n: (i, n)),
        out_shape=jax.ShapeDtypeStruct((M, D), F32),
        compiler_params=_cparams("parallel", "arbitrary"),
        name="ffn_down",
    )(h, w_all, xf, gate)


def _alibi_slopes(n_heads):
    sl = 2.0 ** (-8.0 * np.arange(1, n_heads + 1) / n_heads)
    return jnp.asarray(sl, F32).reshape(N_KV, n_heads // N_KV)


def _overlap_t(T):
    nc = T // CMP_STRIDE
    n_slc = T // SLC_LEN
    cst = np.arange(nc) * CMP_STRIDE
    sst = np.arange(n_slc) * SLC_LEN
    ov = (cst[None, :] < sst[:, None] + SLC_LEN) & (cst[None, :] + CMP_LEN > sst[:, None])
    ov[:, (T - CMP_LEN) // CMP_STRIDE + 1:] = False
    return jnp.asarray(ov.astype(np.float32), BF16)


def kernel(x, c, w_ada, b_ada, norm_g, w_in, q_gain, k_gain, pe_cmp, w_cmp1, w_cmp2,
           w_pool, pool_scale, w_out, w_gate_up, w_down):
    B, T, D = x.shape
    L = w_ada.shape[0]
    pool_w = w_pool.shape[1] * w_pool.shape[2]
    kvw = N_KV * HEAD_DIM
    n_heads = (w_in.shape[2] - pool_w - 6 * kvw) // (HEAD_DIM + 3)
    att_w = n_heads * HEAD_DIM
    n_rep = n_heads // N_KV
    assert w_in.shape[2] == pool_w + att_w + 6 * kvw + 3 * n_heads
    assert T % ATT_TQ == 0 and T >= WIN + ATT_TQ and T % POOL_BT == 0 and T % KPREP_BT == 0
    assert T // SLC_LEN <= AUG_BLK_COL and pool_w % (n_rep * HEAD_DIM) == 0
    assert 1 << SLC_SHIFT == SLC_LEN and ATT_TQ & (ATT_TQ - 1) == 0
    q_col0 = pool_w
    kc_col0 = pool_w + att_w
    ks_col0 = kc_col0 + 2 * kvw
    gate_col0 = kc_col0 + 6 * kvw
    assert ks_col0 % kvw == 0 and gate_col0 % LANES == 0

    xf = x.reshape(B * T, D)
    rows = -(-B // 8) * 8
    c8 = jnp.pad(c, ((0, rows - B), (0, 0)))
    mod = _ada_call(c8, w_ada, b_ada)
    slopes = _alibi_slopes(n_heads)
    ovl_t = _overlap_t(T)

    assert 3 * n_heads <= LANES
    w_in_p = jnp.pad(w_in.astype(BF16), ((0, 0), (0, 0), (0, -w_in.shape[2] % IN_BN)))
    w_pool_b = w_pool.astype(BF16)
    pool_scale_r = pool_scale.reshape(L, 1, pool_w)
    w_out_b = w_out.astype(BF16)
    w_down_b = w_down.astype(BF16)

    for l in range(L):
        sh1, sc1, g1, sh2, sc2, g2 = [mod[l, :B, k * D:(k + 1) * D].reshape(B, 1, D) for k in range(6)]
        proj = _in_call(xf, norm_g[l, 0:1], sc1, sh1, w_in_p, l, T)
        a_out = _pool_call(proj, w_pool_b, pool_scale_r, l, T)
        ks_aug, vs_t, kw_aug, vw_t = _kprep_call(proj, k_gain[l], B, T, ks_col0)
        pe2 = pe_cmp[l].reshape(2, 2, (CMP_LEN // 2) * HEAD_DIM)
        cmp_k, cmp_vt = _cmp_call(proj, pe2, w_cmp1[l].astype(BF16), w_cmp2[l].astype(BF16), k_gain[l],
                                  B, T, kc_col0)
        o_att = _attn_call(proj, q_gain[l].reshape(1, HEAD_DIM), cmp_k, cmp_vt, ks_aug, vs_t, kw_aug, vw_t,
                           slopes, ovl_t, B, T, q_col0, gate_col0)
        xf, h2 = _out_call(a_out, o_att, w_out_b, l, xf, g1, norm_g[l, 1:2], sc2, sh2, T)
        hidden = _ffn1_call(h2, w_gate_up, l)
        xf = _ffn2_call(hidden, w_down_b, l, xf, g2, T)
    return xf.reshape(B, T, D)
```

```python
import functools

import numpy as np
import jax
import jax.numpy as jnp
from jax import lax
from jax.experimental import pallas as pl
from jax.experimental.pallas import tpu as pltpu

F32 = jnp.float32
BF16 = jnp.bfloat16

POOL_WINDOWS = (2, 4, 8, 16)
HEAD_DIM = 128
N_KV = 2
CMP_LEN = 32
CMP_STRIDE = 16
SLC_LEN = 64
SLC_SHIFT = 6
SLC_TOPK = 16
WIN = 512
NORM_EPS = 1e-6
NEG_INF = -1e30
FORCE_BONUS = 1e3

LANES = 128
POOL_HALO = 16
POOL_PAD = 8
assert all(w & (w - 1) == 0 and w <= POOL_HALO for w in POOL_WINDOWS)
VMEM_LIMIT_BYTES = 56 * 1024 * 1024

ADA_BN = 1024
IN_BM, IN_BN = 1024, 1280
KPREP_BT = 1024
POOL_BT = 1024
ATT_TQ = 256
OUT_BM = 512
OUT_ROW_CHUNKS = 2
FFN1_BM, FFN1_BN = 1024, 512
FFN2_BM, FFN2_BN = 1024, 512


def _cparams(*sem):
    return pltpu.CompilerParams(dimension_semantics=sem, vmem_limit_bytes=VMEM_LIMIT_BYTES)


def _dot(a, b):
    return jnp.dot(a, b, preferred_element_type=F32)


def _dot_nt(a, b):
    return lax.dot_general(a, b, (((1,), (1,)), ((), ())), preferred_element_type=F32)


def _rms(x):
    return x * lax.rsqrt(jnp.mean(x * x, axis=-1, keepdims=True) + NORM_EPS)


def _sigmoid(x):
    return 1.0 / (1.0 + jnp.exp(-x))


def _ada_kernel(c_ref, w_ref, b_ref, o_ref):
    c = c_ref[...]
    cs = c * _sigmoid(c)
    o_ref[0] = _dot(cs, w_ref[0]) + b_ref[0]


def _ada_call(c8, w_ada, b_ada):
    L, D, N = w_ada.shape
    rows = c8.shape[0]
    return pl.pallas_call(
        _ada_kernel,
        grid=(L, N // ADA_BN),
        in_specs=[
            pl.BlockSpec((rows, D), lambda l, n: (0, 0)),
            pl.BlockSpec((1, D, ADA_BN), lambda l, n: (l, 0, n)),
            pl.BlockSpec((1, 1, ADA_BN), lambda l, n: (l, 0, n)),
        ],
        out_specs=pl.BlockSpec((1, rows, ADA_BN), lambda l, n: (l, 0, n)),
        out_shape=jax.ShapeDtypeStruct((L, rows, N), F32),
        compiler_params=_cparams("parallel", "arbitrary"),
        name="ada_mod",
    )(c8, w_ada, b_ada.reshape(L, 1, N))


def _norm_mod(x, ng_ref, sc_ref, sh_ref):
    y = _rms(x) * ng_ref[...]
    return (y * (1.0 + sc_ref[0]) + sh_ref[0]).astype(BF16)


def _lookahead_row_tile(n_row_tiles):
    def idx(i, n):
        return jnp.where((i == 0) & (n == 0), 0, jnp.minimum(i + 1, n_row_tiles - 1))
    return idx


def _norm_chunks(n_col_steps):
    assert n_col_steps >= 2
    return 1 << ((n_col_steps - 1).bit_length() - 1)


def _norm_matmul_steps(x_ref, ng_ref, sc_ref, sh_ref, h_sc, emit, n_chunks):
    i, n = pl.program_id(0), pl.program_id(1)
    rows = x_ref.shape[0] // n_chunks
    slab_step = (n >= 1) & (n <= n_chunks)

    @pl.when((i == 0) & (n == 0))
    def _():
        h_sc[0] = _norm_mod(x_ref[...], ng_ref, sc_ref, sh_ref)

    @pl.when(jnp.logical_not(slab_step))
    def _():
        emit(h_sc[i % 2])

    for slot in (0, 1):
        @pl.when(slab_step & (i % 2 == slot))
        def _(slot=slot):
            emit(h_sc[slot])
            slab = pl.ds(pl.multiple_of((n - 1) * rows, rows), rows)
            h_sc[1 - slot, slab, :] = _norm_mod(x_ref[slab, :], ng_ref, sc_ref, sh_ref)


def _in_kernel(x_ref, ng_ref, sc_ref, sh_ref, w_ref, o_ref, h_sc, *, n_chunks):
    def emit(h):
        o_ref[...] = _dot(h, w_ref[...]).astype(o_ref.dtype)

    _norm_matmul_steps(x_ref, ng_ref, sc_ref, sh_ref, h_sc, emit, n_chunks)


def _in_call(xf, ng, sc, sh, w_all, layer, T):
    M, D = xf.shape
    N = w_all.shape[2]
    per_b = T // IN_BM
    nm = M // IN_BM
    n_chunks = _norm_chunks(N // IN_BN)
    row = _lookahead_row_tile(nm)
    return pl.pallas_call(
        functools.partial(_in_kernel, n_chunks=n_chunks),
        grid=(nm, N // IN_BN),
        in_specs=[
            pl.BlockSpec((IN_BM, D), lambda i, n: (row(i, n), 0)),
            pl.BlockSpec((1, D), lambda i, n: (0, 0)),
            pl.BlockSpec((1, 1, D), lambda i, n: (row(i, n) // per_b, 0, 0)),
            pl.BlockSpec((1, 1, D), lambda i, n: (row(i, n) // per_b, 0, 0)),
            pl.BlockSpec((None, D, IN_BN), lambda i, n: (layer, 0, n)),
        ],
        out_specs=pl.BlockSpec((IN_BM, IN_BN), lambda i, n: (i, n)),
        out_shape=jax.ShapeDtypeStruct((M, N), BF16),
        scratch_shapes=[pltpu.VMEM((2, IN_BM, D), BF16)],
        compiler_params=_cparams("arbitrary", "arbitrary"),
        name="in_proj",
    )(xf, ng, sc, sh, w_all)


def _pool_kernel(u_ref, halo_ref, w_ref, ps_ref, o_ref, ext_sc, sa_sc, sb_sc, *, tiles_per_batch):
    i = pl.program_id(0)
    bt = u_ref.shape[0]
    group = w_ref.shape[1]
    first = POOL_PAD + POOL_HALO
    n = first + bt
    tile_in_batch = i % tiles_per_batch
    for buf in (ext_sc, sa_sc, sb_sc):
        buf[0:POOL_PAD, :] = jnp.zeros((POOL_PAD, buf.shape[1]), F32)
    ext_sc[POOL_PAD:first, :] = jnp.where(tile_in_batch == 0, 0.0, halo_ref[...].astype(F32))
    ext_sc[first:, :] = u_ref[...].astype(F32)
    t_head = lax.broadcasted_iota(jnp.int32, (POOL_HALO, 1), 0)
    for gi, w in enumerate(POOL_WINDOWS):
        cols = slice(gi * group, (gi + 1) * group)
        src, col_sel, k = ext_sc, cols, 1
        for dst in (sa_sc, sb_sc, sa_sc, sb_sc):
            if k >= w:
                break
            dst[POOL_PAD:n, :] = src[POOL_PAD:n, col_sel] + src[POOL_PAD - k:n - k, col_sel]
            src, col_sel, k = dst, slice(None), 2 * k
        tok = ext_sc[first:, cols]
        pooled = src[first:n, col_sel] * (1.0 / w) - tok
        cnt = jnp.minimum(t_head + 1, w).astype(F32)
        head = src[first:first + POOL_HALO, col_sel] / cnt - tok[0:POOL_HALO]
        head = jnp.where(tile_in_batch == 0, head, pooled[0:POOL_HALO])
        pooled = jnp.concatenate([head, pooled[POOL_HALO:]], axis=0)
        mixed = _dot(pooled.astype(BF16), w_ref[gi])
        o_ref[:, cols] = (mixed * ps_ref[:, cols]).astype(o_ref.dtype)


def _pool_call(proj, w_pool_all, pool_scale_all, layer, T):
    M = proj.shape[0]
    _, ng, group, _ = w_pool_all.shape
    width = ng * group
    tiles_per_batch = T // POOL_BT
    halo_blocks = POOL_BT // POOL_HALO
    return pl.pallas_call(
        functools.partial(_pool_kernel, tiles_per_batch=tiles_per_batch),
        grid=(M // POOL_BT,),
        in_specs=[
            pl.BlockSpec((POOL_BT, width), lambda i: (i, 0)),
            pl.BlockSpec((POOL_HALO, width), lambda i: (jnp.maximum(i * halo_blocks - 1, 0), 0)),
            pl.BlockSpec((None, ng, group, group), lambda i: (layer, 0, 0, 0)),
            pl.BlockSpec((None, 1, width), lambda i: (layer, 0, 0)),
        ],
        out_specs=pl.BlockSpec((POOL_BT, width), lambda i: (i, 0)),
        out_shape=jax.ShapeDtypeStruct((M, width), BF16),
        scratch_shapes=[pltpu.VMEM((POOL_PAD + POOL_HALO + POOL_BT, width), F32),
                        pltpu.VMEM((POOL_PAD + POOL_HALO + POOL_BT, group), F32),
                        pltpu.VMEM((POOL_PAD + POOL_HALO + POOL_BT, group), F32)],
        compiler_params=_cparams("parallel"),
        name="pool_mixer",
    )(proj, proj, w_pool_all, pool_scale_all)


AUG_SPLIT = 3
AUG_BLK_COL = SLC_LEN
AUG_OFF_COL = SLC_LEN + AUG_SPLIT
V_ROWS = HEAD_DIM + 16
LOG2E = 1.4426950408889634


def _key_extra(t):
    lane = lax.broadcasted_iota(jnp.int32, (t.shape[0], LANES), 1)
    blk = lax.shift_right_logical(t, SLC_SHIFT)
    off = t & (SLC_LEN - 1)
    extra = jnp.where(lane == blk, 1.0, 0.0)
    extra = jnp.where((lane >= AUG_BLK_COL) & (lane < AUG_OFF_COL), blk.astype(F32), extra)
    extra = jnp.where((lane >= AUG_OFF_COL) & (lane < AUG_OFF_COL + AUG_SPLIT), off.astype(F32), extra)
    return extra


def _value_tile_t(v_t):
    pad = lax.broadcasted_iota(jnp.int32, (V_ROWS - HEAD_DIM, v_t.shape[1]), 0)
    return jnp.concatenate([v_t, jnp.where(pad == 0, 1.0, 0.0)], axis=0).astype(BF16)


def _kprep_kernel(ks_ref, vs_ref, kw_ref, vw_ref, kg_ref, ksa_ref, vso_ref, kwa_ref, vwo_ref,
                  *, tiles_per_batch):
    bt = ks_ref.shape[0]
    kt = vso_ref.shape[4]
    t = (pl.program_id(0) % tiles_per_batch) * bt + lax.broadcasted_iota(jnp.int32, (bt, 1), 0)
    extra = _key_extra(t).astype(BF16)
    for g in range(N_KV):
        cols = slice(g * HEAD_DIM, (g + 1) * HEAD_DIM)
        ksn = _rms(ks_ref[:, cols].astype(F32)) * kg_ref[1:2, :]
        kwn = _rms(kw_ref[:, cols].astype(F32)) * kg_ref[2:3, :]
        ksa_ref[0, g, :, 0:HEAD_DIM] = ksn.astype(BF16)
        ksa_ref[0, g, :, HEAD_DIM:] = extra
        kwa_ref[0, g, :, 0:HEAD_DIM] = kwn.astype(BF16)
        kwa_ref[0, g, :, HEAD_DIM:] = extra
        vs_t = _value_tile_t(vs_ref[:, cols].astype(F32).T)
        vw_t = _value_tile_t(vw_ref[:, cols].astype(F32).T)
        for j in range(bt // kt):
            vso_ref[0, g, j] = vs_t[:, j * kt:(j + 1) * kt]
            vwo_ref[0, g, j] = vw_t[:, j * kt:(j + 1) * kt]


def _kprep_call(proj, k_gain, B, T, col0):
    kvw = N_KV * HEAD_DIM
    cb = col0 // kvw
    tiles_per_batch = T // KPREP_BT
    vt_per_tile = KPREP_BT // ATT_TQ
    aug = jax.ShapeDtypeStruct((B, N_KV, T, 2 * HEAD_DIM), BF16)
    val = jax.ShapeDtypeStruct((B, N_KV, T // ATT_TQ, V_ROWS, ATT_TQ), BF16)
    in_spec = lambda j: pl.BlockSpec((KPREP_BT, kvw), lambda i: (i, cb + j))
    out_map = lambda i: (i // tiles_per_batch, 0, i % tiles_per_batch, 0)
    val_map = lambda i: (i // tiles_per_batch, 0, i % tiles_per_batch, 0, 0)
    return pl.pallas_call(
        functools.partial(_kprep_kernel, tiles_per_batch=tiles_per_batch),
        grid=(B * tiles_per_batch,),
        in_specs=[in_spec(0), in_spec(1), in_spec(2), in_spec(3),
                  pl.BlockSpec((3, HEAD_DIM), lambda i: (0, 0))],
        out_specs=[pl.BlockSpec((1, N_KV, KPREP_BT, 2 * HEAD_DIM), out_map),
                   pl.BlockSpec((1, N_KV, vt_per_tile, V_ROWS, ATT_TQ), val_map),
                   pl.BlockSpec((1, N_KV, KPREP_BT, 2 * HEAD_DIM), out_map),
                   pl.BlockSpec((1, N_KV, vt_per_tile, V_ROWS, ATT_TQ), val_map)],
        out_shape=[aug, val, aug, val],
        compiler_params=_cparams("parallel"),
        name="kv_prep",
    )(proj, proj, proj, proj, k_gain)


def _compress_one(src_ref, f32_sc, pe_ref, w1_ref, kv, nc):
    half = CMP_LEN // 2
    assert CMP_STRIDE == half
    f32_sc[...] = src_ref[...].astype(F32)
    xs = [f32_sc[pl.ds(j, nc, stride=CMP_STRIDE), :] for j in range(half)]
    x = jnp.concatenate(xs, axis=1)
    kdim = half * HEAD_DIM
    lo = _dot((x + pe_ref[kv, 0:1, :]).astype(BF16), w1_ref[kv, 0:kdim, :])
    hi = _dot((x + pe_ref[kv, 1:2, :]).astype(BF16), w1_ref[kv, kdim:, :])
    pre = lo + pltpu.roll(hi, nc - 1, 0)
    return (pre * _sigmoid(pre)).astype(BF16)


def _cmp_kernel(k_ref, v_ref, pe_ref, w1_ref, w2k_ref, w2vt_ref, kg_ref, ko_ref, vo_ref, f32_sc):
    nc = ko_ref.shape[2]
    kc = _dot(_compress_one(k_ref, f32_sc, pe_ref, w1_ref, 0, nc), w2k_ref[...])
    ko_ref[0, 0] = (_rms(kc) * kg_ref[0:1, :]).astype(ko_ref.dtype)
    vct = _dot_nt(w2vt_ref[...], _compress_one(v_ref, f32_sc, pe_ref, w1_ref, 1, nc))
    vo_ref[0, 0] = vct.astype(vo_ref.dtype)


def _cmp_call(proj, pe2, w1, w2, k_gain, B, T, col0):
    nc = T // CMP_STRIDE
    cb = col0 // HEAD_DIM
    kdim = CMP_LEN * HEAD_DIM
    return pl.pallas_call(
        _cmp_kernel,
        grid=(B, N_KV),
        in_specs=[
            pl.BlockSpec((T, HEAD_DIM), lambda b, g: (b, cb + g)),
            pl.BlockSpec((T, HEAD_DIM), lambda b, g: (b, cb + N_KV + g)),
            pl.BlockSpec((2, 2, kdim // 2), lambda b, g: (0, 0, 0)),
            pl.BlockSpec((2, kdim, HEAD_DIM), lambda b, g: (0, 0, 0)),
            pl.BlockSpec((HEAD_DIM, HEAD_DIM), lambda b, g: (0, 0)),
            pl.BlockSpec((HEAD_DIM, HEAD_DIM), lambda b, g: (0, 0)),
            pl.BlockSpec((3, HEAD_DIM), lambda b, g: (0, 0)),
        ],
        out_specs=[pl.BlockSpec((1, 1, nc, HEAD_DIM), lambda b, g: (b, g, 0, 0)),
                   pl.BlockSpec((1, 1, HEAD_DIM, nc), lambda b, g: (b, g, 0, 0))],
        out_shape=[jax.ShapeDtypeStruct((B, N_KV, nc, HEAD_DIM), BF16),
                   jax.ShapeDtypeStruct((B, N_KV, HEAD_DIM, nc), BF16)],
        scratch_shapes=[pltpu.VMEM((T, HEAD_DIM), F32)],
        compiler_params=_cparams("parallel", "arbitrary"),
        name="compress",
    )(proj, proj, pe2, w1, w2[0], w2[1].T, k_gain)


MAX_FLOOR = 0.1 * NEG_INF


def _exp2_cols(s, mask):
    s = jnp.where(mask, s, NEG_INF)
    m = jnp.maximum(jnp.max(s, axis=0, keepdims=True), MAX_FLOOR)
    return jnp.exp2(s - m)


def _split3(c, shape):
    c = jnp.full(shape, c, F32)
    c1 = c.astype(BF16).astype(F32)
    r1 = c - c1
    c2 = r1.astype(BF16).astype(F32)
    c3 = (r1 - c2).astype(BF16).astype(F32)
    return c1, c2, c3


def _attn_kernel(slope_ref, q_ref, gl_ref, qg_ref, cmp_k_ref, cmp_vt_ref, ksa_ref, vst_ref,
                 kwa_ref, vwt_ref, ovl_ref, o_ref, score_sc, qs_sc, sa_sc, sb_sc, m_sc, acc_sc, gate_sc,
                 seq_ref, *, n_rep):
    g = pl.program_id(1)
    i = pl.program_id(2)
    tq = q_ref.shape[0]
    nc = cmp_k_ref.shape[2]
    n_slc = ovl_ref.shape[0]
    t0 = i * tq
    slopes = [slope_ref[g, r] * LOG2E for r in range(n_rep)]
    scale = HEAD_DIM ** -0.5 * LOG2E
    head = lambda a, r: a[:, r * tq:(r + 1) * tq]

    qt = []
    for r in range(n_rep):
        x = q_ref[:, r * HEAD_DIM:(r + 1) * HEAD_DIM].astype(F32)
        qt.append((_rms(x) * qg_ref[...] * scale).T.astype(BF16))
    qt_all = jnp.concatenate(qt, axis=1)

    s_all = _dot(cmp_k_ref[0, 0], qt_all)
    c_idx = lax.broadcasted_iota(jnp.int32, (nc, tq), 0)
    t_idx = lax.broadcasted_iota(jnp.int32, (nc, tq), 1) + t0
    dist_c = (t_idx - (c_idx * CMP_STRIDE + (CMP_LEN - 1))).astype(F32)
    mask_c = dist_c >= 0.0
    p_cmp = []
    p_sum = jnp.zeros((nc, tq), F32)
    for r in range(n_rep):
        p = _exp2_cols(head(s_all, r) - slopes[r] * dist_c, mask_c)
        l = jnp.sum(p, axis=0, keepdims=True)
        p = p * (1.0 / jnp.where(l > 0.0, l, 1.0))
        p_cmp.append(p.astype(BF16))
        p_sum = p_sum + p
    o_cmp_t = _dot(cmp_vt_ref[0, 0], jnp.concatenate(p_cmp, axis=1))

    p_hi = p_sum.astype(BF16)
    p_lo = (p_sum - p_hi.astype(F32)).astype(BF16)
    ovl = ovl_ref[...]
    imp = _dot(ovl, p_hi) + _dot(ovl, p_lo)
    jb = lax.broadcasted_iota(jnp.int32, (n_slc, tq), 0)
    tt = lax.broadcasted_iota(jnp.int32, (n_slc, tq), 1) + t0
    cur = lax.shift_right_logical(tt, SLC_SHIFT)
    forced = (jb == 0) | (jb == cur) | (jb == cur - 1)
    score = jnp.where(jb * SLC_LEN <= tt, imp + jnp.where(forced, FORCE_BONUS, 0.0), NEG_INF)
    score_sc[...] = score
    sub = 8
    groups = [score[sub * rg:sub * (rg + 1)] for rg in range(n_slc // sub)]
    ranks = [jnp.zeros((sub, tq), F32) for _ in groups]
    jrow = lax.broadcasted_iota(jnp.int32, (sub, tq), 0)
    for b2 in range(n_slc):
        sb = jnp.broadcast_to(score_sc[b2:b2 + 1, :], (sub, tq))
        for rg, sg in enumerate(groups):
            if sub * rg > b2:
                beats = sb >= sg
            elif sub * rg + sub - 1 < b2:
                beats = sb > sg
            else:
                beats = (sb > sg) | ((sb == sg) & (jrow > b2 - sub * rg))
            ranks[rg] = ranks[rg] + jnp.where(beats, 1.0, 0.0)
    n_sel = min(SLC_TOPK, n_slc)
    sel_bias_t = jnp.concatenate([jnp.where(rk < n_sel, 0.0, NEG_INF) for rk in ranks], axis=0)

    bpt = tq // SLC_LEN
    n_list = jnp.int32(0)
    for kt in range(n_slc // bpt - 1):
        rg, off = divmod(kt * bpt, sub)
        hit = jnp.max(jnp.where(ranks[rg][off:off + bpt] < n_sel, 1.0, 0.0), axis=1, keepdims=True)
        hit = jnp.max(hit, axis=0, keepdims=True)[0, 0]
        seq_ref[n_list] = jnp.int32(kt)
        n_list = n_list + ((hit > 0.0) & (kt < i)).astype(jnp.int32)
    seq_ref[n_list] = i

    pshape = (LANES - AUG_BLK_COL, tq)
    frow = lax.broadcasted_iota(jnp.int32, pshape, 0)
    sel_rows = sel_bias_t.astype(BF16)
    if n_slc < AUG_BLK_COL:
        sel_rows = jnp.concatenate([sel_rows, jnp.zeros((AUG_BLK_COL - n_slc, tq), BF16)], axis=0)
    q_slc, q_win = [], []
    for r in range(n_rep):
        pos_rows = jnp.zeros(pshape, F32)
        for k, ck in enumerate(_split3(slopes[r], pshape)):
            pos_rows = jnp.where(frow == k, ck * SLC_LEN, pos_rows)
            pos_rows = jnp.where(frow == AUG_SPLIT + k, ck, pos_rows)
        pos_rows = pos_rows.astype(BF16)
        q_slc.append(jnp.concatenate([qt[r], sel_rows, pos_rows], axis=0))
        q_win.append(jnp.concatenate([qt[r], jnp.zeros_like(sel_rows), pos_rows], axis=0))
    qs_sc[...] = jnp.concatenate(q_slc, axis=1)
    q_win = jnp.concatenate(q_win, axis=1)

    n_wt = WIN // tq + 1
    span = n_wt * tq
    j0 = jnp.maximum(i - (n_wt - 1), 0)
    start_w = pl.multiple_of(j0 * tq, tq)
    s_w = _dot(kwa_ref[0, 0, pl.ds(start_w, span), :], q_win)
    d_w = (lax.broadcasted_iota(jnp.int32, (span, tq), 1) + t0) - \
          (lax.broadcasted_iota(jnp.int32, (span, tq), 0) + start_w)
    mask_w = (d_w >= 0) & (d_w < WIN)
    p_w = jnp.concatenate([_exp2_cols(head(s_w, r), mask_w).astype(BF16) for r in range(n_rep)], axis=1)
    o_win_t = _dot(vwt_ref[0, 0, j0], p_w[0:tq])
    for jj in range(1, n_wt):
        o_win_t = o_win_t + _dot(vwt_ref[0, 0, j0 + jj], p_w[jj * tq:(jj + 1) * tq])
    o_win_t = o_win_t[0:HEAD_DIM] * (1.0 / o_win_t[HEAD_DIM:HEAD_DIM + 1])

    m_sc[...] = jnp.full(m_sc.shape, NEG_INF, F32)
    acc_sc[...] = jnp.zeros(acc_sc.shape, F32)

    def produce(kt, buf):
        start = pl.multiple_of(kt * tq, tq)
        buf[...] = _dot(ksa_ref[0, 0, pl.ds(start, tq), :], qs_sc[...])

    def consume(kt, buf, causal):
        s = buf[...]
        if causal:
            kpos = lax.broadcasted_iota(jnp.int32, s.shape, 0)
            qpos = lax.broadcasted_iota(jnp.int32, s.shape, 1) & (tq - 1)
            s = jnp.where(kpos <= qpos, s, NEG_INF)
        m_old = m_sc[...]
        m_new = jnp.maximum(m_old, jnp.max(s, axis=0, keepdims=True))
        alpha = jnp.exp2(m_old - m_new)
        p = jnp.exp2(s - m_new)
        acc_sc[...] = alpha * acc_sc[...] + _dot(vst_ref[0, 0, kt], p.astype(BF16))
        m_sc[...] = m_new

    def pair(j, carry):
        produce(seq_ref[2 * j + 1], sb_sc)
        consume(seq_ref[2 * j], sa_sc, False)
        produce(seq_ref[2 * j + 2], sa_sc)
        consume(seq_ref[2 * j + 1], sb_sc, False)
        return carry

    produce(seq_ref[0], sa_sc)
    lax.fori_loop(0, n_list // 2, pair, 0)

    @pl.when(n_list % 2 == 1)
    def _():
        produce(i, sb_sc)
        consume(seq_ref[n_list - 1], sa_sc, False)
        consume(i, sb_sc, True)

    @pl.when(n_list % 2 == 0)
    def _():
        consume(i, sa_sc, True)

    o_slc_t = acc_sc[0:HEAD_DIM, :] * (1.0 / acc_sc[HEAD_DIM:HEAD_DIM + 1, :])

    gate_sc[...] = _sigmoid(gl_ref[...].astype(F32).T)
    row0 = 3 * n_rep * g
    gate = lambda k: gate_sc[pl.ds(row0 + k, 1), :]
    for r in range(n_rep):
        o_t = (gate(3 * r) * head(o_cmp_t, r)
               + gate(3 * r + 1) * head(o_slc_t, r)
               + gate(3 * r + 2) * head(o_win_t, r))
        o_ref[:, r * HEAD_DIM:(r + 1) * HEAD_DIM] = o_t.T.astype(o_ref.dtype)


def _attn_call(proj, q_gain, cmp_k, cmp_vt, ks_aug, vs_t, kw_aug, vw_t, slopes, ovl_t,
               B, T, q_col0, gate_col0):
    n_rep = slopes.shape[1]
    gw = n_rep * HEAD_DIM
    nq = T // ATT_TQ
    nc = cmp_k.shape[2]
    n_slc = ovl_t.shape[0]
    qcb = q_col0 // gw
    gcb = gate_col0 // LANES
    assert WIN % ATT_TQ == 0
    keys = pl.BlockSpec((1, 1, T, 2 * HEAD_DIM), lambda b, g, i: (b, g, 0, 0))
    vals = pl.BlockSpec((1, 1, nq, V_ROWS, ATT_TQ), lambda b, g, i: (b, g, 0, 0, 0))
    return pl.pallas_call(
        functools.partial(_attn_kernel, n_rep=n_rep),
        grid=(B, N_KV, nq),
        in_specs=[
            pl.BlockSpec(memory_space=pltpu.SMEM),
            pl.BlockSpec((ATT_TQ, gw), lambda b, g, i: (b * nq + i, qcb + g)),
            pl.BlockSpec((ATT_TQ, LANES), lambda b, g, i: (b * nq + i, gcb)),
            pl.BlockSpec((1, HEAD_DIM), lambda b, g, i: (0, 0)),
            pl.BlockSpec((1, 1, nc, HEAD_DIM), lambda b, g, i: (b, g, 0, 0)),
            pl.BlockSpec((1, 1, HEAD_DIM, nc), lambda b, g, i: (b, g, 0, 0)),
            keys, vals, keys, vals,
            pl.BlockSpec((n_slc, nc), lambda b, g, i: (0, 0)),
        ],
        out_specs=pl.BlockSpec((ATT_TQ, gw), lambda b, g, i: (b * nq + i, g)),
        out_shape=jax.ShapeDtypeStruct((B * T, N_KV * gw), BF16),
        scratch_shapes=[
            pltpu.VMEM((n_slc, ATT_TQ), F32),
            pltpu.VMEM((2 * HEAD_DIM, n_rep * ATT_TQ), BF16),
            pltpu.VMEM((ATT_TQ, n_rep * ATT_TQ), F32),
            pltpu.VMEM((ATT_TQ, n_rep * ATT_TQ), F32),
            pltpu.VMEM((1, n_rep * ATT_TQ), F32),
            pltpu.VMEM((V_ROWS, n_rep * ATT_TQ), F32),
            pltpu.VMEM((LANES, ATT_TQ), F32),
            pltpu.SMEM((nq + 1,), jnp.int32),
        ],
        compiler_params=_cparams("parallel", "parallel", "arbitrary"),
        name="nsa_attention",
    )(slopes, proj, proj, q_gain, cmp_k, cmp_vt, ks_aug, vs_t, kw_aug, vw_t, ovl_t)


def _out_kernel(a_ref, o_ref, w_ref, x_ref, g_ref, ng_ref, sc_ref, sh_ref, y_ref, h_ref):
    ka = a_ref.shape[1]
    rc = a_ref.shape[0] // OUT_ROW_CHUNKS
    for c in range(OUT_ROW_CHUNKS):
        rows = slice(c * rc, (c + 1) * rc)
        acc = _dot(a_ref[rows, :], w_ref[0:ka, :]) + _dot(o_ref[rows, :], w_ref[ka:, :])
        y = x_ref[rows, :] + g_ref[0] * acc
        y_ref[rows, :] = y
        h_ref[rows, :] = _norm_mod(y, ng_ref, sc_ref, sh_ref)


def _out_call(a, o, w_all, layer, xf, gate, ng, sc, sh, T):
    M, D = xf.shape
    ka, ko = a.shape[1], o.shape[1]
    per_b = T // OUT_BM
    row_spec = lambda width: pl.BlockSpec((OUT_BM, width), lambda i: (i, 0))
    mod_spec = pl.BlockSpec((1, 1, D), lambda i: (i // per_b, 0, 0))
    return pl.pallas_call(
        _out_kernel,
        grid=(M // OUT_BM,),
        in_specs=[
            row_spec(ka), row_spec(ko),
            pl.BlockSpec((None, ka + ko, D), lambda i: (layer, 0, 0)),
            row_spec(D), mod_spec,
            pl.BlockSpec((1, D), lambda i: (0, 0)),
            mod_spec, mod_spec,
        ],
        out_specs=[row_spec(D), row_spec(D)],
        out_shape=[jax.ShapeDtypeStruct((M, D), F32), jax.ShapeDtypeStruct((M, D), BF16)],
        compiler_params=_cparams("parallel"),
        name="out_proj",
    )(a, o, w_all, xf, gate, ng, sc, sh)


def _ffn1_kernel(h_ref, wg_ref, wu_ref, wd_ref, o_ref, wdb_ref, wg_sc, wu_sc):
    @pl.when(pl.program_id(1) == 0)
    def _():
        wg_sc[...] = wg_ref[...].astype(BF16)
        wu_sc[...] = wu_ref[...].astype(BF16)

    wdb_ref[...] = wd_ref[...].astype(BF16)
    h = h_ref[...]
    gate = _dot(h, wg_sc[...])
    up = _dot(h, wu_sc[...])
    o_ref[...] = (gate * _sigmoid(gate) * up).astype(o_ref.dtype)


def _ffn1_call(h, w_gu, w_down, layer):
    M, D = h.shape
    dff = w_gu.shape[2] // 2
    nt, nm = dff // FFN1_BN, M // FFN1_BM
    slab = dff // (nt * nm)
    assert slab * nt * nm == dff and slab % 16 == 0
    return pl.pallas_call(
        _ffn1_kernel,
        grid=(nt, nm),
        in_specs=[
            pl.BlockSpec((FFN1_BM, D), lambda n, i: (i, 0)),
            pl.BlockSpec((None, D, FFN1_BN), lambda n, i: (layer, 0, n)),
            pl.BlockSpec((None, D, FFN1_BN), lambda n, i: (layer, 0, n + nt)),
            pl.BlockSpec((None, slab, D), lambda n, i: (layer, n * nm + i, 0)),
        ],
        out_specs=[pl.BlockSpec((FFN1_BM, FFN1_BN), lambda n, i: (i, n)),
                   pl.BlockSpec((slab, D), lambda n, i: (n * nm + i, 0))],
        out_shape=[jax.ShapeDtypeStruct((M, dff), BF16), jax.ShapeDtypeStruct((dff, D), BF16)],
        scratch_shapes=[pltpu.VMEM((D, FFN1_BN), BF16), pltpu.VMEM((D, FFN1_BN), BF16)],
        compiler_params=_cparams("arbitrary", "arbitrary"),
        name="ffn_up",
    )(h, w_gu, w_gu, w_down)


def _ffn2_kernel(h_ref, w_ref, x_ref, g_ref, y_ref):
    y_ref[...] = x_ref[...] + g_ref[0] * _dot(h_ref[...], w_ref[...])


def _ffn2_call(h, w, xf, gate, T):
    M, D = xf.shape
    dff = h.shape[1]
    per_b = T // FFN2_BM
    return pl.pallas_call(
        _ffn2_kernel,
        grid=(M // FFN2_BM, D // FFN2_BN),
        in_specs=[
            pl.BlockSpec((FFN2_BM, dff), lambda i, n: (i, 0)),
            pl.BlockSpec((dff, FFN2_BN), lambda i, n: (0, n)),
            pl.BlockSpec((FFN2_BM, FFN2_BN), lambda i, n: (i, n)),
            pl.BlockSpec((1, 1, FFN2_BN), lambda i, n: (i // per_b, 0, n)),
        ],
        out_specs=pl.BlockSpec((FFN2_BM, FFN2_BN), lambda i, n: (i, n)),
        out_shape=jax.ShapeDtypeStruct((M, D), F32),
        compiler_params=_cparams("parallel", "arbitrary"),
        name="ffn_down",
    )(h, w, xf, gate)


def _alibi_slopes(n_heads):
    sl = 2.0 ** (-8.0 * np.arange(1, n_heads + 1) / n_heads)
    return jnp.asarray(sl, F32).reshape(N_KV, n_heads // N_KV)


def _overlap_t(T):
    nc = T // CMP_STRIDE
    n_slc = T // SLC_LEN
    cst = np.arange(nc) * CMP_STRIDE
    sst = np.arange(n_slc) * SLC_LEN
    ov = (cst[None, :] < sst[:, None] + SLC_LEN) & (cst[None, :] + CMP_LEN > sst[:, None])
    ov[:, (T - CMP_LEN) // CMP_STRIDE + 1:] = False
    return jnp.asarray(ov.astype(np.float32), BF16)


def kernel(x, c, w_ada, b_ada, norm_g, w_in, q_gain, k_gain, pe_cmp, w_cmp1, w_cmp2,
           w_pool, pool_scale, w_out, w_gate_up, w_down):
    B, T, D = x.shape
    L = w_ada.shape[0]
    pool_w = w_pool.shape[1] * w_pool.shape[2]
    kvw = N_KV * HEAD_DIM
    n_heads = (w_in.shape[2] - pool_w - 6 * kvw) // (HEAD_DIM + 3)
    att_w = n_heads * HEAD_DIM
    n_rep = n_heads // N_KV
    assert w_in.shape[2] == pool_w + att_w + 6 * kvw + 3 * n_heads
    assert T % ATT_TQ == 0 and T >= WIN + ATT_TQ and T % POOL_BT == 0 and T % KPREP_BT == 0
    assert T // SLC_LEN <= AUG_BLK_COL and pool_w % (n_rep * HEAD_DIM) == 0
    assert 1 << SLC_SHIFT == SLC_LEN and ATT_TQ & (ATT_TQ - 1) == 0
    q_col0 = pool_w
    kc_col0 = pool_w + att_w
    ks_col0 = kc_col0 + 2 * kvw
    gate_col0 = kc_col0 + 6 * kvw
    assert ks_col0 % kvw == 0 and gate_col0 % LANES == 0

    xf = x.reshape(B * T, D)
    rows = -(-B // 8) * 8
    c8 = jnp.pad(c, ((0, rows - B), (0, 0)))
    mod = _ada_call(c8, w_ada, b_ada)
    slopes = _alibi_slopes(n_heads)
    ovl_t = _overlap_t(T)

    assert 3 * n_heads <= LANES
    w_in_p = jnp.pad(w_in.astype(BF16), ((0, 0), (0, 0), (0, -w_in.shape[2] % IN_BN)))
    w_pool_b = w_pool.astype(BF16)
    pool_scale_r = pool_scale.reshape(L, 1, pool_w)
    w_out_b = w_out.astype(BF16)

    for l in range(L):
        sh1, sc1, g1, sh2, sc2, g2 = [mod[l, :B, k * D:(k + 1) * D].reshape(B, 1, D) for k in range(6)]
        proj = _in_call(xf, norm_g[l, 0:1], sc1, sh1, w_in_p, l, T)
        a_out = _pool_call(proj, w_pool_b, pool_scale_r, l, T)
        ks_aug, vs_t, kw_aug, vw_t = _kprep_call(proj, k_gain[l], B, T, ks_col0)
        pe2 = pe_cmp[l].reshape(2, 2, (CMP_LEN // 2) * HEAD_DIM)
        cmp_k, cmp_vt = _cmp_call(proj, pe2, w_cmp1[l].astype(BF16), w_cmp2[l].astype(BF16), k_gain[l],
                                  B, T, kc_col0)
        o_att = _attn_call(proj, q_gain[l].reshape(1, HEAD_DIM), cmp_k, cmp_vt, ks_aug, vs_t, kw_aug, vw_t,
                           slopes, ovl_t, B, T, q_col0, gate_col0)
        xf, h2 = _out_call(a_out, o_att, w_out_b, l, xf, g1, norm_g[l, 1:2], sc2, sh2, T)
        hidden, w_down_b = _ffn1_call(h2, w_gate_up, w_down, l)
        xf = _ffn2_call(hidden, w_down_b, xf, g2, T)
    return xf.reshape(B, T, D)
```

```python
import functools

import numpy as np
import jax
import jax.numpy as jnp
from jax import lax
from jax.experimental import pallas as pl
from jax.experimental.pallas import tpu as pltpu

F32 = jnp.float32
BF16 = jnp.bfloat16

POOL_WINDOWS = (2, 4, 8, 16)
HEAD_DIM = 128
N_KV = 2
CMP_LEN = 32
CMP_STRIDE = 16
SLC_LEN = 64
SLC_SHIFT = 6
SLC_TOPK = 16
WIN = 512
NORM_EPS = 1e-6
NEG_INF = -1e30
FORCE_BONUS = 1e3

LANES = 128
POOL_HALO = 16
POOL_PAD = 8
assert all(w & (w - 1) == 0 and w <= POOL_HALO for w in POOL_WINDOWS)
VMEM_LIMIT_BYTES = 56 * 1024 * 1024

ADA_BN = 1024
IN_BM, IN_BN = 1024, 1280
KPREP_BT = 1024
POOL_BT = 1024
ATT_TQ = 256
OUT_BM = 512
OUT_ROW_CHUNKS = 2
FFN1_BM, FFN1_BN = 1024, 512
FFN2_BM, FFN2_BN = 1024, 512


def _cparams(*sem):
    return pltpu.CompilerParams(dimension_semantics=sem, vmem_limit_bytes=VMEM_LIMIT_BYTES)


def _dot(a, b):
    return jnp.dot(a, b, preferred_element_type=F32)


def _dot_nt(a, b):
    return lax.dot_general(a, b, (((1,), (1,)), ((), ())), preferred_element_type=F32)


def _rms(x):
    return x * lax.rsqrt(jnp.mean(x * x, axis=-1, keepdims=True) + NORM_EPS)


def _sigmoid(x):
    return 1.0 / (1.0 + jnp.exp(-x))


def _ada_kernel(c_ref, w_ref, b_ref, o_ref):
    c = c_ref[...]
    cs = c * _sigmoid(c)
    o_ref[0] = _dot(cs, w_ref[0]) + b_ref[0]


def _ada_call(c8, w_ada, b_ada):
    L, D, N = w_ada.shape
    rows = c8.shape[0]
    return pl.pallas_call(
        _ada_kernel,
        grid=(L, N // ADA_BN),
        in_specs=[
            pl.BlockSpec((rows, D), lambda l, n: (0, 0)),
            pl.BlockSpec((1, D, ADA_BN), lambda l, n: (l, 0, n)),
            pl.BlockSpec((1, 1, ADA_BN), lambda l, n: (l, 0, n)),
        ],
        out_specs=pl.BlockSpec((1, rows, ADA_BN), lambda l, n: (l, 0, n)),
        out_shape=jax.ShapeDtypeStruct((L, rows, N), F32),
        compiler_params=_cparams("parallel", "arbitrary"),
        name="ada_mod",
    )(c8, w_ada, b_ada.reshape(L, 1, N))


def _norm_mod(x, ng_ref, sc_ref, sh_ref):
    y = _rms(x) * ng_ref[...]
    return (y * (1.0 + sc_ref[0]) + sh_ref[0]).astype(BF16)


def _lookahead_row_tile(n_row_tiles):
    def idx(i, n):
        return jnp.where((i == 0) & (n == 0), 0, jnp.minimum(i + 1, n_row_tiles - 1))
    return idx


def _norm_chunks(n_col_steps):
    assert n_col_steps >= 2
    return 1 << ((n_col_steps - 1).bit_length() - 1)


def _norm_matmul_steps(x_ref, ng_ref, sc_ref, sh_ref, h_sc, emit, n_chunks):
    i, n = pl.program_id(0), pl.program_id(1)
    rows = x_ref.shape[0] // n_chunks
    slab_step = (n >= 1) & (n <= n_chunks)

    @pl.when((i == 0) & (n == 0))
    def _():
        h_sc[0] = _norm_mod(x_ref[...], ng_ref, sc_ref, sh_ref)

    @pl.when(jnp.logical_not(slab_step))
    def _():
        emit(h_sc[i % 2])

    for slot in (0, 1):
        @pl.when(slab_step & (i % 2 == slot))
        def _(slot=slot):
            emit(h_sc[slot])
            slab = pl.ds(pl.multiple_of((n - 1) * rows, rows), rows)
            h_sc[1 - slot, slab, :] = _norm_mod(x_ref[slab, :], ng_ref, sc_ref, sh_ref)


def _in_kernel(x_ref, ng_ref, sc_ref, sh_ref, w_ref, o_ref, h_sc, *, n_chunks):
    def emit(h):
        o_ref[...] = _dot(h, w_ref[...]).astype(o_ref.dtype)

    _norm_matmul_steps(x_ref, ng_ref, sc_ref, sh_ref, h_sc, emit, n_chunks)


def _in_call(xf, ng, sc, sh, w_all, layer, T):
    M, D = xf.shape
    N = w_all.shape[2]
    per_b = T // IN_BM
    nm = M // IN_BM
    n_chunks = _norm_chunks(N // IN_BN)
    row = _lookahead_row_tile(nm)
    return pl.pallas_call(
        functools.partial(_in_kernel, n_chunks=n_chunks),
        grid=(nm, N // IN_BN),
        in_specs=[
            pl.BlockSpec((IN_BM, D), lambda i, n: (row(i, n), 0)),
            pl.BlockSpec((1, D), lambda i, n: (0, 0)),
            pl.BlockSpec((1, 1, D), lambda i, n: (row(i, n) // per_b, 0, 0)),
            pl.BlockSpec((1, 1, D), lambda i, n: (row(i, n) // per_b, 0, 0)),
            pl.BlockSpec((None, D, IN_BN), lambda i, n: (layer, 0, n)),
        ],
        out_specs=pl.BlockSpec((IN_BM, IN_BN), lambda i, n: (i, n)),
        out_shape=jax.ShapeDtypeStruct((M, N), BF16),
        scratch_shapes=[pltpu.VMEM((2, IN_BM, D), BF16)],
        compiler_params=_cparams("arbitrary", "arbitrary"),
        name="in_proj",
    )(xf, ng, sc, sh, w_all)


def _pool_kernel(u_ref, halo_ref, w_ref, ps_ref, o_ref, ext_sc, sa_sc, sb_sc, *, tiles_per_batch):
    i = pl.program_id(0)
    bt = u_ref.shape[0]
    group = w_ref.shape[1]
    first = POOL_PAD + POOL_HALO
    n = first + bt
    tile_in_batch = i % tiles_per_batch
    for buf in (ext_sc, sa_sc, sb_sc):
        buf[0:POOL_PAD, :] = jnp.zeros((POOL_PAD, buf.shape[1]), F32)
    ext_sc[POOL_PAD:first, :] = jnp.where(tile_in_batch == 0, 0.0, halo_ref[...].astype(F32))
    ext_sc[first:, :] = u_ref[...].astype(F32)
    t_head = lax.broadcasted_iota(jnp.int32, (POOL_HALO, 1), 0)
    for gi, w in enumerate(POOL_WINDOWS):
        cols = slice(gi * group, (gi + 1) * group)
        src, col_sel, k = ext_sc, cols, 1
        for dst in (sa_sc, sb_sc, sa_sc, sb_sc):
            if k >= w:
                break
            dst[POOL_PAD:n, :] = src[POOL_PAD:n, col_sel] + src[POOL_PAD - k:n - k, col_sel]
            src, col_sel, k = dst, slice(None), 2 * k
        tok = ext_sc[first:, cols]
        pooled = src[first:n, col_sel] * (1.0 / w) - tok
        cnt = jnp.minimum(t_head + 1, w).astype(F32)
        head = src[first:first + POOL_HALO, col_sel] / cnt - tok[0:POOL_HALO]
        head = jnp.where(tile_in_batch == 0, head, pooled[0:POOL_HALO])
        pooled = jnp.concatenate([head, pooled[POOL_HALO:]], axis=0)
        mixed = _dot(pooled.astype(BF16), w_ref[gi])
        o_ref[:, cols] = (mixed * ps_ref[:, cols]).astype(o_ref.dtype)


def _pool_call(proj, w_pool_all, pool_scale_all, layer, T):
    M = proj.shape[0]
    _, ng, group, _ = w_pool_all.shape
    width = ng * group
    tiles_per_batch = T // POOL_BT
    halo_blocks = POOL_BT // POOL_HALO
    return pl.pallas_call(
        functools.partial(_pool_kernel, tiles_per_batch=tiles_per_batch),
        grid=(M // POOL_BT,),
        in_specs=[
            pl.BlockSpec((POOL_BT, width), lambda i: (i, 0)),
            pl.BlockSpec((POOL_HALO, width), lambda i: (jnp.maximum(i * halo_blocks - 1, 0), 0)),
            pl.BlockSpec((None, ng, group, group), lambda i: (layer, 0, 0, 0)),
            pl.BlockSpec((None, 1, width), lambda i: (layer, 0, 0)),
        ],
        out_specs=pl.BlockSpec((POOL_BT, width), lambda i: (i, 0)),
        out_shape=jax.ShapeDtypeStruct((M, width), BF16),
        scratch_shapes=[pltpu.VMEM((POOL_PAD + POOL_HALO + POOL_BT, width), F32),
                        pltpu.VMEM((POOL_PAD + POOL_HALO + POOL_BT, group), F32),
                        pltpu.VMEM((POOL_PAD + POOL_HALO + POOL_BT, group), F32)],
        compiler_params=_cparams("parallel"),
        name="pool_mixer",
    )(proj, proj, w_pool_all, pool_scale_all)


AUG_SPLIT = 3
AUG_BLK_COL = SLC_LEN
AUG_OFF_COL = SLC_LEN + AUG_SPLIT
V_ROWS = HEAD_DIM + 16
LOG2E = 1.4426950408889634


def _key_extra(t):
    lane = lax.broadcasted_iota(jnp.int32, (t.shape[0], LANES), 1)
    blk = lax.shift_right_logical(t, SLC_SHIFT)
    off = t & (SLC_LEN - 1)
    extra = jnp.where(lane == blk, 1.0, 0.0)
    extra = jnp.where((lane >= AUG_BLK_COL) & (lane < AUG_OFF_COL), blk.astype(F32), extra)
    extra = jnp.where((lane >= AUG_OFF_COL) & (lane < AUG_OFF_COL + AUG_SPLIT), off.astype(F32), extra)
    return extra


def _value_tile_t(v_t):
    pad = lax.broadcasted_iota(jnp.int32, (V_ROWS - HEAD_DIM, v_t.shape[1]), 0)
    return jnp.concatenate([v_t, jnp.where(pad == 0, 1.0, 0.0)], axis=0).astype(BF16)


def _kprep_kernel(ks_ref, vs_ref, kw_ref, vw_ref, kg_ref, ksa_ref, vso_ref, kwa_ref, vwo_ref,
                  *, tiles_per_batch):
    bt = ks_ref.shape[0]
    kt = vso_ref.shape[4]
    t = (pl.program_id(0) % tiles_per_batch) * bt + lax.broadcasted_iota(jnp.int32, (bt, 1), 0)
    extra = _key_extra(t).astype(BF16)
    for g in range(N_KV):
        cols = slice(g * HEAD_DIM, (g + 1) * HEAD_DIM)
        ksn = _rms(ks_ref[:, cols].astype(F32)) * kg_ref[1:2, :]
        kwn = _rms(kw_ref[:, cols].astype(F32)) * kg_ref[2:3, :]
        ksa_ref[0, g, :, 0:HEAD_DIM] = ksn.astype(BF16)
        ksa_ref[0, g, :, HEAD_DIM:] = extra
        kwa_ref[0, g, :, 0:HEAD_DIM] = kwn.astype(BF16)
        kwa_ref[0, g, :, HEAD_DIM:] = extra
        vs_t = _value_tile_t(vs_ref[:, cols].astype(F32).T)
        vw_t = _value_tile_t(vw_ref[:, cols].astype(F32).T)
        for j in range(bt // kt):
            vso_ref[0, g, j] = vs_t[:, j * kt:(j + 1) * kt]
            vwo_ref[0, g, j] = vw_t[:, j * kt:(j + 1) * kt]


def _kprep_call(proj, k_gain, B, T, col0):
    kvw = N_KV * HEAD_DIM
    cb = col0 // kvw
    tiles_per_batch = T // KPREP_BT
    vt_per_tile = KPREP_BT // ATT_TQ
    aug = jax.ShapeDtypeStruct((B, N_KV, T, 2 * HEAD_DIM), BF16)
    val = jax.ShapeDtypeStruct((B, N_KV, T // ATT_TQ, V_ROWS, ATT_TQ), BF16)
    in_spec = lambda j: pl.BlockSpec((KPREP_BT, kvw), lambda i: (i, cb + j))
    out_map = lambda i: (i // tiles_per_batch, 0, i % tiles_per_batch, 0)
    val_map = lambda i: (i // tiles_per_batch, 0, i % tiles_per_batch, 0, 0)
    return pl.pallas_call(
        functools.partial(_kprep_kernel, tiles_per_batch=tiles_per_batch),
        grid=(B * tiles_per_batch,),
        in_specs=[in_spec(0), in_spec(1), in_spec(2), in_spec(3),
                  pl.BlockSpec((3, HEAD_DIM), lambda i: (0, 0))],
        out_specs=[pl.BlockSpec((1, N_KV, KPREP_BT, 2 * HEAD_DIM), out_map),
                   pl.BlockSpec((1, N_KV, vt_per_tile, V_ROWS, ATT_TQ), val_map),
                   pl.BlockSpec((1, N_KV, KPREP_BT, 2 * HEAD_DIM), out_map),
                   pl.BlockSpec((1, N_KV, vt_per_tile, V_ROWS, ATT_TQ), val_map)],
        out_shape=[aug, val, aug, val],
        compiler_params=_cparams("parallel"),
        name="kv_prep",
    )(proj, proj, proj, proj, k_gain)


def _compress_one(src_ref, f32_sc, pe_ref, w1_ref, kv, nc):
    half = CMP_LEN // 2
    assert CMP_STRIDE == half
    f32_sc[...] = src_ref[...].astype(F32)
    xs = [f32_sc[pl.ds(j, nc, stride=CMP_STRIDE), :] for j in range(half)]
    x = jnp.concatenate(xs, axis=1)
    kdim = half * HEAD_DIM
    lo = _dot((x + pe_ref[kv, 0:1, :]).astype(BF16), w1_ref[kv, 0:kdim, :])
    hi = _dot((x + pe_ref[kv, 1:2, :]).astype(BF16), w1_ref[kv, kdim:, :])
    pre = lo + pltpu.roll(hi, nc - 1, 0)
    return (pre * _sigmoid(pre)).astype(BF16)


def _cmp_kernel(k_ref, v_ref, pe_ref, w1_ref, w2k_ref, w2vt_ref, kg_ref, ko_ref, vo_ref, f32_sc):
    nc = ko_ref.shape[2]
    kc = _dot(_compress_one(k_ref, f32_sc, pe_ref, w1_ref, 0, nc), w2k_ref[...])
    ko_ref[0, 0] = (_rms(kc) * kg_ref[0:1, :]).astype(ko_ref.dtype)
    vct = _dot_nt(w2vt_ref[...], _compress_one(v_ref, f32_sc, pe_ref, w1_ref, 1, nc))
    vo_ref[0, 0] = vct.astype(vo_ref.dtype)


def _cmp_call(proj, pe2, w1, w2, k_gain, B, T, col0):
    nc = T // CMP_STRIDE
    cb = col0 // HEAD_DIM
    kdim = CMP_LEN * HEAD_DIM
    return pl.pallas_call(
        _cmp_kernel,
        grid=(B, N_KV),
        in_specs=[
            pl.BlockSpec((T, HEAD_DIM), lambda b, g: (b, cb + g)),
            pl.BlockSpec((T, HEAD_DIM), lambda b, g: (b, cb + N_KV + g)),
            pl.BlockSpec((2, 2, kdim // 2), lambda b, g: (0, 0, 0)),
            pl.BlockSpec((2, kdim, HEAD_DIM), lambda b, g: (0, 0, 0)),
            pl.BlockSpec((HEAD_DIM, HEAD_DIM), lambda b, g: (0, 0)),
            pl.BlockSpec((HEAD_DIM, HEAD_DIM), lambda b, g: (0, 0)),
            pl.BlockSpec((3, HEAD_DIM), lambda b, g: (0, 0)),
        ],
        out_specs=[pl.BlockSpec((1, 1, nc, HEAD_DIM), lambda b, g: (b, g, 0, 0)),
                   pl.BlockSpec((1, 1, HEAD_DIM, nc), lambda b, g: (b, g, 0, 0))],
        out_shape=[jax.ShapeDtypeStruct((B, N_KV, nc, HEAD_DIM), BF16),
                   jax.ShapeDtypeStruct((B, N_KV, HEAD_DIM, nc), BF16)],
        scratch_shapes=[pltpu.VMEM((T, HEAD_DIM), F32)],
        compiler_params=_cparams("parallel", "arbitrary"),
        name="compress",
    )(proj, proj, pe2, w1, w2[0], w2[1].T, k_gain)


MAX_FLOOR = 0.1 * NEG_INF


def _exp2_cols(s, mask):
    s = jnp.where(mask, s, NEG_INF)
    m = jnp.maximum(jnp.max(s, axis=0, keepdims=True), MAX_FLOOR)
    return jnp.exp2(s - m)


def _split3(c, shape):
    c = jnp.full(shape, c, F32)
    c1 = c.astype(BF16).astype(F32)
    r1 = c - c1
    c2 = r1.astype(BF16).astype(F32)
    c3 = (r1 - c2).astype(BF16).astype(F32)
    return c1, c2, c3


def _attn_kernel(slope_ref, q_ref, gl_ref, qg_ref, cmp_k_ref, cmp_vt_ref, ksa_ref, vst_ref,
                 kwa_ref, vwt_ref, ovl_ref, wo_ref, o_ref, wob_ref, score_sc, qs_sc, sa_sc, sb_sc, m_sc,
                 acc_sc, gate_sc, seq_ref, *, n_rep):
    g = pl.program_id(1)
    i = pl.program_id(2)
    wob_ref[...] = wo_ref[...].astype(BF16)
    tq = q_ref.shape[0]
    nc = cmp_k_ref.shape[2]
    n_slc = ovl_ref.shape[0]
    t0 = i * tq
    slopes = [slope_ref[g, r] * LOG2E for r in range(n_rep)]
    scale = HEAD_DIM ** -0.5 * LOG2E
    head = lambda a, r: a[:, r * tq:(r + 1) * tq]

    qt = []
    for r in range(n_rep):
        x = q_ref[:, r * HEAD_DIM:(r + 1) * HEAD_DIM].astype(F32)
        qt.append((_rms(x) * qg_ref[...] * scale).T.astype(BF16))
    qt_all = jnp.concatenate(qt, axis=1)

    s_all = _dot(cmp_k_ref[0, 0], qt_all)
    c_idx = lax.broadcasted_iota(jnp.int32, (nc, tq), 0)
    t_idx = lax.broadcasted_iota(jnp.int32, (nc, tq), 1) + t0
    dist_c = (t_idx - (c_idx * CMP_STRIDE + (CMP_LEN - 1))).astype(F32)
    mask_c = dist_c >= 0.0
    p_cmp = []
    p_sum = jnp.zeros((nc, tq), F32)
    for r in range(n_rep):
        p = _exp2_cols(head(s_all, r) - slopes[r] * dist_c, mask_c)
        l = jnp.sum(p, axis=0, keepdims=True)
        p = p * (1.0 / jnp.where(l > 0.0, l, 1.0))
        p_cmp.append(p.astype(BF16))
        p_sum = p_sum + p
    o_cmp_t = _dot(cmp_vt_ref[0, 0], jnp.concatenate(p_cmp, axis=1))

    p_hi = p_sum.astype(BF16)
    p_lo = (p_sum - p_hi.astype(F32)).astype(BF16)
    ovl = ovl_ref[...]
    imp = _dot(ovl, p_hi) + _dot(ovl, p_lo)
    jb = lax.broadcasted_iota(jnp.int32, (n_slc, tq), 0)
    tt = lax.broadcasted_iota(jnp.int32, (n_slc, tq), 1) + t0
    cur = lax.shift_right_logical(tt, SLC_SHIFT)
    forced = (jb == 0) | (jb == cur) | (jb == cur - 1)
    score = jnp.where(jb * SLC_LEN <= tt, imp + jnp.where(forced, FORCE_BONUS, 0.0), NEG_INF)
    score_sc[...] = score
    sub = 8
    groups = [score[sub * rg:sub * (rg + 1)] for rg in range(n_slc // sub)]
    ranks = [jnp.zeros((sub, tq), F32) for _ in groups]
    jrow = lax.broadcasted_iota(jnp.int32, (sub, tq), 0)
    for b2 in range(n_slc):
        sb = jnp.broadcast_to(score_sc[b2:b2 + 1, :], (sub, tq))
        for rg, sg in enumerate(groups):
            if sub * rg > b2:
                beats = sb >= sg
            elif sub * rg + sub - 1 < b2:
                beats = sb > sg
            else:
                beats = (sb > sg) | ((sb == sg) & (jrow > b2 - sub * rg))
            ranks[rg] = ranks[rg] + jnp.where(beats, 1.0, 0.0)
    n_sel = min(SLC_TOPK, n_slc)
    sel_bias_t = jnp.concatenate([jnp.where(rk < n_sel, 0.0, NEG_INF) for rk in ranks], axis=0)

    bpt = tq // SLC_LEN
    n_list = jnp.int32(0)
    for kt in range(n_slc // bpt - 1):
        rg, off = divmod(kt * bpt, sub)
        hit = jnp.max(jnp.where(ranks[rg][off:off + bpt] < n_sel, 1.0, 0.0), axis=1, keepdims=True)
        hit = jnp.max(hit, axis=0, keepdims=True)[0, 0]
        seq_ref[n_list] = jnp.int32(kt)
        n_list = n_list + ((hit > 0.0) & (kt < i)).astype(jnp.int32)
    seq_ref[n_list] = i

    pshape = (LANES - AUG_BLK_COL, tq)
    frow = lax.broadcasted_iota(jnp.int32, pshape, 0)
    sel_rows = sel_bias_t.astype(BF16)
    if n_slc < AUG_BLK_COL:
        sel_rows = jnp.concatenate([sel_rows, jnp.zeros((AUG_BLK_COL - n_slc, tq), BF16)], axis=0)
    q_slc, q_win = [], []
    for r in range(n_rep):
        pos_rows = jnp.zeros(pshape, F32)
        for k, ck in enumerate(_split3(slopes[r], pshape)):
            pos_rows = jnp.where(frow == k, ck * SLC_LEN, pos_rows)
            pos_rows = jnp.where(frow == AUG_SPLIT + k, ck, pos_rows)
        pos_rows = pos_rows.astype(BF16)
        q_slc.append(jnp.concatenate([qt[r], sel_rows, pos_rows], axis=0))
        q_win.append(jnp.concatenate([qt[r], jnp.zeros_like(sel_rows), pos_rows], axis=0))
    qs_sc[...] = jnp.concatenate(q_slc, axis=1)
    q_win = jnp.concatenate(q_win, axis=1)

    n_wt = WIN // tq + 1
    span = n_wt * tq
    j0 = jnp.maximum(i - (n_wt - 1), 0)
    start_w = pl.multiple_of(j0 * tq, tq)
    s_w = _dot(kwa_ref[0, 0, pl.ds(start_w, span), :], q_win)
    d_w = (lax.broadcasted_iota(jnp.int32, (span, tq), 1) + t0) - \
          (lax.broadcasted_iota(jnp.int32, (span, tq), 0) + start_w)
    mask_w = (d_w >= 0) & (d_w < WIN)
    p_w = jnp.concatenate([_exp2_cols(head(s_w, r), mask_w).astype(BF16) for r in range(n_rep)], axis=1)
    o_win_t = _dot(vwt_ref[0, 0, j0], p_w[0:tq])
    for jj in range(1, n_wt):
        o_win_t = o_win_t + _dot(vwt_ref[0, 0, j0 + jj], p_w[jj * tq:(jj + 1) * tq])
    o_win_t = o_win_t[0:HEAD_DIM] * (1.0 / o_win_t[HEAD_DIM:HEAD_DIM + 1])

    m_sc[...] = jnp.full(m_sc.shape, NEG_INF, F32)
    acc_sc[...] = jnp.zeros(acc_sc.shape, F32)

    def produce(kt, buf):
        start = pl.multiple_of(kt * tq, tq)
        buf[...] = _dot(ksa_ref[0, 0, pl.ds(start, tq), :], qs_sc[...])

    def consume(kt, buf, causal):
        s = buf[...]
        if causal:
            kpos = lax.broadcasted_iota(jnp.int32, s.shape, 0)
            qpos = lax.broadcasted_iota(jnp.int32, s.shape, 1) & (tq - 1)
            s = jnp.where(kpos <= qpos, s, NEG_INF)
        m_old = m_sc[...]
        m_new = jnp.maximum(m_old, jnp.max(s, axis=0, keepdims=True))
        alpha = jnp.exp2(m_old - m_new)
        p = jnp.exp2(s - m_new)
        acc_sc[...] = alpha * acc_sc[...] + _dot(vst_ref[0, 0, kt], p.astype(BF16))
        m_sc[...] = m_new

    def pair(j, carry):
        produce(seq_ref[2 * j + 1], sb_sc)
        consume(seq_ref[2 * j], sa_sc, False)
        produce(seq_ref[2 * j + 2], sa_sc)
        consume(seq_ref[2 * j + 1], sb_sc, False)
        return carry

    produce(seq_ref[0], sa_sc)
    lax.fori_loop(0, n_list // 2, pair, 0)

    @pl.when(n_list % 2 == 1)
    def _():
        produce(i, sb_sc)
        consume(seq_ref[n_list - 1], sa_sc, False)
        consume(i, sb_sc, True)

    @pl.when(n_list % 2 == 0)
    def _():
        consume(i, sa_sc, True)

    o_slc_t = acc_sc[0:HEAD_DIM, :] * (1.0 / acc_sc[HEAD_DIM:HEAD_DIM + 1, :])

    gate_sc[...] = _sigmoid(gl_ref[...].astype(F32).T)
    row0 = 3 * n_rep * g
    gate = lambda k: gate_sc[pl.ds(row0 + k, 1), :]
    for r in range(n_rep):
        o_t = (gate(3 * r) * head(o_cmp_t, r)
               + gate(3 * r + 1) * head(o_slc_t, r)
               + gate(3 * r + 2) * head(o_win_t, r))
        o_ref[:, r * HEAD_DIM:(r + 1) * HEAD_DIM] = o_t.T.astype(o_ref.dtype)


def _attn_call(proj, q_gain, cmp_k, cmp_vt, ks_aug, vs_t, kw_aug, vw_t, slopes, ovl_t, w_out, layer,
               B, T, q_col0, gate_col0):
    n_rep = slopes.shape[1]
    wo_rows, wo_cols = w_out.shape[1:]
    slab = wo_rows // (B * N_KV * (T // ATT_TQ))
    assert slab * B * N_KV * (T // ATT_TQ) == wo_rows and slab % 16 == 0
    gw = n_rep * HEAD_DIM
    nq = T // ATT_TQ
    nc = cmp_k.shape[2]
    n_slc = ovl_t.shape[0]
    qcb = q_col0 // gw
    gcb = gate_col0 // LANES
    assert WIN % ATT_TQ == 0
    keys = pl.BlockSpec((1, 1, T, 2 * HEAD_DIM), lambda b, g, i: (b, g, 0, 0))
    vals = pl.BlockSpec((1, 1, nq, V_ROWS, ATT_TQ), lambda b, g, i: (b, g, 0, 0, 0))
    return pl.pallas_call(
        functools.partial(_attn_kernel, n_rep=n_rep),
        grid=(B, N_KV, nq),
        in_specs=[
            pl.BlockSpec(memory_space=pltpu.SMEM),
            pl.BlockSpec((ATT_TQ, gw), lambda b, g, i: (b * nq + i, qcb + g)),
            pl.BlockSpec((ATT_TQ, LANES), lambda b, g, i: (b * nq + i, gcb)),
            pl.BlockSpec((1, HEAD_DIM), lambda b, g, i: (0, 0)),
            pl.BlockSpec((1, 1, nc, HEAD_DIM), lambda b, g, i: (b, g, 0, 0)),
            pl.BlockSpec((1, 1, HEAD_DIM, nc), lambda b, g, i: (b, g, 0, 0)),
            keys, vals, keys, vals,
            pl.BlockSpec((n_slc, nc), lambda b, g, i: (0, 0)),
            pl.BlockSpec((None, slab, wo_cols), lambda b, g, i: (layer, (b * N_KV + g) * nq + i, 0)),
        ],
        out_specs=[pl.BlockSpec((ATT_TQ, gw), lambda b, g, i: (b * nq + i, g)),
                   pl.BlockSpec((slab, wo_cols), lambda b, g, i: ((b * N_KV + g) * nq + i, 0))],
        out_shape=[jax.ShapeDtypeStruct((B * T, N_KV * gw), BF16),
                   jax.ShapeDtypeStruct((wo_rows, wo_cols), BF16)],
        scratch_shapes=[
            pltpu.VMEM((n_slc, ATT_TQ), F32),
            pltpu.VMEM((2 * HEAD_DIM, n_rep * ATT_TQ), BF16),
            pltpu.VMEM((ATT_TQ, n_rep * ATT_TQ), F32),
            pltpu.VMEM((ATT_TQ, n_rep * ATT_TQ), F32),
            pltpu.VMEM((1, n_rep * ATT_TQ), F32),
            pltpu.VMEM((V_ROWS, n_rep * ATT_TQ), F32),
            pltpu.VMEM((LANES, ATT_TQ), F32),
            pltpu.SMEM((nq + 1,), jnp.int32),
        ],
        compiler_params=_cparams("parallel", "parallel", "arbitrary"),
        name="nsa_attention",
    )(slopes, proj, proj, q_gain, cmp_k, cmp_vt, ks_aug, vs_t, kw_aug, vw_t, ovl_t, w_out)


def _out_kernel(a_ref, o_ref, w_ref, x_ref, g_ref, ng_ref, sc_ref, sh_ref, y_ref, h_ref):
    ka = a_ref.shape[1]
    rc = a_ref.shape[0] // OUT_ROW_CHUNKS
    for c in range(OUT_ROW_CHUNKS):
        rows = slice(c * rc, (c + 1) * rc)
        acc = _dot(a_ref[rows, :], w_ref[0:ka, :]) + _dot(o_ref[rows, :], w_ref[ka:, :])
        y = x_ref[rows, :] + g_ref[0] * acc
        y_ref[rows, :] = y
        h_ref[rows, :] = _norm_mod(y, ng_ref, sc_ref, sh_ref)


def _out_call(a, o, w, xf, gate, ng, sc, sh, T):
    M, D = xf.shape
    ka, ko = a.shape[1], o.shape[1]
    per_b = T // OUT_BM
    row_spec = lambda width: pl.BlockSpec((OUT_BM, width), lambda i: (i, 0))
    mod_spec = pl.BlockSpec((1, 1, D), lambda i: (i // per_b, 0, 0))
    return pl.pallas_call(
        _out_kernel,
        grid=(M // OUT_BM,),
        in_specs=[
            row_spec(ka), row_spec(ko),
            pl.BlockSpec((ka + ko, D), lambda i: (0, 0)),
            row_spec(D), mod_spec,
            pl.BlockSpec((1, D), lambda i: (0, 0)),
            mod_spec, mod_spec,
        ],
        out_specs=[row_spec(D), row_spec(D)],
        out_shape=[jax.ShapeDtypeStruct((M, D), F32), jax.ShapeDtypeStruct((M, D), BF16)],
        compiler_params=_cparams("parallel"),
        name="out_proj",
    )(a, o, w, xf, gate, ng, sc, sh)


def _ffn1_kernel(h_ref, wg_ref, wu_ref, wd_ref, o_ref, wdb_ref, wg_sc, wu_sc):
    @pl.when(pl.program_id(1) == 0)
    def _():
        wg_sc[...] = wg_ref[...].astype(BF16)
        wu_sc[...] = wu_ref[...].astype(BF16)

    wdb_ref[...] = wd_ref[...].astype(BF16)
    h = h_ref[...]
    gate = _dot(h, wg_sc[...])
    up = _dot(h, wu_sc[...])
    o_ref[...] = (gate * _sigmoid(gate) * up).astype(o_ref.dtype)


def _ffn1_call(h, w_gu, w_down, layer):
    M, D = h.shape
    dff = w_gu.shape[2] // 2
    nt, nm = dff // FFN1_BN, M // FFN1_BM
    slab = dff // (nt * nm)
    assert slab * nt * nm == dff and slab % 16 == 0
    return pl.pallas_call(
        _ffn1_kernel,
        grid=(nt, nm),
        in_specs=[
            pl.BlockSpec((FFN1_BM, D), lambda n, i: (i, 0)),
            pl.BlockSpec((None, D, FFN1_BN), lambda n, i: (layer, 0, n)),
            pl.BlockSpec((None, D, FFN1_BN), lambda n, i: (layer, 0, n + nt)),
            pl.BlockSpec((None, slab, D), lambda n, i: (layer, n * nm + i, 0)),
        ],
        out_specs=[pl.BlockSpec((FFN1_BM, FFN1_BN), lambda n, i: (i, n)),
                   pl.BlockSpec((slab, D), lambda n, i: (n * nm + i, 0))],
        out_shape=[jax.ShapeDtypeStruct((M, dff), BF16), jax.ShapeDtypeStruct((dff, D), BF16)],
        scratch_shapes=[pltpu.VMEM((D, FFN1_BN), BF16), pltpu.VMEM((D, FFN1_BN), BF16)],
        compiler_params=_cparams("arbitrary", "arbitrary"),
        name="ffn_up",
    )(h, w_gu, w_gu, w_down)


def _ffn2_kernel(h_ref, w_ref, x_ref, g_ref, *rest):
    y_ref = rest[-1] if len(rest) == 1 else rest[1]
    y_ref[...] = x_ref[...] + g_ref[0] * _dot(h_ref[...], w_ref[...])
    if len(rest) == 3:
        wi_ref, _, wib_ref = rest
        n_in = wi_ref.shape[1]
        wib_ref[:, 0:n_in] = wi_ref[...].astype(BF16)
        wib_ref[:, n_in:] = jnp.zeros((wib_ref.shape[0], wib_ref.shape[1] - n_in), BF16)


def _ffn2_call(h, w, xf, gate, T, w_in_next=None, n_cols_padded=None):
    M, D = xf.shape
    dff = h.shape[1]
    per_b = T // FFN2_BM
    nm, nn = M // FFN2_BM, D // FFN2_BN
    in_specs = [
        pl.BlockSpec((FFN2_BM, dff), lambda i, n: (i, 0)),
        pl.BlockSpec((dff, FFN2_BN), lambda i, n: (0, n)),
        pl.BlockSpec((FFN2_BM, FFN2_BN), lambda i, n: (i, n)),
        pl.BlockSpec((1, 1, FFN2_BN), lambda i, n: (i // per_b, 0, n)),
    ]
    out_specs = [pl.BlockSpec((FFN2_BM, FFN2_BN), lambda i, n: (i, n))]
    out_shape = [jax.ShapeDtypeStruct((M, D), F32)]
    args = [h, w, xf, gate]
    if w_in_next is not None:
        w_in_all, layer = w_in_next
        slab = D // (nm * nn)
        assert slab * nm * nn == D and slab % 16 == 0
        in_specs.append(pl.BlockSpec((None, slab, w_in_all.shape[2]), lambda i, n: (layer, i * nn + n, 0)))
        out_specs.append(pl.BlockSpec((slab, n_cols_padded), lambda i, n: (i * nn + n, 0)))
        out_shape.append(jax.ShapeDtypeStruct((D, n_cols_padded), BF16))
        args.append(w_in_all)
    res = pl.pallas_call(
        _ffn2_kernel,
        grid=(nm, nn),
        in_specs=in_specs,
        out_specs=out_specs,
        out_shape=out_shape,
        compiler_params=_cparams("arbitrary", "arbitrary"),
        name="ffn_down",
    )(*args)
    return res if w_in_next is not None else res[0]


def _alibi_slopes(n_heads):
    sl = 2.0 ** (-8.0 * np.arange(1, n_heads + 1) / n_heads)
    return jnp.asarray(sl, F32).reshape(N_KV, n_heads // N_KV)


def _overlap_t(T):
    nc = T // CMP_STRIDE
    n_slc = T // SLC_LEN
    cst = np.arange(nc) * CMP_STRIDE
    sst = np.arange(n_slc) * SLC_LEN
    ov = (cst[None, :] < sst[:, None] + SLC_LEN) & (cst[None, :] + CMP_LEN > sst[:, None])
    ov[:, (T - CMP_LEN) // CMP_STRIDE + 1:] = False
    return jnp.asarray(ov.astype(np.float32), BF16)


def kernel(x, c, w_ada, b_ada, norm_g, w_in, q_gain, k_gain, pe_cmp, w_cmp1, w_cmp2,
           w_pool, pool_scale, w_out, w_gate_up, w_down):
    B, T, D = x.shape
    L = w_ada.shape[0]
    pool_w = w_pool.shape[1] * w_pool.shape[2]
    kvw = N_KV * HEAD_DIM
    n_heads = (w_in.shape[2] - pool_w - 6 * kvw) // (HEAD_DIM + 3)
    att_w = n_heads * HEAD_DIM
    n_rep = n_heads // N_KV
    assert w_in.shape[2] == pool_w + att_w + 6 * kvw + 3 * n_heads
    assert T % ATT_TQ == 0 and T >= WIN + ATT_TQ and T % POOL_BT == 0 and T % KPREP_BT == 0
    assert T // SLC_LEN <= AUG_BLK_COL and pool_w % (n_rep * HEAD_DIM) == 0
    assert 1 << SLC_SHIFT == SLC_LEN and ATT_TQ & (ATT_TQ - 1) == 0
    q_col0 = pool_w
    kc_col0 = pool_w + att_w
    ks_col0 = kc_col0 + 2 * kvw
    gate_col0 = kc_col0 + 6 * kvw
    assert ks_col0 % kvw == 0 and gate_col0 % LANES == 0

    xf = x.reshape(B * T, D)
    rows = -(-B // 8) * 8
    c8 = jnp.pad(c, ((0, rows - B), (0, 0)))
    mod = _ada_call(c8, w_ada, b_ada)
    slopes = _alibi_slopes(n_heads)
    ovl_t = _overlap_t(T)

    assert 3 * n_heads <= LANES
    n_in_cols = -(-w_in.shape[2] // IN_BN) * IN_BN
    w_in_p = jnp.pad(w_in[0:1].astype(BF16), ((0, 0), (0, 0), (0, n_in_cols - w_in.shape[2])))
    w_pool_b = w_pool.astype(BF16)
    pool_scale_r = pool_scale.reshape(L, 1, pool_w)

    for l in range(L):
        sh1, sc1, g1, sh2, sc2, g2 = [mod[l, :B, k * D:(k + 1) * D].reshape(B, 1, D) for k in range(6)]
        proj = _in_call(xf, norm_g[l, 0:1], sc1, sh1, w_in_p, 0, T)
        a_out = _pool_call(proj, w_pool_b, pool_scale_r, l, T)
        ks_aug, vs_t, kw_aug, vw_t = _kprep_call(proj, k_gain[l], B, T, ks_col0)
        pe2 = pe_cmp[l].reshape(2, 2, (CMP_LEN // 2) * HEAD_DIM)
        cmp_k, cmp_vt = _cmp_call(proj, pe2, w_cmp1[l].astype(BF16), w_cmp2[l].astype(BF16), k_gain[l],
                                  B, T, kc_col0)
        o_att, w_out_b = _attn_call(proj, q_gain[l].reshape(1, HEAD_DIM), cmp_k, cmp_vt, ks_aug, vs_t,
                                    kw_aug, vw_t, slopes, ovl_t, w_out, l, B, T, q_col0, gate_col0)
        xf, h2 = _out_call(a_out, o_att, w_out_b, xf, g1, norm_g[l, 1:2], sc2, sh2, T)
        hidden, w_down_b = _ffn1_call(h2, w_gate_up, w_down, l)
        if l + 1 < L:
            xf, w_in_next = _ffn2_call(hidden, w_down_b, xf, g2, T, (w_in, l + 1), n_in_cols)
            w_in_p = w_in_next.reshape(1, D, n_in_cols)
        else:
            xf = _ffn2_call(hidden, w_down_b, xf, g2, T)
    return xf.reshape(B, T, D)
```

```python
import functools

import numpy as np
import jax
import jax.numpy as jnp
from jax import lax
from jax.experimental import pallas as pl
from jax.experimental.pallas import tpu as pltpu

F32 = jnp.float32
BF16 = jnp.bfloat16

POOL_WINDOWS = (2, 4, 8, 16)
HEAD_DIM = 128
N_KV = 2
CMP_LEN = 32
CMP_STRIDE = 16
SLC_LEN = 64
SLC_SHIFT = 6
SLC_TOPK = 16
WIN = 512
NORM_EPS = 1e-6
NEG_INF = -1e30
FORCE_BONUS = 1e3

LANES = 128
POOL_HALO = 16
POOL_PAD = 8
assert all(w & (w - 1) == 0 and w <= POOL_HALO for w in POOL_WINDOWS)
VMEM_LIMIT_BYTES = 56 * 1024 * 1024

ADA_BN = 1024
IN_BM, IN_BN = 1024, 1280
KPREP_BT = 1024
POOL_BT = 1024
ATT_TQ = 256
OUT_BM = 512
OUT_ROW_CHUNKS = 2
FFN1_BM, FFN1_BN = 1024, 512
FFN2_BM, FFN2_BN = 1024, 512


def _cparams(*sem):
    return pltpu.CompilerParams(dimension_semantics=sem, vmem_limit_bytes=VMEM_LIMIT_BYTES)


def _dot(a, b):
    return jnp.dot(a, b, preferred_element_type=F32)


def _dot_nt(a, b):
    return lax.dot_general(a, b, (((1,), (1,)), ((), ())), preferred_element_type=F32)


def _rms(x):
    return x * lax.rsqrt(jnp.mean(x * x, axis=-1, keepdims=True) + NORM_EPS)


def _sigmoid(x):
    return 1.0 / (1.0 + jnp.exp(-x))


def _ada_kernel(c_ref, w_ref, b_ref, o_ref):
    c = c_ref[...]
    cs = c * _sigmoid(c)
    o_ref[0] = _dot(cs, w_ref[0]) + b_ref[0]


def _ada_call(c8, w_ada, b_ada):
    L, D, N = w_ada.shape
    rows = c8.shape[0]
    return pl.pallas_call(
        _ada_kernel,
        grid=(L, N // ADA_BN),
        in_specs=[
            pl.BlockSpec((rows, D), lambda l, n: (0, 0)),
            pl.BlockSpec((1, D, ADA_BN), lambda l, n: (l, 0, n)),
            pl.BlockSpec((1, 1, ADA_BN), lambda l, n: (l, 0, n)),
        ],
        out_specs=pl.BlockSpec((1, rows, ADA_BN), lambda l, n: (l, 0, n)),
        out_shape=jax.ShapeDtypeStruct((L, rows, N), F32),
        compiler_params=_cparams("parallel", "arbitrary"),
        name="ada_mod",
    )(c8, w_ada, b_ada.reshape(L, 1, N))


def _norm_mod(x, ng_ref, sc_ref, sh_ref):
    y = _rms(x) * ng_ref[...]
    return (y * (1.0 + sc_ref[0]) + sh_ref[0]).astype(BF16)


def _lookahead_row_tile(n_row_tiles):
    def idx(i, n):
        return jnp.where((i == 0) & (n == 0), 0, jnp.minimum(i + 1, n_row_tiles - 1))
    return idx


def _norm_chunks(n_col_steps):
    assert n_col_steps >= 2
    return 1 << ((n_col_steps - 1).bit_length() - 1)


def _norm_matmul_steps(x_ref, ng_ref, sc_ref, sh_ref, h_sc, emit, n_chunks):
    i, n = pl.program_id(0), pl.program_id(1)
    rows = x_ref.shape[0] // n_chunks
    slab_step = (n >= 1) & (n <= n_chunks)

    @pl.when((i == 0) & (n == 0))
    def _():
        h_sc[0] = _norm_mod(x_ref[...], ng_ref, sc_ref, sh_ref)

    @pl.when(jnp.logical_not(slab_step))
    def _():
        emit(h_sc[i % 2])

    for slot in (0, 1):
        @pl.when(slab_step & (i % 2 == slot))
        def _(slot=slot):
            emit(h_sc[slot])
            slab = pl.ds(pl.multiple_of((n - 1) * rows, rows), rows)
            h_sc[1 - slot, slab, :] = _norm_mod(x_ref[slab, :], ng_ref, sc_ref, sh_ref)


def _in_kernel(x_ref, ng_ref, sc_ref, sh_ref, w_ref, o_ref, h_sc, *, n_chunks):
    def emit(h):
        o_ref[...] = _dot(h, w_ref[...]).astype(o_ref.dtype)

    _norm_matmul_steps(x_ref, ng_ref, sc_ref, sh_ref, h_sc, emit, n_chunks)


def _in_call(xf, ng, sc, sh, w_all, layer, T):
    M, D = xf.shape
    N = w_all.shape[2]
    per_b = T // IN_BM
    nm = M // IN_BM
    n_chunks = _norm_chunks(N // IN_BN)
    row = _lookahead_row_tile(nm)
    return pl.pallas_call(
        functools.partial(_in_kernel, n_chunks=n_chunks),
        grid=(nm, N // IN_BN),
        in_specs=[
            pl.BlockSpec((IN_BM, D), lambda i, n: (row(i, n), 0)),
            pl.BlockSpec((1, D), lambda i, n: (0, 0)),
            pl.BlockSpec((1, 1, D), lambda i, n: (row(i, n) // per_b, 0, 0)),
            pl.BlockSpec((1, 1, D), lambda i, n: (row(i, n) // per_b, 0, 0)),
            pl.BlockSpec((None, D, IN_BN), lambda i, n: (layer, 0, n)),
        ],
        out_specs=pl.BlockSpec((IN_BM, IN_BN), lambda i, n: (i, n)),
        out_shape=jax.ShapeDtypeStruct((M, N), BF16),
        scratch_shapes=[pltpu.VMEM((2, IN_BM, D), BF16)],
        compiler_params=_cparams("arbitrary", "arbitrary"),
        name="in_proj",
    )(xf, ng, sc, sh, w_all)


def _pool_kernel(u_ref, halo_ref, w_ref, ps_ref, o_ref, ext_sc, sa_sc, sb_sc, *, tiles_per_batch):
    i = pl.program_id(0)
    bt = u_ref.shape[0]
    group = w_ref.shape[1]
    first = POOL_PAD + POOL_HALO
    n = first + bt
    tile_in_batch = i % tiles_per_batch
    for buf in (ext_sc, sa_sc, sb_sc):
        buf[0:POOL_PAD, :] = jnp.zeros((POOL_PAD, buf.shape[1]), F32)
    ext_sc[POOL_PAD:first, :] = jnp.where(tile_in_batch == 0, 0.0, halo_ref[...].astype(F32))
    ext_sc[first:, :] = u_ref[...].astype(F32)
    t_head = lax.broadcasted_iota(jnp.int32, (POOL_HALO, 1), 0)
    for gi, w in enumerate(POOL_WINDOWS):
        cols = slice(gi * group, (gi + 1) * group)
        src, col_sel, k = ext_sc, cols, 1
        for dst in (sa_sc, sb_sc, sa_sc, sb_sc):
            if k >= w:
                break
            dst[POOL_PAD:n, :] = src[POOL_PAD:n, col_sel] + src[POOL_PAD - k:n - k, col_sel]
            src, col_sel, k = dst, slice(None), 2 * k
        tok = ext_sc[first:, cols]
        pooled = src[first:n, col_sel] * (1.0 / w) - tok
        cnt = jnp.minimum(t_head + 1, w).astype(F32)
        head = src[first:first + POOL_HALO, col_sel] / cnt - tok[0:POOL_HALO]
        head = jnp.where(tile_in_batch == 0, head, pooled[0:POOL_HALO])
        pooled = jnp.concatenate([head, pooled[POOL_HALO:]], axis=0)
        mixed = _dot(pooled.astype(BF16), w_ref[gi])
        o_ref[:, cols] = (mixed * ps_ref[:, cols]).astype(o_ref.dtype)


def _pool_call(proj, w_pool_all, pool_scale_all, layer, T):
    M = proj.shape[0]
    _, ng, group, _ = w_pool_all.shape
    width = ng * group
    tiles_per_batch = T // POOL_BT
    halo_blocks = POOL_BT // POOL_HALO
    return pl.pallas_call(
        functools.partial(_pool_kernel, tiles_per_batch=tiles_per_batch),
        grid=(M // POOL_BT,),
        in_specs=[
            pl.BlockSpec((POOL_BT, width), lambda i: (i, 0)),
            pl.BlockSpec((POOL_HALO, width), lambda i: (jnp.maximum(i * halo_blocks - 1, 0), 0)),
            pl.BlockSpec((None, ng, group, group), lambda i: (layer, 0, 0, 0)),
            pl.BlockSpec((None, 1, width), lambda i: (layer, 0, 0)),
        ],
        out_specs=pl.BlockSpec((POOL_BT, width), lambda i: (i, 0)),
        out_shape=jax.ShapeDtypeStruct((M, width), BF16),
        scratch_shapes=[pltpu.VMEM((POOL_PAD + POOL_HALO + POOL_BT, width), F32),
                        pltpu.VMEM((POOL_PAD + POOL_HALO + POOL_BT, group), F32),
                        pltpu.VMEM((POOL_PAD + POOL_HALO + POOL_BT, group), F32)],
        compiler_params=_cparams("parallel"),
        name="pool_mixer",
    )(proj, proj, w_pool_all, pool_scale_all)


AUG_SPLIT = 3
AUG_BLK_COL = SLC_LEN
AUG_OFF_COL = SLC_LEN + AUG_SPLIT
V_ROWS = HEAD_DIM + 16
LOG2E = 1.4426950408889634


def _key_extra(t):
    lane = lax.broadcasted_iota(jnp.int32, (t.shape[0], LANES), 1)
    blk = lax.shift_right_logical(t, SLC_SHIFT)
    off = t & (SLC_LEN - 1)
    extra = jnp.where(lane == blk, 1.0, 0.0)
    extra = jnp.where((lane >= AUG_BLK_COL) & (lane < AUG_OFF_COL), blk.astype(F32), extra)
    extra = jnp.where((lane >= AUG_OFF_COL) & (lane < AUG_OFF_COL + AUG_SPLIT), off.astype(F32), extra)
    return extra


def _value_tile_t(v_t):
    pad = lax.broadcasted_iota(jnp.int32, (V_ROWS - HEAD_DIM, v_t.shape[1]), 0)
    return jnp.concatenate([v_t, jnp.where(pad == 0, 1.0, 0.0)], axis=0).astype(BF16)


def _kprep_kernel(ks_ref, vs_ref, kw_ref, vw_ref, kg_ref, ksa_ref, vso_ref, kwa_ref, vwo_ref,
                  *, tiles_per_batch):
    bt = ks_ref.shape[0]
    kt = vso_ref.shape[4]
    t = (pl.program_id(0) % tiles_per_batch) * bt + lax.broadcasted_iota(jnp.int32, (bt, 1), 0)
    extra = _key_extra(t).astype(BF16)
    for g in range(N_KV):
        cols = slice(g * HEAD_DIM, (g + 1) * HEAD_DIM)
        ksn = _rms(ks_ref[:, cols].astype(F32)) * kg_ref[1:2, :]
        kwn = _rms(kw_ref[:, cols].astype(F32)) * kg_ref[2:3, :]
        ksa_ref[0, g, :, 0:HEAD_DIM] = ksn.astype(BF16)
        ksa_ref[0, g, :, HEAD_DIM:] = extra
        kwa_ref[0, g, :, 0:HEAD_DIM] = kwn.astype(BF16)
        kwa_ref[0, g, :, HEAD_DIM:] = extra
        vs_t = _value_tile_t(vs_ref[:, cols].astype(F32).T)
        vw_t = _value_tile_t(vw_ref[:, cols].astype(F32).T)
        for j in range(bt // kt):
            vso_ref[0, g, j] = vs_t[:, j * kt:(j + 1) * kt]
            vwo_ref[0, g, j] = vw_t[:, j * kt:(j + 1) * kt]


def _kprep_call(proj, k_gain, B, T, col0):
    kvw = N_KV * HEAD_DIM
    cb = col0 // kvw
    tiles_per_batch = T // KPREP_BT
    vt_per_tile = KPREP_BT // ATT_TQ
    aug = jax.ShapeDtypeStruct((B, N_KV, T, 2 * HEAD_DIM), BF16)
    val = jax.ShapeDtypeStruct((B, N_KV, T // ATT_TQ, V_ROWS, ATT_TQ), BF16)
    in_spec = lambda j: pl.BlockSpec((KPREP_BT, kvw), lambda i: (i, cb + j))
    out_map = lambda i: (i // tiles_per_batch, 0, i % tiles_per_batch, 0)
    val_map = lambda i: (i // tiles_per_batch, 0, i % tiles_per_batch, 0, 0)
    return pl.pallas_call(
        functools.partial(_kprep_kernel, tiles_per_batch=tiles_per_batch),
        grid=(B * tiles_per_batch,),
        in_specs=[in_spec(0), in_spec(1), in_spec(2), in_spec(3),
                  pl.BlockSpec((3, HEAD_DIM), lambda i: (0, 0))],
        out_specs=[pl.BlockSpec((1, N_KV, KPREP_BT, 2 * HEAD_DIM), out_map),
                   pl.BlockSpec((1, N_KV, vt_per_tile, V_ROWS, ATT_TQ), val_map),
                   pl.BlockSpec((1, N_KV, KPREP_BT, 2 * HEAD_DIM), out_map),
                   pl.BlockSpec((1, N_KV, vt_per_tile, V_ROWS, ATT_TQ), val_map)],
        out_shape=[aug, val, aug, val],
        compiler_params=_cparams("parallel"),
        name="kv_prep",
    )(proj, proj, proj, proj, k_gain)


def _compress_one(src_ref, f32_sc, pe_ref, w1_ref, kv, nc):
    half = CMP_LEN // 2
    assert CMP_STRIDE == half
    f32_sc[...] = src_ref[...].astype(F32)
    xs = [f32_sc[pl.ds(j, nc, stride=CMP_STRIDE), :] for j in range(half)]
    x = jnp.concatenate(xs, axis=1)
    kdim = half * HEAD_DIM
    lo = _dot((x + pe_ref[kv, 0:1, :]).astype(BF16), w1_ref[kv, 0:kdim, :])
    hi = _dot((x + pe_ref[kv, 1:2, :]).astype(BF16), w1_ref[kv, kdim:, :])
    pre = lo + pltpu.roll(hi, nc - 1, 0)
    return (pre * _sigmoid(pre)).astype(BF16)


def _cmp_kernel(k_ref, v_ref, pe_ref, w1_ref, w2k_ref, w2vt_ref, kg_ref, ko_ref, vo_ref, f32_sc):
    nc = ko_ref.shape[2]
    kc = _dot(_compress_one(k_ref, f32_sc, pe_ref, w1_ref, 0, nc), w2k_ref[...])
    ko_ref[0, 0] = (_rms(kc) * kg_ref[0:1, :]).astype(ko_ref.dtype)
    vct = _dot_nt(w2vt_ref[...], _compress_one(v_ref, f32_sc, pe_ref, w1_ref, 1, nc))
    vo_ref[0, 0] = vct.astype(vo_ref.dtype)


def _cmp_call(proj, pe2, w1, w2, k_gain, B, T, col0):
    nc = T // CMP_STRIDE
    cb = col0 // HEAD_DIM
    kdim = CMP_LEN * HEAD_DIM
    return pl.pallas_call(
        _cmp_kernel,
        grid=(B, N_KV),
        in_specs=[
            pl.BlockSpec((T, HEAD_DIM), lambda b, g: (b, cb + g)),
            pl.BlockSpec((T, HEAD_DIM), lambda b, g: (b, cb + N_KV + g)),
            pl.BlockSpec((2, 2, kdim // 2), lambda b, g: (0, 0, 0)),
            pl.BlockSpec((2, kdim, HEAD_DIM), lambda b, g: (0, 0, 0)),
            pl.BlockSpec((HEAD_DIM, HEAD_DIM), lambda b, g: (0, 0)),
            pl.BlockSpec((HEAD_DIM, HEAD_DIM), lambda b, g: (0, 0)),
            pl.BlockSpec((3, HEAD_DIM), lambda b, g: (0, 0)),
        ],
        out_specs=[pl.BlockSpec((1, 1, nc, HEAD_DIM), lambda b, g: (b, g, 0, 0)),
                   pl.BlockSpec((1, 1, HEAD_DIM, nc), lambda b, g: (b, g, 0, 0))],
        out_shape=[jax.ShapeDtypeStruct((B, N_KV, nc, HEAD_DIM), BF16),
                   jax.ShapeDtypeStruct((B, N_KV, HEAD_DIM, nc), BF16)],
        scratch_shapes=[pltpu.VMEM((T, HEAD_DIM), F32)],
        compiler_params=_cparams("parallel", "arbitrary"),
        name="compress",
    )(proj, proj, pe2, w1, w2[0], w2[1].T, k_gain)


MAX_FLOOR = 0.1 * NEG_INF


def _exp2_cols(s, mask):
    s = jnp.where(mask, s, NEG_INF)
    m = jnp.maximum(jnp.max(s, axis=0, keepdims=True), MAX_FLOOR)
    return jnp.exp2(s - m)


def _split3(c, shape):
    c = jnp.full(shape, c, F32)
    c1 = c.astype(BF16).astype(F32)
    r1 = c - c1
    c2 = r1.astype(BF16).astype(F32)
    c3 = (r1 - c2).astype(BF16).astype(F32)
    return c1, c2, c3


def _attn_kernel(slope_ref, q_ref, gl_ref, qg_ref, cmp_k_ref, cmp_vt_ref, ksa_ref, vst_ref,
                 kwa_ref, vwt_ref, ovl_ref, wo_ref, o_ref, wob_ref, score_sc, qs_sc, sa_sc, sb_sc, m_sc,
                 acc_sc, gate_sc, seq_ref, *, n_rep):
    g = pl.program_id(1)
    i = pl.program_id(2)
    wob_ref[...] = wo_ref[...].astype(BF16)
    tq = q_ref.shape[0]
    nc = cmp_k_ref.shape[2]
    n_slc = ovl_ref.shape[0]
    t0 = i * tq
    slopes = [slope_ref[g, r] * LOG2E for r in range(n_rep)]
    scale = HEAD_DIM ** -0.5 * LOG2E
    head = lambda a, r: a[:, r * tq:(r + 1) * tq]

    qt = []
    for r in range(n_rep):
        x = q_ref[:, r * HEAD_DIM:(r + 1) * HEAD_DIM].astype(F32)
        qt.append((_rms(x) * qg_ref[...] * scale).T.astype(BF16))
    qt_all = jnp.concatenate(qt, axis=1)

    s_all = _dot(cmp_k_ref[0, 0], qt_all)
    c_idx = lax.broadcasted_iota(jnp.int32, (nc, tq), 0)
    t_idx = lax.broadcasted_iota(jnp.int32, (nc, tq), 1) + t0
    dist_c = (t_idx - (c_idx * CMP_STRIDE + (CMP_LEN - 1))).astype(F32)
    mask_c = dist_c >= 0.0
    p_cmp = []
    p_sum = jnp.zeros((nc, tq), F32)
    for r in range(n_rep):
        p = _exp2_cols(head(s_all, r) - slopes[r] * dist_c, mask_c)
        l = jnp.sum(p, axis=0, keepdims=True)
        p = p * (1.0 / jnp.where(l > 0.0, l, 1.0))
        p_cmp.append(p.astype(BF16))
        p_sum = p_sum + p
    o_cmp_t = _dot(cmp_vt_ref[0, 0], jnp.concatenate(p_cmp, axis=1))

    p_hi = p_sum.astype(BF16)
    p_lo = (p_sum - p_hi.astype(F32)).astype(BF16)
    ovl = ovl_ref[...]
    imp = _dot(ovl, p_hi) + _dot(ovl, p_lo)
    jb = lax.broadcasted_iota(jnp.int32, (n_slc, tq), 0)
    tt = lax.broadcasted_iota(jnp.int32, (n_slc, tq), 1) + t0
    cur = lax.shift_right_logical(tt, SLC_SHIFT)
    forced = (jb == 0) | (jb == cur) | (jb == cur - 1)
    score = jnp.where(jb * SLC_LEN <= tt, imp + jnp.where(forced, FORCE_BONUS, 0.0), NEG_INF)
    score_sc[...] = score
    sub = 8
    groups = [score[sub * rg:sub * (rg + 1)] for rg in range(n_slc // sub)]
    ranks = [jnp.zeros((sub, tq), F32) for _ in groups]
    jrow = lax.broadcasted_iota(jnp.int32, (sub, tq), 0)
    for b2 in range(n_slc):
        sb = jnp.broadcast_to(score_sc[b2:b2 + 1, :], (sub, tq))
        for rg, sg in enumerate(groups):
            if sub * rg > b2:
                beats = sb >= sg
            elif sub * rg + sub - 1 < b2:
                beats = sb > sg
            else:
                beats = (sb > sg) | ((sb == sg) & (jrow > b2 - sub * rg))
            ranks[rg] = ranks[rg] + jnp.where(beats, 1.0, 0.0)
    n_sel = min(SLC_TOPK, n_slc)
    sel_bias_t = jnp.concatenate([jnp.where(rk < n_sel, 0.0, NEG_INF) for rk in ranks], axis=0)

    bpt = tq // SLC_LEN
    n_list = jnp.int32(0)
    for kt in range(n_slc // bpt - 1):
        rg, off = divmod(kt * bpt, sub)
        hit = jnp.max(jnp.where(ranks[rg][off:off + bpt] < n_sel, 1.0, 0.0), axis=1, keepdims=True)
        hit = jnp.max(hit, axis=0, keepdims=True)[0, 0]
        seq_ref[n_list] = jnp.int32(kt)
        n_list = n_list + ((hit > 0.0) & (kt < i)).astype(jnp.int32)
    seq_ref[n_list] = i

    pshape = (LANES - AUG_BLK_COL, tq)
    frow = lax.broadcasted_iota(jnp.int32, pshape, 0)
    sel_rows = sel_bias_t.astype(BF16)
    if n_slc < AUG_BLK_COL:
        sel_rows = jnp.concatenate([sel_rows, jnp.zeros((AUG_BLK_COL - n_slc, tq), BF16)], axis=0)
    q_slc, q_win = [], []
    for r in range(n_rep):
        pos_rows = jnp.zeros(pshape, F32)
        for k, ck in enumerate(_split3(slopes[r], pshape)):
            pos_rows = jnp.where(frow == k, ck * SLC_LEN, pos_rows)
            pos_rows = jnp.where(frow == AUG_SPLIT + k, ck, pos_rows)
        pos_rows = pos_rows.astype(BF16)
        q_slc.append(jnp.concatenate([qt[r], sel_rows, pos_rows], axis=0))
        q_win.append(jnp.concatenate([qt[r], jnp.zeros_like(sel_rows), pos_rows], axis=0))
    qs_sc[...] = jnp.concatenate(q_slc, axis=1)
    q_win = jnp.concatenate(q_win, axis=1)

    n_wt = WIN // tq + 1
    span = n_wt * tq
    j0 = jnp.maximum(i - (n_wt - 1), 0)
    start_w = pl.multiple_of(j0 * tq, tq)
    s_w = _dot(kwa_ref[0, 0, pl.ds(start_w, span), :], q_win)
    d_w = (lax.broadcasted_iota(jnp.int32, (span, tq), 1) + t0) - \
          (lax.broadcasted_iota(jnp.int32, (span, tq), 0) + start_w)
    mask_w = (d_w >= 0) & (d_w < WIN)
    p_w = jnp.concatenate([_exp2_cols(head(s_w, r), mask_w).astype(BF16) for r in range(n_rep)], axis=1)
    o_win_t = _dot(vwt_ref[0, 0, j0], p_w[0:tq])
    for jj in range(1, n_wt):
        o_win_t = o_win_t + _dot(vwt_ref[0, 0, j0 + jj], p_w[jj * tq:(jj + 1) * tq])
    o_win_t = o_win_t[0:HEAD_DIM] * (1.0 / o_win_t[HEAD_DIM:HEAD_DIM + 1])

    m_sc[...] = jnp.full(m_sc.shape, NEG_INF, F32)
    acc_sc[...] = jnp.zeros(acc_sc.shape, F32)

    def produce(kt, buf):
        start = pl.multiple_of(kt * tq, tq)
        buf[...] = _dot(ksa_ref[0, 0, pl.ds(start, tq), :], qs_sc[...])

    def consume(kt, buf, causal):
        s = buf[...]
        if causal:
            kpos = lax.broadcasted_iota(jnp.int32, s.shape, 0)
            qpos = lax.broadcasted_iota(jnp.int32, s.shape, 1) & (tq - 1)
            s = jnp.where(kpos <= qpos, s, NEG_INF)
        m_old = m_sc[...]
        m_new = jnp.maximum(m_old, jnp.max(s, axis=0, keepdims=True))
        alpha = jnp.exp2(m_old - m_new)
        p = jnp.exp2(s - m_new)
        acc_sc[...] = alpha * acc_sc[...] + _dot(vst_ref[0, 0, kt], p.astype(BF16))
        m_sc[...] = m_new

    def pair(j, carry):
        produce(seq_ref[2 * j + 1], sb_sc)
        consume(seq_ref[2 * j], sa_sc, False)
        produce(seq_ref[2 * j + 2], sa_sc)
        consume(seq_ref[2 * j + 1], sb_sc, False)
        return carry

    produce(seq_ref[0], sa_sc)
    lax.fori_loop(0, n_list // 2, pair, 0)

    @pl.when(n_list % 2 == 1)
    def _():
        produce(i, sb_sc)
        consume(seq_ref[n_list - 1], sa_sc, False)
        consume(i, sb_sc, True)

    @pl.when(n_list % 2 == 0)
    def _():
        consume(i, sa_sc, True)

    o_slc_t = acc_sc[0:HEAD_DIM, :] * (1.0 / acc_sc[HEAD_DIM:HEAD_DIM + 1, :])

    gate_sc[...] = _sigmoid(gl_ref[...].astype(F32).T)
    row0 = 3 * n_rep * g
    gate = lambda k: gate_sc[pl.ds(row0 + k, 1), :]
    for r in range(n_rep):
        o_t = (gate(3 * r) * head(o_cmp_t, r)
               + gate(3 * r + 1) * head(o_slc_t, r)
               + gate(3 * r + 2) * head(o_win_t, r))
        o_ref[:, r * HEAD_DIM:(r + 1) * HEAD_DIM] = o_t.T.astype(o_ref.dtype)


def _attn_call(proj, q_gain, cmp_k, cmp_vt, ks_aug, vs_t, kw_aug, vw_t, slopes, ovl_t, w_out, layer,
               B, T, q_col0, gate_col0):
    n_rep = slopes.shape[1]
    wo_rows, wo_cols = w_out.shape[1:]
    slab = wo_rows // (B * N_KV * (T // ATT_TQ))
    assert slab * B * N_KV * (T // ATT_TQ) == wo_rows and slab % 16 == 0
    gw = n_rep * HEAD_DIM
    nq = T // ATT_TQ
    nc = cmp_k.shape[2]
    n_slc = ovl_t.shape[0]
    qcb = q_col0 // gw
    gcb = gate_col0 // LANES
    assert WIN % ATT_TQ == 0
    keys = pl.BlockSpec((1, 1, T, 2 * HEAD_DIM), lambda b, g, i: (b, g, 0, 0))
    vals = pl.BlockSpec((1, 1, nq, V_ROWS, ATT_TQ), lambda b, g, i: (b, g, 0, 0, 0))
    return pl.pallas_call(
        functools.partial(_attn_kernel, n_rep=n_rep),
        grid=(B, N_KV, nq),
        in_specs=[
            pl.BlockSpec(memory_space=pltpu.SMEM),
            pl.BlockSpec((ATT_TQ, gw), lambda b, g, i: (b * nq + i, qcb + g)),
            pl.BlockSpec((ATT_TQ, LANES), lambda b, g, i: (b * nq + i, gcb)),
            pl.BlockSpec((1, HEAD_DIM), lambda b, g, i: (0, 0)),
            pl.BlockSpec((1, 1, nc, HEAD_DIM), lambda b, g, i: (b, g, 0, 0)),
            pl.BlockSpec((1, 1, HEAD_DIM, nc), lambda b, g, i: (b, g, 0, 0)),
            keys, vals, keys, vals,
            pl.BlockSpec((n_slc, nc), lambda b, g, i: (0, 0)),
            pl.BlockSpec((None, slab, wo_cols), lambda b, g, i: (layer, (b * N_KV + g) * nq + i, 0)),
        ],
        out_specs=[pl.BlockSpec((ATT_TQ, gw), lambda b, g, i: (b * nq + i, g)),
                   pl.BlockSpec((slab, wo_cols), lambda b, g, i: ((b * N_KV + g) * nq + i, 0))],
        out_shape=[jax.ShapeDtypeStruct((B * T, N_KV * gw), BF16),
                   jax.ShapeDtypeStruct((wo_rows, wo_cols), BF16)],
        scratch_shapes=[
            pltpu.VMEM((n_slc, ATT_TQ), F32),
            pltpu.VMEM((2 * HEAD_DIM, n_rep * ATT_TQ), BF16),
            pltpu.VMEM((ATT_TQ, n_rep * ATT_TQ), F32),
            pltpu.VMEM((ATT_TQ, n_rep * ATT_TQ), F32),
            pltpu.VMEM((1, n_rep * ATT_TQ), F32),
            pltpu.VMEM((V_ROWS, n_rep * ATT_TQ), F32),
            pltpu.VMEM((LANES, ATT_TQ), F32),
            pltpu.SMEM((nq + 1,), jnp.int32),
        ],
        compiler_params=_cparams("parallel", "parallel", "arbitrary"),
        name="nsa_attention",
    )(slopes, proj, proj, q_gain, cmp_k, cmp_vt, ks_aug, vs_t, kw_aug, vw_t, ovl_t, w_out)


def _out_kernel(a_ref, o_ref, w_ref, x_ref, g_ref, ng_ref, sc_ref, sh_ref, y_ref, h_ref):
    ka = a_ref.shape[1]
    rc = a_ref.shape[0] // OUT_ROW_CHUNKS
    for c in range(OUT_ROW_CHUNKS):
        rows = slice(c * rc, (c + 1) * rc)
        acc = _dot(a_ref[rows, :], w_ref[0:ka, :]) + _dot(o_ref[rows, :], w_ref[ka:, :])
        y = x_ref[rows, :] + g_ref[0] * acc
        y_ref[rows, :] = y
        h_ref[rows, :] = _norm_mod(y, ng_ref, sc_ref, sh_ref)


def _out_call(a, o, w, xf, gate, ng, sc, sh, T):
    M, D = xf.shape
    ka, ko = a.shape[1], o.shape[1]
    per_b = T // OUT_BM
    row_spec = lambda width: pl.BlockSpec((OUT_BM, width), lambda i: (i, 0))
    mod_spec = pl.BlockSpec((1, 1, D), lambda i: (i // per_b, 0, 0))
    return pl.pallas_call(
        _out_kernel,
        grid=(M // OUT_BM,),
        in_specs=[
            row_spec(ka), row_spec(ko),
            pl.BlockSpec((ka + ko, D), lambda i: (0, 0)),
            row_spec(D), mod_spec,
            pl.BlockSpec((1, D), lambda i: (0, 0)),
            mod_spec, mod_spec,
        ],
        out_specs=[row_spec(D), row_spec(D)],
        out_shape=[jax.ShapeDtypeStruct((M, D), F32), jax.ShapeDtypeStruct((M, D), BF16)],
        compiler_params=_cparams("parallel"),
        name="out_proj",
    )(a, o, w, xf, gate, ng, sc, sh)


def _ffn1_kernel(h_ref, wg_ref, wu_ref, wd_ref, o_ref, wdb_ref, wg_sc, wu_sc):
    @pl.when(pl.program_id(1) == 0)
    def _():
        wg_sc[...] = wg_ref[...].astype(BF16)
        wu_sc[...] = wu_ref[...].astype(BF16)

    wdb_ref[...] = wd_ref[...].astype(BF16)
    h = h_ref[...]
    gate = _dot(h, wg_sc[...])
    up = _dot(h, wu_sc[...])
    o_ref[...] = (gate * _sigmoid(gate) * up).astype(o_ref.dtype)


def _ffn1_call(h, w_gu, w_down, layer):
    M, D = h.shape
    dff = w_gu.shape[2] // 2
    nt, nm = dff // FFN1_BN, M // FFN1_BM
    slab = dff // (nt * nm)
    assert slab * nt * nm == dff and slab % 16 == 0
    return pl.pallas_call(
        _ffn1_kernel,
        grid=(nt, nm),
        in_specs=[
            pl.BlockSpec((FFN1_BM, D), lambda n, i: (i, 0)),
            pl.BlockSpec((None, D, FFN1_BN), lambda n, i: (layer, 0, n)),
            pl.BlockSpec((None, D, FFN1_BN), lambda n, i: (layer, 0, n + nt)),
            pl.BlockSpec((None, slab, D), lambda n, i: (layer, n * nm + i, 0)),
        ],
        out_specs=[pl.BlockSpec((FFN1_BM, FFN1_BN), lambda n, i: (i, n)),
                   pl.BlockSpec((slab, D), lambda n, i: (n * nm + i, 0))],
        out_shape=[jax.ShapeDtypeStruct((M, dff), BF16), jax.ShapeDtypeStruct((dff, D), BF16)],
        scratch_shapes=[pltpu.VMEM((D, FFN1_BN), BF16), pltpu.VMEM((D, FFN1_BN), BF16)],
        compiler_params=_cparams("arbitrary", "arbitrary"),
        name="ffn_up",
    )(h, w_gu, w_gu, w_down)


def _ffn2_kernel(h_ref, w_ref, x_ref, g_ref, y_ref):
    y_ref[...] = x_ref[...] + g_ref[0] * _dot(h_ref[...], w_ref[...])


def _ffn2_call(h, w, xf, gate, T):
    M, D = xf.shape
    dff = h.shape[1]
    per_b = T // FFN2_BM
    return pl.pallas_call(
        _ffn2_kernel,
        grid=(M // FFN2_BM, D // FFN2_BN),
        in_specs=[
            pl.BlockSpec((FFN2_BM, dff), lambda i, n: (i, 0)),
            pl.BlockSpec((dff, FFN2_BN), lambda i, n: (0, n)),
            pl.BlockSpec((FFN2_BM, FFN2_BN), lambda i, n: (i, n)),
            pl.BlockSpec((1, 1, FFN2_BN), lambda i, n: (i // per_b, 0, n)),
        ],
        out_specs=pl.BlockSpec((FFN2_BM, FFN2_BN), lambda i, n: (i, n)),
        out_shape=jax.ShapeDtypeStruct((M, D), F32),
        compiler_params=_cparams("parallel", "arbitrary"),
        name="ffn_down",
    )(h, w, xf, gate)


def _alibi_slopes(n_heads):
    sl = 2.0 ** (-8.0 * np.arange(1, n_heads + 1) / n_heads)
    return jnp.asarray(sl, F32).reshape(N_KV, n_heads // N_KV)


def _overlap_t(T):
    nc = T // CMP_STRIDE
    n_slc = T // SLC_LEN
    cst = np.arange(nc) * CMP_STRIDE
    sst = np.arange(n_slc) * SLC_LEN
    ov = (cst[None, :] < sst[:, None] + SLC_LEN) & (cst[None, :] + CMP_LEN > sst[:, None])
    ov[:, (T - CMP_LEN) // CMP_STRIDE + 1:] = False
    return jnp.asarray(ov.astype(np.float32), BF16)


def kernel(x, c, w_ada, b_ada, norm_g, w_in, q_gain, k_gain, pe_cmp, w_cmp1, w_cmp2,
           w_pool, pool_scale, w_out, w_gate_up, w_down):
    B, T, D = x.shape
    L = w_ada.shape[0]
    pool_w = w_pool.shape[1] * w_pool.shape[2]
    kvw = N_KV * HEAD_DIM
    n_heads = (w_in.shape[2] - pool_w - 6 * kvw) // (HEAD_DIM + 3)
    att_w = n_heads * HEAD_DIM
    n_rep = n_heads // N_KV
    assert w_in.shape[2] == pool_w + att_w + 6 * kvw + 3 * n_heads
    assert T % ATT_TQ == 0 and T >= WIN + ATT_TQ and T % POOL_BT == 0 and T % KPREP_BT == 0
    assert T // SLC_LEN <= AUG_BLK_COL and pool_w % (n_rep * HEAD_DIM) == 0
    assert 1 << SLC_SHIFT == SLC_LEN and ATT_TQ & (ATT_TQ - 1) == 0
    q_col0 = pool_w
    kc_col0 = pool_w + att_w
    ks_col0 = kc_col0 + 2 * kvw
    gate_col0 = kc_col0 + 6 * kvw
    assert ks_col0 % kvw == 0 and gate_col0 % LANES == 0

    xf = x.reshape(B * T, D)
    rows = -(-B // 8) * 8
    c8 = jnp.pad(c, ((0, rows - B), (0, 0)))
    mod = _ada_call(c8, w_ada, b_ada)
    slopes = _alibi_slopes(n_heads)
    ovl_t = _overlap_t(T)

    assert 3 * n_heads <= LANES
    w_in_p = jnp.pad(w_in.astype(BF16), ((0, 0), (0, 0), (0, -w_in.shape[2] % IN_BN)))
    w_pool_b = w_pool.astype(BF16)
    pool_scale_r = pool_scale.reshape(L, 1, pool_w)

    for l in range(L):
        sh1, sc1, g1, sh2, sc2, g2 = [mod[l, :B, k * D:(k + 1) * D].reshape(B, 1, D) for k in range(6)]
        proj = _in_call(xf, norm_g[l, 0:1], sc1, sh1, w_in_p, l, T)
        a_out = _pool_call(proj, w_pool_b, pool_scale_r, l, T)
        ks_aug, vs_t, kw_aug, vw_t = _kprep_call(proj, k_gain[l], B, T, ks_col0)
        pe2 = pe_cmp[l].reshape(2, 2, (CMP_LEN // 2) * HEAD_DIM)
        cmp_k, cmp_vt = _cmp_call(proj, pe2, w_cmp1[l].astype(BF16), w_cmp2[l].astype(BF16), k_gain[l],
                                  B, T, kc_col0)
        o_att, w_out_b = _attn_call(proj, q_gain[l].reshape(1, HEAD_DIM), cmp_k, cmp_vt, ks_aug, vs_t,
                                    kw_aug, vw_t, slopes, ovl_t, w_out, l, B, T, q_col0, gate_col0)
        xf, h2 = _out_call(a_out, o_att, w_out_b, xf, g1, norm_g[l, 1:2], sc2, sh2, T)
        hidden, w_down_b = _ffn1_call(h2, w_gate_up, w_down, l)
        xf = _ffn2_call(hidden, w_down_b, xf, g2, T)
    return xf.reshape(B, T, D)
```

```python
import functools

import numpy as np
import jax
import jax.numpy as jnp
from jax import lax
from jax.experimental import pallas as pl
from jax.experimental.pallas import tpu as pltpu

F32 = jnp.float32
BF16 = jnp.bfloat16

POOL_WINDOWS = (2, 4, 8, 16)
HEAD_DIM = 128
N_KV = 2
CMP_LEN = 32
CMP_STRIDE = 16
SLC_LEN = 64
SLC_SHIFT = 6
SLC_TOPK = 16
WIN = 512
NORM_EPS = 1e-6
NEG_INF = -1e30
FORCE_BONUS = 1e3

LANES = 128
POOL_HALO = 16
POOL_PAD = 8
assert all(w & (w - 1) == 0 and w <= POOL_HALO for w in POOL_WINDOWS)
VMEM_LIMIT_BYTES = 56 * 1024 * 1024

ADA_BN = 2048
IN_BM, IN_BN = 1024, 1280
KPREP_BT = 2048
POOL_BT = 1024
ATT_TQ = 256
OUT_BM = 512
OUT_ROW_CHUNKS = 2
FFN1_BM, FFN1_BN = 1024, 512
FFN2_BM, FFN2_BN = 1024, 512


def _cparams(*sem):
    return pltpu.CompilerParams(dimension_semantics=sem, vmem_limit_bytes=VMEM_LIMIT_BYTES)


def _dot(a, b):
    return jnp.dot(a, b, preferred_element_type=F32)


def _dot_nt(a, b):
    return lax.dot_general(a, b, (((1,), (1,)), ((), ())), preferred_element_type=F32)


def _rms(x):
    return x * lax.rsqrt(jnp.mean(x * x, axis=-1, keepdims=True) + NORM_EPS)


def _sigmoid(x):
    return 1.0 / (1.0 + jnp.exp(-x))


def _ada_kernel(c_ref, w_ref, b_ref, o_ref):
    c = c_ref[...]
    cs = c * _sigmoid(c)
    o_ref[0] = _dot(cs, w_ref[0]) + b_ref[0]


def _ada_call(c8, w_ada, b_ada):
    L, D, N = w_ada.shape
    rows = c8.shape[0]
    return pl.pallas_call(
        _ada_kernel,
        grid=(L, N // ADA_BN),
        in_specs=[
            pl.BlockSpec((rows, D), lambda l, n: (0, 0)),
            pl.BlockSpec((1, D, ADA_BN), lambda l, n: (l, 0, n)),
            pl.BlockSpec((1, 1, ADA_BN), lambda l, n: (l, 0, n)),
        ],
        out_specs=pl.BlockSpec((1, rows, ADA_BN), lambda l, n: (l, 0, n)),
        out_shape=jax.ShapeDtypeStruct((L, rows, N), F32),
        compiler_params=_cparams("parallel", "arbitrary"),
        name="ada_mod",
    )(c8, w_ada, b_ada.reshape(L, 1, N))


def _norm_mod(x, ng_ref, sc_ref, sh_ref):
    y = _rms(x) * ng_ref[...]
    return (y * (1.0 + sc_ref[0]) + sh_ref[0]).astype(BF16)


def _lookahead_row_tile(n_row_tiles):
    def idx(i, n):
        return jnp.where((i == 0) & (n == 0), 0, jnp.minimum(i + 1, n_row_tiles - 1))
    return idx


def _norm_chunks(n_col_steps):
    assert n_col_steps >= 2
    return 1 << ((n_col_steps - 1).bit_length() - 1)


def _norm_matmul_steps(x_ref, ng_ref, sc_ref, sh_ref, h_sc, emit, n_chunks):
    i, n = pl.program_id(0), pl.program_id(1)
    rows = x_ref.shape[0] // n_chunks
    slab_step = (n >= 1) & (n <= n_chunks)

    @pl.when((i == 0) & (n == 0))
    def _():
        h_sc[0] = _norm_mod(x_ref[...], ng_ref, sc_ref, sh_ref)

    @pl.when(jnp.logical_not(slab_step))
    def _():
        emit(h_sc[i % 2])

    for slot in (0, 1):
        @pl.when(slab_step & (i % 2 == slot))
        def _(slot=slot):
            emit(h_sc[slot])
            slab = pl.ds(pl.multiple_of((n - 1) * rows, rows), rows)
            h_sc[1 - slot, slab, :] = _norm_mod(x_ref[slab, :], ng_ref, sc_ref, sh_ref)


def _in_kernel(x_ref, ng_ref, sc_ref, sh_ref, w_ref, o_ref, h_sc, *, n_chunks):
    def emit(h):
        o_ref[...] = _dot(h, w_ref[...]).astype(o_ref.dtype)

    _norm_matmul_steps(x_ref, ng_ref, sc_ref, sh_ref, h_sc, emit, n_chunks)


def _in_call(xf, ng, sc, sh, w_all, layer, T):
    M, D = xf.shape
    N = w_all.shape[2]
    per_b = T // IN_BM
    nm = M // IN_BM
    n_chunks = _norm_chunks(N // IN_BN)
    row = _lookahead_row_tile(nm)
    return pl.pallas_call(
        functools.partial(_in_kernel, n_chunks=n_chunks),
        grid=(nm, N // IN_BN),
        in_specs=[
            pl.BlockSpec((IN_BM, D), lambda i, n: (row(i, n), 0)),
            pl.BlockSpec((1, D), lambda i, n: (0, 0)),
            pl.BlockSpec((1, 1, D), lambda i, n: (row(i, n) // per_b, 0, 0)),
            pl.BlockSpec((1, 1, D), lambda i, n: (row(i, n) // per_b, 0, 0)),
            pl.BlockSpec((None, D, IN_BN), lambda i, n: (layer, 0, n)),
        ],
        out_specs=pl.BlockSpec((IN_BM, IN_BN), lambda i, n: (i, n)),
        out_shape=jax.ShapeDtypeStruct((M, N), BF16),
        scratch_shapes=[pltpu.VMEM((2, IN_BM, D), BF16)],
        compiler_params=_cparams("arbitrary", "arbitrary"),
        name="in_proj",
    )(xf, ng, sc, sh, w_all)


def _pool_kernel(u_ref, halo_ref, w_ref, ps_ref, o_ref, ext_sc, sa_sc, sb_sc, *, tiles_per_batch):
    i = pl.program_id(0)
    bt = u_ref.shape[0]
    group = w_ref.shape[1]
    first = POOL_PAD + POOL_HALO
    n = first + bt
    tile_in_batch = i % tiles_per_batch
    for buf in (ext_sc, sa_sc, sb_sc):
        buf[0:POOL_PAD, :] = jnp.zeros((POOL_PAD, buf.shape[1]), F32)
    ext_sc[POOL_PAD:first, :] = jnp.where(tile_in_batch == 0, 0.0, halo_ref[...].astype(F32))
    ext_sc[first:, :] = u_ref[...].astype(F32)
    t_head = lax.broadcasted_iota(jnp.int32, (POOL_HALO, 1), 0)
    for gi, w in enumerate(POOL_WINDOWS):
        cols = slice(gi * group, (gi + 1) * group)
        src, col_sel, k = ext_sc, cols, 1
        for dst in (sa_sc, sb_sc, sa_sc, sb_sc):
            if k >= w:
                break
            dst[POOL_PAD:n, :] = src[POOL_PAD:n, col_sel] + src[POOL_PAD - k:n - k, col_sel]
            src, col_sel, k = dst, slice(None), 2 * k
        tok = ext_sc[first:, cols]
        pooled = src[first:n, col_sel] * (1.0 / w) - tok
        cnt = jnp.minimum(t_head + 1, w).astype(F32)
        head = src[first:first + POOL_HALO, col_sel] / cnt - tok[0:POOL_HALO]
        head = jnp.where(tile_in_batch == 0, head, pooled[0:POOL_HALO])
        pooled = jnp.concatenate([head, pooled[POOL_HALO:]], axis=0)
        mixed = _dot(pooled.astype(BF16), w_ref[gi])
        o_ref[:, cols] = (mixed * ps_ref[:, cols]).astype(o_ref.dtype)


def _pool_call(proj, w_pool_all, pool_scale_all, layer, T):
    M = proj.shape[0]
    _, ng, group, _ = w_pool_all.shape
    width = ng * group
    tiles_per_batch = T // POOL_BT
    halo_blocks = POOL_BT // POOL_HALO
    return pl.pallas_call(
        functools.partial(_pool_kernel, tiles_per_batch=tiles_per_batch),
        grid=(M // POOL_BT,),
        in_specs=[
            pl.BlockSpec((POOL_BT, width), lambda i: (i, 0)),
            pl.BlockSpec((POOL_HALO, width), lambda i: (jnp.maximum(i * halo_blocks - 1, 0), 0)),
            pl.BlockSpec((None, ng, group, group), lambda i: (layer, 0, 0, 0)),
            pl.BlockSpec((None, 1, width), lambda i: (layer, 0, 0)),
        ],
        out_specs=pl.BlockSpec((POOL_BT, width), lambda i: (i, 0)),
        out_shape=jax.ShapeDtypeStruct((M, width), BF16),
        scratch_shapes=[pltpu.VMEM((POOL_PAD + POOL_HALO + POOL_BT, width), F32),
                        pltpu.VMEM((POOL_PAD + POOL_HALO + POOL_BT, group), F32),
                        pltpu.VMEM((POOL_PAD + POOL_HALO + POOL_BT, group), F32)],
        compiler_params=_cparams("parallel"),
        name="pool_mixer",
    )(proj, proj, w_pool_all, pool_scale_all)


AUG_SPLIT = 3
AUG_BLK_COL = SLC_LEN
AUG_OFF_COL = SLC_LEN + AUG_SPLIT
V_ROWS = HEAD_DIM + 16
LOG2E = 1.4426950408889634


def _key_extra(t):
    lane = lax.broadcasted_iota(jnp.int32, (t.shape[0], LANES), 1)
    blk = lax.shift_right_logical(t, SLC_SHIFT)
    off = t & (SLC_LEN - 1)
    extra = jnp.where(lane == blk, 1.0, 0.0)
    extra = jnp.where((lane >= AUG_BLK_COL) & (lane < AUG_OFF_COL), blk.astype(F32), extra)
    extra = jnp.where((lane >= AUG_OFF_COL) & (lane < AUG_OFF_COL + AUG_SPLIT), off.astype(F32), extra)
    return extra


def _value_tile_t(v_t):
    pad = lax.broadcasted_iota(jnp.int32, (V_ROWS - HEAD_DIM, v_t.shape[1]), 0)
    return jnp.concatenate([v_t, jnp.where(pad == 0, 1.0, 0.0)], axis=0).astype(BF16)


def _kprep_kernel(ks_ref, vs_ref, kw_ref, vw_ref, kg_ref, ksa_ref, vso_ref, kwa_ref, vwo_ref,
                  *, tiles_per_batch):
    bt = ks_ref.shape[0]
    kt = vso_ref.shape[4]
    t = (pl.program_id(0) % tiles_per_batch) * bt + lax.broadcasted_iota(jnp.int32, (bt, 1), 0)
    extra = _key_extra(t).astype(BF16)
    for g in range(N_KV):
        cols = slice(g * HEAD_DIM, (g + 1) * HEAD_DIM)
        ksn = _rms(ks_ref[:, cols].astype(F32)) * kg_ref[1:2, :]
        kwn = _rms(kw_ref[:, cols].astype(F32)) * kg_ref[2:3, :]
        ksa_ref[0, g, :, 0:HEAD_DIM] = ksn.astype(BF16)
        ksa_ref[0, g, :, HEAD_DIM:] = extra
        kwa_ref[0, g, :, 0:HEAD_DIM] = kwn.astype(BF16)
        kwa_ref[0, g, :, HEAD_DIM:] = extra
        vs_t = _value_tile_t(vs_ref[:, cols].astype(F32).T)
        vw_t = _value_tile_t(vw_ref[:, cols].astype(F32).T)
        for j in range(bt // kt):
            vso_ref[0, g, j] = vs_t[:, j * kt:(j + 1) * kt]
            vwo_ref[0, g, j] = vw_t[:, j * kt:(j + 1) * kt]


def _kprep_call(proj, k_gain, B, T, col0):
    kvw = N_KV * HEAD_DIM
    cb = col0 // kvw
    tiles_per_batch = T // KPREP_BT
    vt_per_tile = KPREP_BT // ATT_TQ
    aug = jax.ShapeDtypeStruct((B, N_KV, T, 2 * HEAD_DIM), BF16)
    val = jax.ShapeDtypeStruct((B, N_KV, T // ATT_TQ, V_ROWS, ATT_TQ), BF16)
    in_spec = lambda j: pl.BlockSpec((KPREP_BT, kvw), lambda i: (i, cb + j))
    out_map = lambda i: (i // tiles_per_batch, 0, i % tiles_per_batch, 0)
    val_map = lambda i: (i // tiles_per_batch, 0, i % tiles_per_batch, 0, 0)
    return pl.pallas_call(
        functools.partial(_kprep_kernel, tiles_per_batch=tiles_per_batch),
        grid=(B * tiles_per_batch,),
        in_specs=[in_spec(0), in_spec(1), in_spec(2), in_spec(3),
                  pl.BlockSpec((3, HEAD_DIM), lambda i: (0, 0))],
        out_specs=[pl.BlockSpec((1, N_KV, KPREP_BT, 2 * HEAD_DIM), out_map),
                   pl.BlockSpec((1, N_KV, vt_per_tile, V_ROWS, ATT_TQ), val_map),
                   pl.BlockSpec((1, N_KV, KPREP_BT, 2 * HEAD_DIM), out_map),
                   pl.BlockSpec((1, N_KV, vt_per_tile, V_ROWS, ATT_TQ), val_map)],
        out_shape=[aug, val, aug, val],
        compiler_params=_cparams("parallel"),
        name="kv_prep",
    )(proj, proj, proj, proj, k_gain)


def _compress_one(src_ref, f32_sc, pe_ref, w1_ref, kv, nc):
    half = CMP_LEN // 2
    assert CMP_STRIDE == half
    f32_sc[...] = src_ref[...].astype(F32)
    xs = [f32_sc[pl.ds(j, nc, stride=CMP_STRIDE), :] for j in range(half)]
    x = jnp.concatenate(xs, axis=1)
    kdim = half * HEAD_DIM
    lo = _dot((x + pe_ref[kv, 0:1, :]).astype(BF16), w1_ref[kv, 0:kdim, :])
    hi = _dot((x + pe_ref[kv, 1:2, :]).astype(BF16), w1_ref[kv, kdim:, :])
    pre = lo + pltpu.roll(hi, nc - 1, 0)
    return (pre * _sigmoid(pre)).astype(BF16)


def _cmp_kernel(k_ref, v_ref, pe_ref, w1_ref, w2k_ref, w2vt_ref, kg_ref, ko_ref, vo_ref, f32_sc):
    nc = ko_ref.shape[2]
    kc = _dot(_compress_one(k_ref, f32_sc, pe_ref, w1_ref, 0, nc), w2k_ref[...])
    ko_ref[0, 0] = (_rms(kc) * kg_ref[0:1, :]).astype(ko_ref.dtype)
    vct = _dot_nt(w2vt_ref[...], _compress_one(v_ref, f32_sc, pe_ref, w1_ref, 1, nc))
    vo_ref[0, 0] = vct.astype(vo_ref.dtype)


def _cmp_call(proj, pe2, w1, w2, k_gain, B, T, col0):
    nc = T // CMP_STRIDE
    cb = col0 // HEAD_DIM
    kdim = CMP_LEN * HEAD_DIM
    return pl.pallas_call(
        _cmp_kernel,
        grid=(B, N_KV),
        in_specs=[
            pl.BlockSpec((T, HEAD_DIM), lambda b, g: (b, cb + g)),
            pl.BlockSpec((T, HEAD_DIM), lambda b, g: (b, cb + N_KV + g)),
            pl.BlockSpec((2, 2, kdim // 2), lambda b, g: (0, 0, 0)),
            pl.BlockSpec((2, kdim, HEAD_DIM), lambda b, g: (0, 0, 0)),
            pl.BlockSpec((HEAD_DIM, HEAD_DIM), lambda b, g: (0, 0)),
            pl.BlockSpec((HEAD_DIM, HEAD_DIM), lambda b, g: (0, 0)),
            pl.BlockSpec((3, HEAD_DIM), lambda b, g: (0, 0)),
        ],
        out_specs=[pl.BlockSpec((1, 1, nc, HEAD_DIM), lambda b, g: (b, g, 0, 0)),
                   pl.BlockSpec((1, 1, HEAD_DIM, nc), lambda b, g: (b, g, 0, 0))],
        out_shape=[jax.ShapeDtypeStruct((B, N_KV, nc, HEAD_DIM), BF16),
                   jax.ShapeDtypeStruct((B, N_KV, HEAD_DIM, nc), BF16)],
        scratch_shapes=[pltpu.VMEM((T, HEAD_DIM), F32)],
        compiler_params=_cparams("parallel", "arbitrary"),
        name="compress",
    )(proj, proj, pe2, w1, w2[0], w2[1].T, k_gain)


MAX_FLOOR = 0.1 * NEG_INF


def _exp2_cols(s, mask):
    s = jnp.where(mask, s, NEG_INF)
    m = jnp.maximum(jnp.max(s, axis=0, keepdims=True), MAX_FLOOR)
    return jnp.exp2(s - m)


def _split3(c, shape):
    c = jnp.full(shape, c, F32)
    c1 = c.astype(BF16).astype(F32)
    r1 = c - c1
    c2 = r1.astype(BF16).astype(F32)
    c3 = (r1 - c2).astype(BF16).astype(F32)
    return c1, c2, c3


def _attn_kernel(slope_ref, q_ref, gl_ref, qg_ref, cmp_k_ref, cmp_vt_ref, ksa_ref, vst_ref,
                 kwa_ref, vwt_ref, ovl_ref, wo_ref, o_ref, wob_ref, score_sc, qs_sc, sa_sc, sb_sc, m_sc,
                 acc_sc, gate_sc, seq_ref, *, n_rep):
    g = pl.program_id(1)
    i = pl.program_id(2)
    wob_ref[...] = wo_ref[...].astype(BF16)
    tq = q_ref.shape[0]
    nc = cmp_k_ref.shape[2]
    n_slc = ovl_ref.shape[0]
    t0 = i * tq
    slopes = [slope_ref[g, r] * LOG2E for r in range(n_rep)]
    scale = HEAD_DIM ** -0.5 * LOG2E
    head = lambda a, r: a[:, r * tq:(r + 1) * tq]

    qt = []
    for r in range(n_rep):
        x = q_ref[:, r * HEAD_DIM:(r + 1) * HEAD_DIM].astype(F32)
        qt.append((_rms(x) * qg_ref[...] * scale).T.astype(BF16))
    qt_all = jnp.concatenate(qt, axis=1)

    s_all = _dot(cmp_k_ref[0, 0], qt_all)
    c_idx = lax.broadcasted_iota(jnp.int32, (nc, tq), 0)
    t_idx = lax.broadcasted_iota(jnp.int32, (nc, tq), 1) + t0
    dist_c = (t_idx - (c_idx * CMP_STRIDE + (CMP_LEN - 1))).astype(F32)
    mask_c = dist_c >= 0.0
    p_cmp = []
    p_sum = jnp.zeros((nc, tq), F32)
    for r in range(n_rep):
        p = _exp2_cols(head(s_all, r) - slopes[r] * dist_c, mask_c)
        l = jnp.sum(p, axis=0, keepdims=True)
        p = p * (1.0 / jnp.where(l > 0.0, l, 1.0))
        p_cmp.append(p.astype(BF16))
        p_sum = p_sum + p
    o_cmp_t = _dot(cmp_vt_ref[0, 0], jnp.concatenate(p_cmp, axis=1))

    p_hi = p_sum.astype(BF16)
    p_lo = (p_sum - p_hi.astype(F32)).astype(BF16)
    ovl = ovl_ref[...]
    imp = _dot(ovl, p_hi) + _dot(ovl, p_lo)
    jb = lax.broadcasted_iota(jnp.int32, (n_slc, tq), 0)
    tt = lax.broadcasted_iota(jnp.int32, (n_slc, tq), 1) + t0
    cur = lax.shift_right_logical(tt, SLC_SHIFT)
    forced = (jb == 0) | (jb == cur) | (jb == cur - 1)
    score = jnp.where(jb * SLC_LEN <= tt, imp + jnp.where(forced, FORCE_BONUS, 0.0), NEG_INF)
    score_sc[...] = score
    sub = 8
    groups = [score[sub * rg:sub * (rg + 1)] for rg in range(n_slc // sub)]
    ranks = [jnp.zeros((sub, tq), F32) for _ in groups]
    jrow = lax.broadcasted_iota(jnp.int32, (sub, tq), 0)
    for b2 in range(n_slc):
        sb = jnp.broadcast_to(score_sc[b2:b2 + 1, :], (sub, tq))
        for rg, sg in enumerate(groups):
            if sub * rg > b2:
                beats = sb >= sg
            elif sub * rg + sub - 1 < b2:
                beats = sb > sg
            else:
                beats = (sb > sg) | ((sb == sg) & (jrow > b2 - sub * rg))
            ranks[rg] = ranks[rg] + jnp.where(beats, 1.0, 0.0)
    n_sel = min(SLC_TOPK, n_slc)
    sel_bias_t = jnp.concatenate([jnp.where(rk < n_sel, 0.0, NEG_INF) for rk in ranks], axis=0)

    bpt = tq // SLC_LEN
    n_list = jnp.int32(0)
    for kt in range(n_slc // bpt - 1):
        rg, off = divmod(kt * bpt, sub)
        hit = jnp.max(jnp.where(ranks[rg][off:off + bpt] < n_sel, 1.0, 0.0), axis=1, keepdims=True)
        hit = jnp.max(hit, axis=0, keepdims=True)[0, 0]
        seq_ref[n_list] = jnp.int32(kt)
        n_list = n_list + ((hit > 0.0) & (kt < i)).astype(jnp.int32)
    seq_ref[n_list] = i

    pshape = (LANES - AUG_BLK_COL, tq)
    frow = lax.broadcasted_iota(jnp.int32, pshape, 0)
    sel_rows = sel_bias_t.astype(BF16)
    if n_slc < AUG_BLK_COL:
        sel_rows = jnp.concatenate([sel_rows, jnp.zeros((AUG_BLK_COL - n_slc, tq), BF16)], axis=0)
    q_slc, q_win = [], []
    for r in range(n_rep):
        pos_rows = jnp.zeros(pshape, F32)
        for k, ck in enumerate(_split3(slopes[r], pshape)):
            pos_rows = jnp.where(frow == k, ck * SLC_LEN, pos_rows)
            pos_rows = jnp.where(frow == AUG_SPLIT + k, ck, pos_rows)
        pos_rows = pos_rows.astype(BF16)
        q_slc.append(jnp.concatenate([qt[r], sel_rows, pos_rows], axis=0))
        q_win.append(jnp.concatenate([qt[r], jnp.zeros_like(sel_rows), pos_rows], axis=0))
    qs_sc[...] = jnp.concatenate(q_slc, axis=1)
    q_win = jnp.concatenate(q_win, axis=1)

    n_wt = WIN // tq + 1
    span = n_wt * tq
    j0 = jnp.maximum(i - (n_wt - 1), 0)
    start_w = pl.multiple_of(j0 * tq, tq)
    s_w = _dot(kwa_ref[0, 0, pl.ds(start_w, span), :], q_win)
    d_w = (lax.broadcasted_iota(jnp.int32, (span, tq), 1) + t0) - \
          (lax.broadcasted_iota(jnp.int32, (span, tq), 0) + start_w)
    mask_w = (d_w >= 0) & (d_w < WIN)
    p_w = jnp.concatenate([_exp2_cols(head(s_w, r), mask_w).astype(BF16) for r in range(n_rep)], axis=1)
    o_win_t = _dot(vwt_ref[0, 0, j0], p_w[0:tq])
    for jj in range(1, n_wt):
        o_win_t = o_win_t + _dot(vwt_ref[0, 0, j0 + jj], p_w[jj * tq:(jj + 1) * tq])
    o_win_t = o_win_t[0:HEAD_DIM] * (1.0 / o_win_t[HEAD_DIM:HEAD_DIM + 1])

    m_sc[...] = jnp.full(m_sc.shape, NEG_INF, F32)
    acc_sc[...] = jnp.zeros(acc_sc.shape, F32)

    def produce(kt, buf):
        start = pl.multiple_of(kt * tq, tq)
        buf[...] = _dot(ksa_ref[0, 0, pl.ds(start, tq), :], qs_sc[...])

    def consume(kt, buf, causal):
        s = buf[...]
        if causal:
            kpos = lax.broadcasted_iota(jnp.int32, s.shape, 0)
            qpos = lax.broadcasted_iota(jnp.int32, s.shape, 1) & (tq - 1)
            s = jnp.where(kpos <= qpos, s, NEG_INF)
        m_old = m_sc[...]
        m_new = jnp.maximum(m_old, jnp.max(s, axis=0, keepdims=True))
        alpha = jnp.exp2(m_old - m_new)
        p = jnp.exp2(s - m_new)
        acc_sc[...] = alpha * acc_sc[...] + _dot(vst_ref[0, 0, kt], p.astype(BF16))
        m_sc[...] = m_new

    def pair(j, carry):
        produce(seq_ref[2 * j + 1], sb_sc)
        consume(seq_ref[2 * j], sa_sc, False)
        produce(seq_ref[2 * j + 2], sa_sc)
        consume(seq_ref[2 * j + 1], sb_sc, False)
        return carry

    produce(seq_ref[0], sa_sc)
    lax.fori_loop(0, n_list // 2, pair, 0)

    @pl.when(n_list % 2 == 1)
    def _():
        produce(i, sb_sc)
        consume(seq_ref[n_list - 1], sa_sc, False)
        consume(i, sb_sc, True)

    @pl.when(n_list % 2 == 0)
    def _():
        consume(i, sa_sc, True)

    o_slc_t = acc_sc[0:HEAD_DIM, :] * (1.0 / acc_sc[HEAD_DIM:HEAD_DIM + 1, :])

    gate_sc[...] = _sigmoid(gl_ref[...].astype(F32).T)
    row0 = 3 * n_rep * g
    gate = lambda k: gate_sc[pl.ds(row0 + k, 1), :]
    for r in range(n_rep):
        o_t = (gate(3 * r) * head(o_cmp_t, r)
               + gate(3 * r + 1) * head(o_slc_t, r)
               + gate(3 * r + 2) * head(o_win_t, r))
        o_ref[:, r * HEAD_DIM:(r + 1) * HEAD_DIM] = o_t.T.astype(o_ref.dtype)


def _attn_call(proj, q_gain, cmp_k, cmp_vt, ks_aug, vs_t, kw_aug, vw_t, slopes, ovl_t, w_out, layer,
               B, T, q_col0, gate_col0):
    n_rep = slopes.shape[1]
    wo_rows, wo_cols = w_out.shape[1:]
    slab = wo_rows // (B * N_KV * (T // ATT_TQ))
    assert slab * B * N_KV * (T // ATT_TQ) == wo_rows and slab % 16 == 0
    gw = n_rep * HEAD_DIM
    nq = T // ATT_TQ
    nc = cmp_k.shape[2]
    n_slc = ovl_t.shape[0]
    qcb = q_col0 // gw
    gcb = gate_col0 // LANES
    assert WIN % ATT_TQ == 0
    keys = pl.BlockSpec((1, 1, T, 2 * HEAD_DIM), lambda b, g, i: (b, g, 0, 0))
    vals = pl.BlockSpec((1, 1, nq, V_ROWS, ATT_TQ), lambda b, g, i: (b, g, 0, 0, 0))
    return pl.pallas_call(
        functools.partial(_attn_kernel, n_rep=n_rep),
        grid=(B, N_KV, nq),
        in_specs=[
            pl.BlockSpec(memory_space=pltpu.SMEM),
            pl.BlockSpec((ATT_TQ, gw), lambda b, g, i: (b * nq + i, qcb + g)),
            pl.BlockSpec((ATT_TQ, LANES), lambda b, g, i: (b * nq + i, gcb)),
            pl.BlockSpec((1, HEAD_DIM), lambda b, g, i: (0, 0)),
            pl.BlockSpec((1, 1, nc, HEAD_DIM), lambda b, g, i: (b, g, 0, 0)),
            pl.BlockSpec((1, 1, HEAD_DIM, nc), lambda b, g, i: (b, g, 0, 0)),
            keys, vals, keys, vals,
            pl.BlockSpec((n_slc, nc), lambda b, g, i: (0, 0)),
            pl.BlockSpec((None, slab, wo_cols), lambda b, g, i: (layer, (b * N_KV + g) * nq + i, 0)),
        ],
        out_specs=[pl.BlockSpec((ATT_TQ, gw), lambda b, g, i: (b * nq + i, g)),
                   pl.BlockSpec((slab, wo_cols), lambda b, g, i: ((b * N_KV + g) * nq + i, 0))],
        out_shape=[jax.ShapeDtypeStruct((B * T, N_KV * gw), BF16),
                   jax.ShapeDtypeStruct((wo_rows, wo_cols), BF16)],
        scratch_shapes=[
            pltpu.VMEM((n_slc, ATT_TQ), F32),
            pltpu.VMEM((2 * HEAD_DIM, n_rep * ATT_TQ), BF16),
            pltpu.VMEM((ATT_TQ, n_rep * ATT_TQ), F32),
            pltpu.VMEM((ATT_TQ, n_rep * ATT_TQ), F32),
            pltpu.VMEM((1, n_rep * ATT_TQ), F32),
            pltpu.VMEM((V_ROWS, n_rep * ATT_TQ), F32),
            pltpu.VMEM((LANES, ATT_TQ), F32),
            pltpu.SMEM((nq + 1,), jnp.int32),
        ],
        compiler_params=_cparams("parallel", "parallel", "arbitrary"),
        name="nsa_attention",
    )(slopes, proj, proj, q_gain, cmp_k, cmp_vt, ks_aug, vs_t, kw_aug, vw_t, ovl_t, w_out)


def _out_kernel(a_ref, o_ref, w_ref, x_ref, g_ref, ng_ref, sc_ref, sh_ref, y_ref, h_ref):
    ka = a_ref.shape[1]
    rc = a_ref.shape[0] // OUT_ROW_CHUNKS
    for c in range(OUT_ROW_CHUNKS):
        rows = slice(c * rc, (c + 1) * rc)
        acc = _dot(a_ref[rows, :], w_ref[0:ka, :]) + _dot(o_ref[rows, :], w_ref[ka:, :])
        y = x_ref[rows, :] + g_ref[0] * acc
        y_ref[rows, :] = y
        h_ref[rows, :] = _norm_mod(y, ng_ref, sc_ref, sh_ref)


def _out_call(a, o, w, xf, gate, ng, sc, sh, T):
    M, D = xf.shape
    ka, ko = a.shape[1], o.shape[1]
    per_b = T // OUT_BM
    row_spec = lambda width: pl.BlockSpec((OUT_BM, width), lambda i: (i, 0))
    mod_spec = pl.BlockSpec((1, 1, D), lambda i: (i // per_b, 0, 0))
    return pl.pallas_call(
        _out_kernel,
        grid=(M // OUT_BM,),
        in_specs=[
            row_spec(ka), row_spec(ko),
            pl.BlockSpec((ka + ko, D), lambda i: (0, 0)),
            row_spec(D), mod_spec,
            pl.BlockSpec((1, D), lambda i: (0, 0)),
            mod_spec, mod_spec,
        ],
        out_specs=[row_spec(D), row_spec(D)],
        out_shape=[jax.ShapeDtypeStruct((M, D), F32), jax.ShapeDtypeStruct((M, D), BF16)],
        compiler_params=_cparams("parallel"),
        name="out_proj",
    )(a, o, w, xf, gate, ng, sc, sh)


def _ffn1_kernel(h_ref, wg_ref, wu_ref, wd_ref, o_ref, wdb_ref, wg_sc, wu_sc):
    @pl.when(pl.program_id(1) == 0)
    def _():
        wg_sc[...] = wg_ref[...].astype(BF16)
        wu_sc[...] = wu_ref[...].astype(BF16)

    wdb_ref[...] = wd_ref[...].astype(BF16)
    h = h_ref[...]
    gate = _dot(h, wg_sc[...])
    up = _dot(h, wu_sc[...])
    o_ref[...] = (gate * _sigmoid(gate) * up).astype(o_ref.dtype)


def _ffn1_call(h, w_gu, w_down, layer):
    M, D = h.shape
    dff = w_gu.shape[2] // 2
    nt, nm = dff // FFN1_BN, M // FFN1_BM
    slab = dff // (nt * nm)
    assert slab * nt * nm == dff and slab % 16 == 0
    return pl.pallas_call(
        _ffn1_kernel,
        grid=(nt, nm),
        in_specs=[
            pl.BlockSpec((FFN1_BM, D), lambda n, i: (i, 0)),
            pl.BlockSpec((None, D, FFN1_BN), lambda n, i: (layer, 0, n)),
            pl.BlockSpec((None, D, FFN1_BN), lambda n, i: (layer, 0, n + nt)),
            pl.BlockSpec((None, slab, D), lambda n, i: (layer, n * nm + i, 0)),
        ],
        out_specs=[pl.BlockSpec((FFN1_BM, FFN1_BN), lambda n, i: (i, n)),
                   pl.BlockSpec((slab, D), lambda n, i: (n * nm + i, 0))],
        out_shape=[jax.ShapeDtypeStruct((M, dff), BF16), jax.ShapeDtypeStruct((dff, D), BF16)],
        scratch_shapes=[pltpu.VMEM((D, FFN1_BN), BF16), pltpu.VMEM((D, FFN1_BN), BF16)],
        compiler_params=_cparams("arbitrary", "arbitrary"),
        name="ffn_up",
    )(h, w_gu, w_gu, w_down)


def _ffn2_kernel(h_ref, w_ref, x_ref, g_ref, y_ref):
    y_ref[...] = x_ref[...] + g_ref[0] * _dot(h_ref[...], w_ref[...])


def _ffn2_call(h, w, xf, gate, T):
    M, D = xf.shape
    dff = h.shape[1]
    per_b = T // FFN2_BM
    return pl.pallas_call(
        _ffn2_kernel,
        grid=(M // FFN2_BM, D // FFN2_BN),
        in_specs=[
            pl.BlockSpec((FFN2_BM, dff), lambda i, n: (i, 0)),
            pl.BlockSpec((dff, FFN2_BN), lambda i, n: (0, n)),
            pl.BlockSpec((FFN2_BM, FFN2_BN), lambda i, n: (i, n)),
            pl.BlockSpec((1, 1, FFN2_BN), lambda i, n: (i // per_b, 0, n)),
        ],
        out_specs=pl.BlockSpec((FFN2_BM, FFN2_BN), lambda i, n: (i, n)),
        out_shape=jax.ShapeDtypeStruct((M, D), F32),
        compiler_params=_cparams("parallel", "arbitrary"),
        name="ffn_down",
    )(h, w, xf, gate)


def _alibi_slopes(n_heads):
    sl = 2.0 ** (-8.0 * np.arange(1, n_heads + 1) / n_heads)
    return jnp.asarray(sl, F32).reshape(N_KV, n_heads // N_KV)


def _overlap_t(T):
    nc = T // CMP_STRIDE
    n_slc = T // SLC_LEN
    cst = np.arange(nc) * CMP_STRIDE
    sst = np.arange(n_slc) * SLC_LEN
    ov = (cst[None, :] < sst[:, None] + SLC_LEN) & (cst[None, :] + CMP_LEN > sst[:, None])
    ov[:, (T - CMP_LEN) // CMP_STRIDE + 1:] = False
    return jnp.asarray(ov.astype(np.float32), BF16)


def kernel(x, c, w_ada, b_ada, norm_g, w_in, q_gain, k_gain, pe_cmp, w_cmp1, w_cmp2,
           w_pool, pool_scale, w_out, w_gate_up, w_down):
    B, T, D = x.shape
    L = w_ada.shape[0]
    pool_w = w_pool.shape[1] * w_pool.shape[2]
    kvw = N_KV * HEAD_DIM
    n_heads = (w_in.shape[2] - pool_w - 6 * kvw) // (HEAD_DIM + 3)
    att_w = n_heads * HEAD_DIM
    n_rep = n_heads // N_KV
    assert w_in.shape[2] == pool_w + att_w + 6 * kvw + 3 * n_heads
    assert T % ATT_TQ == 0 and T >= WIN + ATT_TQ and T % POOL_BT == 0 and T % KPREP_BT == 0
    assert T // SLC_LEN <= AUG_BLK_COL and pool_w % (n_rep * HEAD_DIM) == 0
    assert 1 << SLC_SHIFT == SLC_LEN and ATT_TQ & (ATT_TQ - 1) == 0
    q_col0 = pool_w
    kc_col0 = pool_w + att_w
    ks_col0 = kc_col0 + 2 * kvw
    gate_col0 = kc_col0 + 6 * kvw
    assert ks_col0 % kvw == 0 and gate_col0 % LANES == 0

    xf = x.reshape(B * T, D)
    rows = -(-B // 8) * 8
    c8 = jnp.pad(c, ((0, rows - B), (0, 0)))
    mod = _ada_call(c8, w_ada, b_ada)
    slopes = _alibi_slopes(n_heads)
    ovl_t = _overlap_t(T)

    assert 3 * n_heads <= LANES
    w_in_p = jnp.pad(w_in.astype(BF16), ((0, 0), (0, 0), (0, -w_in.shape[2] % IN_BN)))
    w_pool_b = w_pool.astype(BF16)
    pool_scale_r = pool_scale.reshape(L, 1, pool_w)

    for l in range(L):
        sh1, sc1, g1, sh2, sc2, g2 = [mod[l, :B, k * D:(k + 1) * D].reshape(B, 1, D) for k in range(6)]
        proj = _in_call(xf, norm_g[l, 0:1], sc1, sh1, w_in_p, l, T)
        a_out = _pool_call(proj, w_pool_b, pool_scale_r, l, T)
        ks_aug, vs_t, kw_aug, vw_t = _kprep_call(proj, k_gain[l], B, T, ks_col0)
        pe2 = pe_cmp[l].reshape(2, 2, (CMP_LEN // 2) * HEAD_DIM)
        cmp_k, cmp_vt = _cmp_call(proj, pe2, w_cmp1[l].astype(BF16), w_cmp2[l].astype(BF16), k_gain[l],
                                  B, T, kc_col0)
        o_att, w_out_b = _attn_call(proj, q_gain[l].reshape(1, HEAD_DIM), cmp_k, cmp_vt, ks_aug, vs_t,
                                    kw_aug, vw_t, slopes, ovl_t, w_out, l, B, T, q_col0, gate_col0)
        xf, h2 = _out_call(a_out, o_att, w_out_b, xf, g1, norm_g[l, 1:2], sc2, sh2, T)
        hidden, w_down_b = _ffn1_call(h2, w_gate_up, w_down, l)
        xf = _ffn2_call(hidden, w_down_b, xf, g2, T)
    return xf.reshape(B, T, D)
```

```python
import functools

import numpy as np
import jax
import jax.numpy as jnp
from jax import lax
from jax.experimental import pallas as pl
from jax.experimental.pallas import tpu as pltpu

F32 = jnp.float32
BF16 = jnp.bfloat16

POOL_WINDOWS = (2, 4, 8, 16)
HEAD_DIM = 128
N_KV = 2
CMP_LEN = 32
CMP_STRIDE = 16
SLC_LEN = 64
SLC_SHIFT = 6
SLC_TOPK = 16
WIN = 512
NORM_EPS = 1e-6
NEG_INF = -1e30
FORCE_BONUS = 1e3

LANES = 128
POOL_HALO = 16
POOL_PAD = 8
assert all(w & (w - 1) == 0 and w <= POOL_HALO for w in POOL_WINDOWS)
VMEM_LIMIT_BYTES = 56 * 1024 * 1024

ADA_BN = 1024
IN_BM, IN_BN = 1024, 1280
KPREP_BT = 2048
POOL_BT = 2048
ATT_TQ = 256
OUT_BM = 512
OUT_ROW_CHUNKS = 2
FFN1_BM, FFN1_BN = 1024, 512
FFN2_BM, FFN2_BN = 1024, 512


def _cparams(*sem):
    return pltpu.CompilerParams(dimension_semantics=sem, vmem_limit_bytes=VMEM_LIMIT_BYTES)


def _dot(a, b):
    return jnp.dot(a, b, preferred_element_type=F32)


def _dot_nt(a, b):
    return lax.dot_general(a, b, (((1,), (1,)), ((), ())), preferred_element_type=F32)


def _rms(x):
    return x * lax.rsqrt(jnp.mean(x * x, axis=-1, keepdims=True) + NORM_EPS)


def _sigmoid(x):
    return 1.0 / (1.0 + jnp.exp(-x))


def _ada_kernel(c_ref, w_ref, b_ref, o_ref):
    c = c_ref[...]
    cs = c * _sigmoid(c)
    o_ref[0] = _dot(cs, w_ref[0]) + b_ref[0]


def _ada_call(c8, w_ada, b_ada):
    L, D, N = w_ada.shape
    rows = c8.shape[0]
    return pl.pallas_call(
        _ada_kernel,
        grid=(L, N // ADA_BN),
        in_specs=[
            pl.BlockSpec((rows, D), lambda l, n: (0, 0)),
            pl.BlockSpec((1, D, ADA_BN), lambda l, n: (l, 0, n)),
            pl.BlockSpec((1, 1, ADA_BN), lambda l, n: (l, 0, n)),
        ],
        out_specs=pl.BlockSpec((1, rows, ADA_BN), lambda l, n: (l, 0, n)),
        out_shape=jax.ShapeDtypeStruct((L, rows, N), F32),
        compiler_params=_cparams("parallel", "arbitrary"),
        name="ada_mod",
    )(c8, w_ada, b_ada.reshape(L, 1, N))


def _norm_mod(x, ng_ref, sc_ref, sh_ref):
    y = _rms(x) * ng_ref[...]
    return (y * (1.0 + sc_ref[0]) + sh_ref[0]).astype(BF16)


def _lookahead_row_tile(n_row_tiles):
    def idx(i, n):
        return jnp.where((i == 0) & (n == 0), 0, jnp.minimum(i + 1, n_row_tiles - 1))
    return idx


def _norm_chunks(n_col_steps):
    assert n_col_steps >= 2
    return 1 << ((n_col_steps - 1).bit_length() - 1)


def _norm_matmul_steps(x_ref, ng_ref, sc_ref, sh_ref, h_sc, emit, n_chunks):
    i, n = pl.program_id(0), pl.program_id(1)
    rows = x_ref.shape[0] // n_chunks
    slab_step = (n >= 1) & (n <= n_chunks)

    @pl.when((i == 0) & (n == 0))
    def _():
        h_sc[0] = _norm_mod(x_ref[...], ng_ref, sc_ref, sh_ref)

    @pl.when(jnp.logical_not(slab_step))
    def _():
        emit(h_sc[i % 2])

    for slot in (0, 1):
        @pl.when(slab_step & (i % 2 == slot))
        def _(slot=slot):
            emit(h_sc[slot])
            slab = pl.ds(pl.multiple_of((n - 1) * rows, rows), rows)
            h_sc[1 - slot, slab, :] = _norm_mod(x_ref[slab, :], ng_ref, sc_ref, sh_ref)


def _in_kernel(x_ref, ng_ref, sc_ref, sh_ref, w_ref, o_ref, h_sc, *, n_chunks):
    def emit(h):
        o_ref[...] = _dot(h, w_ref[...]).astype(o_ref.dtype)

    _norm_matmul_steps(x_ref, ng_ref, sc_ref, sh_ref, h_sc, emit, n_chunks)


def _in_call(xf, ng, sc, sh, w_all, layer, T):
    M, D = xf.shape
    N = w_all.shape[2]
    per_b = T // IN_BM
    nm = M // IN_BM
    n_chunks = _norm_chunks(N // IN_BN)
    row = _lookahead_row_tile(nm)
    return pl.pallas_call(
        functools.partial(_in_kernel, n_chunks=n_chunks),
        grid=(nm, N // IN_BN),
        in_specs=[
            pl.BlockSpec((IN_BM, D), lambda i, n: (row(i, n), 0)),
            pl.BlockSpec((1, D), lambda i, n: (0, 0)),
            pl.BlockSpec((1, 1, D), lambda i, n: (row(i, n) // per_b, 0, 0)),
            pl.BlockSpec((1, 1, D), lambda i, n: (row(i, n) // per_b, 0, 0)),
            pl.BlockSpec((None, D, IN_BN), lambda i, n: (layer, 0, n)),
        ],
        out_specs=pl.BlockSpec((IN_BM, IN_BN), lambda i, n: (i, n)),
        out_shape=jax.ShapeDtypeStruct((M, N), BF16),
        scratch_shapes=[pltpu.VMEM((2, IN_BM, D), BF16)],
        compiler_params=_cparams("arbitrary", "arbitrary"),
        name="in_proj",
    )(xf, ng, sc, sh, w_all)


def _pool_kernel(u_ref, halo_ref, w_ref, ps_ref, o_ref, ext_sc, sa_sc, sb_sc, *, tiles_per_batch):
    i = pl.program_id(0)
    bt = u_ref.shape[0]
    group = w_ref.shape[1]
    first = POOL_PAD + POOL_HALO
    n = first + bt
    tile_in_batch = i % tiles_per_batch
    for buf in (ext_sc, sa_sc, sb_sc):
        buf[0:POOL_PAD, :] = jnp.zeros((POOL_PAD, buf.shape[1]), F32)
    ext_sc[POOL_PAD:first, :] = jnp.where(tile_in_batch == 0, 0.0, halo_ref[...].astype(F32))
    ext_sc[first:, :] = u_ref[...].astype(F32)
    t_head = lax.broadcasted_iota(jnp.int32, (POOL_HALO, 1), 0)
    for gi, w in enumerate(POOL_WINDOWS):
        cols = slice(gi * group, (gi + 1) * group)
        src, col_sel, k = ext_sc, cols, 1
        for dst in (sa_sc, sb_sc, sa_sc, sb_sc):
            if k >= w:
                break
            dst[POOL_PAD:n, :] = src[POOL_PAD:n, col_sel] + src[POOL_PAD - k:n - k, col_sel]
            src, col_sel, k = dst, slice(None), 2 * k
        tok = ext_sc[first:, cols]
        pooled = src[first:n, col_sel] * (1.0 / w) - tok
        cnt = jnp.minimum(t_head + 1, w).astype(F32)
        head = src[first:first + POOL_HALO, col_sel] / cnt - tok[0:POOL_HALO]
        head = jnp.where(tile_in_batch == 0, head, pooled[0:POOL_HALO])
        pooled = jnp.concatenate([head, pooled[POOL_HALO:]], axis=0)
        mixed = _dot(pooled.astype(BF16), w_ref[gi])
        o_ref[:, cols] = (mixed * ps_ref[:, cols]).astype(o_ref.dtype)


def _pool_call(proj, w_pool_all, pool_scale_all, layer, T):
    M = proj.shape[0]
    _, ng, group, _ = w_pool_all.shape
    width = ng * group
    tiles_per_batch = T // POOL_BT
    halo_blocks = POOL_BT // POOL_HALO
    return pl.pallas_call(
        functools.partial(_pool_kernel, tiles_per_batch=tiles_per_batch),
        grid=(M // POOL_BT,),
        in_specs=[
            pl.BlockSpec((POOL_BT, width), lambda i: (i, 0)),
            pl.BlockSpec((POOL_HALO, width), lambda i: (jnp.maximum(i * halo_blocks - 1, 0), 0)),
            pl.BlockSpec((None, ng, group, group), lambda i: (layer, 0, 0, 0)),
            pl.BlockSpec((None, 1, width), lambda i: (layer, 0, 0)),
        ],
        out_specs=pl.BlockSpec((POOL_BT, width), lambda i: (i, 0)),
        out_shape=jax.ShapeDtypeStruct((M, width), BF16),
        scratch_shapes=[pltpu.VMEM((POOL_PAD + POOL_HALO + POOL_BT, width), F32),
                        pltpu.VMEM((POOL_PAD + POOL_HALO + POOL_BT, group), F32),
                        pltpu.VMEM((POOL_PAD + POOL_HALO + POOL_BT, group), F32)],
        compiler_params=_cparams("parallel"),
        name="pool_mixer",
    )(proj, proj, w_pool_all, pool_scale_all)


AUG_SPLIT = 3
AUG_BLK_COL = SLC_LEN
AUG_OFF_COL = SLC_LEN + AUG_SPLIT
V_ROWS = HEAD_DIM + 16
LOG2E = 1.4426950408889634


def _key_extra(t):
    lane = lax.broadcasted_iota(jnp.int32, (t.shape[0], LANES), 1)
    blk = lax.shift_right_logical(t, SLC_SHIFT)
    off = t & (SLC_LEN - 1)
    extra = jnp.where(lane == blk, 1.0, 0.0)
    extra = jnp.where((lane >= AUG_BLK_COL) & (lane < AUG_OFF_COL), blk.astype(F32), extra)
    extra = jnp.where((lane >= AUG_OFF_COL) & (lane < AUG_OFF_COL + AUG_SPLIT), off.astype(F32), extra)
    return extra


def _value_tile_t(v_t):
    pad = lax.broadcasted_iota(jnp.int32, (V_ROWS - HEAD_DIM, v_t.shape[1]), 0)
    return jnp.concatenate([v_t, jnp.where(pad == 0, 1.0, 0.0)], axis=0).astype(BF16)


def _kprep_kernel(ks_ref, vs_ref, kw_ref, vw_ref, kg_ref, ksa_ref, vso_ref, kwa_ref, vwo_ref,
                  *, tiles_per_batch):
    bt = ks_ref.shape[0]
    kt = vso_ref.shape[4]
    t = (pl.program_id(0) % tiles_per_batch) * bt + lax.broadcasted_iota(jnp.int32, (bt, 1), 0)
    extra = _key_extra(t).astype(BF16)
    for g in range(N_KV):
        cols = slice(g * HEAD_DIM, (g + 1) * HEAD_DIM)
        ksn = _rms(ks_ref[:, cols].astype(F32)) * kg_ref[1:2, :]
        kwn = _rms(kw_ref[:, cols].astype(F32)) * kg_ref[2:3, :]
        ksa_ref[0, g, :, 0:HEAD_DIM] = ksn.astype(BF16)
        ksa_ref[0, g, :, HEAD_DIM:] = extra
        kwa_ref[0, g, :, 0:HEAD_DIM] = kwn.astype(BF16)
        kwa_ref[0, g, :, HEAD_DIM:] = extra
        vs_t = _value_tile_t(vs_ref[:, cols].astype(F32).T)
        vw_t = _value_tile_t(vw_ref[:, cols].astype(F32).T)
        for j in range(bt // kt):
            vso_ref[0, g, j] = vs_t[:, j * kt:(j + 1) * kt]
            vwo_ref[0, g, j] = vw_t[:, j * kt:(j + 1) * kt]


def _kprep_call(proj, k_gain, B, T, col0):
    kvw = N_KV * HEAD_DIM
    cb = col0 // kvw
    tiles_per_batch = T // KPREP_BT
    vt_per_tile = KPREP_BT // ATT_TQ
    aug = jax.ShapeDtypeStruct((B, N_KV, T, 2 * HEAD_DIM), BF16)
    val = jax.ShapeDtypeStruct((B, N_KV, T // ATT_TQ, V_ROWS, ATT_TQ), BF16)
    in_spec = lambda j: pl.BlockSpec((KPREP_BT, kvw), lambda i: (i, cb + j))
    out_map = lambda i: (i // tiles_per_batch, 0, i % tiles_per_batch, 0)
    val_map = lambda i: (i // tiles_per_batch, 0, i % tiles_per_batch, 0, 0)
    return pl.pallas_call(
        functools.partial(_kprep_kernel, tiles_per_batch=tiles_per_batch),
        grid=(B * tiles_per_batch,),
        in_specs=[in_spec(0), in_spec(1), in_spec(2), in_spec(3),
                  pl.BlockSpec((3, HEAD_DIM), lambda i: (0, 0))],
        out_specs=[pl.BlockSpec((1, N_KV, KPREP_BT, 2 * HEAD_DIM), out_map),
                   pl.BlockSpec((1, N_KV, vt_per_tile, V_ROWS, ATT_TQ), val_map),
                   pl.BlockSpec((1, N_KV, KPREP_BT, 2 * HEAD_DIM), out_map),
                   pl.BlockSpec((1, N_KV, vt_per_tile, V_ROWS, ATT_TQ), val_map)],
        out_shape=[aug, val, aug, val],
        compiler_params=_cparams("parallel"),
        name="kv_prep",
    )(proj, proj, proj, proj, k_gain)


def _compress_one(src_ref, f32_sc, pe_ref, w1_ref, kv, nc):
    half = CMP_LEN // 2
    assert CMP_STRIDE == half
    f32_sc[...] = src_ref[...].astype(F32)
    xs = [f32_sc[pl.ds(j, nc, stride=CMP_STRIDE), :] for j in range(half)]
    x = jnp.concatenate(xs, axis=1)
    kdim = half * HEAD_DIM
    lo = _dot((x + pe_ref[kv, 0:1, :]).astype(BF16), w1_ref[kv, 0:kdim, :])
    hi = _dot((x + pe_ref[kv, 1:2, :]).astype(BF16), w1_ref[kv, kdim:, :])
    pre = lo + pltpu.roll(hi, nc - 1, 0)
    return (pre * _sigmoid(pre)).astype(BF16)


def _cmp_kernel(k_ref, v_ref, pe_ref, w1_ref, w2k_ref, w2vt_ref, kg_ref, ko_ref, vo_ref, f32_sc):
    nc = ko_ref.shape[2]
    kc = _dot(_compress_one(k_ref, f32_sc, pe_ref, w1_ref, 0, nc), w2k_ref[...])
    ko_ref[0, 0] = (_rms(kc) * kg_ref[0:1, :]).astype(ko_ref.dtype)
    vct = _dot_nt(w2vt_ref[...], _compress_one(v_ref, f32_sc, pe_ref, w1_ref, 1, nc))
    vo_ref[0, 0] = vct.astype(vo_ref.dtype)


def _cmp_call(proj, pe2, w1, w2, k_gain, B, T, col0):
    nc = T // CMP_STRIDE
    cb = col0 // HEAD_DIM
    kdim = CMP_LEN * HEAD_DIM
    return pl.pallas_call(
        _cmp_kernel,
        grid=(B, N_KV),
        in_specs=[
            pl.BlockSpec((T, HEAD_DIM), lambda b, g: (b, cb + g)),
            pl.BlockSpec((T, HEAD_DIM), lambda b, g: (b, cb + N_KV + g)),
            pl.BlockSpec((2, 2, kdim // 2), lambda b, g: (0, 0, 0)),
            pl.BlockSpec((2, kdim, HEAD_DIM), lambda b, g: (0, 0, 0)),
            pl.BlockSpec((HEAD_DIM, HEAD_DIM), lambda b, g: (0, 0)),
            pl.BlockSpec((HEAD_DIM, HEAD_DIM), lambda b, g: (0, 0)),
            pl.BlockSpec((3, HEAD_DIM), lambda b, g: (0, 0)),
        ],
        out_specs=[pl.BlockSpec((1, 1, nc, HEAD_DIM), lambda b, g: (b, g, 0, 0)),
                   pl.BlockSpec((1, 1, HEAD_DIM, nc), lambda b, g: (b, g, 0, 0))],
        out_shape=[jax.ShapeDtypeStruct((B, N_KV, nc, HEAD_DIM), BF16),
                   jax.ShapeDtypeStruct((B, N_KV, HEAD_DIM, nc), BF16)],
        scratch_shapes=[pltpu.VMEM((T, HEAD_DIM), F32)],
        compiler_params=_cparams("parallel", "arbitrary"),
        name="compress",
    )(proj, proj, pe2, w1, w2[0], w2[1].T, k_gain)


MAX_FLOOR = 0.1 * NEG_INF


def _exp2_cols(s, mask):
    s = jnp.where(mask, s, NEG_INF)
    m = jnp.maximum(jnp.max(s, axis=0, keepdims=True), MAX_FLOOR)
    return jnp.exp2(s - m)


def _split3(c, shape):
    c = jnp.full(shape, c, F32)
    c1 = c.astype(BF16).astype(F32)
    r1 = c - c1
    c2 = r1.astype(BF16).astype(F32)
    c3 = (r1 - c2).astype(BF16).astype(F32)
    return c1, c2, c3


def _attn_kernel(slope_ref, q_ref, gl_ref, qg_ref, cmp_k_ref, cmp_vt_ref, ksa_ref, vst_ref,
                 kwa_ref, vwt_ref, ovl_ref, wo_ref, o_ref, wob_ref, score_sc, qs_sc, sa_sc, sb_sc, m_sc,
                 acc_sc, gate_sc, seq_ref, *, n_rep):
    g = pl.program_id(1)
    i = pl.program_id(2)
    wob_ref[...] = wo_ref[...].astype(BF16)
    tq = q_ref.shape[0]
    nc = cmp_k_ref.shape[2]
    n_slc = ovl_ref.shape[0]
    t0 = i * tq
    slopes = [slope_ref[g, r] * LOG2E for r in range(n_rep)]
    scale = HEAD_DIM ** -0.5 * LOG2E
    head = lambda a, r: a[:, r * tq:(r + 1) * tq]

    qt = []
    for r in range(n_rep):
        x = q_ref[:, r * HEAD_DIM:(r + 1) * HEAD_DIM].astype(F32)
        qt.append((_rms(x) * qg_ref[...] * scale).T.astype(BF16))
    qt_all = jnp.concatenate(qt, axis=1)

    s_all = _dot(cmp_k_ref[0, 0], qt_all)
    c_idx = lax.broadcasted_iota(jnp.int32, (nc, tq), 0)
    t_idx = lax.broadcasted_iota(jnp.int32, (nc, tq), 1) + t0
    dist_c = (t_idx - (c_idx * CMP_STRIDE + (CMP_LEN - 1))).astype(F32)
    mask_c = dist_c >= 0.0
    p_cmp = []
    p_sum = jnp.zeros((nc, tq), F32)
    for r in range(n_rep):
        p = _exp2_cols(head(s_all, r) - slopes[r] * dist_c, mask_c)
        l = jnp.sum(p, axis=0, keepdims=True)
        p = p * (1.0 / jnp.where(l > 0.0, l, 1.0))
        p_cmp.append(p.astype(BF16))
        p_sum = p_sum + p
    o_cmp_t = _dot(cmp_vt_ref[0, 0], jnp.concatenate(p_cmp, axis=1))

    p_hi = p_sum.astype(BF16)
    p_lo = (p_sum - p_hi.astype(F32)).astype(BF16)
    ovl = ovl_ref[...]
    imp = _dot(ovl, p_hi) + _dot(ovl, p_lo)
    jb = lax.broadcasted_iota(jnp.int32, (n_slc, tq), 0)
    tt = lax.broadcasted_iota(jnp.int32, (n_slc, tq), 1) + t0
    cur = lax.shift_right_logical(tt, SLC_SHIFT)
    forced = (jb == 0) | (jb == cur) | (jb == cur - 1)
    score = jnp.where(jb * SLC_LEN <= tt, imp + jnp.where(forced, FORCE_BONUS, 0.0), NEG_INF)
    score_sc[...] = score
    sub = 8
    groups = [score[sub * rg:sub * (rg + 1)] for rg in range(n_slc // sub)]
    ranks = [jnp.zeros((sub, tq), F32) for _ in groups]
    jrow = lax.broadcasted_iota(jnp.int32, (sub, tq), 0)
    for b2 in range(n_slc):
        sb = jnp.broadcast_to(score_sc[b2:b2 + 1, :], (sub, tq))
        for rg, sg in enumerate(groups):
            if sub * rg > b2:
                beats = sb >= sg
            elif sub * rg + sub - 1 < b2:
                beats = sb > sg
            else:
                beats = (sb > sg) | ((sb == sg) & (jrow > b2 - sub * rg))
            ranks[rg] = ranks[rg] + jnp.where(beats, 1.0, 0.0)
    n_sel = min(SLC_TOPK, n_slc)
    sel_bias_t = jnp.concatenate([jnp.where(rk < n_sel, 0.0, NEG_INF) for rk in ranks], axis=0)

    bpt = tq // SLC_LEN
    n_list = jnp.int32(0)
    for kt in range(n_slc // bpt - 1):
        rg, off = divmod(kt * bpt, sub)
        hit = jnp.max(jnp.where(ranks[rg][off:off + bpt] < n_sel, 1.0, 0.0), axis=1, keepdims=True)
        hit = jnp.max(hit, axis=0, keepdims=True)[0, 0]
        seq_ref[n_list] = jnp.int32(kt)
        n_list = n_list + ((hit > 0.0) & (kt < i)).astype(jnp.int32)
    seq_ref[n_list] = i

    pshape = (LANES - AUG_BLK_COL, tq)
    frow = lax.broadcasted_iota(jnp.int32, pshape, 0)
    sel_rows = sel_bias_t.astype(BF16)
    if n_slc < AUG_BLK_COL:
        sel_rows = jnp.concatenate([sel_rows, jnp.zeros((AUG_BLK_COL - n_slc, tq), BF16)], axis=0)
    q_slc, q_win = [], []
    for r in range(n_rep):
        pos_rows = jnp.zeros(pshape, F32)
        for k, ck in enumerate(_split3(slopes[r], pshape)):
            pos_rows = jnp.where(frow == k, ck * SLC_LEN, pos_rows)
            pos_rows = jnp.where(frow == AUG_SPLIT + k, ck, pos_rows)
        pos_rows = pos_rows.astype(BF16)
        q_slc.append(jnp.concatenate([qt[r], sel_rows, pos_rows], axis=0))
        q_win.append(jnp.concatenate([qt[r], jnp.zeros_like(sel_rows), pos_rows], axis=0))
    qs_sc[...] = jnp.concatenate(q_slc, axis=1)
    q_win = jnp.concatenate(q_win, axis=1)

    n_wt = WIN // tq + 1
    span = n_wt * tq
    j0 = jnp.maximum(i - (n_wt - 1), 0)
    start_w = pl.multiple_of(j0 * tq, tq)
    s_w = _dot(kwa_ref[0, 0, pl.ds(start_w, span), :], q_win)
    d_w = (lax.broadcasted_iota(jnp.int32, (span, tq), 1) + t0) - \
          (lax.broadcasted_iota(jnp.int32, (span, tq), 0) + start_w)
    mask_w = (d_w >= 0) & (d_w < WIN)
    p_w = jnp.concatenate([_exp2_cols(head(s_w, r), mask_w).astype(BF16) for r in range(n_rep)], axis=1)
    o_win_t = _dot(vwt_ref[0, 0, j0], p_w[0:tq])
    for jj in range(1, n_wt):
        o_win_t = o_win_t + _dot(vwt_ref[0, 0, j0 + jj], p_w[jj * tq:(jj + 1) * tq])
    o_win_t = o_win_t[0:HEAD_DIM] * (1.0 / o_win_t[HEAD_DIM:HEAD_DIM + 1])

    m_sc[...] = jnp.full(m_sc.shape, NEG_INF, F32)
    acc_sc[...] = jnp.zeros(acc_sc.shape, F32)

    def produce(kt, buf):
        start = pl.multiple_of(kt * tq, tq)
        buf[...] = _dot(ksa_ref[0, 0, pl.ds(start, tq), :], qs_sc[...])

    def consume(kt, buf, causal):
        s = buf[...]
        if causal:
            kpos = lax.broadcasted_iota(jnp.int32, s.shape, 0)
            qpos = lax.broadcasted_iota(jnp.int32, s.shape, 1) & (tq - 1)
            s = jnp.where(kpos <= qpos, s, NEG_INF)
        m_old = m_sc[...]
        m_new = jnp.maximum(m_old, jnp.max(s, axis=0, keepdims=True))
        alpha = jnp.exp2(m_old - m_new)
        p = jnp.exp2(s - m_new)
        acc_sc[...] = alpha * acc_sc[...] + _dot(vst_ref[0, 0, kt], p.astype(BF16))
        m_sc[...] = m_new

    def pair(j, carry):
        produce(seq_ref[2 * j + 1], sb_sc)
        consume(seq_ref[2 * j], sa_sc, False)
        produce(seq_ref[2 * j + 2], sa_sc)
        consume(seq_ref[2 * j + 1], sb_sc, False)
        return carry

    produce(seq_ref[0], sa_sc)
    lax.fori_loop(0, n_list // 2, pair, 0)

    @pl.when(n_list % 2 == 1)
    def _():
        produce(i, sb_sc)
        consume(seq_ref[n_list - 1], sa_sc, False)
        consume(i, sb_sc, True)

    @pl.when(n_list % 2 == 0)
    def _():
        consume(i, sa_sc, True)

    o_slc_t = acc_sc[0:HEAD_DIM, :] * (1.0 / acc_sc[HEAD_DIM:HEAD_DIM + 1, :])

    gate_sc[...] = _sigmoid(gl_ref[...].astype(F32).T)
    row0 = 3 * n_rep * g
    gate = lambda k: gate_sc[pl.ds(row0 + k, 1), :]
    for r in range(n_rep):
        o_t = (gate(3 * r) * head(o_cmp_t, r)
               + gate(3 * r + 1) * head(o_slc_t, r)
               + gate(3 * r + 2) * head(o_win_t, r))
        o_ref[:, r * HEAD_DIM:(r + 1) * HEAD_DIM] = o_t.T.astype(o_ref.dtype)


def _attn_call(proj, q_gain, cmp_k, cmp_vt, ks_aug, vs_t, kw_aug, vw_t, slopes, ovl_t, w_out, layer,
               B, T, q_col0, gate_col0):
    n_rep = slopes.shape[1]
    wo_rows, wo_cols = w_out.shape[1:]
    slab = wo_rows // (B * N_KV * (T // ATT_TQ))
    assert slab * B * N_KV * (T // ATT_TQ) == wo_rows and slab % 16 == 0
    gw = n_rep * HEAD_DIM
    nq = T // ATT_TQ
    nc = cmp_k.shape[2]
    n_slc = ovl_t.shape[0]
    qcb = q_col0 // gw
    gcb = gate_col0 // LANES
    assert WIN % ATT_TQ == 0
    keys = pl.BlockSpec((1, 1, T, 2 * HEAD_DIM), lambda b, g, i: (b, g, 0, 0))
    vals = pl.BlockSpec((1, 1, nq, V_ROWS, ATT_TQ), lambda b, g, i: (b, g, 0, 0, 0))
    return pl.pallas_call(
        functools.partial(_attn_kernel, n_rep=n_rep),
        grid=(B, N_KV, nq),
        in_specs=[
            pl.BlockSpec(memory_space=pltpu.SMEM),
            pl.BlockSpec((ATT_TQ, gw), lambda b, g, i: (b * nq + i, qcb + g)),
            pl.BlockSpec((ATT_TQ, LANES), lambda b, g, i: (b * nq + i, gcb)),
            pl.BlockSpec((1, HEAD_DIM), lambda b, g, i: (0, 0)),
            pl.BlockSpec((1, 1, nc, HEAD_DIM), lambda b, g, i: (b, g, 0, 0)),
            pl.BlockSpec((1, 1, HEAD_DIM, nc), lambda b, g, i: (b, g, 0, 0)),
            keys, vals, keys, vals,
            pl.BlockSpec((n_slc, nc), lambda b, g, i: (0, 0)),
            pl.BlockSpec((None, slab, wo_cols), lambda b, g, i: (layer, (b * N_KV + g) * nq + i, 0)),
        ],
        out_specs=[pl.BlockSpec((ATT_TQ, gw), lambda b, g, i: (b * nq + i, g)),
                   pl.BlockSpec((slab, wo_cols), lambda b, g, i: ((b * N_KV + g) * nq + i, 0))],
        out_shape=[jax.ShapeDtypeStruct((B * T, N_KV * gw), BF16),
                   jax.ShapeDtypeStruct((wo_rows, wo_cols), BF16)],
        scratch_shapes=[
            pltpu.VMEM((n_slc, ATT_TQ), F32),
            pltpu.VMEM((2 * HEAD_DIM, n_rep * ATT_TQ), BF16),
            pltpu.VMEM((ATT_TQ, n_rep * ATT_TQ), F32),
            pltpu.VMEM((ATT_TQ, n_rep * ATT_TQ), F32),
            pltpu.VMEM((1, n_rep * ATT_TQ), F32),
            pltpu.VMEM((V_ROWS, n_rep * ATT_TQ), F32),
            pltpu.VMEM((LANES, ATT_TQ), F32),
            pltpu.SMEM((nq + 1,), jnp.int32),
        ],
        compiler_params=_cparams("parallel", "parallel", "arbitrary"),
        name="nsa_attention",
    )(slopes, proj, proj, q_gain, cmp_k, cmp_vt, ks_aug, vs_t, kw_aug, vw_t, ovl_t, w_out)


def _out_kernel(a_ref, o_ref, w_ref, x_ref, g_ref, ng_ref, sc_ref, sh_ref, y_ref, h_ref):
    ka = a_ref.shape[1]
    rc = a_ref.shape[0] // OUT_ROW_CHUNKS
    for c in range(OUT_ROW_CHUNKS):
        rows = slice(c * rc, (c + 1) * rc)
        acc = _dot(a_ref[rows, :], w_ref[0:ka, :]) + _dot(o_ref[rows, :], w_ref[ka:, :])
        y = x_ref[rows, :] + g_ref[0] * acc
        y_ref[rows, :] = y
        h_ref[rows, :] = _norm_mod(y, ng_ref, sc_ref, sh_ref)


def _out_call(a, o, w, xf, gate, ng, sc, sh, T):
    M, D = xf.shape
    ka, ko = a.shape[1], o.shape[1]
    per_b = T // OUT_BM
    row_spec = lambda width: pl.BlockSpec((OUT_BM, width), lambda i: (i, 0))
    mod_spec = pl.BlockSpec((1, 1, D), lambda i: (i // per_b, 0, 0))
    return pl.pallas_call(
        _out_kernel,
        grid=(M // OUT_BM,),
        in_specs=[
            row_spec(ka), row_spec(ko),
            pl.BlockSpec((ka + ko, D), lambda i: (0, 0)),
            row_spec(D), mod_spec,
            pl.BlockSpec((1, D), lambda i: (0, 0)),
            mod_spec, mod_spec,
        ],
        out_specs=[row_spec(D), row_spec(D)],
        out_shape=[jax.ShapeDtypeStruct((M, D), F32), jax.ShapeDtypeStruct((M, D), BF16)],
        compiler_params=_cparams("parallel"),
        name="out_proj",
    )(a, o, w, xf, gate, ng, sc, sh)


def _ffn1_kernel(h_ref, wg_ref, wu_ref, wd_ref, o_ref, wdb_ref, wg_sc, wu_sc):
    @pl.when(pl.program_id(1) == 0)
    def _():
        wg_sc[...] = wg_ref[...].astype(BF16)
        wu_sc[...] = wu_ref[...].astype(BF16)

    wdb_ref[...] = wd_ref[...].astype(BF16)
    h = h_ref[...]
    gate = _dot(h, wg_sc[...])
    up = _dot(h, wu_sc[...])
    o_ref[...] = (gate * _sigmoid(gate) * up).astype(o_ref.dtype)


def _ffn1_call(h, w_gu, w_down, layer):
    M, D = h.shape
    dff = w_gu.shape[2] // 2
    nt, nm = dff // FFN1_BN, M // FFN1_BM
    slab = dff // (nt * nm)
    assert slab * nt * nm == dff and slab % 16 == 0
    return pl.pallas_call(
        _ffn1_kernel,
        grid=(nt, nm),
        in_specs=[
            pl.BlockSpec((FFN1_BM, D), lambda n, i: (i, 0)),
            pl.BlockSpec((None, D, FFN1_BN), lambda n, i: (layer, 0, n)),
            pl.BlockSpec((None, D, FFN1_BN), lambda n, i: (layer, 0, n + nt)),
            pl.BlockSpec((None, slab, D), lambda n, i: (layer, n * nm + i, 0)),
        ],
        out_specs=[pl.BlockSpec((FFN1_BM, FFN1_BN), lambda n, i: (i, n)),
                   pl.BlockSpec((slab, D), lambda n, i: (n * nm + i, 0))],
        out_shape=[jax.ShapeDtypeStruct((M, dff), BF16), jax.ShapeDtypeStruct((dff, D), BF16)],
        scratch_shapes=[pltpu.VMEM((D, FFN1_BN), BF16), pltpu.VMEM((D, FFN1_BN), BF16)],
        compiler_params=_cparams("arbitrary", "arbitrary"),
        name="ffn_up",
    )(h, w_gu, w_gu, w_down)


def _ffn2_kernel(h_ref, w_ref, x_ref, g_ref, y_ref):
    y_ref[...] = x_ref[...] + g_ref[0] * _dot(h_ref[...], w_ref[...])


def _ffn2_call(h, w, xf, gate, T):
    M, D = xf.shape
    dff = h.shape[1]
    per_b = T // FFN2_BM
    return pl.pallas_call(
        _ffn2_kernel,
        grid=(M // FFN2_BM, D // FFN2_BN),
        in_specs=[
            pl.BlockSpec((FFN2_BM, dff), lambda i, n: (i, 0)),
            pl.BlockSpec((dff, FFN2_BN), lambda i, n: (0, n)),
            pl.BlockSpec((FFN2_BM, FFN2_BN), lambda i, n: (i, n)),
            pl.BlockSpec((1, 1, FFN2_BN), lambda i, n: (i // per_b, 0, n)),
        ],
        out_specs=pl.BlockSpec((FFN2_BM, FFN2_BN), lambda i, n: (i, n)),
        out_shape=jax.ShapeDtypeStruct((M, D), F32),
        compiler_params=_cparams("parallel", "arbitrary"),
        name="ffn_down",
    )(h, w, xf, gate)


def _alibi_slopes(n_heads):
    sl = 2.0 ** (-8.0 * np.arange(1, n_heads + 1) / n_heads)
    return jnp.asarray(sl, F32).reshape(N_KV, n_heads // N_KV)


def _overlap_t(T):
    nc = T // CMP_STRIDE
    n_slc = T // SLC_LEN
    cst = np.arange(nc) * CMP_STRIDE
    sst = np.arange(n_slc) * SLC_LEN
    ov = (cst[None, :] < sst[:, None] + SLC_LEN) & (cst[None, :] + CMP_LEN > sst[:, None])
    ov[:, (T - CMP_LEN) // CMP_STRIDE + 1:] = False
    return jnp.asarray(ov.astype(np.float32), BF16)


def kernel(x, c, w_ada, b_ada, norm_g, w_in, q_gain, k_gain, pe_cmp, w_cmp1, w_cmp2,
           w_pool, pool_scale, w_out, w_gate_up, w_down):
    B, T, D = x.shape
    L = w_ada.shape[0]
    pool_w = w_pool.shape[1] * w_pool.shape[2]
    kvw = N_KV * HEAD_DIM
    n_heads = (w_in.shape[2] - pool_w - 6 * kvw) // (HEAD_DIM + 3)
    att_w = n_heads * HEAD_DIM
    n_rep = n_heads // N_KV
    assert w_in.shape[2] == pool_w + att_w + 6 * kvw + 3 * n_heads
    assert T % ATT_TQ == 0 and T >= WIN + ATT_TQ and T % POOL_BT == 0 and T % KPREP_BT == 0
    assert T // SLC_LEN <= AUG_BLK_COL and pool_w % (n_rep * HEAD_DIM) == 0
    assert 1 << SLC_SHIFT == SLC_LEN and ATT_TQ & (ATT_TQ - 1) == 0
    q_col0 = pool_w
    kc_col0 = pool_w + att_w
    ks_col0 = kc_col0 + 2 * kvw
    gate_col0 = kc_col0 + 6 * kvw
    assert ks_col0 % kvw == 0 and gate_col0 % LANES == 0

    xf = x.reshape(B * T, D)
    rows = -(-B // 8) * 8
    c8 = jnp.pad(c, ((0, rows - B), (0, 0)))
    mod = _ada_call(c8, w_ada, b_ada)
    slopes = _alibi_slopes(n_heads)
    ovl_t = _overlap_t(T)

    assert 3 * n_heads <= LANES
    w_in_p = jnp.pad(w_in.astype(BF16), ((0, 0), (0, 0), (0, -w_in.shape[2] % IN_BN)))
    w_pool_b = w_pool.astype(BF16)
    pool_scale_r = pool_scale.reshape(L, 1, pool_w)

    for l in range(L):
        sh1, sc1, g1, sh2, sc2, g2 = [mod[l, :B, k * D:(k + 1) * D].reshape(B, 1, D) for k in range(6)]
        proj = _in_call(xf, norm_g[l, 0:1], sc1, sh1, w_in_p, l, T)
        a_out = _pool_call(proj, w_pool_b, pool_scale_r, l, T)
        ks_aug, vs_t, kw_aug, vw_t = _kprep_call(proj, k_gain[l], B, T, ks_col0)
        pe2 = pe_cmp[l].reshape(2, 2, (CMP_LEN // 2) * HEAD_DIM)
        cmp_k, cmp_vt = _cmp_call(proj, pe2, w_cmp1[l].astype(BF16), w_cmp2[l].astype(BF16), k_gain[l],
                                  B, T, kc_col0)
        o_att, w_out_b = _attn_call(proj, q_gain[l].reshape(1, HEAD_DIM), cmp_k, cmp_vt, ks_aug, vs_t,
                                    kw_aug, vw_t, slopes, ovl_t, w_out, l, B, T, q_col0, gate_col0)
        xf, h2 = _out_call(a_out, o_att, w_out_b, xf, g1, norm_g[l, 1:2], sc2, sh2, T)
        hidden, w_down_b = _ffn1_call(h2, w_gate_up, w_down, l)
        xf = _ffn2_call(hidden, w_down_b, xf, g2, T)
    return xf.reshape(B, T, D)
```

```python
import functools

import numpy as np
import jax
import jax.numpy as jnp
from jax import lax
from jax.experimental import pallas as pl
from jax.experimental.pallas import tpu as pltpu

F32 = jnp.float32
BF16 = jnp.bfloat16

POOL_WINDOWS = (2, 4, 8, 16)
HEAD_DIM = 128
N_KV = 2
CMP_LEN = 32
CMP_STRIDE = 16
SLC_LEN = 64
SLC_SHIFT = 6
SLC_TOPK = 16
WIN = 512
NORM_EPS = 1e-6
NEG_INF = -1e30
FORCE_BONUS = 1e3

LANES = 128
POOL_HALO = 16
POOL_PAD = 8
assert all(w & (w - 1) == 0 and w <= POOL_HALO for w in POOL_WINDOWS)
VMEM_LIMIT_BYTES = 56 * 1024 * 1024

ADA_BN = 2048
IN_BM, IN_BN = 1024, 1280
KPREP_BT = 2048
POOL_BT = 1024
ATT_TQ = 256
OUT_BM = 512
OUT_ROW_CHUNKS = 2
FFN1_BM, FFN1_BN = 1024, 512
FFN2_BM, FFN2_BN = 1024, 512


def _cparams(*sem):
    return pltpu.CompilerParams(dimension_semantics=sem, vmem_limit_bytes=VMEM_LIMIT_BYTES)


def _dot(a, b):
    return jnp.dot(a, b, preferred_element_type=F32)


def _dot_nt(a, b):
    return lax.dot_general(a, b, (((1,), (1,)), ((), ())), preferred_element_type=F32)


def _rms(x):
    return x * lax.rsqrt(jnp.mean(x * x, axis=-1, keepdims=True) + NORM_EPS)


def _sigmoid(x):
    return 1.0 / (1.0 + jnp.exp(-x))


def _ada_kernel(c_ref, w_ref, b_ref, o_ref):
    c = c_ref[...]
    cs = c * _sigmoid(c)
    o_ref[0] = _dot(cs, w_ref[0]) + b_ref[0]


def _ada_call(c8, w_ada, b_ada):
    L, D, N = w_ada.shape
    rows = c8.shape[0]
    return pl.pallas_call(
        _ada_kernel,
        grid=(L, N // ADA_BN),
        in_specs=[
            pl.BlockSpec((rows, D), lambda l, n: (0, 0)),
            pl.BlockSpec((1, D, ADA_BN), lambda l, n: (l, 0, n)),
            pl.BlockSpec((1, 1, ADA_BN), lambda l, n: (l, 0, n)),
        ],
        out_specs=pl.BlockSpec((1, rows, ADA_BN), lambda l, n: (l, 0, n)),
        out_shape=jax.ShapeDtypeStruct((L, rows, N), F32),
        compiler_params=_cparams("parallel", "arbitrary"),
        name="ada_mod",
    )(c8, w_ada, b_ada.reshape(L, 1, N))


def _norm_mod(x, ng_ref, sc_ref, sh_ref):
    y = _rms(x) * ng_ref[...]
    return (y * (1.0 + sc_ref[0]) + sh_ref[0]).astype(BF16)


def _lookahead_row_tile(n_row_tiles):
    def idx(i, n):
        return jnp.where((i == 0) & (n == 0), 0, jnp.minimum(i + 1, n_row_tiles - 1))
    return idx


def _norm_chunks(n_col_steps):
    assert n_col_steps >= 2
    return 1 << ((n_col_steps - 1).bit_length() - 1)


def _norm_matmul_steps(x_ref, ng_ref, sc_ref, sh_ref, h_sc, emit, n_chunks):
    i, n = pl.program_id(0), pl.program_id(1)
    rows = x_ref.shape[0] // n_chunks
    slab_step = (n >= 1) & (n <= n_chunks)

    @pl.when((i == 0) & (n == 0))
    def _():
        h_sc[0] = _norm_mod(x_ref[...], ng_ref, sc_ref, sh_ref)

    @pl.when(jnp.logical_not(slab_step))
    def _():
        emit(h_sc[i % 2])

    for slot in (0, 1):
        @pl.when(slab_step & (i % 2 == slot))
        def _(slot=slot):
            emit(h_sc[slot])
            slab = pl.ds(pl.multiple_of((n - 1) * rows, rows), rows)
            h_sc[1 - slot, slab, :] = _norm_mod(x_ref[slab, :], ng_ref, sc_ref, sh_ref)


def _in_kernel(x_ref, ng_ref, sc_ref, sh_ref, w_ref, o_ref, h_sc, *, n_chunks):
    def emit(h):
        o_ref[...] = _dot(h, w_ref[...]).astype(o_ref.dtype)

    _norm_matmul_steps(x_ref, ng_ref, sc_ref, sh_ref, h_sc, emit, n_chunks)


def _in_call(xf, ng, sc, sh, w_all, layer, T):
    M, D = xf.shape
    N = w_all.shape[2]
    per_b = T // IN_BM
    nm = M // IN_BM
    n_chunks = _norm_chunks(N // IN_BN)
    row = _lookahead_row_tile(nm)
    return pl.pallas_call(
        functools.partial(_in_kernel, n_chunks=n_chunks),
        grid=(nm, N // IN_BN),
        in_specs=[
            pl.BlockSpec((IN_BM, D), lambda i, n: (row(i, n), 0)),
            pl.BlockSpec((1, D), lambda i, n: (0, 0)),
            pl.BlockSpec((1, 1, D), lambda i, n: (row(i, n) // per_b, 0, 0)),
            pl.BlockSpec((1, 1, D), lambda i, n: (row(i, n) // per_b, 0, 0)),
            pl.BlockSpec((None, D, IN_BN), lambda i, n: (layer, 0, n)),
        ],
        out_specs=pl.BlockSpec((IN_BM, IN_BN), lambda i, n: (i, n)),
        out_shape=jax.ShapeDtypeStruct((M, N), BF16),
        scratch_shapes=[pltpu.VMEM((2, IN_BM, D), BF16)],
        compiler_params=_cparams("arbitrary", "arbitrary"),
        name="in_proj",
    )(xf, ng, sc, sh, w_all)


def _pool_kernel(u_ref, halo_ref, w_ref, ps_ref, o_ref, ext_sc, sa_sc, sb_sc, *, tiles_per_batch):
    i = pl.program_id(0)
    bt = u_ref.shape[0]
    group = w_ref.shape[1]
    first = POOL_PAD + POOL_HALO
    n = first + bt
    tile_in_batch = i % tiles_per_batch
    for buf in (ext_sc, sa_sc, sb_sc):
        buf[0:POOL_PAD, :] = jnp.zeros((POOL_PAD, buf.shape[1]), F32)
    ext_sc[POOL_PAD:first, :] = jnp.where(tile_in_batch == 0, 0.0, halo_ref[...].astype(F32))
    ext_sc[first:, :] = u_ref[...].astype(F32)
    t_head = lax.broadcasted_iota(jnp.int32, (POOL_HALO, 1), 0)
    for gi, w in enumerate(POOL_WINDOWS):
        cols = slice(gi * group, (gi + 1) * group)
        src, col_sel, k = ext_sc, cols, 1
        for dst in (sa_sc, sb_sc, sa_sc, sb_sc):
            if k >= w:
                break
            dst[POOL_PAD:n, :] = src[POOL_PAD:n, col_sel] + src[POOL_PAD - k:n - k, col_sel]
            src, col_sel, k = dst, slice(None), 2 * k
        tok = ext_sc[first:, cols]
        pooled = src[first:n, col_sel] * (1.0 / w) - tok
        cnt = jnp.minimum(t_head + 1, w).astype(F32)
        head = src[first:first + POOL_HALO, col_sel] / cnt - tok[0:POOL_HALO]
        head = jnp.where(tile_in_batch == 0, head, pooled[0:POOL_HALO])
        pooled = jnp.concatenate([head, pooled[POOL_HALO:]], axis=0)
        mixed = _dot(pooled.astype(BF16), w_ref[gi])
        o_ref[:, cols] = (mixed * ps_ref[:, cols]).astype(o_ref.dtype)


def _pool_call(proj, w_pool_all, pool_scale_all, layer, T):
    M = proj.shape[0]
    _, ng, group, _ = w_pool_all.shape
    width = ng * group
    tiles_per_batch = T // POOL_BT
    halo_blocks = POOL_BT // POOL_HALO
    return pl.pallas_call(
        functools.partial(_pool_kernel, tiles_per_batch=tiles_per_batch),
        grid=(M // POOL_BT,),
        in_specs=[
            pl.BlockSpec((POOL_BT, width), lambda i: (i, 0)),
            pl.BlockSpec((POOL_HALO, width), lambda i: (jnp.maximum(i * halo_blocks - 1, 0), 0)),
            pl.BlockSpec((None, ng, group, group), lambda i: (layer, 0, 0, 0)),
            pl.BlockSpec((None, 1, width), lambda i: (layer, 0, 0)),
        ],
        out_specs=pl.BlockSpec((POOL_BT, width), lambda i: (i, 0)),
        out_shape=jax.ShapeDtypeStruct((M, width), BF16),
        scratch_shapes=[pltpu.VMEM((POOL_PAD + POOL_HALO + POOL_BT, width), F32),
                        pltpu.VMEM((POOL_PAD + POOL_HALO + POOL_BT, group), F32),
                        pltpu.VMEM((POOL_PAD + POOL_HALO + POOL_BT, group), F32)],
        compiler_params=_cparams("parallel"),
        name="pool_mixer",
    )(proj, proj, w_pool_all, pool_scale_all)


AUG_SPLIT = 3
AUG_BLK_COL = SLC_LEN
AUG_OFF_COL = SLC_LEN + AUG_SPLIT
V_ROWS = HEAD_DIM + 16
LOG2E = 1.4426950408889634


def _key_extra(t):
    lane = lax.broadcasted_iota(jnp.int32, (t.shape[0], LANES), 1)
    blk = lax.shift_right_logical(t, SLC_SHIFT)
    off = t & (SLC_LEN - 1)
    extra = jnp.where(lane == blk, 1.0, 0.0)
    extra = jnp.where((lane >= AUG_BLK_COL) & (lane < AUG_OFF_COL), blk.astype(F32), extra)
    extra = jnp.where((lane >= AUG_OFF_COL) & (lane < AUG_OFF_COL + AUG_SPLIT), off.astype(F32), extra)
    return extra


def _value_tile_t(v_t):
    pad = lax.broadcasted_iota(jnp.int32, (V_ROWS - HEAD_DIM, v_t.shape[1]), 0)
    return jnp.concatenate([v_t, jnp.where(pad == 0, 1.0, 0.0)], axis=0).astype(BF16)


def _kprep_kernel(ks_ref, vs_ref, kw_ref, vw_ref, kg_ref, ksa_ref, vso_ref, kwa_ref, vwo_ref,
                  *, tiles_per_batch):
    bt = ks_ref.shape[0]
    kt = vso_ref.shape[4]
    t = (pl.program_id(0) % tiles_per_batch) * bt + lax.broadcasted_iota(jnp.int32, (bt, 1), 0)
    extra = _key_extra(t).astype(BF16)
    for g in range(N_KV):
        cols = slice(g * HEAD_DIM, (g + 1) * HEAD_DIM)
        ksn = _rms(ks_ref[:, cols].astype(F32)) * kg_ref[1:2, :]
        kwn = _rms(kw_ref[:, cols].astype(F32)) * kg_ref[2:3, :]
        ksa_ref[0, g, :, 0:HEAD_DIM] = ksn.astype(BF16)
        ksa_ref[0, g, :, HEAD_DIM:] = extra
        kwa_ref[0, g, :, 0:HEAD_DIM] = kwn.astype(BF16)
        kwa_ref[0, g, :, HEAD_DIM:] = extra
        vs_t = _value_tile_t(vs_ref[:, cols].astype(F32).T)
        vw_t = _value_tile_t(vw_ref[:, cols].astype(F32).T)
        for j in range(bt // kt):
            vso_ref[0, g, j] = vs_t[:, j * kt:(j + 1) * kt]
            vwo_ref[0, g, j] = vw_t[:, j * kt:(j + 1) * kt]


def _kprep_call(proj, k_gain, B, T, col0):
    kvw = N_KV * HEAD_DIM
    cb = col0 // kvw
    tiles_per_batch = T // KPREP_BT
    vt_per_tile = KPREP_BT // ATT_TQ
    aug = jax.ShapeDtypeStruct((B, N_KV, T, 2 * HEAD_DIM), BF16)
    val = jax.ShapeDtypeStruct((B, N_KV, T // ATT_TQ, V_ROWS, ATT_TQ), BF16)
    in_spec = lambda j: pl.BlockSpec((KPREP_BT, kvw), lambda i: (i, cb + j))
    out_map = lambda i: (i // tiles_per_batch, 0, i % tiles_per_batch, 0)
    val_map = lambda i: (i // tiles_per_batch, 0, i % tiles_per_batch, 0, 0)
    return pl.pallas_call(
        functools.partial(_kprep_kernel, tiles_per_batch=tiles_per_batch),
        grid=(B * tiles_per_batch,),
        in_specs=[in_spec(0), in_spec(1), in_spec(2), in_spec(3),
                  pl.BlockSpec((3, HEAD_DIM), lambda i: (0, 0))],
        out_specs=[pl.BlockSpec((1, N_KV, KPREP_BT, 2 * HEAD_DIM), out_map),
                   pl.BlockSpec((1, N_KV, vt_per_tile, V_ROWS, ATT_TQ), val_map),
                   pl.BlockSpec((1, N_KV, KPREP_BT, 2 * HEAD_DIM), out_map),
                   pl.BlockSpec((1, N_KV, vt_per_tile, V_ROWS, ATT_TQ), val_map)],
        out_shape=[aug, val, aug, val],
        compiler_params=_cparams("parallel"),
        name="kv_prep",
    )(proj, proj, proj, proj, k_gain)


def _compress_one(src_ref, f32_sc, pe_ref, w1_ref, kv, nc):
    half = CMP_LEN // 2
    assert CMP_STRIDE == half
    f32_sc[...] = src_ref[...].astype(F32)
    xs = [f32_sc[pl.ds(j, nc, stride=CMP_STRIDE), :] for j in range(half)]
    x = jnp.concatenate(xs, axis=1)
    kdim = half * HEAD_DIM
    lo = _dot((x + pe_ref[kv, 0:1, :]).astype(BF16), w1_ref[kv, 0:kdim, :])
    hi = _dot((x + pe_ref[kv, 1:2, :]).astype(BF16), w1_ref[kv, kdim:, :])
    pre = lo + pltpu.roll(hi, nc - 1, 0)
    return (pre * _sigmoid(pre)).astype(BF16)


def _cmp_kernel(k_ref, v_ref, pe_ref, w1_ref, w2k_ref, w2vt_ref, kg_ref, ko_ref, vo_ref, f32_sc):
    nc = ko_ref.shape[2]
    kc = _dot(_compress_one(k_ref, f32_sc, pe_ref, w1_ref, 0, nc), w2k_ref[...])
    ko_ref[0, 0] = (_rms(kc) * kg_ref[0:1, :]).astype(ko_ref.dtype)
    vct = _dot_nt(w2vt_ref[...], _compress_one(v_ref, f32_sc, pe_ref, w1_ref, 1, nc))
    vo_ref[0, 0] = vct.astype(vo_ref.dtype)


def _cmp_call(proj, pe2, w1, w2, k_gain, B, T, col0):
    nc = T // CMP_STRIDE
    cb = col0 // HEAD_DIM
    kdim = CMP_LEN * HEAD_DIM
    return pl.pallas_call(
        _cmp_kernel,
        grid=(B, N_KV),
        in_specs=[
            pl.BlockSpec((T, HEAD_DIM), lambda b, g: (b, cb + g)),
            pl.BlockSpec((T, HEAD_DIM), lambda b, g: (b, cb + N_KV + g)),
            pl.BlockSpec((2, 2, kdim // 2), lambda b, g: (0, 0, 0)),
            pl.BlockSpec((2, kdim, HEAD_DIM), lambda b, g: (0, 0, 0)),
            pl.BlockSpec((HEAD_DIM, HEAD_DIM), lambda b, g: (0, 0)),
            pl.BlockSpec((HEAD_DIM, HEAD_DIM), lambda b, g: (0, 0)),
            pl.BlockSpec((3, HEAD_DIM), lambda b, g: (0, 0)),
        ],
        out_specs=[pl.BlockSpec((1, 1, nc, HEAD_DIM), lambda b, g: (b, g, 0, 0)),
                   pl.BlockSpec((1, 1, HEAD_DIM, nc), lambda b, g: (b, g, 0, 0))],
        out_shape=[jax.ShapeDtypeStruct((B, N_KV, nc, HEAD_DIM), BF16),
                   jax.ShapeDtypeStruct((B, N_KV, HEAD_DIM, nc), BF16)],
        scratch_shapes=[pltpu.VMEM((T, HEAD_DIM), F32)],
        compiler_params=_cparams("parallel", "arbitrary"),
        name="compress",
    )(proj, proj, pe2, w1, w2[0], w2[1].T, k_gain)


MAX_FLOOR = 0.1 * NEG_INF


def _exp2_cols(s, mask):
    s = jnp.where(mask, s, NEG_INF)
    m = jnp.maximum(jnp.max(s, axis=0, keepdims=True), MAX_FLOOR)
    return jnp.exp2(s - m)


def _split3(c, shape):
    c = jnp.full(shape, c, F32)
    c1 = c.astype(BF16).astype(F32)
    r1 = c - c1
    c2 = r1.astype(BF16).astype(F32)
    c3 = (r1 - c2).astype(BF16).astype(F32)
    return c1, c2, c3


def _attn_kernel(slope_ref, q_ref, gl_ref, qg_ref, cmp_k_ref, cmp_vt_ref, ksa_ref, vst_ref,
                 kwa_ref, vwt_ref, ovl_ref, wo_ref, o_ref, wob_ref, score_sc, qs_sc, sa_sc, sb_sc, m_sc,
                 acc_sc, gate_sc, seq_ref, *, n_rep):
    g = pl.program_id(1)
    i = pl.program_id(2)
    wob_ref[...] = wo_ref[...].astype(BF16)
    tq = q_ref.shape[0]
    nc = cmp_k_ref.shape[2]
    n_slc = ovl_ref.shape[0]
    t0 = i * tq
    slopes = [slope_ref[g, r] * LOG2E for r in range(n_rep)]
    scale = HEAD_DIM ** -0.5 * LOG2E
    head = lambda a, r: a[:, r * tq:(r + 1) * tq]

    qt = []
    for r in range(n_rep):
        x = q_ref[:, r * HEAD_DIM:(r + 1) * HEAD_DIM].astype(F32)
        qt.append((_rms(x) * qg_ref[...] * scale).T.astype(BF16))
    qt_all = jnp.concatenate(qt, axis=1)

    s_all = _dot(cmp_k_ref[0, 0], qt_all)
    c_idx = lax.broadcasted_iota(jnp.int32, (nc, tq), 0)
    t_idx = lax.broadcasted_iota(jnp.int32, (nc, tq), 1) + t0
    dist_c = (t_idx - (c_idx * CMP_STRIDE + (CMP_LEN - 1))).astype(F32)
    mask_c = dist_c >= 0.0
    p_cmp = []
    p_sum = jnp.zeros((nc, tq), F32)
    for r in range(n_rep):
        p = _exp2_cols(head(s_all, r) - slopes[r] * dist_c, mask_c)
        l = jnp.sum(p, axis=0, keepdims=True)
        p = p * (1.0 / jnp.where(l > 0.0, l, 1.0))
        p_cmp.append(p.astype(BF16))
        p_sum = p_sum + p
    o_cmp_t = _dot(cmp_vt_ref[0, 0], jnp.concatenate(p_cmp, axis=1))

    p_hi = p_sum.astype(BF16)
    p_lo = (p_sum - p_hi.astype(F32)).astype(BF16)
    ovl = ovl_ref[...]
    imp = _dot(ovl, p_hi) + _dot(ovl, p_lo)
    jb = lax.broadcasted_iota(jnp.int32, (n_slc, tq), 0)
    tt = lax.broadcasted_iota(jnp.int32, (n_slc, tq), 1) + t0
    cur = lax.shift_right_logical(tt, SLC_SHIFT)
    forced = (jb == 0) | (jb == cur) | (jb == cur - 1)
    score = jnp.where(jb * SLC_LEN <= tt, imp + jnp.where(forced, FORCE_BONUS, 0.0), NEG_INF)
    score_sc[...] = score
    sub = 8
    groups = [score[sub * rg:sub * (rg + 1)] for rg in range(n_slc // sub)]
    ranks = [jnp.zeros((sub, tq), F32) for _ in groups]
    jrow = lax.broadcasted_iota(jnp.int32, (sub, tq), 0)
    for b2 in range(n_slc):
        sb = jnp.broadcast_to(score_sc[b2:b2 + 1, :], (sub, tq))
        for rg, sg in enumerate(groups):
            if sub * rg > b2:
                beats = sb >= sg
            elif sub * rg + sub - 1 < b2:
                beats = sb > sg
            else:
                beats = (sb > sg) | ((sb == sg) & (jrow > b2 - sub * rg))
            ranks[rg] = ranks[rg] + jnp.where(beats, 1.0, 0.0)
    n_sel = min(SLC_TOPK, n_slc)
    sel_bias_t = jnp.concatenate([jnp.where(rk < n_sel, 0.0, NEG_INF) for rk in ranks], axis=0)

    bpt = tq // SLC_LEN
    n_list = jnp.int32(0)
    for kt in range(n_slc // bpt - 1):
        rg, off = divmod(kt * bpt, sub)
        hit = jnp.max(jnp.where(ranks[rg][off:off + bpt] < n_sel, 1.0, 0.0), axis=1, keepdims=True)
        hit = jnp.max(hit, axis=0, keepdims=True)[0, 0]
        seq_ref[n_list] = jnp.int32(kt)
        n_list = n_list + ((hit > 0.0) & (kt < i)).astype(jnp.int32)
    seq_ref[n_list] = i

    pshape = (LANES - AUG_BLK_COL, tq)
    frow = lax.broadcasted_iota(jnp.int32, pshape, 0)
    sel_rows = sel_bias_t.astype(BF16)
    if n_slc < AUG_BLK_COL:
        sel_rows = jnp.concatenate([sel_rows, jnp.zeros((AUG_BLK_COL - n_slc, tq), BF16)], axis=0)
    q_slc, q_win = [], []
    for r in range(n_rep):
        pos_rows = jnp.zeros(pshape, F32)
        for k, ck in enumerate(_split3(slopes[r], pshape)):
            pos_rows = jnp.where(frow == k, ck * SLC_LEN, pos_rows)
            pos_rows = jnp.where(frow == AUG_SPLIT + k, ck, pos_rows)
        pos_rows = pos_rows.astype(BF16)
        q_slc.append(jnp.concatenate([qt[r], sel_rows, pos_rows], axis=0))
        q_win.append(jnp.concatenate([qt[r], jnp.zeros_like(sel_rows), pos_rows], axis=0))
    qs_sc[...] = jnp.concatenate(q_slc, axis=1)
    q_win = jnp.concatenate(q_win, axis=1)

    n_wt = WIN // tq + 1
    span = n_wt * tq
    j0 = jnp.maximum(i - (n_wt - 1), 0)
    start_w = pl.multiple_of(j0 * tq, tq)
    s_w = _dot(kwa_ref[0, 0, pl.ds(start_w, span), :], q_win)
    d_w = (lax.broadcasted_iota(jnp.int32, (span, tq), 1) + t0) - \
          (lax.broadcasted_iota(jnp.int32, (span, tq), 0) + start_w)
    mask_w = (d_w >= 0) & (d_w < WIN)
    p_w = jnp.concatenate([_exp2_cols(head(s_w, r), mask_w).astype(BF16) for r in range(n_rep)], axis=1)
    o_win_t = _dot(vwt_ref[0, 0, j0], p_w[0:tq])
    for jj in range(1, n_wt):
        o_win_t = o_win_t + _dot(vwt_ref[0, 0, j0 + jj], p_w[jj * tq:(jj + 1) * tq])
    o_win_t = o_win_t[0:HEAD_DIM] * (1.0 / o_win_t[HEAD_DIM:HEAD_DIM + 1])

    m_sc[...] = jnp.full(m_sc.shape, NEG_INF, F32)
    acc_sc[...] = jnp.zeros(acc_sc.shape, F32)

    def produce(kt, buf):
        start = pl.multiple_of(kt * tq, tq)
        buf[...] = _dot(ksa_ref[0, 0, pl.ds(start, tq), :], qs_sc[...])

    def consume(kt, buf, causal):
        s = buf[...]
        if causal:
            kpos = lax.broadcasted_iota(jnp.int32, s.shape, 0)
            qpos = lax.broadcasted_iota(jnp.int32, s.shape, 1) & (tq - 1)
            s = jnp.where(kpos <= qpos, s, NEG_INF)
        m_old = m_sc[...]
        m_new = jnp.maximum(m_old, jnp.max(s, axis=0, keepdims=True))
        alpha = jnp.exp2(m_old - m_new)
        p = jnp.exp2(s - m_new)
        acc_sc[...] = alpha * acc_sc[...] + _dot(vst_ref[0, 0, kt], p.astype(BF16))
        m_sc[...] = m_new

    def pair(j, carry):
        produce(seq_ref[2 * j + 1], sb_sc)
        consume(seq_ref[2 * j], sa_sc, False)
        produce(seq_ref[2 * j + 2], sa_sc)
        consume(seq_ref[2 * j + 1], sb_sc, False)
        return carry

    produce(seq_ref[0], sa_sc)
    lax.fori_loop(0, n_list // 2, pair, 0)

    @pl.when(n_list % 2 == 1)
    def _():
        produce(i, sb_sc)
        consume(seq_ref[n_list - 1], sa_sc, False)
        consume(i, sb_sc, True)

    @pl.when(n_list % 2 == 0)
    def _():
        consume(i, sa_sc, True)

    o_slc_t = acc_sc[0:HEAD_DIM, :] * (1.0 / acc_sc[HEAD_DIM:HEAD_DIM + 1, :])

    gate_sc[...] = _sigmoid(gl_ref[...].astype(F32).T)
    row0 = 3 * n_rep * g
    gate = lambda k: gate_sc[pl.ds(row0 + k, 1), :]
    for r in range(n_rep):
        o_t = (gate(3 * r) * head(o_cmp_t, r)
               + gate(3 * r + 1) * head(o_slc_t, r)
               + gate(3 * r + 2) * head(o_win_t, r))
        o_ref[:, r * HEAD_DIM:(r + 1) * HEAD_DIM] = o_t.T.astype(o_ref.dtype)


def _attn_call(proj, q_gain, cmp_k, cmp_vt, ks_aug, vs_t, kw_aug, vw_t, slopes, ovl_t, w_out, layer,
               B, T, q_col0, gate_col0):
    n_rep = slopes.shape[1]
    wo_rows, wo_cols = w_out.shape[1:]
    slab = wo_rows // (B * N_KV * (T // ATT_TQ))
    assert slab * B * N_KV * (T // ATT_TQ) == wo_rows and slab % 16 == 0
    gw = n_rep * HEAD_DIM
    nq = T // ATT_TQ
    nc = cmp_k.shape[2]
    n_slc = ovl_t.shape[0]
    qcb = q_col0 // gw
    gcb = gate_col0 // LANES
    assert WIN % ATT_TQ == 0
    keys = pl.BlockSpec((1, 1, T, 2 * HEAD_DIM), lambda b, g, i: (b, g, 0, 0))
    vals = pl.BlockSpec((1, 1, nq, V_ROWS, ATT_TQ), lambda b, g, i: (b, g, 0, 0, 0))
    return pl.pallas_call(
        functools.partial(_attn_kernel, n_rep=n_rep),
        grid=(B, N_KV, nq),
        in_specs=[
            pl.BlockSpec(memory_space=pltpu.SMEM),
            pl.BlockSpec((ATT_TQ, gw), lambda b, g, i: (b * nq + i, qcb + g)),
            pl.BlockSpec((ATT_TQ, LANES), lambda b, g, i: (b * nq + i, gcb)),
            pl.BlockSpec((1, HEAD_DIM), lambda b, g, i: (0, 0)),
            pl.BlockSpec((1, 1, nc, HEAD_DIM), lambda b, g, i: (b, g, 0, 0)),
            pl.BlockSpec((1, 1, HEAD_DIM, nc), lambda b, g, i: (b, g, 0, 0)),
            keys, vals, keys, vals,
            pl.BlockSpec((n_slc, nc), lambda b, g, i: (0, 0)),
            pl.BlockSpec((None, slab, wo_cols), lambda b, g, i: (layer, (b * N_KV + g) * nq + i, 0)),
        ],
        out_specs=[pl.BlockSpec((ATT_TQ, gw), lambda b, g, i: (b * nq + i, g)),
                   pl.BlockSpec((slab, wo_cols), lambda b, g, i: ((b * N_KV + g) * nq + i, 0))],
        out_shape=[jax.ShapeDtypeStruct((B * T, N_KV * gw), BF16),
                   jax.ShapeDtypeStruct((wo_rows, wo_cols), BF16)],
        scratch_shapes=[
            pltpu.VMEM((n_slc, ATT_TQ), F32),
            pltpu.VMEM((2 * HEAD_DIM, n_rep * ATT_TQ), BF16),
            pltpu.VMEM((ATT_TQ, n_rep * ATT_TQ), F32),
            pltpu.VMEM((ATT_TQ, n_rep * ATT_TQ), F32),
            pltpu.VMEM((1, n_rep * ATT_TQ), F32),
            pltpu.VMEM((V_ROWS, n_rep * ATT_TQ), F32),
            pltpu.VMEM((LANES, ATT_TQ), F32),
            pltpu.SMEM((nq + 1,), jnp.int32),
        ],
        compiler_params=_cparams("parallel", "parallel", "arbitrary"),
        name="nsa_attention",
    )(slopes, proj, proj, q_gain, cmp_k, cmp_vt, ks_aug, vs_t, kw_aug, vw_t, ovl_t, w_out)


def _out_kernel(a_ref, o_ref, w_ref, x_ref, g_ref, ng_ref, sc_ref, sh_ref, y_ref, h_ref):
    ka = a_ref.shape[1]
    rc = a_ref.shape[0] // OUT_ROW_CHUNKS
    for c in range(OUT_ROW_CHUNKS):
        rows = slice(c * rc, (c + 1) * rc)
        acc = _dot(a_ref[rows, :], w_ref[0:ka, :]) + _dot(o_ref[rows, :], w_ref[ka:, :])
        y = x_ref[rows, :] + g_ref[0] * acc
        y_ref[rows, :] = y
        h_ref[rows, :] = _norm_mod(y, ng_ref, sc_ref, sh_ref)


def _out_call(a, o, w, xf, gate, ng, sc, sh, T):
    M, D = xf.shape
    ka, ko = a.shape[1], o.shape[1]
    per_b = T // OUT_BM
    row_spec = lambda width: pl.BlockSpec((OUT_BM, width), lambda i: (i, 0))
    mod_spec = pl.BlockSpec((1, 1, D), lambda i: (i // per_b, 0, 0))
    return pl.pallas_call(
        _out_kernel,
        grid=(M // OUT_BM,),
        in_specs=[
            row_spec(ka), row_spec(ko),
            pl.BlockSpec((ka + ko, D), lambda i: (0, 0)),
            row_spec(D), mod_spec,
            pl.BlockSpec((1, D), lambda i: (0, 0)),
            mod_spec, mod_spec,
        ],
        out_specs=[row_spec(D), row_spec(D)],
        out_shape=[jax.ShapeDtypeStruct((M, D), F32), jax.ShapeDtypeStruct((M, D), BF16)],
        compiler_params=_cparams("parallel"),
        name="out_proj",
    )(a, o, w, xf, gate, ng, sc, sh)


def _ffn1_kernel(h_ref, wg_ref, wu_ref, wd_ref, o_ref, wdb_ref, wg_sc, wu_sc):
    @pl.when(pl.program_id(1) == 0)
    def _():
        wg_sc[...] = wg_ref[...].astype(BF16)
        wu_sc[...] = wu_ref[...].astype(BF16)

    wdb_ref[...] = wd_ref[...].astype(BF16)
    h = h_ref[...]
    gate = _dot(h, wg_sc[...])
    up = _dot(h, wu_sc[...])
    o_ref[...] = (gate * _sigmoid(gate) * up).astype(o_ref.dtype)


def _ffn1_call(h, w_gu, w_down, layer):
    M, D = h.shape
    dff = w_gu.shape[2] // 2
    nt, nm = dff // FFN1_BN, M // FFN1_BM
    slab = dff // (nt * nm)
    assert slab * nt * nm == dff and slab % 16 == 0
    return pl.pallas_call(
        _ffn1_kernel,
        grid=(nt, nm),
        in_specs=[
            pl.BlockSpec((FFN1_BM, D), lambda n, i: (i, 0)),
            pl.BlockSpec((None, D, FFN1_BN), lambda n, i: (layer, 0, n)),
            pl.BlockSpec((None, D, FFN1_BN), lambda n, i: (layer, 0, n + nt)),
            pl.BlockSpec((None, slab, D), lambda n, i: (layer, n * nm + i, 0)),
        ],
        out_specs=[pl.BlockSpec((FFN1_BM, FFN1_BN), lambda n, i: (i, n)),
                   pl.BlockSpec((slab, D), lambda n, i: (n * nm + i, 0))],
        out_shape=[jax.ShapeDtypeStruct((M, dff), BF16), jax.ShapeDtypeStruct((dff, D), BF16)],
        scratch_shapes=[pltpu.VMEM((D, FFN1_BN), BF16), pltpu.VMEM((D, FFN1_BN), BF16)],
        compiler_params=_cparams("arbitrary", "arbitrary"),
        name="ffn_up",
    )(h, w_gu, w_gu, w_down)


def _ffn2_kernel(h_ref, w_ref, x_ref, g_ref, y_ref):
    y_ref[...] = x_ref[...] + g_ref[0] * _dot(h_ref[...], w_ref[...])


def _ffn2_call(h, w, xf, gate, T):
    M, D = xf.shape
    dff = h.shape[1]
    per_b = T // FFN2_BM
    return pl.pallas_call(
        _ffn2_kernel,
        grid=(D // FFN2_BN, M // FFN2_BM),
        in_specs=[
            pl.BlockSpec((FFN2_BM, dff), lambda n, i: (i, 0)),
            pl.BlockSpec((dff, FFN2_BN), lambda n, i: (0, n)),
            pl.BlockSpec((FFN2_BM, FFN2_BN), lambda n, i: (i, n)),
            pl.BlockSpec((1, 1, FFN2_BN), lambda n, i: (i // per_b, 0, n)),
        ],
        out_specs=pl.BlockSpec((FFN2_BM, FFN2_BN), lambda n, i: (i, n)),
        out_shape=jax.ShapeDtypeStruct((M, D), F32),
        compiler_params=_cparams("arbitrary", "arbitrary"),
        name="ffn_down",
    )(h, w, xf, gate)


def _alibi_slopes(n_heads):
    sl = 2.0 ** (-8.0 * np.arange(1, n_heads + 1) / n_heads)
    return jnp.asarray(sl, F32).reshape(N_KV, n_heads // N_KV)


def _overlap_t(T):
    nc = T // CMP_STRIDE
    n_slc = T // SLC_LEN
    cst = np.arange(nc) * CMP_STRIDE
    sst = np.arange(n_slc) * SLC_LEN
    ov = (cst[None, :] < sst[:, None] + SLC_LEN) & (cst[None, :] + CMP_LEN > sst[:, None])
    ov[:, (T - CMP_LEN) // CMP_STRIDE + 1:] = False
    return jnp.asarray(ov.astype(np.float32), BF16)


def kernel(x, c, w_ada, b_ada, norm_g, w_in, q_gain, k_gain, pe_cmp, w_cmp1, w_cmp2,
           w_pool, pool_scale, w_out, w_gate_up, w_down):
    B, T, D = x.shape
    L = w_ada.shape[0]
    pool_w = w_pool.shape[1] * w_pool.shape[2]
    kvw = N_KV * HEAD_DIM
    n_heads = (w_in.shape[2] - pool_w - 6 * kvw) // (HEAD_DIM + 3)
    att_w = n_heads * HEAD_DIM
    n_rep = n_heads // N_KV
    assert w_in.shape[2] == pool_w + att_w + 6 * kvw + 3 * n_heads
    assert T % ATT_TQ == 0 and T >= WIN + ATT_TQ and T % POOL_BT == 0 and T % KPREP_BT == 0
    assert T // SLC_LEN <= AUG_BLK_COL and pool_w % (n_rep * HEAD_DIM) == 0
    assert 1 << SLC_SHIFT == SLC_LEN and ATT_TQ & (ATT_TQ - 1) == 0
    q_col0 = pool_w
    kc_col0 = pool_w + att_w
    ks_col0 = kc_col0 + 2 * kvw
    gate_col0 = kc_col0 + 6 * kvw
    assert ks_col0 % kvw == 0 and gate_col0 % LANES == 0

    xf = x.reshape(B * T, D)
    rows = -(-B // 8) * 8
    c8 = jnp.pad(c, ((0, rows - B), (0, 0)))
    mod = _ada_call(c8, w_ada, b_ada)
    slopes = _alibi_slopes(n_heads)
    ovl_t = _overlap_t(T)

    assert 3 * n_heads <= LANES
    w_in_p = jnp.pad(w_in.astype(BF16), ((0, 0), (0, 0), (0, -w_in.shape[2] % IN_BN)))
    w_pool_b = w_pool.astype(BF16)
    pool_scale_r = pool_scale.reshape(L, 1, pool_w)

    for l in range(L):
        sh1, sc1, g1, sh2, sc2, g2 = [mod[l, :B, k * D:(k + 1) * D].reshape(B, 1, D) for k in range(6)]
        proj = _in_call(xf, norm_g[l, 0:1], sc1, sh1, w_in_p, l, T)
        a_out = _pool_call(proj, w_pool_b, pool_scale_r, l, T)
        ks_aug, vs_t, kw_aug, vw_t = _kprep_call(proj, k_gain[l], B, T, ks_col0)
        pe2 = pe_cmp[l].reshape(2, 2, (CMP_LEN // 2) * HEAD_DIM)
        cmp_k, cmp_vt = _cmp_call(proj, pe2, w_cmp1[l].astype(BF16), w_cmp2[l].astype(BF16), k_gain[l],
                                  B, T, kc_col0)
        o_att, w_out_b = _attn_call(proj, q_gain[l].reshape(1, HEAD_DIM), cmp_k, cmp_vt, ks_aug, vs_t,
                                    kw_aug, vw_t, slopes, ovl_t, w_out, l, B, T, q_col0, gate_col0)
        xf, h2 = _out_call(a_out, o_att, w_out_b, xf, g1, norm_g[l, 1:2], sc2, sh2, T)
        hidden, w_down_b = _ffn1_call(h2, w_gate_up, w_down, l)
        xf = _ffn2_call(hidden, w_down_b, xf, g2, T)
    return xf.reshape(B, T, D)
```

```python
import functools

import numpy as np
import jax
import jax.numpy as jnp
from jax import lax
from jax.experimental import pallas as pl
from jax.experimental.pallas import tpu as pltpu

F32 = jnp.float32
BF16 = jnp.bfloat16

POOL_WINDOWS = (2, 4, 8, 16)
HEAD_DIM = 128
N_KV = 2
CMP_LEN = 32
CMP_STRIDE = 16
SLC_LEN = 64
SLC_SHIFT = 6
SLC_TOPK = 16
WIN = 512
NORM_EPS = 1e-6
NEG_INF = -1e30
FORCE_BONUS = 1e3

LANES = 128
POOL_HALO = 16
POOL_PAD = 8
assert all(w & (w - 1) == 0 and w <= POOL_HALO for w in POOL_WINDOWS)
VMEM_LIMIT_BYTES = 56 * 1024 * 1024

ADA_BN = 2048
IN_BM, IN_BN = 1024, 1280
KPREP_BT = 2048
POOL_BT = 1024
ATT_TQ = 256
OUT_BM = 512
OUT_ROW_CHUNKS = 2
FFN1_BM, FFN1_BN = 1024, 512
FFN2_BM, FFN2_BN = 512, 1024


def _cparams(*sem):
    return pltpu.CompilerParams(dimension_semantics=sem, vmem_limit_bytes=VMEM_LIMIT_BYTES)


def _dot(a, b):
    return jnp.dot(a, b, preferred_element_type=F32)


def _dot_nt(a, b):
    return lax.dot_general(a, b, (((1,), (1,)), ((), ())), preferred_element_type=F32)


def _rms(x):
    return x * lax.rsqrt(jnp.mean(x * x, axis=-1, keepdims=True) + NORM_EPS)


def _sigmoid(x):
    return 1.0 / (1.0 + jnp.exp(-x))


def _ada_kernel(c_ref, w_ref, b_ref, o_ref):
    c = c_ref[...]
    cs = c * _sigmoid(c)
    o_ref[0] = _dot(cs, w_ref[0]) + b_ref[0]


def _ada_call(c8, w_ada, b_ada):
    L, D, N = w_ada.shape
    rows = c8.shape[0]
    return pl.pallas_call(
        _ada_kernel,
        grid=(L, N // ADA_BN),
        in_specs=[
            pl.BlockSpec((rows, D), lambda l, n: (0, 0)),
            pl.BlockSpec((1, D, ADA_BN), lambda l, n: (l, 0, n)),
            pl.BlockSpec((1, 1, ADA_BN), lambda l, n: (l, 0, n)),
        ],
        out_specs=pl.BlockSpec((1, rows, ADA_BN), lambda l, n: (l, 0, n)),
        out_shape=jax.ShapeDtypeStruct((L, rows, N), F32),
        compiler_params=_cparams("parallel", "arbitrary"),
        name="ada_mod",
    )(c8, w_ada, b_ada.reshape(L, 1, N))


def _norm_mod(x, ng_ref, sc_ref, sh_ref):
    y = _rms(x) * ng_ref[...]
    return (y * (1.0 + sc_ref[0]) + sh_ref[0]).astype(BF16)


def _lookahead_row_tile(n_row_tiles):
    def idx(i, n):
        return jnp.where((i == 0) & (n == 0), 0, jnp.minimum(i + 1, n_row_tiles - 1))
    return idx


def _norm_chunks(n_col_steps):
    assert n_col_steps >= 2
    return 1 << ((n_col_steps - 1).bit_length() - 1)


def _norm_matmul_steps(x_ref, ng_ref, sc_ref, sh_ref, h_sc, emit, n_chunks):
    i, n = pl.program_id(0), pl.program_id(1)
    rows = x_ref.shape[0] // n_chunks
    slab_step = (n >= 1) & (n <= n_chunks)

    @pl.when((i == 0) & (n == 0))
    def _():
        h_sc[0] = _norm_mod(x_ref[...], ng_ref, sc_ref, sh_ref)

    @pl.when(jnp.logical_not(slab_step))
    def _():
        emit(h_sc[i % 2])

    for slot in (0, 1):
        @pl.when(slab_step & (i % 2 == slot))
        def _(slot=slot):
            emit(h_sc[slot])
            slab = pl.ds(pl.multiple_of((n - 1) * rows, rows), rows)
            h_sc[1 - slot, slab, :] = _norm_mod(x_ref[slab, :], ng_ref, sc_ref, sh_ref)


def _in_kernel(x_ref, ng_ref, sc_ref, sh_ref, w_ref, o_ref, h_sc, *, n_chunks):
    def emit(h):
        o_ref[...] = _dot(h, w_ref[...]).astype(o_ref.dtype)

    _norm_matmul_steps(x_ref, ng_ref, sc_ref, sh_ref, h_sc, emit, n_chunks)


def _in_call(xf, ng, sc, sh, w_all, layer, T):
    M, D = xf.shape
    N = w_all.shape[2]
    per_b = T // IN_BM
    nm = M // IN_BM
    n_chunks = _norm_chunks(N // IN_BN)
    row = _lookahead_row_tile(nm)
    return pl.pallas_call(
        functools.partial(_in_kernel, n_chunks=n_chunks),
        grid=(nm, N // IN_BN),
        in_specs=[
            pl.BlockSpec((IN_BM, D), lambda i, n: (row(i, n), 0)),
            pl.BlockSpec((1, D), lambda i, n: (0, 0)),
            pl.BlockSpec((1, 1, D), lambda i, n: (row(i, n) // per_b, 0, 0)),
            pl.BlockSpec((1, 1, D), lambda i, n: (row(i, n) // per_b, 0, 0)),
            pl.BlockSpec((None, D, IN_BN), lambda i, n: (layer, 0, n)),
        ],
        out_specs=pl.BlockSpec((IN_BM, IN_BN), lambda i, n: (i, n)),
        out_shape=jax.ShapeDtypeStruct((M, N), BF16),
        scratch_shapes=[pltpu.VMEM((2, IN_BM, D), BF16)],
        compiler_params=_cparams("arbitrary", "arbitrary"),
        name="in_proj",
    )(xf, ng, sc, sh, w_all)


def _pool_kernel(u_ref, halo_ref, w_ref, ps_ref, o_ref, ext_sc, sa_sc, sb_sc, *, tiles_per_batch):
    i = pl.program_id(0)
    bt = u_ref.shape[0]
    group = w_ref.shape[1]
    first = POOL_PAD + POOL_HALO
    n = first + bt
    tile_in_batch = i % tiles_per_batch
    for buf in (ext_sc, sa_sc, sb_sc):
        buf[0:POOL_PAD, :] = jnp.zeros((POOL_PAD, buf.shape[1]), F32)
    ext_sc[POOL_PAD:first, :] = jnp.where(tile_in_batch == 0, 0.0, halo_ref[...].astype(F32))
    ext_sc[first:, :] = u_ref[...].astype(F32)
    t_head = lax.broadcasted_iota(jnp.int32, (POOL_HALO, 1), 0)
    for gi, w in enumerate(POOL_WINDOWS):
        cols = slice(gi * group, (gi + 1) * group)
        src, col_sel, k = ext_sc, cols, 1
        for dst in (sa_sc, sb_sc, sa_sc, sb_sc):
            if k >= w:
                break
            dst[POOL_PAD:n, :] = src[POOL_PAD:n, col_sel] + src[POOL_PAD - k:n - k, col_sel]
            src, col_sel, k = dst, slice(None), 2 * k
        tok = ext_sc[first:, cols]
        pooled = src[first:n, col_sel] * (1.0 / w) - tok
        cnt = jnp.minimum(t_head + 1, w).astype(F32)
        head = src[first:first + POOL_HALO, col_sel] / cnt - tok[0:POOL_HALO]
        head = jnp.where(tile_in_batch == 0, head, pooled[0:POOL_HALO])
        pooled = jnp.concatenate([head, pooled[POOL_HALO:]], axis=0)
        mixed = _dot(pooled.astype(BF16), w_ref[gi])
        o_ref[:, cols] = (mixed * ps_ref[:, cols]).astype(o_ref.dtype)


def _pool_call(proj, w_pool_all, pool_scale_all, layer, T):
    M = proj.shape[0]
    _, ng, group, _ = w_pool_all.shape
    width = ng * group
    tiles_per_batch = T // POOL_BT
    halo_blocks = POOL_BT // POOL_HALO
    return pl.pallas_call(
        functools.partial(_pool_kernel, tiles_per_batch=tiles_per_batch),
        grid=(M // POOL_BT,),
        in_specs=[
            pl.BlockSpec((POOL_BT, width), lambda i: (i, 0)),
            pl.BlockSpec((POOL_HALO, width), lambda i: (jnp.maximum(i * halo_blocks - 1, 0), 0)),
            pl.BlockSpec((None, ng, group, group), lambda i: (layer, 0, 0, 0)),
            pl.BlockSpec((None, 1, width), lambda i: (layer, 0, 0)),
        ],
        out_specs=pl.BlockSpec((POOL_BT, width), lambda i: (i, 0)),
        out_shape=jax.ShapeDtypeStruct((M, width), BF16),
        scratch_shapes=[pltpu.VMEM((POOL_PAD + POOL_HALO + POOL_BT, width), F32),
                        pltpu.VMEM((POOL_PAD + POOL_HALO + POOL_BT, group), F32),
                        pltpu.VMEM((POOL_PAD + POOL_HALO + POOL_BT, group), F32)],
        compiler_params=_cparams("parallel"),
        name="pool_mixer",
    )(proj, proj, w_pool_all, pool_scale_all)


AUG_SPLIT = 3
AUG_BLK_COL = SLC_LEN
AUG_OFF_COL = SLC_LEN + AUG_SPLIT
V_ROWS = HEAD_DIM + 16
LOG2E = 1.4426950408889634


def _key_extra(t):
    lane = lax.broadcasted_iota(jnp.int32, (t.shape[0], LANES), 1)
    blk = lax.shift_right_logical(t, SLC_SHIFT)
    off = t & (SLC_LEN - 1)
    extra = jnp.where(lane == blk, 1.0, 0.0)
    extra = jnp.where((lane >= AUG_BLK_COL) & (lane < AUG_OFF_COL), blk.astype(F32), extra)
    extra = jnp.where((lane >= AUG_OFF_COL) & (lane < AUG_OFF_COL + AUG_SPLIT), off.astype(F32), extra)
    return extra


def _value_tile_t(v_t):
    pad = lax.broadcasted_iota(jnp.int32, (V_ROWS - HEAD_DIM, v_t.shape[1]), 0)
    return jnp.concatenate([v_t, jnp.where(pad == 0, 1.0, 0.0)], axis=0).astype(BF16)


def _kprep_kernel(ks_ref, vs_ref, kw_ref, vw_ref, kg_ref, ksa_ref, vso_ref, kwa_ref, vwo_ref,
                  *, tiles_per_batch):
    bt = ks_ref.shape[0]
    kt = vso_ref.shape[4]
    t = (pl.program_id(0) % tiles_per_batch) * bt + lax.broadcasted_iota(jnp.int32, (bt, 1), 0)
    extra = _key_extra(t).astype(BF16)
    for g in range(N_KV):
        cols = slice(g * HEAD_DIM, (g + 1) * HEAD_DIM)
        ksn = _rms(ks_ref[:, cols].astype(F32)) * kg_ref[1:2, :]
        kwn = _rms(kw_ref[:, cols].astype(F32)) * kg_ref[2:3, :]
        ksa_ref[0, g, :, 0:HEAD_DIM] = ksn.astype(BF16)
        ksa_ref[0, g, :, HEAD_DIM:] = extra
        kwa_ref[0, g, :, 0:HEAD_DIM] = kwn.astype(BF16)
        kwa_ref[0, g, :, HEAD_DIM:] = extra
        vs_t = _value_tile_t(vs_ref[:, cols].astype(F32).T)
        vw_t = _value_tile_t(vw_ref[:, cols].astype(F32).T)
        for j in range(bt // kt):
            vso_ref[0, g, j] = vs_t[:, j * kt:(j + 1) * kt]
            vwo_ref[0, g, j] = vw_t[:, j * kt:(j + 1) * kt]


def _kprep_call(proj, k_gain, B, T, col0):
    kvw = N_KV * HEAD_DIM
    cb = col0 // kvw
    tiles_per_batch = T // KPREP_BT
    vt_per_tile = KPREP_BT // ATT_TQ
    aug = jax.ShapeDtypeStruct((B, N_KV, T, 2 * HEAD_DIM), BF16)
    val = jax.ShapeDtypeStruct((B, N_KV, T // ATT_TQ, V_ROWS, ATT_TQ), BF16)
    in_spec = lambda j: pl.BlockSpec((KPREP_BT, kvw), lambda i: (i, cb + j))
    out_map = lambda i: (i // tiles_per_batch, 0, i % tiles_per_batch, 0)
    val_map = lambda i: (i // tiles_per_batch, 0, i % tiles_per_batch, 0, 0)
    return pl.pallas_call(
        functools.partial(_kprep_kernel, tiles_per_batch=tiles_per_batch),
        grid=(B * tiles_per_batch,),
        in_specs=[in_spec(0), in_spec(1), in_spec(2), in_spec(3),
                  pl.BlockSpec((3, HEAD_DIM), lambda i: (0, 0))],
        out_specs=[pl.BlockSpec((1, N_KV, KPREP_BT, 2 * HEAD_DIM), out_map),
                   pl.BlockSpec((1, N_KV, vt_per_tile, V_ROWS, ATT_TQ), val_map),
                   pl.BlockSpec((1, N_KV, KPREP_BT, 2 * HEAD_DIM), out_map),
                   pl.BlockSpec((1, N_KV, vt_per_tile, V_ROWS, ATT_TQ), val_map)],
        out_shape=[aug, val, aug, val],
        compiler_params=_cparams("parallel"),
        name="kv_prep",
    )(proj, proj, proj, proj, k_gain)


def _compress_one(src_ref, f32_sc, pe_ref, w1_ref, kv, nc):
    half = CMP_LEN // 2
    assert CMP_STRIDE == half
    f32_sc[...] = src_ref[...].astype(F32)
    xs = [f32_sc[pl.ds(j, nc, stride=CMP_STRIDE), :] for j in range(half)]
    x = jnp.concatenate(xs, axis=1)
    kdim = half * HEAD_DIM
    lo = _dot((x + pe_ref[kv, 0:1, :]).astype(BF16), w1_ref[kv, 0:kdim, :])
    hi = _dot((x + pe_ref[kv, 1:2, :]).astype(BF16), w1_ref[kv, kdim:, :])
    pre = lo + pltpu.roll(hi, nc - 1, 0)
    return (pre * _sigmoid(pre)).astype(BF16)


def _cmp_kernel(k_ref, v_ref, pe_ref, w1_ref, w2k_ref, w2vt_ref, kg_ref, ko_ref, vo_ref, f32_sc):
    nc = ko_ref.shape[2]
    kc = _dot(_compress_one(k_ref, f32_sc, pe_ref, w1_ref, 0, nc), w2k_ref[...])
    ko_ref[0, 0] = (_rms(kc) * kg_ref[0:1, :]).astype(ko_ref.dtype)
    vct = _dot_nt(w2vt_ref[...], _compress_one(v_ref, f32_sc, pe_ref, w1_ref, 1, nc))
    vo_ref[0, 0] = vct.astype(vo_ref.dtype)


def _cmp_call(proj, pe2, w1, w2, k_gain, B, T, col0):
    nc = T // CMP_STRIDE
    cb = col0 // HEAD_DIM
    kdim = CMP_LEN * HEAD_DIM
    return pl.pallas_call(
        _cmp_kernel,
        grid=(B, N_KV),
        in_specs=[
            pl.BlockSpec((T, HEAD_DIM), lambda b, g: (b, cb + g)),
            pl.BlockSpec((T, HEAD_DIM), lambda b, g: (b, cb + N_KV + g)),
            pl.BlockSpec((2, 2, kdim // 2), lambda b, g: (0, 0, 0)),
            pl.BlockSpec((2, kdim, HEAD_DIM), lambda b, g: (0, 0, 0)),
            pl.BlockSpec((HEAD_DIM, HEAD_DIM), lambda b, g: (0, 0)),
            pl.BlockSpec((HEAD_DIM, HEAD_DIM), lambda b, g: (0, 0)),
            pl.BlockSpec((3, HEAD_DIM), lambda b, g: (0, 0)),
        ],
        out_specs=[pl.BlockSpec((1, 1, nc, HEAD_DIM), lambda b, g: (b, g, 0, 0)),
                   pl.BlockSpec((1, 1, HEAD_DIM, nc), lambda b, g: (b, g, 0, 0))],
        out_shape=[jax.ShapeDtypeStruct((B, N_KV, nc, HEAD_DIM), BF16),
                   jax.ShapeDtypeStruct((B, N_KV, HEAD_DIM, nc), BF16)],
        scratch_shapes=[pltpu.VMEM((T, HEAD_DIM), F32)],
        compiler_params=_cparams("parallel", "arbitrary"),
        name="compress",
    )(proj, proj, pe2, w1, w2[0], w2[1].T, k_gain)


MAX_FLOOR = 0.1 * NEG_INF


def _exp2_cols(s, mask):
    s = jnp.where(mask, s, NEG_INF)
    m = jnp.maximum(jnp.max(s, axis=0, keepdims=True), MAX_FLOOR)
    return jnp.exp2(s - m)


def _split3(c, shape):
    c = jnp.full(shape, c, F32)
    c1 = c.astype(BF16).astype(F32)
    r1 = c - c1
    c2 = r1.astype(BF16).astype(F32)
    c3 = (r1 - c2).astype(BF16).astype(F32)
    return c1, c2, c3


def _attn_kernel(slope_ref, q_ref, gl_ref, qg_ref, cmp_k_ref, cmp_vt_ref, ksa_ref, vst_ref,
                 kwa_ref, vwt_ref, ovl_ref, wo_ref, o_ref, wob_ref, score_sc, qs_sc, sa_sc, sb_sc, m_sc,
                 acc_sc, gate_sc, seq_ref, *, n_rep):
    g = pl.program_id(1)
    i = pl.program_id(2)
    wob_ref[...] = wo_ref[...].astype(BF16)
    tq = q_ref.shape[0]
    nc = cmp_k_ref.shape[2]
    n_slc = ovl_ref.shape[0]
    t0 = i * tq
    slopes = [slope_ref[g, r] * LOG2E for r in range(n_rep)]
    scale = HEAD_DIM ** -0.5 * LOG2E
    head = lambda a, r: a[:, r * tq:(r + 1) * tq]

    qt = []
    for r in range(n_rep):
        x = q_ref[:, r * HEAD_DIM:(r + 1) * HEAD_DIM].astype(F32)
        qt.append((_rms(x) * qg_ref[...] * scale).T.astype(BF16))
    qt_all = jnp.concatenate(qt, axis=1)

    s_all = _dot(cmp_k_ref[0, 0], qt_all)
    c_idx = lax.broadcasted_iota(jnp.int32, (nc, tq), 0)
    t_idx = lax.broadcasted_iota(jnp.int32, (nc, tq), 1) + t0
    dist_c = (t_idx - (c_idx * CMP_STRIDE + (CMP_LEN - 1))).astype(F32)
    mask_c = dist_c >= 0.0
    p_cmp = []
    p_sum = jnp.zeros((nc, tq), F32)
    for r in range(n_rep):
        p = _exp2_cols(head(s_all, r) - slopes[r] * dist_c, mask_c)
        l = jnp.sum(p, axis=0, keepdims=True)
        p = p * (1.0 / jnp.where(l > 0.0, l, 1.0))
        p_cmp.append(p.astype(BF16))
        p_sum = p_sum + p
    o_cmp_t = _dot(cmp_vt_ref[0, 0], jnp.concatenate(p_cmp, axis=1))

    p_hi = p_sum.astype(BF16)
    p_lo = (p_sum - p_hi.astype(F32)).astype(BF16)
    ovl = ovl_ref[...]
    imp = _dot(ovl, p_hi) + _dot(ovl, p_lo)
    jb = lax.broadcasted_iota(jnp.int32, (n_slc, tq), 0)
    tt = lax.broadcasted_iota(jnp.int32, (n_slc, tq), 1) + t0
    cur = lax.shift_right_logical(tt, SLC_SHIFT)
    forced = (jb == 0) | (jb == cur) | (jb == cur - 1)
    score = jnp.where(jb * SLC_LEN <= tt, imp + jnp.where(forced, FORCE_BONUS, 0.0), NEG_INF)
    score_sc[...] = score
    sub = 8
    groups = [score[sub * rg:sub * (rg + 1)] for rg in range(n_slc // sub)]
    ranks = [jnp.zeros((sub, tq), F32) for _ in groups]
    jrow = lax.broadcasted_iota(jnp.int32, (sub, tq), 0)
    for b2 in range(n_slc):
        sb = jnp.broadcast_to(score_sc[b2:b2 + 1, :], (sub, tq))
        for rg, sg in enumerate(groups):
            if sub * rg > b2:
                beats = sb >= sg
            elif sub * rg + sub - 1 < b2:
                beats = sb > sg
            else:
                beats = (sb > sg) | ((sb == sg) & (jrow > b2 - sub * rg))
            ranks[rg] = ranks[rg] + jnp.where(beats, 1.0, 0.0)
    n_sel = min(SLC_TOPK, n_slc)
    sel_bias_t = jnp.concatenate([jnp.where(rk < n_sel, 0.0, NEG_INF) for rk in ranks], axis=0)

    bpt = tq // SLC_LEN
    n_list = jnp.int32(0)
    for kt in range(n_slc // bpt - 1):
        rg, off = divmod(kt * bpt, sub)
        hit = jnp.max(jnp.where(ranks[rg][off:off + bpt] < n_sel, 1.0, 0.0), axis=1, keepdims=True)
        hit = jnp.max(hit, axis=0, keepdims=True)[0, 0]
        seq_ref[n_list] = jnp.int32(kt)
        n_list = n_list + ((hit > 0.0) & (kt < i)).astype(jnp.int32)
    seq_ref[n_list] = i

    pshape = (LANES - AUG_BLK_COL, tq)
    frow = lax.broadcasted_iota(jnp.int32, pshape, 0)
    sel_rows = sel_bias_t.astype(BF16)
    if n_slc < AUG_BLK_COL:
        sel_rows = jnp.concatenate([sel_rows, jnp.zeros((AUG_BLK_COL - n_slc, tq), BF16)], axis=0)
    q_slc, q_win = [], []
    for r in range(n_rep):
        pos_rows = jnp.zeros(pshape, F32)
        for k, ck in enumerate(_split3(slopes[r], pshape)):
            pos_rows = jnp.where(frow == k, ck * SLC_LEN, pos_rows)
            pos_rows = jnp.where(frow == AUG_SPLIT + k, ck, pos_rows)
        pos_rows = pos_rows.astype(BF16)
        q_slc.append(jnp.concatenate([qt[r], sel_rows, pos_rows], axis=0))
        q_win.append(jnp.concatenate([qt[r], jnp.zeros_like(sel_rows), pos_rows], axis=0))
    qs_sc[...] = jnp.concatenate(q_slc, axis=1)
    q_win = jnp.concatenate(q_win, axis=1)

    n_wt = WIN // tq + 1
    span = n_wt * tq
    j0 = jnp.maximum(i - (n_wt - 1), 0)
    start_w = pl.multiple_of(j0 * tq, tq)
    s_w = _dot(kwa_ref[0, 0, pl.ds(start_w, span), :], q_win)
    d_w = (lax.broadcasted_iota(jnp.int32, (span, tq), 1) + t0) - \
          (lax.broadcasted_iota(jnp.int32, (span, tq), 0) + start_w)
    mask_w = (d_w >= 0) & (d_w < WIN)
    p_w = jnp.concatenate([_exp2_cols(head(s_w, r), mask_w).astype(BF16) for r in range(n_rep)], axis=1)
    o_win_t = _dot(vwt_ref[0, 0, j0], p_w[0:tq])
    for jj in range(1, n_wt):
        o_win_t = o_win_t + _dot(vwt_ref[0, 0, j0 + jj], p_w[jj * tq:(jj + 1) * tq])
    o_win_t = o_win_t[0:HEAD_DIM] * (1.0 / o_win_t[HEAD_DIM:HEAD_DIM + 1])

    m_sc[...] = jnp.full(m_sc.shape, NEG_INF, F32)
    acc_sc[...] = jnp.zeros(acc_sc.shape, F32)

    def produce(kt, buf):
        start = pl.multiple_of(kt * tq, tq)
        buf[...] = _dot(ksa_ref[0, 0, pl.ds(start, tq), :], qs_sc[...])

    def consume(kt, buf, causal):
        s = buf[...]
        if causal:
            kpos = lax.broadcasted_iota(jnp.int32, s.shape, 0)
            qpos = lax.broadcasted_iota(jnp.int32, s.shape, 1) & (tq - 1)
            s = jnp.where(kpos <= qpos, s, NEG_INF)
        m_old = m_sc[...]
        m_new = jnp.maximum(m_old, jnp.max(s, axis=0, keepdims=True))
        alpha = jnp.exp2(m_old - m_new)
        p = jnp.exp2(s - m_new)
        acc_sc[...] = alpha * acc_sc[...] + _dot(vst_ref[0, 0, kt], p.astype(BF16))
        m_sc[...] = m_new

    def pair(j, carry):
        produce(seq_ref[2 * j + 1], sb_sc)
        consume(seq_ref[2 * j], sa_sc, False)
        produce(seq_ref[2 * j + 2], sa_sc)
        consume(seq_ref[2 * j + 1], sb_sc, False)
        return carry

    produce(seq_ref[0], sa_sc)
    lax.fori_loop(0, n_list // 2, pair, 0)

    @pl.when(n_list % 2 == 1)
    def _():
        produce(i, sb_sc)
        consume(seq_ref[n_list - 1], sa_sc, False)
        consume(i, sb_sc, True)

    @pl.when(n_list % 2 == 0)
    def _():
        consume(i, sa_sc, True)

    o_slc_t = acc_sc[0:HEAD_DIM, :] * (1.0 / acc_sc[HEAD_DIM:HEAD_DIM + 1, :])

    gate_sc[...] = _sigmoid(gl_ref[...].astype(F32).T)
    row0 = 3 * n_rep * g
    gate = lambda k: gate_sc[pl.ds(row0 + k, 1), :]
    for r in range(n_rep):
        o_t = (gate(3 * r) * head(o_cmp_t, r)
               + gate(3 * r + 1) * head(o_slc_t, r)
               + gate(3 * r + 2) * head(o_win_t, r))
        o_ref[:, r * HEAD_DIM:(r + 1) * HEAD_DIM] = o_t.T.astype(o_ref.dtype)


def _attn_call(proj, q_gain, cmp_k, cmp_vt, ks_aug, vs_t, kw_aug, vw_t, slopes, ovl_t, w_out, layer,
               B, T, q_col0, gate_col0):
    n_rep = slopes.shape[1]
    wo_rows, wo_cols = w_out.shape[1:]
    slab = wo_rows // (B * N_KV * (T // ATT_TQ))
    assert slab * B * N_KV * (T // ATT_TQ) == wo_rows and slab % 16 == 0
    gw = n_rep * HEAD_DIM
    nq = T // ATT_TQ
    nc = cmp_k.shape[2]
    n_slc = ovl_t.shape[0]
    qcb = q_col0 // gw
    gcb = gate_col0 // LANES
    assert WIN % ATT_TQ == 0
    keys = pl.BlockSpec((1, 1, T, 2 * HEAD_DIM), lambda b, g, i: (b, g, 0, 0))
    vals = pl.BlockSpec((1, 1, nq, V_ROWS, ATT_TQ), lambda b, g, i: (b, g, 0, 0, 0))
    return pl.pallas_call(
        functools.partial(_attn_kernel, n_rep=n_rep),
        grid=(B, N_KV, nq),
        in_specs=[
            pl.BlockSpec(memory_space=pltpu.SMEM),
            pl.BlockSpec((ATT_TQ, gw), lambda b, g, i: (b * nq + i, qcb + g)),
            pl.BlockSpec((ATT_TQ, LANES), lambda b, g, i: (b * nq + i, gcb)),
            pl.BlockSpec((1, HEAD_DIM), lambda b, g, i: (0, 0)),
            pl.BlockSpec((1, 1, nc, HEAD_DIM), lambda b, g, i: (b, g, 0, 0)),
            pl.BlockSpec((1, 1, HEAD_DIM, nc), lambda b, g, i: (b, g, 0, 0)),
            keys, vals, keys, vals,
            pl.BlockSpec((n_slc, nc), lambda b, g, i: (0, 0)),
            pl.BlockSpec((None, slab, wo_cols), lambda b, g, i: (layer, (b * N_KV + g) * nq + i, 0)),
        ],
        out_specs=[pl.BlockSpec((ATT_TQ, gw), lambda b, g, i: (b * nq + i, g)),
                   pl.BlockSpec((slab, wo_cols), lambda b, g, i: ((b * N_KV + g) * nq + i, 0))],
        out_shape=[jax.ShapeDtypeStruct((B * T, N_KV * gw), BF16),
                   jax.ShapeDtypeStruct((wo_rows, wo_cols), BF16)],
        scratch_shapes=[
            pltpu.VMEM((n_slc, ATT_TQ), F32),
            pltpu.VMEM((2 * HEAD_DIM, n_rep * ATT_TQ), BF16),
            pltpu.VMEM((ATT_TQ, n_rep * ATT_TQ), F32),
            pltpu.VMEM((ATT_TQ, n_rep * ATT_TQ), F32),
            pltpu.VMEM((1, n_rep * ATT_TQ), F32),
            pltpu.VMEM((V_ROWS, n_rep * ATT_TQ), F32),
            pltpu.VMEM((LANES, ATT_TQ), F32),
            pltpu.SMEM((nq + 1,), jnp.int32),
        ],
        compiler_params=_cparams("parallel", "parallel", "arbitrary"),
        name="nsa_attention",
    )(slopes, proj, proj, q_gain, cmp_k, cmp_vt, ks_aug, vs_t, kw_aug, vw_t, ovl_t, w_out)


def _out_kernel(a_ref, o_ref, w_ref, x_ref, g_ref, ng_ref, sc_ref, sh_ref, y_ref, h_ref):
    ka = a_ref.shape[1]
    rc = a_ref.shape[0] // OUT_ROW_CHUNKS
    for c in range(OUT_ROW_CHUNKS):
        rows = slice(c * rc, (c + 1) * rc)
        acc = _dot(a_ref[rows, :], w_ref[0:ka, :]) + _dot(o_ref[rows, :], w_ref[ka:, :])
        y = x_ref[rows, :] + g_ref[0] * acc
        y_ref[rows, :] = y
        h_ref[rows, :] = _norm_mod(y, ng_ref, sc_ref, sh_ref)


def _out_call(a, o, w, xf, gate, ng, sc, sh, T):
    M, D = xf.shape
    ka, ko = a.shape[1], o.shape[1]
    per_b = T // OUT_BM
    row_spec = lambda width: pl.BlockSpec((OUT_BM, width), lambda i: (i, 0))
    mod_spec = pl.BlockSpec((1, 1, D), lambda i: (i // per_b, 0, 0))
    return pl.pallas_call(
        _out_kernel,
        grid=(M // OUT_BM,),
        in_specs=[
            row_spec(ka), row_spec(ko),
            pl.BlockSpec((ka + ko, D), lambda i: (0, 0)),
            row_spec(D), mod_spec,
            pl.BlockSpec((1, D), lambda i: (0, 0)),
            mod_spec, mod_spec,
        ],
        out_specs=[row_spec(D), row_spec(D)],
        out_shape=[jax.ShapeDtypeStruct((M, D), F32), jax.ShapeDtypeStruct((M, D), BF16)],
        compiler_params=_cparams("parallel"),
        name="out_proj",
    )(a, o, w, xf, gate, ng, sc, sh)


def _ffn1_kernel(h_ref, wg_ref, wu_ref, wd_ref, o_ref, wdb_ref, wg_sc, wu_sc):
    @pl.when(pl.program_id(1) == 0)
    def _():
        wg_sc[...] = wg_ref[...].astype(BF16)
        wu_sc[...] = wu_ref[...].astype(BF16)

    wdb_ref[...] = wd_ref[...].astype(BF16)
    h = h_ref[...]
    gate = _dot(h, wg_sc[...])
    up = _dot(h, wu_sc[...])
    o_ref[...] = (gate * _sigmoid(gate) * up).astype(o_ref.dtype)


def _ffn1_call(h, w_gu, w_down, layer):
    M, D = h.shape
    dff = w_gu.shape[2] // 2
    nt, nm = dff // FFN1_BN, M // FFN1_BM
    slab = dff // (nt * nm)
    assert slab * nt * nm == dff and slab % 16 == 0
    return pl.pallas_call(
        _ffn1_kernel,
        grid=(nt, nm),
        in_specs=[
            pl.BlockSpec((FFN1_BM, D), lambda n, i: (i, 0)),
            pl.BlockSpec((None, D, FFN1_BN), lambda n, i: (layer, 0, n)),
            pl.BlockSpec((None, D, FFN1_BN), lambda n, i: (layer, 0, n + nt)),
            pl.BlockSpec((None, slab, D), lambda n, i: (layer, n * nm + i, 0)),
        ],
        out_specs=[pl.BlockSpec((FFN1_BM, FFN1_BN), lambda n, i: (i, n)),
                   pl.BlockSpec((slab, D), lambda n, i: (n * nm + i, 0))],
        out_shape=[jax.ShapeDtypeStruct((M, dff), BF16), jax.ShapeDtypeStruct((dff, D), BF16)],
        scratch_shapes=[pltpu.VMEM((D, FFN1_BN), BF16), pltpu.VMEM((D, FFN1_BN), BF16)],
        compiler_params=_cparams("arbitrary", "arbitrary"),
        name="ffn_up",
    )(h, w_gu, w_gu, w_down)


def _ffn2_kernel(h_ref, w_ref, x_ref, g_ref, y_ref):
    y_ref[...] = x_ref[...] + g_ref[0] * _dot(h_ref[...], w_ref[...])


def _ffn2_call(h, w, xf, gate, T):
    M, D = xf.shape
    dff = h.shape[1]
    per_b = T // FFN2_BM
    return pl.pallas_call(
        _ffn2_kernel,
        grid=(D // FFN2_BN, M // FFN2_BM),
        in_specs=[
            pl.BlockSpec((FFN2_BM, dff), lambda n, i: (i, 0)),
            pl.BlockSpec((dff, FFN2_BN), lambda n, i: (0, n)),
            pl.BlockSpec((FFN2_BM, FFN2_BN), lambda n, i: (i, n)),
            pl.BlockSpec((1, 1, FFN2_BN), lambda n, i: (i // per_b, 0, n)),
        ],
        out_specs=pl.BlockSpec((FFN2_BM, FFN2_BN), lambda n, i: (i, n)),
        out_shape=jax.ShapeDtypeStruct((M, D), F32),
        compiler_params=_cparams("arbitrary", "arbitrary"),
        name="ffn_down",
    )(h, w, xf, gate)


def _alibi_slopes(n_heads):
    sl = 2.0 ** (-8.0 * np.arange(1, n_heads + 1) / n_heads)
    return jnp.asarray(sl, F32).reshape(N_KV, n_heads // N_KV)


def _overlap_t(T):
    nc = T // CMP_STRIDE
    n_slc = T // SLC_LEN
    cst = np.arange(nc) * CMP_STRIDE
    sst = np.arange(n_slc) * SLC_LEN
    ov = (cst[None, :] < sst[:, None] + SLC_LEN) & (cst[None, :] + CMP_LEN > sst[:, None])
    ov[:, (T - CMP_LEN) // CMP_STRIDE + 1:] = False
    return jnp.asarray(ov.astype(np.float32), BF16)


def kernel(x, c, w_ada, b_ada, norm_g, w_in, q_gain, k_gain, pe_cmp, w_cmp1, w_cmp2,
           w_pool, pool_scale, w_out, w_gate_up, w_down):
    B, T, D = x.shape
    L = w_ada.shape[0]
    pool_w = w_pool.shape[1] * w_pool.shape[2]
    kvw = N_KV * HEAD_DIM
    n_heads = (w_in.shape[2] - pool_w - 6 * kvw) // (HEAD_DIM + 3)
    att_w = n_heads * HEAD_DIM
    n_rep = n_heads // N_KV
    assert w_in.shape[2] == pool_w + att_w + 6 * kvw + 3 * n_heads
    assert T % ATT_TQ == 0 and T >= WIN + ATT_TQ and T % POOL_BT == 0 and T % KPREP_BT == 0
    assert T // SLC_LEN <= AUG_BLK_COL and pool_w % (n_rep * HEAD_DIM) == 0
    assert 1 << SLC_SHIFT == SLC_LEN and ATT_TQ & (ATT_TQ - 1) == 0
    q_col0 = pool_w
    kc_col0 = pool_w + att_w
    ks_col0 = kc_col0 + 2 * kvw
    gate_col0 = kc_col0 + 6 * kvw
    assert ks_col0 % kvw == 0 and gate_col0 % LANES == 0

    xf = x.reshape(B * T, D)
    rows = -(-B // 8) * 8
    c8 = jnp.pad(c, ((0, rows - B), (0, 0)))
    mod = _ada_call(c8, w_ada, b_ada)
    slopes = _alibi_slopes(n_heads)
    ovl_t = _overlap_t(T)

    assert 3 * n_heads <= LANES
    w_in_p = jnp.pad(w_in.astype(BF16), ((0, 0), (0, 0), (0, -w_in.shape[2] % IN_BN)))
    w_pool_b = w_pool.astype(BF16)
    pool_scale_r = pool_scale.reshape(L, 1, pool_w)

    for l in range(L):
        sh1, sc1, g1, sh2, sc2, g2 = [mod[l, :B, k * D:(k + 1) * D].reshape(B, 1, D) for k in range(6)]
        proj = _in_call(xf, norm_g[l, 0:1], sc1, sh1, w_in_p, l, T)
        a_out = _pool_call(proj, w_pool_b, pool_scale_r, l, T)
        ks_aug, vs_t, kw_aug, vw_t = _kprep_call(proj, k_gain[l], B, T, ks_col0)
        pe2 = pe_cmp[l].reshape(2, 2, (CMP_LEN // 2) * HEAD_DIM)
        cmp_k, cmp_vt = _cmp_call(proj, pe2, w_cmp1[l].astype(BF16), w_cmp2[l].astype(BF16), k_gain[l],
                                  B, T, kc_col0)
        o_att, w_out_b = _attn_call(proj, q_gain[l].reshape(1, HEAD_DIM), cmp_k, cmp_vt, ks_aug, vs_t,
                                    kw_aug, vw_t, slopes, ovl_t, w_out, l, B, T, q_col0, gate_col0)
        xf, h2 = _out_call(a_out, o_att, w_out_b, xf, g1, norm_g[l, 1:2], sc2, sh2, T)
        hidden, w_down_b = _ffn1_call(h2, w_gate_up, w_down, l)
        xf = _ffn2_call(hidden, w_down_b, xf, g2, T)
    return xf.reshape(B, T, D)
```
